```python
import math
import jax, jax.numpy as jnp
from jax import lax
import numpy as np


D_MODEL = 1024
BATCH = 4
SEQ = 4096
DEPTH = 4

D_MIX = D_MODEL
S5_WIDTH = D_MIX // 4
S5_GROUP = 16
S5_GROUPS = S5_WIDTH // S5_GROUP
S5_STATE = 64
RWKV_HEAD = 64
RWKV_WIDTH = 3 * D_MIX // 8
RWKV_HEADS = RWKV_WIDTH // RWKV_HEAD
RWKV_DECAY_RANK = 64
RWKV_A_RANK = 64
RWKV_GATE_RANK = 128
RWKV_GN_EPS = 64e-5
NA_HEAD = 64
NA_WIDTH = D_MIX - S5_WIDTH - RWKV_WIDTH
NA_HEADS = NA_WIDTH // NA_HEAD
GRID_W = 64
NA_KH = 8
NA_KW = 16
NA_QB = 16
NA_KB = NA_QB + NA_KW
IN_COLS = S5_WIDTH + 4 * RWKV_WIDTH + 3 * NA_WIDTH
N_EXPERTS = 32
TOP_K = 4
D_FF = D_MODEL
SWIGLU_ALPHA = 1.702
SWIGLU_LIMIT = 7.0
MOE_BLOCK = 256
RMS_EPS = 1e-6

kernel_name = 'hybrid_s5_rwkv7_natten_moe_encoder'


def rms_norm(x, g):
    xf = x.astype(jnp.float32)
    y = xf * lax.rsqrt(jnp.mean(xf * xf, axis=-1, keepdims=True) + RMS_EPS)
    return (y * g.astype(jnp.float32)).astype(x.dtype)


def modulate(h, shift, scale):
    return h * (1.0 + scale[:, None, :]) + shift[:, None, :]


def _ssm_combine(e1, e2):
    a1, b1 = e1
    a2, b2 = e2
    return a1 * a2, a2 * b1 + b2


def s5_mixer(u, lam_re, lam_im, log_dt, b_re, b_im, c_re, c_im, d_skip, glu_w):
    bsz, seq, _ = u.shape
    f32 = jnp.float32
    ug = u.astype(f32).reshape(bsz, seq, S5_GROUPS, S5_GROUP)
    y = d_skip.astype(f32) * ug
    for direction in range(2):
        lam = lax.complex(lam_re[direction].astype(f32), lam_im[direction].astype(f32))
        dt = jnp.exp(log_dt[direction].astype(f32))[:, None]
        lam_bar = jnp.exp(lam * dt)
        b_mat = lax.complex(b_re[direction].astype(f32), b_im[direction].astype(f32))
        b_bar = ((lam_bar - 1.0) / lam)[..., None] * b_mat
        bu = jnp.einsum('gph,bsgh->bsgp', b_bar, ug.astype(b_bar.dtype))
        a = jnp.broadcast_to(lam_bar, bu.shape)
        _, states = lax.associative_scan(_ssm_combine, (a, bu), axis=1, reverse=(direction == 1))
        c_mat = lax.complex(c_re[direction].astype(f32), c_im[direction].astype(f32))
        y = y + jnp.real(jnp.einsum('ghp,bsgp->bsgh', c_mat, states))
    y = jax.nn.gelu(y.reshape(bsz, seq, S5_WIDTH))
    out = y * jax.nn.sigmoid(y @ glu_w.astype(f32))
    return out.astype(u.dtype)


def centred_shift(t):
    prev = jnp.pad(t[:, :-1], ((0, 0), (1, 0), (0, 0)))
    nxt = jnp.pad(t[:, 1:], ((0, 0), (0, 1), (0, 0)))
    return 0.5 * (prev + nxt)


def wkv7_scan(r, w, k, v, a, b, reverse):
    def step(state, inp):
        rt, wt, kt, vt, at, bt = inp
        sa = jnp.einsum('bhij,bhj->bhi', state, at)
        state = state * wt[:, :, None, :] + sa[..., None] * bt[:, :, None, :] + vt[..., None] * kt[:, :, None, :]
        return state, jnp.einsum('bhij,bhj->bhi', state, rt)
    s0 = jnp.zeros(r.shape[1:] + (r.shape[-1],), jnp.float32)
    _, y = lax.scan(step, s0, (r, w, k, v, a, b), reverse=reverse)
    return y


def rwkv7_mixer(xr, mu, w0, w1, w2, a0, a1, a2, g1, g2, k_k, k_a, r_k, ln_w, ln_b):
    bsz, seq, _ = xr.shape
    f32 = jnp.float32
    xf = xr.astype(f32)
    xs = xf + (centred_shift(xf) - xf) * mu.astype(f32)
    r, k, v, z = jnp.split(xs, 4, axis=-1)
    heads = lambda t: t.reshape(bsz, seq, RWKV_HEADS, RWKV_HEAD)
    tmaj = lambda t: jnp.moveaxis(heads(t), 1, 0)
    gate = jax.nn.sigmoid(z @ g1.astype(f32)) @ g2.astype(f32)
    kk = heads(k * k_k.astype(f32))
    kk = (kk / jnp.maximum(jnp.linalg.norm(kk, axis=-1, keepdims=True), 1e-12)).reshape(bsz, seq, RWKV_WIDTH)
    y = 0.0
    for direction in range(2):
        w = -jax.nn.softplus(-(w0[direction].astype(f32) + jnp.tanh(z @ w1[direction].astype(f32)) @ w2[direction].astype(f32))) - 0.5
        decay = jnp.exp(-jnp.exp(w))
        a = jax.nn.sigmoid(a0[direction].astype(f32) + (z @ a1[direction].astype(f32)) @ a2[direction].astype(f32))
        k_dir = k * (1.0 + (a - 1.0) * k_a.astype(f32))
        y = y + wkv7_scan(tmaj(r), tmaj(decay), tmaj(k_dir), tmaj(v), tmaj(-kk), tmaj(kk * a), direction == 1)
    y = jnp.moveaxis(y, 0, 1)
    mean = jnp.mean(y, axis=-1, keepdims=True)
    var = jnp.mean(jnp.square(y - mean), axis=-1, keepdims=True)
    y = ((y - mean) * lax.rsqrt(var + RWKV_GN_EPS)).reshape(bsz, seq, RWKV_WIDTH)
    y = y * ln_w.astype(f32) + ln_b.astype(f32)
    bonus = jnp.sum(heads(r) * heads(k) * r_k.astype(f32), axis=-1, keepdims=True) * heads(v)
    out = (y + bonus.reshape(bsz, seq, RWKV_WIDTH)) * gate
    return out.astype(xr.dtype)


def na_mixer(q, k, v, q_g, k_g, rpb):
    bsz, seq, _ = q.shape
    rows = seq // GRID_W
    kh = min(NA_KH, rows)
    n_qb = GRID_W // NA_QB
    f32 = jnp.float32
    split_h = lambda t: t.reshape(bsz, seq, NA_HEADS, NA_HEAD)
    to_grid = lambda t: jnp.transpose(t.reshape(bsz, rows, GRID_W, NA_HEADS, NA_HEAD), (0, 3, 1, 2, 4))
    qg = to_grid(rms_norm(split_h(q), q_g).astype(f32) * (NA_HEAD ** -0.5))
    kg = to_grid(rms_norm(split_h(k), k_g).astype(f32))
    vg = to_grid(v.astype(f32))
    q_col = np.arange(GRID_W).reshape(n_qb, NA_QB)
    c_start = np.clip(q_col - NA_KW // 2, 0, GRID_W - NA_KW)
    kb0 = np.clip(q_col[:, 0] - NA_KW // 2, 0, GRID_W - NA_KB)
    k_col = kb0[:, None] + np.arange(NA_KB)
    kc = k_col[:, None, :]
    col_valid = (kc >= c_start[..., None]) & (kc < c_start[..., None] + NA_KW)
    dx_idx = np.clip(kc - q_col[..., None] + NA_KW - 1, 0, 2 * NA_KW - 2)
    rpb_cols = rpb.astype(f32)[:, :, dx_idx]
    valid = jnp.asarray(col_valid[:, :, None, :])

    def row_block(r):
        rs = jnp.clip(r - kh // 2, 0, rows - kh)
        q_row = lax.dynamic_index_in_dim(qg, r, axis=2, keepdims=False).reshape(bsz, NA_HEADS, n_qb, NA_QB, NA_HEAD)
        k_blk = lax.dynamic_slice_in_dim(kg, rs, kh, axis=2)[:, :, :, k_col]
        v_blk = lax.dynamic_slice_in_dim(vg, rs, kh, axis=2)[:, :, :, k_col]
        dy_idx = rs + jnp.arange(kh) - r + NA_KH - 1
        bias = jnp.transpose(rpb_cols[:, dy_idx], (0, 2, 3, 1, 4))
        s = jnp.einsum('bhnqd,bhynkd->bhnqyk', q_row, k_blk) + bias
        s = jnp.where(valid, s, -jnp.inf)
        p = jax.nn.softmax(s.reshape(s.shape[:4] + (kh * NA_KB,)), axis=-1).reshape(s.shape)
        o = jnp.einsum('bhnqyk,bhynkd->bhnqd', p, v_blk)
        return o.reshape(bsz, NA_HEADS, GRID_W, NA_HEAD)

    out = lax.map(row_block, jnp.arange(rows))
    out = jnp.transpose(out, (1, 0, 3, 2, 4)).reshape(bsz, seq, NA_WIDTH)
    return out.astype(q.dtype)


def clamped_swiglu(hdn):
    glu, lin = hdn[..., ::2], hdn[..., 1::2]
    glu = jnp.minimum(glu, SWIGLU_LIMIT)
    lin = jnp.clip(lin, -SWIGLU_LIMIT, SWIGLU_LIMIT)
    return glu * jax.nn.sigmoid(SWIGLU_ALPHA * glu) * (lin + 1.0)


def moe_ffn(h, router_w, router_b, w1, b1, w2, b2):
    bsz, seq, d = h.shape
    tokens = h.reshape(-1, d)
    n_tok = tokens.shape[0]
    logits = (tokens @ router_w + router_b).astype(jnp.float32)
    top_val, top_idx = lax.top_k(logits, TOP_K)
    gates = jax.nn.softmax(top_val, axis=-1)
    n_assign = n_tok * TOP_K
    n_blocks = -(-n_assign // MOE_BLOCK) + N_EXPERTS
    n_rows = n_blocks * MOE_BLOCK
    expert_flat = top_idx.reshape(-1)
    token_flat = jnp.repeat(jnp.arange(n_tok, dtype=jnp.int32), TOP_K)
    gate_flat = gates.reshape(-1)
    order = jnp.argsort(expert_flat)
    sorted_e = expert_flat[order]
    counts = jnp.bincount(expert_flat, length=N_EXPERTS)
    padded = ((counts + MOE_BLOCK - 1) // MOE_BLOCK) * MOE_BLOCK
    pad_end = jnp.cumsum(padded)
    pad_start = pad_end - padded
    start = jnp.cumsum(counts) - counts
    dest = pad_start[sorted_e] + (jnp.arange(n_assign) - start[sorted_e])
    row_tok = jnp.full((n_rows,), n_tok, jnp.int32).at[dest].set(token_flat[order])
    row_gate = jnp.zeros((n_rows,), jnp.float32).at[dest].set(gate_flat[order])
    block_e = jnp.minimum(jnp.searchsorted(pad_end, jnp.arange(n_blocks) * MOE_BLOCK, side='right'), N_EXPERTS - 1)
    tok_pad = jnp.concatenate([tokens, jnp.zeros((1, d), tokens.dtype)], axis=0)
    xb = tok_pad[row_tok].reshape(n_blocks, MOE_BLOCK, d)

    def expert_block(args):
        xblk, e = args
        hdn = xblk @ w1[e] + b1[e]
        return clamped_swiglu(hdn) @ w2[e] + b2[e]

    yb = lax.map(expert_block, (xb, block_e)).reshape(n_rows, d)
    yb = yb * row_gate[:, None].astype(yb.dtype)
    out = jax.ops.segment_sum(yb, row_tok, num_segments=n_tok + 1)[:n_tok]
    return out.reshape(bsz, seq, d)


def setup_inputs(seed: int = 0) -> dict:
    key = jax.random.key(seed)
    ks = iter(jax.random.split(key, 48))
    f32 = jnp.float32
    L = DEPTH

    def nrm(shape, scale):
        return scale * jax.random.normal(next(ks), shape, f32)

    def unif(shape, lo, hi):
        return jax.random.uniform(next(ks), shape, f32, lo, hi)

    n_idx = jnp.arange(S5_STATE, dtype=f32)
    return {
        'x': nrm((BATCH, SEQ, D_MODEL), 1.0),
        'c': nrm((BATCH, D_MODEL), 1.0),
        'ada_w': nrm((L, D_MODEL, 6 * D_MODEL), 0.5 * D_MODEL ** -0.5),
        'ada_b': nrm((L, 6 * D_MODEL), 0.02),
        'norm1_g': 1.0 + nrm((L, D_MODEL), 0.02),
        'norm2_g': 1.0 + nrm((L, D_MODEL), 0.02),
        'w_in': nrm((L, D_MODEL, IN_COLS), D_MODEL ** -0.5),
        'w_out': nrm((L, D_MIX, D_MODEL), D_MIX ** -0.5),
        's5_lam_re': -0.5 + nrm((L, 2, S5_GROUPS, S5_STATE), 0.01),
        's5_lam_im': math.pi * n_idx + nrm((L, 2, S5_GROUPS, S5_STATE), 0.01),
        's5_log_dt': unif((L, 2, S5_GROUPS), math.log(1e-3), math.log(1e-1)),
        's5_b_re': nrm((L, 2, S5_GROUPS, S5_STATE, S5_GROUP), (2 * S5_GROUP) ** -0.5),
        's5_b_im': nrm((L, 2, S5_GROUPS, S5_STATE, S5_GROUP), (2 * S5_GROUP) ** -0.5),
        's5_c_re': nrm((L, 2, S5_GROUPS, S5_GROUP, S5_STATE), (2 * S5_STATE) ** -0.5),
        's5_c_im': nrm((L, 2, S5_GROUPS, S5_GROUP, S5_STATE), (2 * S5_STATE) ** -0.5),
        's5_d': nrm((L, S5_GROUPS, S5_GROUP), 1.0),
        's5_glu_w': nrm((L, S5_WIDTH, S5_WIDTH), S5_WIDTH ** -0.5),
        'rwkv_mu': unif((L, 4 * RWKV_WIDTH), 0.0, 1.0),
        'rwkv_w0': unif((L, 2, RWKV_WIDTH), -6.0, 1.0),
        'rwkv_w1': nrm((L, 2, RWKV_WIDTH, RWKV_DECAY_RANK), RWKV_WIDTH ** -0.5),
        'rwkv_w2': nrm((L, 2, RWKV_DECAY_RANK, RWKV_WIDTH), 0.1 * RWKV_DECAY_RANK ** -0.5),
        'rwkv_a0': nrm((L, 2, RWKV_WIDTH), 0.5),
        'rwkv_a1': nrm((L, 2, RWKV_WIDTH, RWKV_A_RANK), RWKV_WIDTH ** -0.5),
        'rwkv_a2': nrm((L, 2, RWKV_A_RANK, RWKV_WIDTH), 0.1 * RWKV_A_RANK ** -0.5),
        'rwkv_g1': nrm((L, RWKV_WIDTH, RWKV_GATE_RANK), RWKV_WIDTH ** -0.5),
        'rwkv_g2': nrm((L, RWKV_GATE_RANK, RWKV_WIDTH), RWKV_GATE_RANK ** -0.5),
        'rwkv_k_k': 0.85 + nrm((L, RWKV_WIDTH), 0.02),
        'rwkv_k_a': 1.0 + nrm((L, RWKV_WIDTH), 0.02),
        'rwkv_r_k': nrm((L, RWKV_HEADS, RWKV_HEAD), 0.1),
        'rwkv_ln_w': 1.0 + nrm((L, RWKV_WIDTH), 0.02),
        'rwkv_ln_b': nrm((L, RWKV_WIDTH), 0.02),
        'na_q_g': 1.0 + nrm((L, NA_HEAD), 0.02),
        'na_k_g': 1.0 + nrm((L, NA_HEAD), 0.02),
        'na_rpb': nrm((L, NA_HEADS, 2 * NA_KH - 1, 2 * NA_KW - 1), 0.1),
        'router_w': nrm((L, D_MODEL, N_EXPERTS), D_MODEL ** -0.5),
        'router_b': nrm((L, N_EXPERTS), 0.01),
        'exp_w1': nrm((L, N_EXPERTS, D_MODEL, 2 * D_FF), D_MODEL ** -0.5),
        'exp_b1': nrm((L, N_EXPERTS, 2 * D_FF), 0.02),
        'exp_w2': nrm((L, N_EXPERTS, D_FF, D_MODEL), D_FF ** -0.5),
        'exp_b2': nrm((L, N_EXPERTS, D_MODEL), 0.02),
    }


def reference(x, c, ada_w, ada_b, norm1_g, norm2_g, w_in, w_out,
              s5_lam_re, s5_lam_im, s5_log_dt, s5_b_re, s5_b_im, s5_c_re, s5_c_im, s5_d, s5_glu_w,
              rwkv_mu, rwkv_w0, rwkv_w1, rwkv_w2, rwkv_a0, rwkv_a1, rwkv_a2, rwkv_g1, rwkv_g2,
              rwkv_k_k, rwkv_k_a, rwkv_r_k, rwkv_ln_w, rwkv_ln_b,
              na_q_g, na_k_g, na_rpb,
              router_w, router_b, exp_w1, exp_b1, exp_w2, exp_b2):
    cond = jax.nn.silu(c)
    rwkv_lo = S5_WIDTH
    na_lo = S5_WIDTH + 4 * RWKV_WIDTH
    for l in range(DEPTH):
        mod = cond @ ada_w[l] + ada_b[l]
        shift_mix, scale_mix, gate_mix, shift_ffn, scale_ffn, gate_ffn = jnp.split(mod, 6, axis=-1)
        h = modulate(rms_norm(x, norm1_g[l]), shift_mix, scale_mix)
        proj = h @ w_in[l]
        q_na, k_na, v_na = jnp.split(proj[..., na_lo:], 3, axis=-1)
        mixed = jnp.concatenate([
            s5_mixer(proj[..., :rwkv_lo], s5_lam_re[l], s5_lam_im[l], s5_log_dt[l], s5_b_re[l], s5_b_im[l],
                     s5_c_re[l], s5_c_im[l], s5_d[l], s5_glu_w[l]),
            rwkv7_mixer(proj[..., rwkv_lo:na_lo], rwkv_mu[l], rwkv_w0[l], rwkv_w1[l], rwkv_w2[l], rwkv_a0[l],
                        rwkv_a1[l], rwkv_a2[l], rwkv_g1[l], rwkv_g2[l], rwkv_k_k[l], rwkv_k_a[l], rwkv_r_k[l],
                        rwkv_ln_w[l], rwkv_ln_b[l]),
            na_mixer(q_na, k_na, v_na, na_q_g[l], na_k_g[l], na_rpb[l]),
        ], axis=-1)
        x = x + gate_mix[:, None, :] * (mixed @ w_out[l])
        h = modulate(rms_norm(x, norm2_g[l]), shift_ffn, scale_ffn)
        x = x + gate_ffn[:, None, :] * moe_ffn(h, router_w[l], router_b[l], exp_w1[l], exp_b1[l], exp_w2[l], exp_b2[l])
    return x
```

```python
import functools
import math

import numpy as np
import jax
import jax.numpy as jnp
from jax import lax
from jax.experimental import pallas as pl
from jax.experimental.pallas import tpu as pltpu

F32 = jnp.float32
BF16 = jnp.bfloat16
HIGHEST = lax.Precision.HIGHEST

D_MODEL = 1024
S5_WIDTH = 256
S5_GROUP = 16
S5_GROUPS = 16
S5_STATE = 64
S5_CHUNK = 64
S5_FLAT = S5_GROUPS * S5_STATE
RW = 384
HEAD = 64
RWKV_GN_EPS = 64e-5
NA_W = 384
GRID_W = 64
NA_KH = 8
NA_KW = 16
N_EXPERTS = 32
TOP_K = 4
MOE_BLOCK = 256
SWIGLU_ALPHA = 1.702
SWIGLU_LIMIT = 7.0
RMS_EPS = 1e-6
LANES = 128
WKV_CHUNK = 64
VMEM_LIMIT = 56 * 1024 * 1024


def _cparams(sem):
    return pltpu.CompilerParams(dimension_semantics=sem, vmem_limit_bytes=VMEM_LIMIT)


def _seg_ones(n, seg=HEAD, dtype=F32, scale=1.0):
    idx = np.arange(n) // seg
    return jnp.asarray((idx[:, None] == idx[None, :]).astype(np.float32) * scale, dtype)


def _ada_kernel(c_ref, w_ref, b_ref, o_ref):
    c = c_ref[...]
    cond = c * jax.nn.sigmoid(c)
    o_ref[0] = jnp.dot(cond, w_ref[0], preferred_element_type=F32) + b_ref[0]


def _ada_mod(c, ada_w, ada_b):
    nl, d, n6 = ada_w.shape
    bsz = c.shape[0]
    tn = 1536
    return pl.pallas_call(
        _ada_kernel,
        grid=(nl, n6 // tn),
        in_specs=[pl.BlockSpec((bsz, d), lambda l, j: (0, 0)),
                  pl.BlockSpec((1, d, tn), lambda l, j: (l, 0, j)),
                  pl.BlockSpec((1, 1, tn), lambda l, j: (l, 0, j))],
        out_specs=pl.BlockSpec((1, bsz, tn), lambda l, j: (l, 0, j)),
        out_shape=jax.ShapeDtypeStruct((nl, bsz, n6), F32),
        compiler_params=_cparams(("arbitrary", "arbitrary")),
        name="ada_mod",
    )(c, ada_w, ada_b.reshape(nl, 1, n6))


def _rms_mod(x, g, shift, scale):
    ms = jnp.mean(x * x, axis=-1, keepdims=True)
    h = x * lax.rsqrt(ms + RMS_EPS) * g
    return h * (1.0 + scale) + shift


def _proj_kernel(x_ref, g_ref, sh_ref, sc_ref, w_ref, o_s5, o_rw, o_na):
    h = _rms_mod(x_ref[...], g_ref[...], sh_ref[0], sc_ref[0])
    p = jnp.dot(h.astype(BF16), w_ref[...], preferred_element_type=F32)
    o_s5[...] = p[:, :S5_WIDTH]
    o_rw[...] = p[:, S5_WIDTH:S5_WIDTH + 4 * RW]
    o_na[...] = p[:, S5_WIDTH + 4 * RW:]


def _in_proj(x2, g, shift, scale, w_bf, seq):
    t, d = x2.shape
    n = w_bf.shape[1]
    tm = 256
    per_b = seq // tm
    row = lambda i: (i, 0)
    bvec = lambda i: (i // per_b, 0, 0)
    return pl.pallas_call(
        _proj_kernel,
        grid=(t // tm,),
        in_specs=[pl.BlockSpec((tm, d), row),
                  pl.BlockSpec((1, d), lambda i: (0, 0)),
                  pl.BlockSpec((1, 1, d), bvec),
                  pl.BlockSpec((1, 1, d), bvec),
                  pl.BlockSpec((d, n), lambda i: (0, 0))],
        out_specs=[pl.BlockSpec((tm, S5_WIDTH), row),
                   pl.BlockSpec((tm, 4 * RW), row),
                   pl.BlockSpec((tm, 3 * NA_W), row)],
        out_shape=[jax.ShapeDtypeStruct((t, S5_WIDTH), F32),
                   jax.ShapeDtypeStruct((t, 4 * RW), F32),
                   jax.ShapeDtypeStruct((t, 3 * NA_W), F32)],
        compiler_params=_cparams(("arbitrary",)),
        name="in_proj",
    )(x2, g, shift, scale, w_bf)


def _softplus(x):
    return jnp.maximum(x, 0.0) + jnp.log(1.0 + jnp.exp(-jnp.abs(x)))


def _prep_kernel(xr_ref, prev_ref, next_ref, qkv_ref,
                 mu_ref, kk_ref, ka_ref, rk_ref, w0_ref, a0_ref,
                 w1_ref, w2_ref, a1_ref, a2_ref, g1_ref, g2_ref, qg_ref, kg_ref, ob_ref,
                 nkk_o, v_o, dec0_o, b0_o, k0_o, q0_o, dec1_o, b1_o, k1_o, q1_o,
                 extra_o, gate_o, bonus_o, naq_o, nak_o, nav_o):
    i = pl.program_id(1)
    nblk = pl.num_programs(1)
    x = xr_ref[0]
    tm = x.shape[0]
    prow = jnp.where(i == 0, 0.0, prev_ref[0][7:8, :])
    nrow = jnp.where(i == nblk - 1, 0.0, next_ref[0][0:1, :])
    rid = lax.broadcasted_iota(jnp.int32, x.shape, 0)
    prev = jnp.where(rid == 0, prow, pltpu.roll(x, 1, axis=0))
    nxt = jnp.where(rid == tm - 1, nrow, pltpu.roll(x, tm - 1, axis=0))
    xs = x + (0.5 * (prev + nxt) - x) * mu_ref[...]
    r = xs[:, 0:RW]
    k = xs[:, RW:2 * RW]
    v = xs[:, 2 * RW:3 * RW]
    z = xs[:, 3 * RW:4 * RW]
    ob = ob_ref[...]
    seg = lambda t: jnp.dot(t, ob, precision=HIGHEST, preferred_element_type=F32)
    zb = z.astype(BF16)
    bdot = lambda a, w: jnp.dot(a.astype(BF16), w, preferred_element_type=F32)
    gate_o[0] = bdot(jax.nn.sigmoid(bdot(zb, g1_ref[...])), g2_ref[...])
    kk = k * kk_ref[...]
    kk = kk / jnp.maximum(jnp.sqrt(seg(kk * kk)), 1e-12)
    nkk_o[0] = -kk
    v_o[0] = v
    bonus_o[0] = seg(r * k * rk_ref[...]) * v
    extra = jnp.zeros_like(v)
    outs = ((dec0_o, b0_o, k0_o, q0_o), (dec1_o, b1_o, k1_o, q1_o))
    for d in range(2):
        wl = w0_ref[d:d + 1, :] + bdot(jnp.tanh(bdot(zb, w1_ref[d])), w2_ref[d])
        w = -_softplus(-wl) - 0.5
        dec = jnp.exp(-jnp.exp(w))
        a = jax.nn.sigmoid(a0_ref[d:d + 1, :] + bdot(bdot(zb, a1_ref[d]), a2_ref[d]))
        kd = k * (1.0 + (a - 1.0) * ka_ref[...])
        bv = kk * a
        dec_o, b_o, k_o, q_o = outs[d]
        dec_o[0] = dec
        b_o[0] = bv
        k_o[0] = kd
        q_o[0] = dec * r - kk * seg(bv * r)
        extra = extra + v * seg(kd * r)
    extra_o[0] = extra
    qkv = qkv_ref[0]
    segm = lambda t: seg(t) * (1.0 / HEAD)
    qn = qkv[:, 0:NA_W]
    kn = qkv[:, NA_W:2 * NA_W]
    naq_o[0] = (qn * lax.rsqrt(segm(qn * qn) + RMS_EPS) * qg_ref[...] * (HEAD ** -0.5)).astype(BF16)
    nak_o[0] = (kn * lax.rsqrt(segm(kn * kn) + RMS_EPS) * kg_ref[...]).astype(BF16)
    nav_o[0] = qkv[:, 2 * NA_W:].astype(BF16)


def _prep(xr, qkv, p):
    bsz, seq, _ = xr.shape
    tm = 256
    nb = seq // tm
    h8 = tm // 8
    blk = lambda w: pl.BlockSpec((1, tm, w), lambda b, i: (b, i, 0))
    full = lambda a: pl.BlockSpec(a.shape, lambda b, i, _n=a.ndim: (0,) * _n)
    params = [p["mu"], p["k_k"], p["k_a"], p["r_k"], p["w0"], p["a0"], p["w1"], p["w2"], p["a1"], p["a2"],
              p["g1"], p["g2"], p["q_g"], p["k_g"], p["ob"]]
    f32o = jax.ShapeDtypeStruct((bsz, seq, RW), F32)
    bfo = jax.ShapeDtypeStruct((bsz, seq, NA_W), BF16)
    return pl.pallas_call(
        _prep_kernel,
        grid=(bsz, nb),
        in_specs=[blk(4 * RW),
                  pl.BlockSpec((1, 8, 4 * RW), lambda b, i: (b, jnp.maximum(i * h8 - 1, 0), 0)),
                  pl.BlockSpec((1, 8, 4 * RW), lambda b, i: (b, jnp.minimum((i + 1) * h8, seq // 8 - 1), 0)),
                  blk(3 * NA_W)] + [full(a) for a in params],
        out_specs=[blk(RW)] * 16,
        out_shape=[f32o] * 13 + [bfo] * 3,
        compiler_params=_cparams(("arbitrary", "arbitrary")),
        name="mixer_prep",
    )(xr, xr, xr, qkv, *params)


def _wkv_kernel(*refs, bsz, tc):
    f_in = refs[0:6]
    b_in = refs[6:12]
    e_ref, ob256_ref, ob128_ref = refs[12:15]
    yf_ref, yb_ref = refs[15:17]
    s_ref = refs[17]

    @pl.when(pl.program_id(0) == 0)
    def _():
        s_ref[...] = jnp.zeros_like(s_ref)

    eye = e_ref[...]
    ob256 = ob256_ref[...]
    ob128 = ob128_ref[...]

    def seg(t):
        return jnp.concatenate(
            [jnp.dot(t[:, :256], ob256, preferred_element_type=F32),
             jnp.dot(t[:, 256:], ob128, preferred_element_type=F32)], axis=1)

    def step(i, carry):
        for b in range(bsz):
            for d in range(2):
                src = f_in if d == 0 else b_in
                out = yf_ref if d == 0 else yb_ref
                t = i if d == 0 else tc - 1 - i
                a, v, w, bb, kd, q = [s[b, pl.ds(t, 1), :] for s in src]
                st = s_ref[2 * b + d]
                lhs = jnp.concatenate([st * a, st * q, eye * v], axis=0).astype(BF16)
                res = seg(lhs)
                sa = res[0:HEAD]
                yq = res[HEAD:2 * HEAD]
                vb = res[2 * HEAD:3 * HEAD]
                s_ref[2 * b + d] = st * w + sa * bb + vb * kd
                out[b, pl.ds(t, 1), :] = jnp.sum(yq * eye, axis=0, keepdims=True)
        return carry

    lax.fori_loop(0, tc, step, 0)


def _wkv_scan(ins, eye, ob256, ob128):
    bsz, seq, _ = ins["nkk"].shape
    tc = WKV_CHUNK
    nc = seq // tc
    fwd = pl.BlockSpec((bsz, tc, RW), lambda c: (0, c, 0))
    bwd = pl.BlockSpec((bsz, tc, RW), lambda c: (0, nc - 1 - c, 0))
    full = lambda a: pl.BlockSpec(a.shape, lambda c, _n=a.ndim: (0,) * _n)
    f_args = [ins["nkk"], ins["v"], ins["dec0"], ins["b0"], ins["k0"], ins["q0"]]
    b_args = [ins["nkk"], ins["v"], ins["dec1"], ins["b1"], ins["k1"], ins["q1"]]
    o = jax.ShapeDtypeStruct((bsz, seq, RW), F32)
    return pl.pallas_call(
        functools.partial(_wkv_kernel, bsz=bsz, tc=tc),
        grid=(nc,),
        in_specs=[fwd] * 6 + [bwd] * 6 + [full(eye), full(ob256), full(ob128)],
        out_specs=[fwd, bwd],
        out_shape=[o, o],
        scratch_shapes=[pltpu.VMEM((2 * bsz, HEAD, RW), F32)],
        compiler_params=_cparams(("arbitrary",)),
        name="wkv_scan",
    )(*f_args, *b_args, eye, ob256, ob128)


def _gelu_tanh(x):
    return 0.5 * x * (1.0 + jnp.tanh(math.sqrt(2.0 / math.pi) * (x + 0.044715 * (x * x * x))))


def _s5_kernel(ua_ref, ub_ref, bblk_ref, cblk_ref, lam_ref, lamc_ref, d_ref, glu_ref, o_ref,
               y_ref, st_ref, end_ref, carry_ref, *, seq):
    ch = S5_CHUNK
    nc = seq // ch
    n = S5_FLAT
    u_halves = (ua_ref, ub_ref)
    for hf in range(2):
        y_ref[hf] = u_halves[hf][0] * d_ref[:, hf * LANES:(hf + 1) * LANES]

    def cmul_add(lre, lim, s, add):
        sre = s[:, :n]
        sim = s[:, n:]
        return jnp.concatenate([lre * sre - lim * sim + add[:, :n],
                                lre * sim + lim * sre + add[:, n:]], axis=1)

    for d in range(2):
        lre = lam_ref[d, 0:1, :]
        lim = lam_ref[d, 1:2, :]
        lcre = lamc_ref[d, 0:1, :]
        lcim = lamc_ref[d, 1:2, :]
        tloc = (lambda i: i) if d == 0 else (lambda i: ch - 1 - i)
        cloc = (lambda i: i) if d == 0 else (lambda i: nc - 1 - i)

        def advance(tl):
            rows = jnp.concatenate([r[0, pl.ds(tl, nc, stride=ch), :] for r in u_halves], axis=1)
            bu = jnp.dot(rows.astype(BF16), bblk_ref[d], preferred_element_type=F32)
            st_ref[...] = cmul_add(lre, lim, st_ref[...], bu)

        st_ref[...] = jnp.zeros_like(st_ref)

        def p1(i, c):
            advance(tloc(i))
            return c

        lax.fori_loop(0, ch, p1, 0)
        end_ref[...] = st_ref[...]

        def cs(i, car):
            c = cloc(i)
            carry_ref[pl.ds(c, 1), :] = car
            return cmul_add(lcre, lcim, car, end_ref[pl.ds(c, 1), :])

        lax.fori_loop(0, nc, cs, jnp.zeros((1, 2 * n), F32))

        st_ref[...] = carry_ref[...]

        def p2(i, c):
            tl = tloc(i)
            advance(tl)
            yr = jnp.dot(st_ref[...].astype(BF16), cblk_ref[d], preferred_element_type=F32)
            idx = pl.ds(tl, nc, stride=ch)
            for hf in range(2):
                y_ref[hf, idx, :] = y_ref[hf, idx, :] + yr[:, hf * LANES:(hf + 1) * LANES]
            return c

        lax.fori_loop(0, ch, p2, 0)

    g = _gelu_tanh(jnp.concatenate([y_ref[0], y_ref[1]], axis=1))
    o_ref[0] = g * jax.nn.sigmoid(jnp.dot(g.astype(BF16), glu_ref[...], preferred_element_type=F32))


def _s5_params(lam_re, lam_im, log_dt, b_re, b_im, c_re, c_im):
    lam = lax.complex(lam_re.astype(F32), lam_im.astype(F32))
    dt = jnp.exp(log_dt.astype(F32))[..., None]
    lam_bar = jnp.exp(lam * dt)
    b_bar = ((lam_bar - 1.0) / lam)[..., None] * lax.complex(b_re.astype(F32), b_im.astype(F32))
    eye_g = jnp.eye(S5_GROUPS, dtype=F32)

    def blockdiag_in(m):
        return jnp.einsum("dgph,gk->dghkp", m, eye_g).reshape(2, S5_WIDTH, S5_FLAT)

    def blockdiag_out(m):
        return jnp.einsum("dghp,gk->dgpkh", m, eye_g).reshape(2, S5_FLAT, S5_WIDTH)

    bblk = jnp.concatenate([blockdiag_in(jnp.real(b_bar)), blockdiag_in(jnp.imag(b_bar))], axis=2)
    cblk = jnp.concatenate([blockdiag_out(c_re.astype(F32)), -blockdiag_out(c_im.astype(F32))], axis=1)
    lam_c = jnp.exp(lam * dt * float(S5_CHUNK))
    flat = lambda z: jnp.stack([jnp.real(z).reshape(2, S5_FLAT), jnp.imag(z).reshape(2, S5_FLAT)], axis=1)
    return bblk.astype(BF16), cblk.astype(BF16), flat(lam_bar), flat(lam_c)


def _s5_mixer(u, bblk, cblk, lam, lamc, d_skip, glu_bf):
    bsz, seq, w = u.shape
    nc = seq // S5_CHUNK
    full = lambda a: pl.BlockSpec(a.shape, lambda b, _n=a.ndim: (0,) * _n)
    args = [bblk, cblk, lam, lamc, d_skip, glu_bf]
    return pl.pallas_call(
        functools.partial(_s5_kernel, seq=seq),
        grid=(bsz,),
        in_specs=[pl.BlockSpec((1, seq, LANES), lambda b: (b, 0, 0)),
                  pl.BlockSpec((1, seq, LANES), lambda b: (b, 0, 1))] + [full(a) for a in args],
        out_specs=pl.BlockSpec((1, seq, w), lambda b: (b, 0, 0)),
        out_shape=jax.ShapeDtypeStruct((bsz, seq, w), F32),
        scratch_shapes=[pltpu.VMEM((w // LANES, seq, LANES), F32),
                        pltpu.VMEM((nc, 2 * S5_FLAT), F32),
                        pltpu.VMEM((nc, 2 * S5_FLAT), F32),
                        pltpu.VMEM((nc, 2 * S5_FLAT), F32)],
        compiler_params=_cparams(("arbitrary",)),
        name="s5_mixer",
    )(u, u, *args)


def _na_bias_table(rpb):
    q_col = np.arange(GRID_W)
    c_start = np.clip(q_col - NA_KW // 2, 0, GRID_W - NA_KW)
    k_col = np.arange(GRID_W)
    valid = (k_col[None, :] >= c_start[:, None]) & (k_col[None, :] < c_start[:, None] + NA_KW)
    dx = np.clip(k_col[None, :] - q_col[:, None] + NA_KW - 1, 0, 2 * NA_KW - 2)
    off = np.arange(NA_KH)
    dy = np.arange(NA_KH)[None, :] - off[:, None] + NA_KH - 1
    tab = rpb.astype(F32)[:, dy[:, None, :, None], dx[None, :, None, :]]
    tab = jnp.where(jnp.asarray(valid)[None, None, :, None, :], tab, -jnp.inf)
    return tab.reshape(rpb.shape[0], NA_KH, GRID_W, NA_KH * GRID_W)


def _na_kernel(q_ref, k_ref, v_ref, bias_ref, o_ref, *, rows, rblk):
    rb = pl.program_id(1)
    lane = lax.broadcasted_iota(jnp.int32, (GRID_W, LANES), 1)
    low = lane < HEAD

    def row(j, carry):
        r = rb * rblk + j
        rs = jnp.clip(r - NA_KH // 2, 0, rows - NA_KH)
        off = r - rs
        q = q_ref[0, j]
        kmat = k_ref[0, pl.ds(rs, NA_KH)].reshape(NA_KH * GRID_W, NA_W)
        vmat = v_ref[0, pl.ds(rs, NA_KH)].reshape(NA_KH * GRID_W, NA_W)
        outs = []
        for c in range(NA_W // LANES):
            sl = slice(c * LANES, (c + 1) * LANES)
            q2 = q[:, sl].astype(F32)
            lhs = jnp.concatenate([jnp.where(low, q2, 0.0), jnp.where(low, 0.0, q2)], axis=0).astype(BF16)
            s = lax.dot_general(lhs, kmat[:, sl], (((1,), (1,)), ((), ())), preferred_element_type=F32)
            s = s + jnp.concatenate([bias_ref[2 * c, off], bias_ref[2 * c + 1, off]], axis=0)
            m = jnp.max(s, axis=-1, keepdims=True)
            p = jnp.exp(s - m)
            l = jnp.sum(p, axis=-1, keepdims=True)
            o = jnp.dot(p.astype(BF16), vmat[:, sl], preferred_element_type=F32) / l
            outs.append(jnp.where(low, o[:GRID_W], o[GRID_W:]))
        o_ref[0, j] = jnp.concatenate(outs, axis=1)
        return carry

    lax.fori_loop(0, rblk, row, 0)


def _na_mixer(q, k, v, bias):
    bsz, seq, w = q.shape
    rows = seq // GRID_W
    rblk = 8
    g4 = lambda a: a.reshape(bsz, rows, GRID_W, w)
    img = pl.BlockSpec((1, rows, GRID_W, w), lambda b, i: (b, 0, 0, 0))
    blk = pl.BlockSpec((1, rblk, GRID_W, w), lambda b, i: (b, i, 0, 0))
    out = pl.pallas_call(
        functools.partial(_na_kernel, rows=rows, rblk=rblk),
        grid=(bsz, rows // rblk),
        in_specs=[blk, img, img, pl.BlockSpec(bias.shape, lambda b, i: (0, 0, 0, 0))],
        out_specs=blk,
        out_shape=jax.ShapeDtypeStruct((bsz, rows, GRID_W, w), F32),
        compiler_params=_cparams(("arbitrary", "arbitrary")),
        name="na_mixer",
    )(g4(q), g4(k), g4(v), bias)
    return out.reshape(bsz, seq, w)


def _outproj_kernel(x_ref, s5_ref, yf_ref, yb_ref, extra_ref, gate_ref, bonus_ref, na_ref,
                    lnw_ref, lnb_ref, obm_ref, w_ref, gm_ref, g2_ref, sh_ref, sc_ref,
                    xo_ref, h_ref):
    obm = obm_ref[...]
    segm = lambda t: jnp.dot(t, obm, precision=HIGHEST, preferred_element_type=F32)
    y = yf_ref[...] + yb_ref[...] + extra_ref[...]
    yc = y - segm(y)
    yn = yc * lax.rsqrt(segm(yc * yc) + RWKV_GN_EPS) * lnw_ref[...] + lnb_ref[...]
    rw = (yn + bonus_ref[...]) * gate_ref[...]
    mixed = jnp.concatenate([s5_ref[...], rw, na_ref[...]], axis=1).astype(BF16)
    xo = x_ref[...] + gm_ref[0] * jnp.dot(mixed, w_ref[...], preferred_element_type=F32)
    xo_ref[...] = xo
    h_ref[...] = _rms_mod(xo, g2_ref[...], sh_ref[0], sc_ref[0])


def _out_proj(x2, s5o, yf, yb, extra, gate, bonus, nao, lnw, lnb, obm, w_bf, gate_mix, g2, shift, scale, seq):
    t, d = x2.shape
    tm = 256
    per_b = seq // tm
    row = lambda w: pl.BlockSpec((tm, w), lambda i: (i, 0))
    full = lambda a: pl.BlockSpec(a.shape, lambda i, _n=a.ndim: (0,) * _n)
    bvec = pl.BlockSpec((1, 1, d), lambda i: (i // per_b, 0, 0))
    o = jax.ShapeDtypeStruct((t, d), F32)
    return pl.pallas_call(
        _outproj_kernel,
        grid=(t // tm,),
        in_specs=[row(d), row(S5_WIDTH)] + [row(RW)] * 6 +
                 [full(lnw), full(lnb), full(obm), full(w_bf), bvec, full(g2), bvec, bvec],
        out_specs=[row(d), row(d)],
        out_shape=[o, o],
        compiler_params=_cparams(("arbitrary",)),
        name="out_proj",
    )(x2, s5o, yf, yb, extra, gate, bonus, nao, lnw, lnb, obm, w_bf, gate_mix, g2, shift, scale)


def _router_kernel(h_ref, w_ref, b_ref, tri_ref, e_ref, rank_ref, gate_ref, cnt_ref, carry_ref):
    @pl.when(pl.program_id(0) == 0)
    def _():
        carry_ref[...] = jnp.zeros_like(carry_ref)

    logits = jnp.dot(h_ref[...], w_ref[...], precision=HIGHEST, preferred_element_type=F32) + b_ref[...]
    tm = logits.shape[0]
    lane = lax.broadcasted_iota(jnp.int32, (tm, LANES), 1)
    vals, idxs, hots = [], [], []
    cur = logits
    for _ in range(TOP_K):
        m = jnp.max(cur, axis=-1, keepdims=True)
        idx = jnp.min(jnp.where(cur == m, lane, LANES), axis=-1, keepdims=True)
        hot = lane == idx
        vals.append(m)
        idxs.append(idx)
        hots.append(hot)
        cur = jnp.where(hot, -jnp.inf, cur)
    exps = [jnp.exp(v - vals[0]) for v in vals]
    den = exps[0] + exps[1] + exps[2] + exps[3]
    assign = sum(h.astype(F32) for h in hots)
    before = jnp.dot(tri_ref[...], assign.astype(BF16), preferred_element_type=F32) + carry_ref[...]
    e_out = jnp.zeros((tm, LANES), jnp.int32)
    r_out = jnp.zeros((tm, LANES), jnp.int32)
    g_out = jnp.zeros((tm, LANES), F32)
    for kk in range(TOP_K):
        rank = jnp.sum(jnp.where(hots[kk], before, 0.0), axis=-1, keepdims=True)
        sel = lane == kk
        e_out = jnp.where(sel, idxs[kk], e_out)
        r_out = jnp.where(sel, rank.astype(jnp.int32), r_out)
        g_out = jnp.where(sel, exps[kk] / den, g_out)
    e_ref[...] = e_out
    rank_ref[...] = r_out
    gate_ref[...] = g_out
    total = carry_ref[...] + jnp.sum(assign, axis=0, keepdims=True)
    carry_ref[...] = total
    cnt_ref[...] = total


def _router(h2, rw_pad, rb_pad):
    t, d = h2.shape
    tm = 256
    tri = jnp.asarray(np.tril(np.ones((tm, tm), np.float32), -1), BF16)
    row = pl.BlockSpec((tm, LANES), lambda i: (i, 0))
    full = lambda a: pl.BlockSpec(a.shape, lambda i, _n=a.ndim: (0,) * _n)
    return pl.pallas_call(
        _router_kernel,
        grid=(t // tm,),
        in_specs=[pl.BlockSpec((tm, d), lambda i: (i, 0)), full(rw_pad), full(rb_pad), full(tri)],
        out_specs=[row, row, row, pl.BlockSpec((1, LANES), lambda i: (0, 0))],
        out_shape=[jax.ShapeDtypeStruct((t, LANES), jnp.int32),
                   jax.ShapeDtypeStruct((t, LANES), jnp.int32),
                   jax.ShapeDtypeStruct((t, LANES), F32),
                   jax.ShapeDtypeStruct((1, LANES), F32)],
        scratch_shapes=[pltpu.VMEM((1, LANES), F32)],
        compiler_params=_cparams(("arbitrary",)),
        name="moe_router",
    )(h2, rw_pad, rb_pad, tri)


DISPATCH_TOKENS = 256


def _row_copy(src, dst, sem):
    return pltpu.make_async_copy(src, dst, sem)


def _dispatch_kernel(dest_ref, h_ref, xb_in_ref, xb_ref, sem):
    del xb_in_ref
    n = DISPATCH_TOKENS * TOP_K

    def issue(a, c):
        _row_copy(h_ref.at[pl.ds(a // TOP_K, 1), :], xb_ref.at[pl.ds(dest_ref[a], 1), :], sem).start()
        return c

    lax.fori_loop(0, n, issue, 0)

    def drain(a, c):
        _row_copy(h_ref.at[pl.ds(0, 1), :], xb_ref.at[pl.ds(0, 1), :], sem).wait()
        return c

    lax.fori_loop(0, n, drain, 0)


def _dispatch(dest_flat, h2, n_rows):
    t, d = h2.shape
    tm = DISPATCH_TOKENS
    zeros = jnp.zeros((n_rows, d), F32)
    return pl.pallas_call(
        _dispatch_kernel,
        grid=(t // tm,),
        in_specs=[pl.BlockSpec((tm * TOP_K,), lambda i: (i,), memory_space=pltpu.SMEM),
                  pl.BlockSpec((tm, d), lambda i: (i, 0)),
                  pl.BlockSpec(memory_space=pl.ANY)],
        out_specs=pl.BlockSpec(memory_space=pl.ANY),
        out_shape=jax.ShapeDtypeStruct((n_rows, d), F32),
        scratch_shapes=[pltpu.SemaphoreType.DMA(())],
        input_output_aliases={2: 0},
        compiler_params=_cparams(("arbitrary",)),
        name="moe_dispatch",
    )(dest_flat, h2, zeros)


def _expert_kernel(be_ref, x_ref, w1_ref, b1_ref, w2_ref, b2_ref, y_ref):
    del be_ref
    dff = w2_ref.shape[1]
    hdn = jnp.dot(x_ref[...].astype(BF16), w1_ref[0], preferred_element_type=F32) + b1_ref[0]
    glu = jnp.minimum(hdn[:, :dff], SWIGLU_LIMIT)
    lin = jnp.clip(hdn[:, dff:], -SWIGLU_LIMIT, SWIGLU_LIMIT)
    act = glu * jax.nn.sigmoid(SWIGLU_ALPHA * glu) * (lin + 1.0)
    y_ref[...] = jnp.dot(act.astype(BF16), w2_ref[0], preferred_element_type=F32) + b2_ref[0]


def _experts(block_e, xb, w1_bf, b1_de, w2_bf, b2):
    n_rows, d = xb.shape
    ne, _, f2 = w1_bf.shape
    dff = w2_bf.shape[1]
    nblk = n_rows // MOE_BLOCK
    grid_spec = pltpu.PrefetchScalarGridSpec(
        num_scalar_prefetch=1,
        grid=(nblk,),
        in_specs=[pl.BlockSpec((MOE_BLOCK, d), lambda i, be: (i, 0)),
                  pl.BlockSpec((1, d, f2), lambda i, be: (be[i], 0, 0)),
                  pl.BlockSpec((1, 1, f2), lambda i, be: (be[i], 0, 0)),
                  pl.BlockSpec((1, dff, d), lambda i, be: (be[i], 0, 0)),
                  pl.BlockSpec((1, 1, d), lambda i, be: (be[i], 0, 0))],
        out_specs=pl.BlockSpec((MOE_BLOCK, d), lambda i, be: (i, 0)),
    )
    return pl.pallas_call(
        _expert_kernel,
        grid_spec=grid_spec,
        out_shape=jax.ShapeDtypeStruct((n_rows, d), F32),
        compiler_params=_cparams(("arbitrary",)),
        name="moe_experts",
    )(block_e, xb, w1_bf, b1_de.reshape(ne, 1, f2), w2_bf, b2.reshape(ne, 1, d))


COMBINE_TOKENS = 256


def _combine_kernel(dest_ref, gates_ref, x_ref, gf_ref, yb_ref, o_ref, buf_ref, sem):
    n = COMBINE_TOKENS * TOP_K

    def issue(a, c):
        _row_copy(yb_ref.at[pl.ds(dest_ref[a], 1), :],
                  buf_ref.at[a % TOP_K, pl.ds(a // TOP_K, 1), :], sem).start()
        return c

    lax.fori_loop(0, n, issue, 0)

    def drain(a, c):
        _row_copy(yb_ref.at[pl.ds(0, 1), :], buf_ref.at[0, pl.ds(0, 1), :], sem).wait()
        return c

    lax.fori_loop(0, n, drain, 0)
    gates = gates_ref[...]
    acc = gates[:, 0:1] * buf_ref[0]
    for kk in range(1, TOP_K):
        acc = acc + gates[:, kk:kk + 1] * buf_ref[kk]
    o_ref[...] = x_ref[...] + gf_ref[0] * acc


def _combine(dest_flat, gates, x2, gate_ffn, yb, seq):
    t, d = x2.shape
    tm = COMBINE_TOKENS
    per_b = seq // tm
    return pl.pallas_call(
        _combine_kernel,
        grid=(t // tm,),
        in_specs=[pl.BlockSpec((tm * TOP_K,), lambda i: (i,), memory_space=pltpu.SMEM),
                  pl.BlockSpec((tm, LANES), lambda i: (i, 0)),
                  pl.BlockSpec((tm, d), lambda i: (i, 0)),
                  pl.BlockSpec((1, 1, d), lambda i: (i // per_b, 0, 0)),
                  pl.BlockSpec(memory_space=pl.ANY)],
        out_specs=pl.BlockSpec((tm, d), lambda i: (i, 0)),
        out_shape=jax.ShapeDtypeStruct((t, d), F32),
        scratch_shapes=[pltpu.VMEM((TOP_K, tm, d), F32), pltpu.SemaphoreType.DMA(())],
        compiler_params=_cparams(("arbitrary",)),
        name="moe_combine",
    )(dest_flat, gates, x2, gate_ffn, yb)


def _moe_layer(x2, h2, gate_ffn, router_w, router_b, w1_bf, b1_de, w2_bf, b2, seq):
    t, d = x2.shape
    ne = router_w.shape[1]
    rw_pad = jnp.zeros((d, LANES), F32).at[:, :ne].set(router_w.astype(F32))
    rb_pad = jnp.full((1, LANES), -jnp.inf, F32).at[0, :ne].set(router_b.astype(F32))
    eidx, rank, gates, counts = _router(h2, rw_pad, rb_pad)
    n_assign = t * TOP_K
    n_blocks = -(-n_assign // MOE_BLOCK) + ne
    cnt = counts[0, :ne].astype(jnp.int32)
    padded = ((cnt + MOE_BLOCK - 1) // MOE_BLOCK) * MOE_BLOCK
    pad_end = jnp.cumsum(padded)
    pad_start = pad_end - padded
    dest = (pad_start[eidx[:, :TOP_K]] + rank[:, :TOP_K]).reshape(-1).astype(jnp.int32)
    block_e = jnp.minimum(jnp.searchsorted(pad_end, jnp.arange(n_blocks, dtype=jnp.int32) * MOE_BLOCK, side="right"),
                          ne - 1).astype(jnp.int32)
    xb = _dispatch(dest, h2, n_blocks * MOE_BLOCK)
    yb = _experts(block_e, xb, w1_bf, b1_de, w2_bf, b2)
    return _combine(dest, gates, x2, gate_ffn, yb, seq)


def kernel(x, c, ada_w, ada_b, norm1_g, norm2_g, w_in, w_out, s5_lam_re, s5_lam_im, s5_log_dt, s5_b_re, s5_b_im, s5_c_re, s5_c_im, s5_d, s5_glu_w, rwkv_mu, rwkv_w0, rwkv_w1, rwkv_w2, rwkv_a0, rwkv_a1, rwkv_a2, rwkv_g1, rwkv_g2, rwkv_k_k, rwkv_k_a, rwkv_r_k, rwkv_ln_w, rwkv_ln_b, na_q_g, na_k_g, na_rpb, router_w, router_b, exp_w1, exp_b1, exp_w2, exp_b2):
    bsz, seq, d = x.shape
    depth = ada_w.shape[0]
    t = bsz * seq
    mod = _ada_mod(c, ada_w, ada_b).reshape(depth, bsz, 6, 1, d)
    x2 = x.reshape(t, d)
    ob_f32 = _seg_ones(RW)
    obm_f32 = _seg_ones(RW, scale=1.0 / HEAD)
    ob256 = _seg_ones(256, dtype=BF16)
    ob128 = _seg_ones(128, dtype=BF16)
    eye_t = jnp.asarray(np.tile(np.eye(HEAD, dtype=np.float32), (1, RW // HEAD)))
    row = lambda a: a.reshape(1, -1).astype(F32)
    for l in range(depth):
        m = lambda j: mod[l, :, j]
        s5u, xr, qkv = _in_proj(x2, row(norm1_g[l]), m(0), m(1), w_in[l].astype(BF16), seq)
        prep_params = dict(
            mu=row(rwkv_mu[l]), k_k=row(rwkv_k_k[l]), k_a=row(rwkv_k_a[l]), r_k=row(rwkv_r_k[l]),
            w0=rwkv_w0[l].astype(F32), a0=rwkv_a0[l].astype(F32),
            w1=rwkv_w1[l].astype(BF16), w2=rwkv_w2[l].astype(BF16),
            a1=rwkv_a1[l].astype(BF16), a2=rwkv_a2[l].astype(BF16),
            g1=rwkv_g1[l].astype(BF16), g2=rwkv_g2[l].astype(BF16),
            q_g=row(jnp.tile(na_q_g[l], NA_W // HEAD)), k_g=row(jnp.tile(na_k_g[l], NA_W // HEAD)), ob=ob_f32)
        (nkk, v, dec0, b0, k0, q0, dec1, b1, k1, q1, extra, gate, bonus, naq, nak, nav) = _prep(
            xr.reshape(bsz, seq, 4 * RW), qkv.reshape(bsz, seq, 3 * NA_W), prep_params)
        yf, yb = _wkv_scan(dict(nkk=nkk, v=v, dec0=dec0, b0=b0, k0=k0, q0=q0, dec1=dec1, b1=b1, k1=k1, q1=q1),
                           eye_t, ob256, ob128)
        bblk, cblk, lam, lamc = _s5_params(s5_lam_re[l], s5_lam_im[l], s5_log_dt[l], s5_b_re[l], s5_b_im[l],
                                           s5_c_re[l], s5_c_im[l])
        s5o = _s5_mixer(s5u.reshape(bsz, seq, S5_WIDTH), bblk, cblk, lam, lamc, row(s5_d[l]),
                        s5_glu_w[l].astype(BF16))
        nao = _na_mixer(naq, nak, nav, _na_bias_table(na_rpb[l]))
        flat = lambda a: a.reshape(t, -1)
        x2, h2 = _out_proj(x2, flat(s5o), flat(yf), flat(yb), flat(extra), flat(gate), flat(bonus), flat(nao),
                           row(rwkv_ln_w[l]), row(rwkv_ln_b[l]), obm_f32, w_out[l].astype(BF16),
                           m(2), row(norm2_g[l]), m(3), m(4), seq)
        w1 = exp_w1[l]
        w1_bf = jnp.concatenate([w1[:, :, 0::2], w1[:, :, 1::2]], axis=-1).astype(BF16)
        b1 = exp_b1[l].astype(F32)
        b1_de = jnp.concatenate([b1[:, 0::2], b1[:, 1::2]], axis=-1)
        x2 = _moe_layer(x2, h2, m(5), router_w[l], router_b[l], w1_bf, b1_de, exp_w2[l].astype(BF16),
                        exp_b2[l].astype(F32), seq)
    return x2.reshape(bsz, seq, d)
```

```python
import functools
import math

import numpy as np
import jax
import jax.numpy as jnp
from jax import lax
from jax.experimental import pallas as pl
from jax.experimental.pallas import tpu as pltpu

F32 = jnp.float32
BF16 = jnp.bfloat16
HIGHEST = lax.Precision.HIGHEST

D_MODEL = 1024
S5_WIDTH = 256
S5_GROUP = 16
S5_GROUPS = 16
S5_STATE = 64
S5_CHUNK = 64
S5_FLAT = S5_GROUPS * S5_STATE
RW = 384
HEAD = 64
RWKV_GN_EPS = 64e-5
NA_W = 384
GRID_W = 64
NA_KH = 8
NA_KW = 16
N_EXPERTS = 32
TOP_K = 4
MOE_BLOCK = 256
SWIGLU_ALPHA = 1.702
SWIGLU_LIMIT = 7.0
RMS_EPS = 1e-6
LANES = 128
WKV_CHUNK = 64
VMEM_LIMIT = 56 * 1024 * 1024


def _cparams(sem):
    return pltpu.CompilerParams(dimension_semantics=sem, vmem_limit_bytes=VMEM_LIMIT)


def _seg_ones(n, seg=HEAD, dtype=F32, scale=1.0):
    idx = np.arange(n) // seg
    return jnp.asarray((idx[:, None] == idx[None, :]).astype(np.float32) * scale, dtype)


def _ada_kernel(c_ref, w_ref, b_ref, o_ref):
    c = c_ref[...]
    cond = c * jax.nn.sigmoid(c)
    o_ref[0] = jnp.dot(cond, w_ref[0], preferred_element_type=F32) + b_ref[0]


def _ada_mod(c, ada_w, ada_b):
    nl, d, n6 = ada_w.shape
    bsz = c.shape[0]
    tn = 1536
    return pl.pallas_call(
        _ada_kernel,
        grid=(nl, n6 // tn),
        in_specs=[pl.BlockSpec((bsz, d), lambda l, j: (0, 0)),
                  pl.BlockSpec((1, d, tn), lambda l, j: (l, 0, j)),
                  pl.BlockSpec((1, 1, tn), lambda l, j: (l, 0, j))],
        out_specs=pl.BlockSpec((1, bsz, tn), lambda l, j: (l, 0, j)),
        out_shape=jax.ShapeDtypeStruct((nl, bsz, n6), F32),
        compiler_params=_cparams(("arbitrary", "arbitrary")),
        name="ada_mod",
    )(c, ada_w, ada_b.reshape(nl, 1, n6))


def _rms_mod(x, g, shift, scale):
    ms = jnp.mean(x * x, axis=-1, keepdims=True)
    h = x * lax.rsqrt(ms + RMS_EPS) * g
    return h * (1.0 + scale) + shift


def _proj_kernel(x_ref, g_ref, sh_ref, sc_ref, w_ref, o_s5, o_rw, o_na):
    h = _rms_mod(x_ref[...], g_ref[...], sh_ref[0], sc_ref[0])
    p = jnp.dot(h.astype(BF16), w_ref[...], preferred_element_type=F32)
    o_s5[...] = p[:, :S5_WIDTH]
    o_rw[...] = p[:, S5_WIDTH:S5_WIDTH + 4 * RW]
    o_na[...] = p[:, S5_WIDTH + 4 * RW:]


def _in_proj(x2, g, shift, scale, w_bf, seq):
    t, d = x2.shape
    n = w_bf.shape[1]
    tm = 256
    per_b = seq // tm
    row = lambda i: (i, 0)
    bvec = lambda i: (i // per_b, 0, 0)
    return pl.pallas_call(
        _proj_kernel,
        grid=(t // tm,),
        in_specs=[pl.BlockSpec((tm, d), row),
                  pl.BlockSpec((1, d), lambda i: (0, 0)),
                  pl.BlockSpec((1, 1, d), bvec),
                  pl.BlockSpec((1, 1, d), bvec),
                  pl.BlockSpec((d, n), lambda i: (0, 0))],
        out_specs=[pl.BlockSpec((tm, S5_WIDTH), row),
                   pl.BlockSpec((tm, 4 * RW), row),
                   pl.BlockSpec((tm, 3 * NA_W), row)],
        out_shape=[jax.ShapeDtypeStruct((t, S5_WIDTH), F32),
                   jax.ShapeDtypeStruct((t, 4 * RW), F32),
                   jax.ShapeDtypeStruct((t, 3 * NA_W), F32)],
        compiler_params=_cparams(("arbitrary",)),
        name="in_proj",
    )(x2, g, shift, scale, w_bf)


def _softplus(x):
    return jnp.maximum(x, 0.0) + jnp.log(1.0 + jnp.exp(-jnp.abs(x)))


def _prep_kernel(xr_ref, prev_ref, next_ref, qkv_ref,
                 mu_ref, kk_ref, ka_ref, rk_ref, w0_ref, a0_ref,
                 w1_ref, w2_ref, a1_ref, a2_ref, g1_ref, g2_ref, qg_ref, kg_ref, ob_ref,
                 nkk_o, v_o, dec0_o, b0_o, k0_o, q0_o, dec1_o, b1_o, k1_o, q1_o,
                 extra_o, gate_o, bonus_o, naq_o, nak_o, nav_o):
    i = pl.program_id(1)
    nblk = pl.num_programs(1)
    x = xr_ref[0]
    tm = x.shape[0]
    prow = jnp.where(i == 0, 0.0, prev_ref[0][7:8, :])
    nrow = jnp.where(i == nblk - 1, 0.0, next_ref[0][0:1, :])
    rid = lax.broadcasted_iota(jnp.int32, x.shape, 0)
    prev = jnp.where(rid == 0, prow, pltpu.roll(x, 1, axis=0))
    nxt = jnp.where(rid == tm - 1, nrow, pltpu.roll(x, tm - 1, axis=0))
    xs = x + (0.5 * (prev + nxt) - x) * mu_ref[...]
    r = xs[:, 0:RW]
    k = xs[:, RW:2 * RW]
    v = xs[:, 2 * RW:3 * RW]
    z = xs[:, 3 * RW:4 * RW]
    ob = ob_ref[...]
    seg = lambda t: jnp.dot(t, ob, precision=HIGHEST, preferred_element_type=F32)
    zb = z.astype(BF16)
    bdot = lambda a, w: jnp.dot(a.astype(BF16), w, preferred_element_type=F32)
    gate_o[0] = bdot(jax.nn.sigmoid(bdot(zb, g1_ref[...])), g2_ref[...])
    kk = k * kk_ref[...]
    kk = kk / jnp.maximum(jnp.sqrt(seg(kk * kk)), 1e-12)
    nkk_o[0] = -kk
    v_o[0] = v
    bonus_o[0] = seg(r * k * rk_ref[...]) * v
    extra = jnp.zeros_like(v)
    outs = ((dec0_o, b0_o, k0_o, q0_o), (dec1_o, b1_o, k1_o, q1_o))
    for d in range(2):
        wl = w0_ref[d:d + 1, :] + bdot(jnp.tanh(bdot(zb, w1_ref[d])), w2_ref[d])
        w = -_softplus(-wl) - 0.5
        dec = jnp.exp(-jnp.exp(w))
        a = jax.nn.sigmoid(a0_ref[d:d + 1, :] + bdot(bdot(zb, a1_ref[d]), a2_ref[d]))
        kd = k * (1.0 + (a - 1.0) * ka_ref[...])
        bv = kk * a
        dec_o, b_o, k_o, q_o = outs[d]
        dec_o[0] = dec
        b_o[0] = bv
        k_o[0] = kd
        q_o[0] = dec * r - kk * seg(bv * r)
        extra = extra + v * seg(kd * r)
    extra_o[0] = extra
    qkv = qkv_ref[0]
    segm = lambda t: seg(t) * (1.0 / HEAD)
    qn = qkv[:, 0:NA_W]
    kn = qkv[:, NA_W:2 * NA_W]
    naq_o[0] = (qn * lax.rsqrt(segm(qn * qn) + RMS_EPS) * qg_ref[...] * (HEAD ** -0.5)).astype(BF16)
    nak_o[0] = (kn * lax.rsqrt(segm(kn * kn) + RMS_EPS) * kg_ref[...]).astype(BF16)
    nav_o[0] = qkv[:, 2 * NA_W:].astype(BF16)


def _prep(xr, qkv, p):
    bsz, seq, _ = xr.shape
    tm = 256
    nb = seq // tm
    h8 = tm // 8
    blk = lambda w: pl.BlockSpec((1, tm, w), lambda b, i: (b, i, 0))
    full = lambda a: pl.BlockSpec(a.shape, lambda b, i, _n=a.ndim: (0,) * _n)
    params = [p["mu"], p["k_k"], p["k_a"], p["r_k"], p["w0"], p["a0"], p["w1"], p["w2"], p["a1"], p["a2"],
              p["g1"], p["g2"], p["q_g"], p["k_g"], p["ob"]]
    f32o = jax.ShapeDtypeStruct((bsz, seq, RW), F32)
    bfo = jax.ShapeDtypeStruct((bsz, seq, NA_W), BF16)
    return pl.pallas_call(
        _prep_kernel,
        grid=(bsz, nb),
        in_specs=[blk(4 * RW),
                  pl.BlockSpec((1, 8, 4 * RW), lambda b, i: (b, jnp.maximum(i * h8 - 1, 0), 0)),
                  pl.BlockSpec((1, 8, 4 * RW), lambda b, i: (b, jnp.minimum((i + 1) * h8, seq // 8 - 1), 0)),
                  blk(3 * NA_W)] + [full(a) for a in params],
        out_specs=[blk(RW)] * 16,
        out_shape=[f32o] * 13 + [bfo] * 3,
        compiler_params=_cparams(("arbitrary", "arbitrary")),
        name="mixer_prep",
    )(xr, xr, xr, qkv, *params)


def _wkv_kernel(*refs, bsz, tc):
    f_in = refs[0:6]
    b_in = refs[6:12]
    e_ref, ob256_ref, ob128_ref = refs[12:15]
    yf_ref, yb_ref = refs[15:17]
    s_ref = refs[17]

    @pl.when(pl.program_id(0) == 0)
    def _():
        s_ref[...] = jnp.zeros_like(s_ref)

    eye = e_ref[...]
    ob256 = ob256_ref[...]
    ob128 = ob128_ref[...]

    def seg(t):
        return jnp.concatenate(
            [jnp.dot(t[:, :256], ob256, preferred_element_type=F32),
             jnp.dot(t[:, 256:], ob128, preferred_element_type=F32)], axis=1)

    def step(i, carry):
        for b in range(bsz):
            for d in range(2):
                src = f_in if d == 0 else b_in
                out = yf_ref if d == 0 else yb_ref
                t = i if d == 0 else tc - 1 - i
                a, v, w, bb, kd, q = [s[b, pl.ds(t, 1), :] for s in src]
                st = s_ref[2 * b + d]
                lhs = jnp.concatenate([st * a, st * q, eye * v], axis=0).astype(BF16)
                res = seg(lhs)
                sa = res[0:HEAD]
                yq = res[HEAD:2 * HEAD]
                vb = res[2 * HEAD:3 * HEAD]
                s_ref[2 * b + d] = st * w + sa * bb + vb * kd
                out[b, pl.ds(t, 1), :] = jnp.sum(yq * eye, axis=0, keepdims=True)
        return carry

    lax.fori_loop(0, tc, step, 0)


def _wkv_scan(ins, eye, ob256, ob128):
    bsz, seq, _ = ins["nkk"].shape
    tc = WKV_CHUNK
    nc = seq // tc
    fwd = pl.BlockSpec((bsz, tc, RW), lambda c: (0, c, 0))
    bwd = pl.BlockSpec((bsz, tc, RW), lambda c: (0, nc - 1 - c, 0))
    full = lambda a: pl.BlockSpec(a.shape, lambda c, _n=a.ndim: (0,) * _n)
    f_args = [ins["nkk"], ins["v"], ins["dec0"], ins["b0"], ins["k0"], ins["q0"]]
    b_args = [ins["nkk"], ins["v"], ins["dec1"], ins["b1"], ins["k1"], ins["q1"]]
    o = jax.ShapeDtypeStruct((bsz, seq, RW), F32)
    return pl.pallas_call(
        functools.partial(_wkv_kernel, bsz=bsz, tc=tc),
        grid=(nc,),
        in_specs=[fwd] * 6 + [bwd] * 6 + [full(eye), full(ob256), full(ob128)],
        out_specs=[fwd, bwd],
        out_shape=[o, o],
        scratch_shapes=[pltpu.VMEM((2 * bsz, HEAD, RW), F32)],
        compiler_params=_cparams(("arbitrary",)),
        name="wkv_scan",
    )(*f_args, *b_args, eye, ob256, ob128)


def _gelu_tanh(x):
    return 0.5 * x * (1.0 + jnp.tanh(math.sqrt(2.0 / math.pi) * (x + 0.044715 * (x * x * x))))


def _s5_kernel(ua_ref, ub_ref, bblk_ref, cblk_ref, lam_ref, lamc_ref, d_ref, glu_ref, o_ref,
               y_ref, st_ref, end_ref, carry_ref, *, seq):
    ch = S5_CHUNK
    nc = seq // ch
    n = S5_FLAT
    u_halves = (ua_ref, ub_ref)
    for hf in range(2):
        y_ref[hf] = u_halves[hf][0] * d_ref[:, hf * LANES:(hf + 1) * LANES]

    def cmul_add(lre, lim, s, add):
        sre = s[:, :n]
        sim = s[:, n:]
        return jnp.concatenate([lre * sre - lim * sim + add[:, :n],
                                lre * sim + lim * sre + add[:, n:]], axis=1)

    for d in range(2):
        lre = lam_ref[d, 0:1, :]
        lim = lam_ref[d, 1:2, :]
        lcre = lamc_ref[d, 0:1, :]
        lcim = lamc_ref[d, 1:2, :]
        tloc = (lambda i: i) if d == 0 else (lambda i: ch - 1 - i)
        cloc = (lambda i: i) if d == 0 else (lambda i: nc - 1 - i)

        def advance(tl):
            rows = jnp.concatenate([r[0, pl.ds(tl, nc, stride=ch), :] for r in u_halves], axis=1)
            bu = jnp.dot(rows.astype(BF16), bblk_ref[d], preferred_element_type=F32)
            st_ref[...] = cmul_add(lre, lim, st_ref[...], bu)

        st_ref[...] = jnp.zeros_like(st_ref)

        def p1(i, c):
            advance(tloc(i))
            return c

        lax.fori_loop(0, ch, p1, 0)
        end_ref[...] = st_ref[...]

        def cs(i, car):
            c = cloc(i)
            carry_ref[pl.ds(c, 1), :] = car
            return cmul_add(lcre, lcim, car, end_ref[pl.ds(c, 1), :])

        lax.fori_loop(0, nc, cs, jnp.zeros((1, 2 * n), F32))

        st_ref[...] = carry_ref[...]

        def p2(i, c):
            tl = tloc(i)
            advance(tl)
            yr = jnp.dot(st_ref[...].astype(BF16), cblk_ref[d], preferred_element_type=F32)
            idx = pl.ds(tl, nc, stride=ch)
            for hf in range(2):
                y_ref[hf, idx, :] = y_ref[hf, idx, :] + yr[:, hf * LANES:(hf + 1) * LANES]
            return c

        lax.fori_loop(0, ch, p2, 0)

    g = _gelu_tanh(jnp.concatenate([y_ref[0], y_ref[1]], axis=1))
    o_ref[0] = g * jax.nn.sigmoid(jnp.dot(g.astype(BF16), glu_ref[...], preferred_element_type=F32))


def _s5_params(lam_re, lam_im, log_dt, b_re, b_im, c_re, c_im):
    lre = lam_re.astype(F32)
    lim = lam_im.astype(F32)
    dt = jnp.exp(log_dt.astype(F32))[..., None]

    def cexp(scale):
        mag = jnp.exp(lre * dt * scale)
        return mag * jnp.cos(lim * dt * scale), mag * jnp.sin(lim * dt * scale)

    bar_re, bar_im = cexp(1.0)
    den = lre * lre + lim * lim
    f_re = ((bar_re - 1.0) * lre + bar_im * lim) / den
    f_im = (bar_im * lre - (bar_re - 1.0) * lim) / den
    bm_re = b_re.astype(F32)
    bm_im = b_im.astype(F32)
    bb_re = f_re[..., None] * bm_re - f_im[..., None] * bm_im
    bb_im = f_re[..., None] * bm_im + f_im[..., None] * bm_re
    eye_g = jnp.eye(S5_GROUPS, dtype=F32)

    def blockdiag_in(m):
        return jnp.einsum("dgph,gk->dghkp", m, eye_g).reshape(2, S5_WIDTH, S5_FLAT)

    def blockdiag_out(m):
        return jnp.einsum("dghp,gk->dgpkh", m, eye_g).reshape(2, S5_FLAT, S5_WIDTH)

    bblk = jnp.concatenate([blockdiag_in(bb_re), blockdiag_in(bb_im)], axis=2)
    cblk = jnp.concatenate([blockdiag_out(c_re.astype(F32)), -blockdiag_out(c_im.astype(F32))], axis=1)
    flat = lambda z: jnp.stack([z[0].reshape(2, S5_FLAT), z[1].reshape(2, S5_FLAT)], axis=1)
    return bblk.astype(BF16), cblk.astype(BF16), flat((bar_re, bar_im)), flat(cexp(float(S5_CHUNK)))


def _s5_mixer(u, bblk, cblk, lam, lamc, d_skip, glu_bf):
    bsz, seq, w = u.shape
    nc = seq // S5_CHUNK
    full = lambda a: pl.BlockSpec(a.shape, lambda b, _n=a.ndim: (0,) * _n)
    args = [bblk, cblk, lam, lamc, d_skip, glu_bf]
    return pl.pallas_call(
        functools.partial(_s5_kernel, seq=seq),
        grid=(bsz,),
        in_specs=[pl.BlockSpec((1, seq, LANES), lambda b: (b, 0, 0)),
                  pl.BlockSpec((1, seq, LANES), lambda b: (b, 0, 1))] + [full(a) for a in args],
        out_specs=pl.BlockSpec((1, seq, w), lambda b: (b, 0, 0)),
        out_shape=jax.ShapeDtypeStruct((bsz, seq, w), F32),
        scratch_shapes=[pltpu.VMEM((w // LANES, seq, LANES), F32),
                        pltpu.VMEM((nc, 2 * S5_FLAT), F32),
                        pltpu.VMEM((nc, 2 * S5_FLAT), F32),
                        pltpu.VMEM((nc, 2 * S5_FLAT), F32)],
        compiler_params=_cparams(("arbitrary",)),
        name="s5_mixer",
    )(u, u, *args)


def _na_bias_table(rpb):
    q_col = np.arange(GRID_W)
    c_start = np.clip(q_col - NA_KW // 2, 0, GRID_W - NA_KW)
    k_col = np.arange(GRID_W)
    valid = (k_col[None, :] >= c_start[:, None]) & (k_col[None, :] < c_start[:, None] + NA_KW)
    dx = np.clip(k_col[None, :] - q_col[:, None] + NA_KW - 1, 0, 2 * NA_KW - 2)
    pick = (np.arange(2 * NA_KW - 1)[:, None, None] == dx[None]).astype(np.float32)
    base = jnp.einsum("hyd,dqk->hyqk", rpb.astype(F32), jnp.asarray(pick), precision=HIGHEST)
    base = jnp.where(jnp.asarray(valid)[None, None], base, -jnp.inf)
    tab = jnp.stack([base[:, NA_KH - 1 - o:2 * NA_KH - 1 - o] for o in range(NA_KH)], axis=1)
    tab = jnp.transpose(tab, (0, 1, 3, 2, 4))
    return tab.reshape(rpb.shape[0], NA_KH, GRID_W, NA_KH * GRID_W)


def _na_kernel(q_ref, k_ref, v_ref, bias_ref, o_ref, *, rows, rblk):
    rb = pl.program_id(1)
    lane = lax.broadcasted_iota(jnp.int32, (GRID_W, LANES), 1)
    low = lane < HEAD

    def row(j, carry):
        r = rb * rblk + j
        rs = jnp.clip(r - NA_KH // 2, 0, rows - NA_KH)
        off = r - rs
        q = q_ref[0, j]
        kmat = k_ref[0, pl.ds(rs, NA_KH)].reshape(NA_KH * GRID_W, NA_W)
        vmat = v_ref[0, pl.ds(rs, NA_KH)].reshape(NA_KH * GRID_W, NA_W)
        outs = []
        for c in range(NA_W // LANES):
            sl = slice(c * LANES, (c + 1) * LANES)
            q2 = q[:, sl].astype(F32)
            lhs = jnp.concatenate([jnp.where(low, q2, 0.0), jnp.where(low, 0.0, q2)], axis=0).astype(BF16)
            s = lax.dot_general(lhs, kmat[:, sl], (((1,), (1,)), ((), ())), preferred_element_type=F32)
            s = s + jnp.concatenate([bias_ref[2 * c, off], bias_ref[2 * c + 1, off]], axis=0)
            m = jnp.max(s, axis=-1, keepdims=True)
            p = jnp.exp(s - m)
            l = jnp.sum(p, axis=-1, keepdims=True)
            o = jnp.dot(p.astype(BF16), vmat[:, sl], preferred_element_type=F32) / l
            outs.append(jnp.where(low, o[:GRID_W], o[GRID_W:]))
        o_ref[0, j] = jnp.concatenate(outs, axis=1)
        return carry

    lax.fori_loop(0, rblk, row, 0)


def _na_mixer(q, k, v, bias):
    bsz, seq, w = q.shape
    rows = seq // GRID_W
    rblk = 8
    g4 = lambda a: a.reshape(bsz, rows, GRID_W, w)
    img = pl.BlockSpec((1, rows, GRID_W, w), lambda b, i: (b, 0, 0, 0))
    blk = pl.BlockSpec((1, rblk, GRID_W, w), lambda b, i: (b, i, 0, 0))
    out = pl.pallas_call(
        functools.partial(_na_kernel, rows=rows, rblk=rblk),
        grid=(bsz, rows // rblk),
        in_specs=[blk, img, img, pl.BlockSpec(bias.shape, lambda b, i: (0, 0, 0, 0))],
        out_specs=blk,
        out_shape=jax.ShapeDtypeStruct((bsz, rows, GRID_W, w), F32),
        compiler_params=_cparams(("arbitrary", "arbitrary")),
        name="na_mixer",
    )(g4(q), g4(k), g4(v), bias)
    return out.reshape(bsz, seq, w)


def _outproj_kernel(x_ref, s5_ref, yf_ref, yb_ref, extra_ref, gate_ref, bonus_ref, na_ref,
                    lnw_ref, lnb_ref, obm_ref, w_ref, gm_ref, g2_ref, sh_ref, sc_ref,
                    xo_ref, h_ref):
    obm = obm_ref[...]
    segm = lambda t: jnp.dot(t, obm, precision=HIGHEST, preferred_element_type=F32)
    y = yf_ref[...] + yb_ref[...] + extra_ref[...]
    yc = y - segm(y)
    yn = yc * lax.rsqrt(segm(yc * yc) + RWKV_GN_EPS) * lnw_ref[...] + lnb_ref[...]
    rw = (yn + bonus_ref[...]) * gate_ref[...]
    mixed = jnp.concatenate([s5_ref[...], rw, na_ref[...]], axis=1).astype(BF16)
    xo = x_ref[...] + gm_ref[0] * jnp.dot(mixed, w_ref[...], preferred_element_type=F32)
    xo_ref[...] = xo
    h_ref[...] = _rms_mod(xo, g2_ref[...], sh_ref[0], sc_ref[0])


def _out_proj(x2, s5o, yf, yb, extra, gate, bonus, nao, lnw, lnb, obm, w_bf, gate_mix, g2, shift, scale, seq):
    t, d = x2.shape
    tm = 256
    per_b = seq // tm
    row = lambda w: pl.BlockSpec((tm, w), lambda i: (i, 0))
    full = lambda a: pl.BlockSpec(a.shape, lambda i, _n=a.ndim: (0,) * _n)
    bvec = pl.BlockSpec((1, 1, d), lambda i: (i // per_b, 0, 0))
    o = jax.ShapeDtypeStruct((t, d), F32)
    return pl.pallas_call(
        _outproj_kernel,
        grid=(t // tm,),
        in_specs=[row(d), row(S5_WIDTH)] + [row(RW)] * 6 +
                 [full(lnw), full(lnb), full(obm), full(w_bf), bvec, full(g2), bvec, bvec],
        out_specs=[row(d), row(d)],
        out_shape=[o, o],
        compiler_params=_cparams(("arbitrary",)),
        name="out_proj",
    )(x2, s5o, yf, yb, extra, gate, bonus, nao, lnw, lnb, obm, w_bf, gate_mix, g2, shift, scale)


def _router_kernel(h_ref, w_ref, b_ref, tri_ref, e_ref, rank_ref, gate_ref, cnt_ref, carry_ref):
    @pl.when(pl.program_id(0) == 0)
    def _():
        carry_ref[...] = jnp.zeros_like(carry_ref)

    logits = jnp.dot(h_ref[...], w_ref[...], precision=HIGHEST, preferred_element_type=F32) + b_ref[...]
    tm = logits.shape[0]
    lane = lax.broadcasted_iota(jnp.int32, (tm, LANES), 1)
    vals, idxs, hots = [], [], []
    cur = logits
    for _ in range(TOP_K):
        m = jnp.max(cur, axis=-1, keepdims=True)
        idx = jnp.min(jnp.where(cur == m, lane, LANES), axis=-1, keepdims=True)
        hot = lane == idx
        vals.append(m)
        idxs.append(idx)
        hots.append(hot)
        cur = jnp.where(hot, -jnp.inf, cur)
    exps = [jnp.exp(v - vals[0]) for v in vals]
    den = exps[0] + exps[1] + exps[2] + exps[3]
    assign = sum(h.astype(F32) for h in hots)
    before = jnp.dot(tri_ref[...], assign.astype(BF16), preferred_element_type=F32) + carry_ref[...]
    e_out = jnp.zeros((tm, LANES), jnp.int32)
    r_out = jnp.zeros((tm, LANES), jnp.int32)
    g_out = jnp.zeros((tm, LANES), F32)
    for kk in range(TOP_K):
        rank = jnp.sum(jnp.where(hots[kk], before, 0.0), axis=-1, keepdims=True)
        sel = lane == kk
        e_out = jnp.where(sel, idxs[kk], e_out)
        r_out = jnp.where(sel, rank.astype(jnp.int32), r_out)
        g_out = jnp.where(sel, exps[kk] / den, g_out)
    e_ref[...] = e_out
    rank_ref[...] = r_out
    gate_ref[...] = g_out
    total = carry_ref[...] + jnp.sum(assign, axis=0, keepdims=True)
    carry_ref[...] = total
    cnt_ref[...] = total


def _router(h2, rw_pad, rb_pad):
    t, d = h2.shape
    tm = 256
    tri = jnp.asarray(np.tril(np.ones((tm, tm), np.float32), -1), BF16)
    row = pl.BlockSpec((tm, LANES), lambda i: (i, 0))
    full = lambda a: pl.BlockSpec(a.shape, lambda i, _n=a.ndim: (0,) * _n)
    return pl.pallas_call(
        _router_kernel,
        grid=(t // tm,),
        in_specs=[pl.BlockSpec((tm, d), lambda i: (i, 0)), full(rw_pad), full(rb_pad), full(tri)],
        out_specs=[row, row, row, pl.BlockSpec((1, LANES), lambda i: (0, 0))],
        out_shape=[jax.ShapeDtypeStruct((t, LANES), jnp.int32),
                   jax.ShapeDtypeStruct((t, LANES), jnp.int32),
                   jax.ShapeDtypeStruct((t, LANES), F32),
                   jax.ShapeDtypeStruct((1, LANES), F32)],
        scratch_shapes=[pltpu.VMEM((1, LANES), F32)],
        compiler_params=_cparams(("arbitrary",)),
        name="moe_router",
    )(h2, rw_pad, rb_pad, tri)


DISPATCH_TOKENS = 256


def _row_copy(src, dst, sem):
    return pltpu.make_async_copy(src, dst, sem)


def _dispatch_kernel(dest_ref, h_ref, xb_in_ref, xb_ref, sem):
    del xb_in_ref
    n = DISPATCH_TOKENS * TOP_K

    def issue(a, c):
        _row_copy(h_ref.at[pl.ds(a // TOP_K, 1), :], xb_ref.at[pl.ds(dest_ref[a], 1), :], sem).start()
        return c

    lax.fori_loop(0, n, issue, 0)

    def drain(a, c):
        _row_copy(h_ref.at[pl.ds(0, 1), :], xb_ref.at[pl.ds(0, 1), :], sem).wait()
        return c

    lax.fori_loop(0, n, drain, 0)


def _dispatch(dest_flat, h2, n_rows):
    t, d = h2.shape
    tm = DISPATCH_TOKENS
    zeros = jnp.zeros((n_rows, d), F32)
    return pl.pallas_call(
        _dispatch_kernel,
        grid=(t // tm,),
        in_specs=[pl.BlockSpec((tm * TOP_K,), lambda i: (i,), memory_space=pltpu.SMEM),
                  pl.BlockSpec((tm, d), lambda i: (i, 0)),
                  pl.BlockSpec(memory_space=pl.ANY)],
        out_specs=pl.BlockSpec(memory_space=pl.ANY),
        out_shape=jax.ShapeDtypeStruct((n_rows, d), F32),
        scratch_shapes=[pltpu.SemaphoreType.DMA(())],
        input_output_aliases={2: 0},
        compiler_params=_cparams(("arbitrary",)),
        name="moe_dispatch",
    )(dest_flat, h2, zeros)


PAIR_GROUP = 2 * LANES


def _pair_perm():
    p = np.zeros((PAIR_GROUP, PAIR_GROUP), np.float32)
    j = np.arange(LANES)
    p[2 * j, j] = 1.0
    p[2 * j + 1, LANES + j] = 1.0
    return jnp.asarray(p, BF16)


def _expert_kernel(be_ref, x_ref, w1_ref, b1_ref, w2_ref, b2_ref, perm_ref, y_ref, w1s_ref, w2s_ref):
    i = pl.program_id(0)
    f2 = w1_ref.shape[2]
    ngrp = f2 // PAIR_GROUP

    @pl.when((i == 0) | (be_ref[i] != be_ref[jnp.maximum(i - 1, 0)]))
    def _():
        for g in range(ngrp):
            sl = slice(g * PAIR_GROUP, (g + 1) * PAIR_GROUP)
            w1s_ref[:, sl] = jnp.dot(w1_ref[0, :, sl].astype(BF16), perm_ref[...],
                                     preferred_element_type=F32).astype(BF16)
        w2s_ref[...] = w2_ref[0].astype(BF16)

    hdn = jnp.dot(x_ref[...].astype(BF16), w1s_ref[...], preferred_element_type=F32) + b1_ref[0]
    glu = jnp.concatenate([hdn[:, g * PAIR_GROUP:g * PAIR_GROUP + LANES] for g in range(ngrp)], axis=1)
    lin = jnp.concatenate([hdn[:, g * PAIR_GROUP + LANES:(g + 1) * PAIR_GROUP] for g in range(ngrp)], axis=1)
    glu = jnp.minimum(glu, SWIGLU_LIMIT)
    lin = jnp.clip(lin, -SWIGLU_LIMIT, SWIGLU_LIMIT)
    act = glu * jax.nn.sigmoid(SWIGLU_ALPHA * glu) * (lin + 1.0)
    y_ref[...] = jnp.dot(act.astype(BF16), w2s_ref[...], preferred_element_type=F32) + b2_ref[0]


def _experts(block_e, xb, w1, b1_grp, w2, b2):
    n_rows, d = xb.shape
    ne, _, f2 = w1.shape
    dff = w2.shape[1]
    nblk = n_rows // MOE_BLOCK
    perm = _pair_perm()
    grid_spec = pltpu.PrefetchScalarGridSpec(
        num_scalar_prefetch=1,
        grid=(nblk,),
        in_specs=[pl.BlockSpec((MOE_BLOCK, d), lambda i, be: (i, 0)),
                  pl.BlockSpec((1, d, f2), lambda i, be: (be[i], 0, 0)),
                  pl.BlockSpec((1, 1, f2), lambda i, be: (be[i], 0, 0)),
                  pl.BlockSpec((1, dff, d), lambda i, be: (be[i], 0, 0)),
                  pl.BlockSpec((1, 1, d), lambda i, be: (be[i], 0, 0)),
                  pl.BlockSpec(perm.shape, lambda i, be: (0, 0))],
        out_specs=pl.BlockSpec((MOE_BLOCK, d), lambda i, be: (i, 0)),
        scratch_shapes=[pltpu.VMEM((d, f2), BF16), pltpu.VMEM((dff, d), BF16)],
    )
    return pl.pallas_call(
        _expert_kernel,
        grid_spec=grid_spec,
        out_shape=jax.ShapeDtypeStruct((n_rows, d), F32),
        compiler_params=_cparams(("arbitrary",)),
        name="moe_experts",
    )(block_e, xb, w1, b1_grp.reshape(ne, 1, f2), w2, b2.reshape(ne, 1, d), perm)


COMBINE_TOKENS = 256


def _combine_kernel(dest_ref, gates_ref, x_ref, gf_ref, yb_ref, o_ref, buf_ref, sem):
    n = COMBINE_TOKENS * TOP_K

    def issue(a, c):
        _row_copy(yb_ref.at[pl.ds(dest_ref[a], 1), :],
                  buf_ref.at[a % TOP_K, pl.ds(a // TOP_K, 1), :], sem).start()
        return c

    lax.fori_loop(0, n, issue, 0)

    def drain(a, c):
        _row_copy(yb_ref.at[pl.ds(0, 1), :], buf_ref.at[0, pl.ds(0, 1), :], sem).wait()
        return c

    lax.fori_loop(0, n, drain, 0)
    gates = gates_ref[...]
    acc = gates[:, 0:1] * buf_ref[0]
    for kk in range(1, TOP_K):
        acc = acc + gates[:, kk:kk + 1] * buf_ref[kk]
    o_ref[...] = x_ref[...] + gf_ref[0] * acc


def _combine(dest_flat, gates, x2, gate_ffn, yb, seq):
    t, d = x2.shape
    tm = COMBINE_TOKENS
    per_b = seq // tm
    return pl.pallas_call(
        _combine_kernel,
        grid=(t // tm,),
        in_specs=[pl.BlockSpec((tm * TOP_K,), lambda i: (i,), memory_space=pltpu.SMEM),
                  pl.BlockSpec((tm, LANES), lambda i: (i, 0)),
                  pl.BlockSpec((tm, d), lambda i: (i, 0)),
                  pl.BlockSpec((1, 1, d), lambda i: (i // per_b, 0, 0)),
                  pl.BlockSpec(memory_space=pl.ANY)],
        out_specs=pl.BlockSpec((tm, d), lambda i: (i, 0)),
        out_shape=jax.ShapeDtypeStruct((t, d), F32),
        scratch_shapes=[pltpu.VMEM((TOP_K, tm, d), F32), pltpu.SemaphoreType.DMA(())],
        compiler_params=_cparams(("arbitrary",)),
        name="moe_combine",
    )(dest_flat, gates, x2, gate_ffn, yb)


def _group_pairs(b1):
    lead = b1.shape[:-1]
    g = b1.reshape(lead + (-1, LANES, 2))
    return jnp.swapaxes(g, -1, -2).reshape(b1.shape)


def _moe_layer(x2, h2, gate_ffn, router_w, router_b, w1, b1_grp, w2, b2, seq):
    t, d = x2.shape
    ne = router_w.shape[1]
    rw_pad = jnp.zeros((d, LANES), F32).at[:, :ne].set(router_w.astype(F32))
    rb_pad = jnp.full((1, LANES), -jnp.inf, F32).at[0, :ne].set(router_b.astype(F32))
    eidx, rank, gates, counts = _router(h2, rw_pad, rb_pad)
    n_assign = t * TOP_K
    n_blocks = -(-n_assign // MOE_BLOCK) + ne
    cnt = counts[0, :ne].astype(jnp.int32)
    padded = ((cnt + MOE_BLOCK - 1) // MOE_BLOCK) * MOE_BLOCK
    pad_end = jnp.cumsum(padded)
    pad_start = pad_end - padded
    hot = eidx[:, :TOP_K, None] == jnp.arange(ne, dtype=jnp.int32)
    dest = (jnp.sum(jnp.where(hot, pad_start, 0), axis=-1) + rank[:, :TOP_K]).reshape(-1).astype(jnp.int32)
    block_start = jnp.arange(n_blocks, dtype=jnp.int32) * MOE_BLOCK
    block_e = jnp.minimum(jnp.sum(block_start[:, None] >= pad_end[None, :], axis=-1), ne - 1).astype(jnp.int32)
    xb = _dispatch(dest, h2, n_blocks * MOE_BLOCK)
    yb = _experts(block_e, xb, w1, b1_grp, w2, b2)
    return _combine(dest, gates, x2, gate_ffn, yb, seq)


def kernel(x, c, ada_w, ada_b, norm1_g, norm2_g, w_in, w_out, s5_lam_re, s5_lam_im, s5_log_dt, s5_b_re, s5_b_im, s5_c_re, s5_c_im, s5_d, s5_glu_w, rwkv_mu, rwkv_w0, rwkv_w1, rwkv_w2, rwkv_a0, rwkv_a1, rwkv_a2, rwkv_g1, rwkv_g2, rwkv_k_k, rwkv_k_a, rwkv_r_k, rwkv_ln_w, rwkv_ln_b, na_q_g, na_k_g, na_rpb, router_w, router_b, exp_w1, exp_b1, exp_w2, exp_b2):
    bsz, seq, d = x.shape
    depth = ada_w.shape[0]
    t = bsz * seq
    mod = _ada_mod(c, ada_w, ada_b).reshape(depth, bsz, 6, 1, d)
    x2 = x.reshape(t, d)
    ob_f32 = _seg_ones(RW)
    obm_f32 = _seg_ones(RW, scale=1.0 / HEAD)
    ob256 = _seg_ones(256, dtype=BF16)
    ob128 = _seg_ones(128, dtype=BF16)
    eye_t = jnp.asarray(np.tile(np.eye(HEAD, dtype=np.float32), (1, RW // HEAD)))
    row = lambda a: a.reshape(1, -1).astype(F32)
    b1_grp = _group_pairs(exp_b1.astype(F32))
    for l in range(depth):
        m = lambda j: mod[l, :, j]
        s5u, xr, qkv = _in_proj(x2, row(norm1_g[l]), m(0), m(1), w_in[l].astype(BF16), seq)
        prep_params = dict(
            mu=row(rwkv_mu[l]), k_k=row(rwkv_k_k[l]), k_a=row(rwkv_k_a[l]), r_k=row(rwkv_r_k[l]),
            w0=rwkv_w0[l].astype(F32), a0=rwkv_a0[l].astype(F32),
            w1=rwkv_w1[l].astype(BF16), w2=rwkv_w2[l].astype(BF16),
            a1=rwkv_a1[l].astype(BF16), a2=rwkv_a2[l].astype(BF16),
            g1=rwkv_g1[l].astype(BF16), g2=rwkv_g2[l].astype(BF16),
            q_g=row(jnp.tile(na_q_g[l], NA_W // HEAD)), k_g=row(jnp.tile(na_k_g[l], NA_W // HEAD)), ob=ob_f32)
        (nkk, v, dec0, b0, k0, q0, dec1, b1, k1, q1, extra, gate, bonus, naq, nak, nav) = _prep(
            xr.reshape(bsz, seq, 4 * RW), qkv.reshape(bsz, seq, 3 * NA_W), prep_params)
        yf, yb = _wkv_scan(dict(nkk=nkk, v=v, dec0=dec0, b0=b0, k0=k0, q0=q0, dec1=dec1, b1=b1, k1=k1, q1=q1),
                           eye_t, ob256, ob128)
        bblk, cblk, lam, lamc = _s5_params(s5_lam_re[l], s5_lam_im[l], s5_log_dt[l], s5_b_re[l], s5_b_im[l],
                                           s5_c_re[l], s5_c_im[l])
        s5o = _s5_mixer(s5u.reshape(bsz, seq, S5_WIDTH), bblk, cblk, lam, lamc, row(s5_d[l]),
                        s5_glu_w[l].astype(BF16))
        nao = _na_mixer(naq, nak, nav, _na_bias_table(na_rpb[l]))
        flat = lambda a: a.reshape(t, -1)
        x2, h2 = _out_proj(x2, flat(s5o), flat(yf), flat(yb), flat(extra), flat(gate), flat(bonus), flat(nao),
                           row(rwkv_ln_w[l]), row(rwkv_ln_b[l]), obm_f32, w_out[l].astype(BF16),
                           m(2), row(norm2_g[l]), m(3), m(4), seq)
        x2 = _moe_layer(x2, h2, m(5), router_w[l], router_b[l], exp_w1[l], b1_grp[l], exp_w2[l],
                        exp_b2[l].astype(F32), seq)
    return x2.reshape(bsz, seq, d)
```

```python
import functools
import math

import numpy as np
import jax
import jax.numpy as jnp
from jax import lax
from jax.experimental import pallas as pl
from jax.experimental.pallas import tpu as pltpu

F32 = jnp.float32
BF16 = jnp.bfloat16
HIGHEST = lax.Precision.HIGHEST

D_MODEL = 1024
S5_WIDTH = 256
S5_GROUP = 16
S5_GROUPS = 16
S5_STATE = 64
S5_CHUNK = 64
S5_FLAT = S5_GROUPS * S5_STATE
RW = 384
HEAD = 64
RWKV_GN_EPS = 64e-5
NA_W = 384
GRID_W = 64
NA_KH = 8
NA_KW = 16
N_EXPERTS = 32
TOP_K = 4
MOE_BLOCK = 256
SWIGLU_ALPHA = 1.702
SWIGLU_LIMIT = 7.0
RMS_EPS = 1e-6
LANES = 128
WKV_CHUNK = 64
VMEM_LIMIT = 56 * 1024 * 1024


def _cparams(sem):
    return pltpu.CompilerParams(dimension_semantics=sem, vmem_limit_bytes=VMEM_LIMIT)


def _seg_ones(n, seg=HEAD, dtype=F32, scale=1.0):
    idx = np.arange(n) // seg
    return jnp.asarray((idx[:, None] == idx[None, :]).astype(np.float32) * scale, dtype)


def _ada_kernel(c_ref, w_ref, b_ref, o_ref):
    c = c_ref[...]
    cond = c * jax.nn.sigmoid(c)
    o_ref[0] = jnp.dot(cond, w_ref[0], preferred_element_type=F32) + b_ref[0]


def _ada_mod(c, ada_w, ada_b):
    nl, d, n6 = ada_w.shape
    bsz = c.shape[0]
    tn = 1536
    return pl.pallas_call(
        _ada_kernel,
        grid=(nl, n6 // tn),
        in_specs=[pl.BlockSpec((bsz, d), lambda l, j: (0, 0)),
                  pl.BlockSpec((1, d, tn), lambda l, j: (l, 0, j)),
                  pl.BlockSpec((1, 1, tn), lambda l, j: (l, 0, j))],
        out_specs=pl.BlockSpec((1, bsz, tn), lambda l, j: (l, 0, j)),
        out_shape=jax.ShapeDtypeStruct((nl, bsz, n6), F32),
        compiler_params=_cparams(("arbitrary", "arbitrary")),
        name="ada_mod",
    )(c, ada_w, ada_b.reshape(nl, 1, n6))


def _rms_mod(x, g, shift, scale):
    ms = jnp.mean(x * x, axis=-1, keepdims=True)
    h = x * lax.rsqrt(ms + RMS_EPS) * g
    return h * (1.0 + scale) + shift


def _proj_kernel(x_ref, g_ref, sh_ref, sc_ref, w_ref, o_s5, o_rw, o_na):
    h = _rms_mod(x_ref[...], g_ref[...], sh_ref[0], sc_ref[0])
    p = jnp.dot(h.astype(BF16), w_ref[...], preferred_element_type=F32)
    o_s5[...] = p[:, :S5_WIDTH]
    o_rw[...] = p[:, S5_WIDTH:S5_WIDTH + 4 * RW]
    o_na[...] = p[:, S5_WIDTH + 4 * RW:]


def _in_proj(x2, g, shift, scale, w_bf, seq):
    t, d = x2.shape
    n = w_bf.shape[1]
    tm = 256
    per_b = seq // tm
    row = lambda i: (i, 0)
    bvec = lambda i: (i // per_b, 0, 0)
    return pl.pallas_call(
        _proj_kernel,
        grid=(t // tm,),
        in_specs=[pl.BlockSpec((tm, d), row),
                  pl.BlockSpec((1, d), lambda i: (0, 0)),
                  pl.BlockSpec((1, 1, d), bvec),
                  pl.BlockSpec((1, 1, d), bvec),
                  pl.BlockSpec((d, n), lambda i: (0, 0))],
        out_specs=[pl.BlockSpec((tm, S5_WIDTH), row),
                   pl.BlockSpec((tm, 4 * RW), row),
                   pl.BlockSpec((tm, 3 * NA_W), row)],
        out_shape=[jax.ShapeDtypeStruct((t, S5_WIDTH), F32),
                   jax.ShapeDtypeStruct((t, 4 * RW), F32),
                   jax.ShapeDtypeStruct((t, 3 * NA_W), F32)],
        compiler_params=_cparams(("arbitrary",)),
        name="in_proj",
    )(x2, g, shift, scale, w_bf)


def _softplus(x):
    return jnp.maximum(x, 0.0) + jnp.log(1.0 + jnp.exp(-jnp.abs(x)))


def _prep_kernel(xr_ref, prev_ref, next_ref, qkv_ref,
                 mu_ref, kk_ref, ka_ref, rk_ref, w0_ref, a0_ref,
                 w1_ref, w2_ref, a1_ref, a2_ref, g1_ref, g2_ref, qg_ref, kg_ref, ob_ref,
                 nkk_o, v_o, dec0_o, b0_o, k0_o, q0_o, dec1_o, b1_o, k1_o, q1_o,
                 extra_o, gate_o, bonus_o, naq_o, nak_o, nav_o):
    i = pl.program_id(1)
    nblk = pl.num_programs(1)
    x = xr_ref[0]
    tm = x.shape[0]
    prow = jnp.where(i == 0, 0.0, prev_ref[0][7:8, :])
    nrow = jnp.where(i == nblk - 1, 0.0, next_ref[0][0:1, :])
    rid = lax.broadcasted_iota(jnp.int32, x.shape, 0)
    prev = jnp.where(rid == 0, prow, pltpu.roll(x, 1, axis=0))
    nxt = jnp.where(rid == tm - 1, nrow, pltpu.roll(x, tm - 1, axis=0))
    xs = x + (0.5 * (prev + nxt) - x) * mu_ref[...]
    r = xs[:, 0:RW]
    k = xs[:, RW:2 * RW]
    v = xs[:, 2 * RW:3 * RW]
    z = xs[:, 3 * RW:4 * RW]
    ob = ob_ref[...]
    seg = lambda t: jnp.dot(t, ob, precision=HIGHEST, preferred_element_type=F32)
    zb = z.astype(BF16)
    bdot = lambda a, w: jnp.dot(a.astype(BF16), w, preferred_element_type=F32)
    gate_o[0] = bdot(jax.nn.sigmoid(bdot(zb, g1_ref[...])), g2_ref[...])
    kk = k * kk_ref[...]
    kk = kk / jnp.maximum(jnp.sqrt(seg(kk * kk)), 1e-12)
    nkk_o[0] = -kk
    v_o[0] = v
    bonus_o[0] = seg(r * k * rk_ref[...]) * v
    extra = jnp.zeros_like(v)
    outs = ((dec0_o, b0_o, k0_o, q0_o), (dec1_o, b1_o, k1_o, q1_o))
    for d in range(2):
        wl = w0_ref[d:d + 1, :] + bdot(jnp.tanh(bdot(zb, w1_ref[d])), w2_ref[d])
        w = -_softplus(-wl) - 0.5
        dec = jnp.exp(-jnp.exp(w))
        a = jax.nn.sigmoid(a0_ref[d:d + 1, :] + bdot(bdot(zb, a1_ref[d]), a2_ref[d]))
        kd = k * (1.0 + (a - 1.0) * ka_ref[...])
        bv = kk * a
        dec_o, b_o, k_o, q_o = outs[d]
        dec_o[0] = dec
        b_o[0] = bv
        k_o[0] = kd
        q_o[0] = dec * r - kk * seg(bv * r)
        extra = extra + v * seg(kd * r)
    extra_o[0] = extra
    qkv = qkv_ref[0]
    segm = lambda t: seg(t) * (1.0 / HEAD)
    qn = qkv[:, 0:NA_W]
    kn = qkv[:, NA_W:2 * NA_W]
    naq_o[0] = (qn * lax.rsqrt(segm(qn * qn) + RMS_EPS) * qg_ref[...] * (HEAD ** -0.5)).astype(BF16)
    nak_o[0] = (kn * lax.rsqrt(segm(kn * kn) + RMS_EPS) * kg_ref[...]).astype(BF16)
    nav_o[0] = qkv[:, 2 * NA_W:].astype(BF16)


def _prep(xr, qkv, p):
    bsz, seq, _ = xr.shape
    tm = 256
    nb = seq // tm
    h8 = tm // 8
    blk = lambda w: pl.BlockSpec((1, tm, w), lambda b, i: (b, i, 0))
    full = lambda a: pl.BlockSpec(a.shape, lambda b, i, _n=a.ndim: (0,) * _n)
    params = [p["mu"], p["k_k"], p["k_a"], p["r_k"], p["w0"], p["a0"], p["w1"], p["w2"], p["a1"], p["a2"],
              p["g1"], p["g2"], p["q_g"], p["k_g"], p["ob"]]
    f32o = jax.ShapeDtypeStruct((bsz, seq, RW), F32)
    bfo = jax.ShapeDtypeStruct((bsz, seq, NA_W), BF16)
    return pl.pallas_call(
        _prep_kernel,
        grid=(bsz, nb),
        in_specs=[blk(4 * RW),
                  pl.BlockSpec((1, 8, 4 * RW), lambda b, i: (b, jnp.maximum(i * h8 - 1, 0), 0)),
                  pl.BlockSpec((1, 8, 4 * RW), lambda b, i: (b, jnp.minimum((i + 1) * h8, seq // 8 - 1), 0)),
                  blk(3 * NA_W)] + [full(a) for a in params],
        out_specs=[blk(RW)] * 16,
        out_shape=[f32o] * 13 + [bfo] * 3,
        compiler_params=_cparams(("arbitrary", "arbitrary")),
        name="mixer_prep",
    )(xr, xr, xr, qkv, *params)


def _wkv_kernel(*refs, bsz, tc):
    f_in = refs[0:6]
    b_in = refs[6:12]
    e_ref, ob256_ref, ob128_ref = refs[12:15]
    yf_ref, yb_ref = refs[15:17]
    s_ref = refs[17]

    @pl.when(pl.program_id(0) == 0)
    def _():
        s_ref[...] = jnp.zeros_like(s_ref)

    eye = e_ref[...]
    ob256 = ob256_ref[...]
    ob128 = ob128_ref[...]

    def seg(t):
        return jnp.concatenate(
            [jnp.dot(t[:, :256], ob256, preferred_element_type=F32),
             jnp.dot(t[:, 256:], ob128, preferred_element_type=F32)], axis=1)

    def step(i, carry):
        for b in range(bsz):
            for d in range(2):
                src = f_in if d == 0 else b_in
                out = yf_ref if d == 0 else yb_ref
                t = i if d == 0 else tc - 1 - i
                a, v, w, bb, kd, q = [s[b, pl.ds(t, 1), :] for s in src]
                st = s_ref[2 * b + d]
                lhs = jnp.concatenate([st * a, st * q, eye * v], axis=0).astype(BF16)
                res = seg(lhs)
                sa = res[0:HEAD]
                yq = res[HEAD:2 * HEAD]
                vb = res[2 * HEAD:3 * HEAD]
                s_ref[2 * b + d] = st * w + sa * bb + vb * kd
                out[b, pl.ds(t, 1), :] = jnp.sum(yq * eye, axis=0, keepdims=True)
        return carry

    lax.fori_loop(0, tc, step, 0)


def _wkv_scan(ins, eye, ob256, ob128):
    bsz, seq, _ = ins["nkk"].shape
    tc = WKV_CHUNK
    nc = seq // tc
    fwd = pl.BlockSpec((bsz, tc, RW), lambda c: (0, c, 0))
    bwd = pl.BlockSpec((bsz, tc, RW), lambda c: (0, nc - 1 - c, 0))
    full = lambda a: pl.BlockSpec(a.shape, lambda c, _n=a.ndim: (0,) * _n)
    f_args = [ins["nkk"], ins["v"], ins["dec0"], ins["b0"], ins["k0"], ins["q0"]]
    b_args = [ins["nkk"], ins["v"], ins["dec1"], ins["b1"], ins["k1"], ins["q1"]]
    o = jax.ShapeDtypeStruct((bsz, seq, RW), F32)
    return pl.pallas_call(
        functools.partial(_wkv_kernel, bsz=bsz, tc=tc),
        grid=(nc,),
        in_specs=[fwd] * 6 + [bwd] * 6 + [full(eye), full(ob256), full(ob128)],
        out_specs=[fwd, bwd],
        out_shape=[o, o],
        scratch_shapes=[pltpu.VMEM((2 * bsz, HEAD, RW), F32)],
        compiler_params=_cparams(("arbitrary",)),
        name="wkv_scan",
    )(*f_args, *b_args, eye, ob256, ob128)


def _gelu_tanh(x):
    return 0.5 * x * (1.0 + jnp.tanh(math.sqrt(2.0 / math.pi) * (x + 0.044715 * (x * x * x))))


def _s5_kernel(ua_ref, ub_ref, bblk_ref, cblk_ref, lam_ref, lamc_ref, d_ref, glu_ref, o_ref,
               y_ref, st_ref, end_ref, carry_ref, *, seq):
    ch = S5_CHUNK
    nc = seq // ch
    n = S5_FLAT
    u_halves = (ua_ref, ub_ref)
    for hf in range(2):
        y_ref[hf] = u_halves[hf][0] * d_ref[:, hf * LANES:(hf + 1) * LANES]

    def cmul_add(lre, lim, s, add):
        sre = s[:, :n]
        sim = s[:, n:]
        return jnp.concatenate([lre * sre - lim * sim + add[:, :n],
                                lre * sim + lim * sre + add[:, n:]], axis=1)

    for d in range(2):
        lre = lam_ref[d, 0:1, :]
        lim = lam_ref[d, 1:2, :]
        lcre = lamc_ref[d, 0:1, :]
        lcim = lamc_ref[d, 1:2, :]
        tloc = (lambda i: i) if d == 0 else (lambda i: ch - 1 - i)
        cloc = (lambda i: i) if d == 0 else (lambda i: nc - 1 - i)

        def advance(tl):
            rows = jnp.concatenate([r[0, pl.ds(tl, nc, stride=ch), :] for r in u_halves], axis=1)
            bu = jnp.dot(rows.astype(BF16), bblk_ref[d], preferred_element_type=F32)
            st_ref[...] = cmul_add(lre, lim, st_ref[...], bu)

        st_ref[...] = jnp.zeros_like(st_ref)

        def p1(i, c):
            advance(tloc(i))
            return c

        lax.fori_loop(0, ch, p1, 0)
        end_ref[...] = st_ref[...]

        def cs(i, car):
            c = cloc(i)
            carry_ref[pl.ds(c, 1), :] = car
            return cmul_add(lcre, lcim, car, end_ref[pl.ds(c, 1), :])

        lax.fori_loop(0, nc, cs, jnp.zeros((1, 2 * n), F32))

        st_ref[...] = carry_ref[...]

        def p2(i, c):
            tl = tloc(i)
            advance(tl)
            yr = jnp.dot(st_ref[...].astype(BF16), cblk_ref[d], preferred_element_type=F32)
            idx = pl.ds(tl, nc, stride=ch)
            for hf in range(2):
                y_ref[hf, idx, :] = y_ref[hf, idx, :] + yr[:, hf * LANES:(hf + 1) * LANES]
            return c

        lax.fori_loop(0, ch, p2, 0)

    g = _gelu_tanh(jnp.concatenate([y_ref[0], y_ref[1]], axis=1))
    o_ref[0] = g * jax.nn.sigmoid(jnp.dot(g.astype(BF16), glu_ref[...], preferred_element_type=F32))


def _s5_params(lam_re, lam_im, log_dt, b_re, b_im, c_re, c_im):
    lre = lam_re.astype(F32)
    lim = lam_im.astype(F32)
    dt = jnp.exp(log_dt.astype(F32))[..., None]

    def cexp(scale):
        mag = jnp.exp(lre * dt * scale)
        return mag * jnp.cos(lim * dt * scale), mag * jnp.sin(lim * dt * scale)

    bar_re, bar_im = cexp(1.0)
    den = lre * lre + lim * lim
    f_re = ((bar_re - 1.0) * lre + bar_im * lim) / den
    f_im = (bar_im * lre - (bar_re - 1.0) * lim) / den
    bm_re = b_re.astype(F32)
    bm_im = b_im.astype(F32)
    bb_re = f_re[..., None] * bm_re - f_im[..., None] * bm_im
    bb_im = f_re[..., None] * bm_im + f_im[..., None] * bm_re
    eye_g = jnp.eye(S5_GROUPS, dtype=F32)

    def blockdiag_in(m):
        return jnp.einsum("dgph,gk->dghkp", m, eye_g).reshape(2, S5_WIDTH, S5_FLAT)

    def blockdiag_out(m):
        return jnp.einsum("dghp,gk->dgpkh", m, eye_g).reshape(2, S5_FLAT, S5_WIDTH)

    bblk = jnp.concatenate([blockdiag_in(bb_re), blockdiag_in(bb_im)], axis=2)
    cblk = jnp.concatenate([blockdiag_out(c_re.astype(F32)), -blockdiag_out(c_im.astype(F32))], axis=1)
    flat = lambda z: jnp.stack([z[0].reshape(2, S5_FLAT), z[1].reshape(2, S5_FLAT)], axis=1)
    return bblk.astype(BF16), cblk.astype(BF16), flat((bar_re, bar_im)), flat(cexp(float(S5_CHUNK)))


def _s5_mixer(u, bblk, cblk, lam, lamc, d_skip, glu_bf):
    bsz, seq, w = u.shape
    nc = seq // S5_CHUNK
    full = lambda a: pl.BlockSpec(a.shape, lambda b, _n=a.ndim: (0,) * _n)
    args = [bblk, cblk, lam, lamc, d_skip, glu_bf]
    return pl.pallas_call(
        functools.partial(_s5_kernel, seq=seq),
        grid=(bsz,),
        in_specs=[pl.BlockSpec((1, seq, LANES), lambda b: (b, 0, 0)),
                  pl.BlockSpec((1, seq, LANES), lambda b: (b, 0, 1))] + [full(a) for a in args],
        out_specs=pl.BlockSpec((1, seq, w), lambda b: (b, 0, 0)),
        out_shape=jax.ShapeDtypeStruct((bsz, seq, w), F32),
        scratch_shapes=[pltpu.VMEM((w // LANES, seq, LANES), F32),
                        pltpu.VMEM((nc, 2 * S5_FLAT), F32),
                        pltpu.VMEM((nc, 2 * S5_FLAT), F32),
                        pltpu.VMEM((nc, 2 * S5_FLAT), F32)],
        compiler_params=_cparams(("arbitrary",)),
        name="s5_mixer",
    )(u, u, *args)


def _na_bias_table(rpb):
    q_col = np.arange(GRID_W)
    c_start = np.clip(q_col - NA_KW // 2, 0, GRID_W - NA_KW)
    k_col = np.arange(GRID_W)
    valid = (k_col[None, :] >= c_start[:, None]) & (k_col[None, :] < c_start[:, None] + NA_KW)
    dx = np.clip(k_col[None, :] - q_col[:, None] + NA_KW - 1, 0, 2 * NA_KW - 2)
    pick = (np.arange(2 * NA_KW - 1)[:, None, None] == dx[None]).astype(np.float32)
    base = jnp.einsum("hyd,dqk->hyqk", rpb.astype(F32), jnp.asarray(pick), precision=HIGHEST)
    base = jnp.where(jnp.asarray(valid)[None, None], base, -jnp.inf)
    tab = jnp.stack([base[:, NA_KH - 1 - o:2 * NA_KH - 1 - o] for o in range(NA_KH)], axis=1)
    tab = jnp.transpose(tab, (0, 1, 3, 2, 4))
    return tab.reshape(rpb.shape[0], NA_KH, GRID_W, NA_KH * GRID_W)


def _na_kernel(q_ref, k_ref, v_ref, bias_ref, o_ref, *, rows, rblk):
    rb = pl.program_id(1)
    lane = lax.broadcasted_iota(jnp.int32, (GRID_W, LANES), 1)
    low = lane < HEAD

    def row(j, carry):
        r = rb * rblk + j
        rs = jnp.clip(r - NA_KH // 2, 0, rows - NA_KH)
        off = r - rs
        q = q_ref[0, j]
        kmat = k_ref[0, pl.ds(rs, NA_KH)].reshape(NA_KH * GRID_W, NA_W)
        vmat = v_ref[0, pl.ds(rs, NA_KH)].reshape(NA_KH * GRID_W, NA_W)
        outs = []
        for c in range(NA_W // LANES):
            sl = slice(c * LANES, (c + 1) * LANES)
            q2 = q[:, sl].astype(F32)
            lhs = jnp.concatenate([jnp.where(low, q2, 0.0), jnp.where(low, 0.0, q2)], axis=0).astype(BF16)
            s = lax.dot_general(lhs, kmat[:, sl], (((1,), (1,)), ((), ())), preferred_element_type=F32)
            s = s + jnp.concatenate([bias_ref[2 * c, off], bias_ref[2 * c + 1, off]], axis=0)
            m = jnp.max(s, axis=-1, keepdims=True)
            p = jnp.exp(s - m)
            l = jnp.sum(p, axis=-1, keepdims=True)
            o = jnp.dot(p.astype(BF16), vmat[:, sl], preferred_element_type=F32) / l
            outs.append(jnp.where(low, o[:GRID_W], o[GRID_W:]))
        o_ref[0, j] = jnp.concatenate(outs, axis=1)
        return carry

    lax.fori_loop(0, rblk, row, 0)


def _na_mixer(q, k, v, bias):
    bsz, seq, w = q.shape
    rows = seq // GRID_W
    rblk = 8
    g4 = lambda a: a.reshape(bsz, rows, GRID_W, w)
    img = pl.BlockSpec((1, rows, GRID_W, w), lambda b, i: (b, 0, 0, 0))
    blk = pl.BlockSpec((1, rblk, GRID_W, w), lambda b, i: (b, i, 0, 0))
    out = pl.pallas_call(
        functools.partial(_na_kernel, rows=rows, rblk=rblk),
        grid=(bsz, rows // rblk),
        in_specs=[blk, img, img, pl.BlockSpec(bias.shape, lambda b, i: (0, 0, 0, 0))],
        out_specs=blk,
        out_shape=jax.ShapeDtypeStruct((bsz, rows, GRID_W, w), F32),
        compiler_params=_cparams(("arbitrary", "arbitrary")),
        name="na_mixer",
    )(g4(q), g4(k), g4(v), bias)
    return out.reshape(bsz, seq, w)


def _outproj_kernel(x_ref, s5_ref, yf_ref, yb_ref, extra_ref, gate_ref, bonus_ref, na_ref,
                    lnw_ref, lnb_ref, obm_ref, w_ref, gm_ref, g2_ref, sh_ref, sc_ref,
                    xo_ref, h_ref):
    obm = obm_ref[...]
    segm = lambda t: jnp.dot(t, obm, precision=HIGHEST, preferred_element_type=F32)
    y = yf_ref[...] + yb_ref[...] + extra_ref[...]
    yc = y - segm(y)
    yn = yc * lax.rsqrt(segm(yc * yc) + RWKV_GN_EPS) * lnw_ref[...] + lnb_ref[...]
    rw = (yn + bonus_ref[...]) * gate_ref[...]
    mixed = jnp.concatenate([s5_ref[...], rw, na_ref[...]], axis=1).astype(BF16)
    xo = x_ref[...] + gm_ref[0] * jnp.dot(mixed, w_ref[...], preferred_element_type=F32)
    xo_ref[...] = xo
    h_ref[...] = _rms_mod(xo, g2_ref[...], sh_ref[0], sc_ref[0])


def _out_proj(x2, s5o, yf, yb, extra, gate, bonus, nao, lnw, lnb, obm, w_bf, gate_mix, g2, shift, scale, seq):
    t, d = x2.shape
    tm = 256
    per_b = seq // tm
    row = lambda w: pl.BlockSpec((tm, w), lambda i: (i, 0))
    full = lambda a: pl.BlockSpec(a.shape, lambda i, _n=a.ndim: (0,) * _n)
    bvec = pl.BlockSpec((1, 1, d), lambda i: (i // per_b, 0, 0))
    o = jax.ShapeDtypeStruct((t, d), F32)
    return pl.pallas_call(
        _outproj_kernel,
        grid=(t // tm,),
        in_specs=[row(d), row(S5_WIDTH)] + [row(RW)] * 6 +
                 [full(lnw), full(lnb), full(obm), full(w_bf), bvec, full(g2), bvec, bvec],
        out_specs=[row(d), row(d)],
        out_shape=[o, o],
        compiler_params=_cparams(("arbitrary",)),
        name="out_proj",
    )(x2, s5o, yf, yb, extra, gate, bonus, nao, lnw, lnb, obm, w_bf, gate_mix, g2, shift, scale)


MOE_TILE = 256
SEG_ALIGN = 8
MOE_SLOTS = -(-(MOE_TILE * TOP_K + N_EXPERTS * (SEG_ALIGN - 1)) // LANES) * LANES


def _router_kernel(h_ref, w_ref, b_ref, tri_ref, upper_ref, slot_ref, gate_ref, cnt_ref, base_ref, loc_ref,
                   carry_ref):
    @pl.when(pl.program_id(0) == 0)
    def _():
        carry_ref[...] = jnp.zeros_like(carry_ref)

    logits = jnp.dot(h_ref[...], w_ref[...], precision=HIGHEST, preferred_element_type=F32) + b_ref[...]
    tm = logits.shape[0]
    lane = lax.broadcasted_iota(jnp.int32, (tm, LANES), 1)
    vals, hots = [], []
    cur = logits
    for _ in range(TOP_K):
        m = jnp.max(cur, axis=-1, keepdims=True)
        idx = jnp.min(jnp.where(cur == m, lane, LANES), axis=-1, keepdims=True)
        hot = lane == idx
        vals.append(m)
        hots.append(hot)
        cur = jnp.where(hot, -jnp.inf, cur)
    exps = [jnp.exp(v - vals[0]) for v in vals]
    den = exps[0] + exps[1] + exps[2] + exps[3]
    assign = sum(h.astype(F32) for h in hots)
    before = jnp.dot(tri_ref[...], assign.astype(BF16), preferred_element_type=F32)
    cnt = jnp.sum(assign, axis=0, keepdims=True)
    cnt = jnp.floor((cnt + (SEG_ALIGN - 1)) * (1.0 / SEG_ALIGN)) * SEG_ALIGN
    cnt8 = jnp.broadcast_to(cnt, (8, LANES)).astype(BF16)
    loc = jnp.dot(cnt8, upper_ref[...], preferred_element_type=F32)[0:1, :]
    place = before + loc
    s_out = jnp.zeros((tm, LANES), jnp.int32)
    g_out = jnp.zeros((tm, LANES), F32)
    for kk in range(TOP_K):
        slot = jnp.sum(jnp.where(hots[kk], place, 0.0), axis=-1, keepdims=True)
        sel = lane == kk
        s_out = jnp.where(sel, slot.astype(jnp.int32), s_out)
        g_out = jnp.where(sel, exps[kk] / den, g_out)
    slot_ref[...] = s_out
    gate_ref[...] = g_out
    cnt_ref[0] = cnt.astype(jnp.int32)
    base_ref[0] = carry_ref[...].astype(jnp.int32)
    loc_ref[0] = loc.astype(jnp.int32)
    carry_ref[...] = carry_ref[...] + cnt


def _router(h2, rw_pad, rb_pad):
    t, d = h2.shape
    tm = MOE_TILE
    nt = t // tm
    tri = jnp.asarray(np.tril(np.ones((tm, tm), np.float32), -1), BF16)
    upper = jnp.asarray(np.triu(np.ones((LANES, LANES), np.float32), 1), BF16)
    row = pl.BlockSpec((tm, LANES), lambda i: (i, 0))
    per_tile = pl.BlockSpec((1, 1, LANES), lambda i: (i, 0, 0))
    full = lambda a: pl.BlockSpec(a.shape, lambda i, _n=a.ndim: (0,) * _n)
    tile_i32 = jax.ShapeDtypeStruct((nt, 1, LANES), jnp.int32)
    return pl.pallas_call(
        _router_kernel,
        grid=(nt,),
        in_specs=[pl.BlockSpec((tm, d), lambda i: (i, 0)), full(rw_pad), full(rb_pad), full(tri), full(upper)],
        out_specs=[row, row, per_tile, per_tile, per_tile],
        out_shape=[jax.ShapeDtypeStruct((t, LANES), jnp.int32),
                   jax.ShapeDtypeStruct((t, LANES), F32),
                   tile_i32, tile_i32, tile_i32],
        scratch_shapes=[pltpu.VMEM((1, LANES), F32)],
        compiler_params=_cparams(("arbitrary",)),
        name="moe_router",
    )(h2, rw_pad, rb_pad, tri, upper)


SEG_PIECES = tuple(SEG_ALIGN << s for s in range((MOE_TILE // SEG_ALIGN).bit_length()))


def _segment_dmas(cnt_ref, loc_ref, row_ref, ne, make_copy, wait):
    tile = pl.program_id(0)

    def per_expert(e, carry):
        i = tile * ne + e
        n = cnt_ref[i]
        off = loc_ref[i]
        row = row_ref[i]
        for p in SEG_PIECES:
            has = (n & p) != 0

            @pl.when(has)
            def _(off=off, row=row, p=p):
                cp = make_copy(pl.multiple_of(off, SEG_ALIGN), pl.multiple_of(row, SEG_ALIGN), p)
                if wait:
                    cp.wait()
                else:
                    cp.start()

            step = jnp.where(has, p, 0)
            off = off + step
            row = row + step
        return carry

    lax.fori_loop(0, ne, per_expert, 0)


def _dispatch_kernel(cnt_ref, loc_ref, row_ref, slot_ref, h_ref, xb_in_ref, xb_ref, sorted_ref, sem, *, ne):
    del xb_in_ref
    tm = h_ref.shape[0]
    ns = MOE_SLOTS
    slot_t = jnp.transpose(slot_ref[...].astype(F32))
    srow = lax.broadcasted_iota(jnp.int32, (ns, tm), 0).astype(F32)
    pick = jnp.zeros((ns, tm), F32)
    for kk in range(TOP_K):
        pick = jnp.where(srow == slot_t[kk:kk + 1, :], 1.0, pick)
    sorted_ref[...] = jnp.dot(pick.astype(BF16), h_ref[...].astype(BF16), preferred_element_type=F32)

    def copy(off, row, p):
        return pltpu.make_async_copy(sorted_ref.at[pl.ds(off, p), :], xb_ref.at[pl.ds(row, p), :], sem)

    _segment_dmas(cnt_ref, loc_ref, row_ref, ne, copy, wait=False)
    _segment_dmas(cnt_ref, loc_ref, row_ref, ne, copy, wait=True)


def _dispatch(cnt, loc, rowstart, slot, h2, n_rows, ne):
    t, d = h2.shape
    tm = MOE_TILE
    zeros = jnp.zeros((n_rows, d), F32)
    grid_spec = pltpu.PrefetchScalarGridSpec(
        num_scalar_prefetch=3,
        grid=(t // tm,),
        in_specs=[pl.BlockSpec((tm, LANES), lambda i, *_: (i, 0)),
                  pl.BlockSpec((tm, d), lambda i, *_: (i, 0)),
                  pl.BlockSpec(memory_space=pl.ANY)],
        out_specs=pl.BlockSpec(memory_space=pl.ANY),
        scratch_shapes=[pltpu.VMEM((MOE_SLOTS, d), F32), pltpu.SemaphoreType.DMA(())],
    )
    return pl.pallas_call(
        functools.partial(_dispatch_kernel, ne=ne),
        grid_spec=grid_spec,
        out_shape=jax.ShapeDtypeStruct((n_rows, d), F32),
        input_output_aliases={5: 0},
        compiler_params=_cparams(("arbitrary",)),
        name="moe_dispatch",
    )(cnt, loc, rowstart, slot, h2, zeros)


PAIR_GROUP = 2 * LANES


def _pair_perm():
    p = np.zeros((PAIR_GROUP, PAIR_GROUP), np.float32)
    j = np.arange(LANES)
    p[2 * j, j] = 1.0
    p[2 * j + 1, LANES + j] = 1.0
    return jnp.asarray(p, BF16)


def _expert_kernel(be_ref, nu_ref, x_ref, w1_ref, b1_ref, w2_ref, b2_ref, perm_ref, y_ref, w1s_ref, w2s_ref):
    i = pl.program_id(0)
    f2 = w1_ref.shape[2]
    ngrp = f2 // PAIR_GROUP

    @pl.when(i >= nu_ref[0])
    def _():
        y_ref[...] = jnp.zeros_like(y_ref)

    @pl.when(i < nu_ref[0])
    def _():
        @pl.when((i == 0) | (be_ref[i] != be_ref[jnp.maximum(i - 1, 0)]))
        def _():
            for g in range(ngrp):
                sl = slice(g * PAIR_GROUP, (g + 1) * PAIR_GROUP)
                w1s_ref[:, sl] = jnp.dot(w1_ref[0, :, sl].astype(BF16), perm_ref[...],
                                         preferred_element_type=F32).astype(BF16)
            w2s_ref[...] = w2_ref[0].astype(BF16)

        hdn = jnp.dot(x_ref[...].astype(BF16), w1s_ref[...], preferred_element_type=F32) + b1_ref[0]
        glu = jnp.concatenate([hdn[:, g * PAIR_GROUP:g * PAIR_GROUP + LANES] for g in range(ngrp)], axis=1)
        lin = jnp.concatenate([hdn[:, g * PAIR_GROUP + LANES:(g + 1) * PAIR_GROUP] for g in range(ngrp)], axis=1)
        glu = jnp.minimum(glu, SWIGLU_LIMIT)
        lin = jnp.clip(lin, -SWIGLU_LIMIT, SWIGLU_LIMIT)
        act = glu * jax.nn.sigmoid(SWIGLU_ALPHA * glu) * (lin + 1.0)
        y_ref[...] = jnp.dot(act.astype(BF16), w2s_ref[...], preferred_element_type=F32) + b2_ref[0]


def _experts(block_e, n_used, xb, w1, b1_grp, w2, b2, layer):
    n_rows, d = xb.shape
    _, ne, _, f2 = w1.shape
    dff = w2.shape[2]
    nblk = n_rows // MOE_BLOCK
    perm = _pair_perm()
    blk = lambda i, be, nu: (jnp.minimum(i, nu[0] - 1), 0)
    wsel = lambda i, be, nu: (layer, be[i], 0, 0)
    grid_spec = pltpu.PrefetchScalarGridSpec(
        num_scalar_prefetch=2,
        grid=(nblk,),
        in_specs=[pl.BlockSpec((MOE_BLOCK, d), blk),
                  pl.BlockSpec((None, 1, d, f2), wsel),
                  pl.BlockSpec((None, 1, 1, f2), wsel),
                  pl.BlockSpec((None, 1, dff, d), wsel),
                  pl.BlockSpec((None, 1, 1, d), wsel),
                  pl.BlockSpec(perm.shape, lambda i, be, nu: (0, 0))],
        out_specs=pl.BlockSpec((MOE_BLOCK, d), lambda i, be, nu: (i, 0)),
        scratch_shapes=[pltpu.VMEM((d, f2), BF16), pltpu.VMEM((dff, d), BF16)],
    )
    nl = w1.shape[0]
    return pl.pallas_call(
        _expert_kernel,
        grid_spec=grid_spec,
        out_shape=jax.ShapeDtypeStruct((n_rows, d), F32),
        compiler_params=_cparams(("arbitrary",)),
        name="moe_experts",
    )(block_e, n_used, xb, w1, b1_grp.reshape(nl, ne, 1, f2), w2, b2.reshape(nl, ne, 1, d), perm)


def _combine_kernel(cnt_ref, loc_ref, row_ref, slot_ref, gates_ref, x_ref, gf_ref, yb_ref, o_ref, sorted_ref, sem,
                    *, ne):
    tm = x_ref.shape[0]
    ns = MOE_SLOTS

    def copy(off, row, p):
        return pltpu.make_async_copy(yb_ref.at[pl.ds(row, p), :], sorted_ref.at[pl.ds(off, p), :], sem)

    @pl.when(pl.program_id(0) == 0)
    def _():
        sorted_ref[...] = jnp.zeros_like(sorted_ref)

    _segment_dmas(cnt_ref, loc_ref, row_ref, ne, copy, wait=False)
    slot = slot_ref[...]
    gates = gates_ref[...]
    scol = lax.broadcasted_iota(jnp.int32, (tm, ns), 1)
    gmat = jnp.zeros((tm, ns), F32)
    for kk in range(TOP_K):
        gmat = jnp.where(scol == slot[:, kk:kk + 1], gates[:, kk:kk + 1], gmat)
    _segment_dmas(cnt_ref, loc_ref, row_ref, ne, copy, wait=True)
    acc = jnp.dot(gmat.astype(BF16), sorted_ref[...].astype(BF16), preferred_element_type=F32)
    o_ref[...] = x_ref[...] + gf_ref[0] * acc


def _combine(cnt, loc, rowstart, slot, gates, x2, gate_ffn, yb, seq, ne):
    t, d = x2.shape
    tm = MOE_TILE
    per_b = seq // tm
    grid_spec = pltpu.PrefetchScalarGridSpec(
        num_scalar_prefetch=3,
        grid=(t // tm,),
        in_specs=[pl.BlockSpec((tm, LANES), lambda i, *_: (i, 0)),
                  pl.BlockSpec((tm, LANES), lambda i, *_: (i, 0)),
                  pl.BlockSpec((tm, d), lambda i, *_: (i, 0)),
                  pl.BlockSpec((1, 1, d), lambda i, *_: (i // per_b, 0, 0)),
                  pl.BlockSpec(memory_space=pl.ANY)],
        out_specs=pl.BlockSpec((tm, d), lambda i, *_: (i, 0)),
        scratch_shapes=[pltpu.VMEM((MOE_SLOTS, d), F32), pltpu.SemaphoreType.DMA(())],
    )
    return pl.pallas_call(
        functools.partial(_combine_kernel, ne=ne),
        grid_spec=grid_spec,
        out_shape=jax.ShapeDtypeStruct((t, d), F32),
        compiler_params=_cparams(("arbitrary",)),
        name="moe_combine",
    )(cnt, loc, rowstart, slot, gates, x2, gate_ffn, yb)


def _group_pairs(b1):
    lead = b1.shape[:-1]
    g = b1.reshape(lead + (-1, LANES, 2))
    return jnp.swapaxes(g, -1, -2).reshape(b1.shape)


def _moe_layer(x2, h2, gate_ffn, router_w, router_b, w1, b1_grp, w2, b2, seq, layer):
    t, d = x2.shape
    ne = router_w.shape[1]
    rw_pad = jnp.zeros((d, LANES), F32).at[:, :ne].set(router_w.astype(F32))
    rb_pad = jnp.full((1, LANES), -jnp.inf, F32).at[0, :ne].set(router_b.astype(F32))
    slot, gates, cnt3, base3, loc3 = _router(h2, rw_pad, rb_pad)
    n_assign = t * TOP_K
    n_tiles = t // MOE_TILE
    n_blocks = -(-(n_assign + n_tiles * ne * (SEG_ALIGN - 1)) // MOE_BLOCK) + ne
    cnt = cnt3[:, 0, :ne]
    base = base3[:, 0, :ne]
    loc = loc3[:, 0, :ne]
    total = base[-1] + cnt[-1]
    padded = ((total + MOE_BLOCK - 1) // MOE_BLOCK) * MOE_BLOCK
    pad_end = jnp.cumsum(padded)
    pad_start = pad_end - padded
    rowstart = (pad_start[None, :] + base).reshape(-1).astype(jnp.int32)
    n_used = (pad_end[-1] // MOE_BLOCK).astype(jnp.int32)
    block_start = jnp.minimum(jnp.arange(n_blocks, dtype=jnp.int32), n_used - 1) * MOE_BLOCK
    block_e = jnp.minimum(jnp.sum(block_start[:, None] >= pad_end[None, :], axis=-1), ne - 1).astype(jnp.int32)
    cnt_f = cnt.reshape(-1)
    loc_f = loc.reshape(-1)
    xb = _dispatch(cnt_f, loc_f, rowstart, slot, h2, n_blocks * MOE_BLOCK, ne)
    yb = _experts(block_e, n_used.reshape(1), xb, w1, b1_grp, w2, b2, layer)
    return _combine(cnt_f, loc_f, rowstart, slot, gates, x2, gate_ffn, yb, seq, ne)


def kernel(x, c, ada_w, ada_b, norm1_g, norm2_g, w_in, w_out, s5_lam_re, s5_lam_im, s5_log_dt, s5_b_re, s5_b_im, s5_c_re, s5_c_im, s5_d, s5_glu_w, rwkv_mu, rwkv_w0, rwkv_w1, rwkv_w2, rwkv_a0, rwkv_a1, rwkv_a2, rwkv_g1, rwkv_g2, rwkv_k_k, rwkv_k_a, rwkv_r_k, rwkv_ln_w, rwkv_ln_b, na_q_g, na_k_g, na_rpb, router_w, router_b, exp_w1, exp_b1, exp_w2, exp_b2):
    bsz, seq, d = x.shape
    depth = ada_w.shape[0]
    t = bsz * seq
    mod = _ada_mod(c, ada_w, ada_b).reshape(depth, bsz, 6, 1, d)
    x2 = x.reshape(t, d)
    ob_f32 = _seg_ones(RW)
    obm_f32 = _seg_ones(RW, scale=1.0 / HEAD)
    ob256 = _seg_ones(256, dtype=BF16)
    ob128 = _seg_ones(128, dtype=BF16)
    eye_t = jnp.asarray(np.tile(np.eye(HEAD, dtype=np.float32), (1, RW // HEAD)))
    row = lambda a: a.reshape(1, -1).astype(F32)
    b1_grp = _group_pairs(exp_b1.astype(F32))
    exp_b2f = exp_b2.astype(F32)
    for l in range(depth):
        m = lambda j: mod[l, :, j]
        s5u, xr, qkv = _in_proj(x2, row(norm1_g[l]), m(0), m(1), w_in[l].astype(BF16), seq)
        prep_params = dict(
            mu=row(rwkv_mu[l]), k_k=row(rwkv_k_k[l]), k_a=row(rwkv_k_a[l]), r_k=row(rwkv_r_k[l]),
            w0=rwkv_w0[l].astype(F32), a0=rwkv_a0[l].astype(F32),
            w1=rwkv_w1[l].astype(BF16), w2=rwkv_w2[l].astype(BF16),
            a1=rwkv_a1[l].astype(BF16), a2=rwkv_a2[l].astype(BF16),
            g1=rwkv_g1[l].astype(BF16), g2=rwkv_g2[l].astype(BF16),
            q_g=row(jnp.tile(na_q_g[l], NA_W // HEAD)), k_g=row(jnp.tile(na_k_g[l], NA_W // HEAD)), ob=ob_f32)
        (nkk, v, dec0, b0, k0, q0, dec1, b1, k1, q1, extra, gate, bonus, naq, nak, nav) = _prep(
            xr.reshape(bsz, seq, 4 * RW), qkv.reshape(bsz, seq, 3 * NA_W), prep_params)
        yf, yb = _wkv_scan(dict(nkk=nkk, v=v, dec0=dec0, b0=b0, k0=k0, q0=q0, dec1=dec1, b1=b1, k1=k1, q1=q1),
                           eye_t, ob256, ob128)
        bblk, cblk, lam, lamc = _s5_params(s5_lam_re[l], s5_lam_im[l], s5_log_dt[l], s5_b_re[l], s5_b_im[l],
                                           s5_c_re[l], s5_c_im[l])
        s5o = _s5_mixer(s5u.reshape(bsz, seq, S5_WIDTH), bblk, cblk, lam, lamc, row(s5_d[l]),
                        s5_glu_w[l].astype(BF16))
        nao = _na_mixer(naq, nak, nav, _na_bias_table(na_rpb[l]))
        flat = lambda a: a.reshape(t, -1)
        x2, h2 = _out_proj(x2, flat(s5o), flat(yf), flat(yb), flat(extra), flat(gate), flat(bonus), flat(nao),
                           row(rwkv_ln_w[l]), row(rwkv_ln_b[l]), obm_f32, w_out[l].astype(BF16),
                           m(2), row(norm2_g[l]), m(3), m(4), seq)
        x2 = _moe_layer(x2, h2, m(5), router_w[l], router_b[l], exp_w1, b1_grp, exp_w2, exp_b2f, seq, l)
    return x2.reshape(bsz, seq, d)
```

```python
import functools
import math

import numpy as np
import jax
import jax.numpy as jnp
from jax import lax
from jax.experimental import pallas as pl
from jax.experimental.pallas import tpu as pltpu

F32 = jnp.float32
BF16 = jnp.bfloat16
HIGHEST = lax.Precision.HIGHEST

D_MODEL = 1024
S5_WIDTH = 256
S5_GROUP = 16
S5_GROUPS = 16
S5_STATE = 64
S5_CHUNK = 64
S5_FLAT = S5_GROUPS * S5_STATE
RW = 384
HEAD = 64
RWKV_GN_EPS = 64e-5
NA_W = 384
GRID_W = 64
NA_KH = 8
NA_KW = 16
N_EXPERTS = 32
TOP_K = 4
MOE_BLOCK = 256
SWIGLU_ALPHA = 1.702
SWIGLU_LIMIT = 7.0
RMS_EPS = 1e-6
LANES = 128
WKV_CHUNK = 64
VMEM_LIMIT = 56 * 1024 * 1024


def _cparams(sem):
    return pltpu.CompilerParams(dimension_semantics=sem, vmem_limit_bytes=VMEM_LIMIT)


def _seg_ones(n, seg=HEAD, dtype=F32, scale=1.0):
    idx = np.arange(n) // seg
    return jnp.asarray((idx[:, None] == idx[None, :]).astype(np.float32) * scale, dtype)


def _ada_kernel(c_ref, w_ref, b_ref, o_ref):
    c = c_ref[...]
    cond = c * jax.nn.sigmoid(c)
    o_ref[0] = jnp.dot(cond, w_ref[0], preferred_element_type=F32) + b_ref[0]


def _ada_mod(c, ada_w, ada_b):
    nl, d, n6 = ada_w.shape
    bsz = c.shape[0]
    tn = 1536
    return pl.pallas_call(
        _ada_kernel,
        grid=(nl, n6 // tn),
        in_specs=[pl.BlockSpec((bsz, d), lambda l, j: (0, 0)),
                  pl.BlockSpec((1, d, tn), lambda l, j: (l, 0, j)),
                  pl.BlockSpec((1, 1, tn), lambda l, j: (l, 0, j))],
        out_specs=pl.BlockSpec((1, bsz, tn), lambda l, j: (l, 0, j)),
        out_shape=jax.ShapeDtypeStruct((nl, bsz, n6), F32),
        compiler_params=_cparams(("arbitrary", "arbitrary")),
        name="ada_mod",
    )(c, ada_w, ada_b.reshape(nl, 1, n6))


def _rms_mod(x, g, shift, scale):
    ms = jnp.mean(x * x, axis=-1, keepdims=True)
    h = x * lax.rsqrt(ms + RMS_EPS) * g
    return h * (1.0 + scale) + shift


def _proj_kernel(x_ref, g_ref, sh_ref, sc_ref, w_ref, o_s5, o_rw, o_na):
    h = _rms_mod(x_ref[...], g_ref[...], sh_ref[0], sc_ref[0])
    p = jnp.dot(h.astype(BF16), w_ref[...], preferred_element_type=F32)
    o_s5[...] = p[:, :S5_WIDTH]
    o_rw[...] = p[:, S5_WIDTH:S5_WIDTH + 4 * RW]
    o_na[...] = p[:, S5_WIDTH + 4 * RW:]


def _in_proj(x2, g, shift, scale, w_bf, seq):
    t, d = x2.shape
    n = w_bf.shape[1]
    tm = 256
    per_b = seq // tm
    row = lambda i: (i, 0)
    bvec = lambda i: (i // per_b, 0, 0)
    return pl.pallas_call(
        _proj_kernel,
        grid=(t // tm,),
        in_specs=[pl.BlockSpec((tm, d), row),
                  pl.BlockSpec((1, d), lambda i: (0, 0)),
                  pl.BlockSpec((1, 1, d), bvec),
                  pl.BlockSpec((1, 1, d), bvec),
                  pl.BlockSpec((d, n), lambda i: (0, 0))],
        out_specs=[pl.BlockSpec((tm, S5_WIDTH), row),
                   pl.BlockSpec((tm, 4 * RW), row),
                   pl.BlockSpec((tm, 3 * NA_W), row)],
        out_shape=[jax.ShapeDtypeStruct((t, S5_WIDTH), F32),
                   jax.ShapeDtypeStruct((t, 4 * RW), F32),
                   jax.ShapeDtypeStruct((t, 3 * NA_W), F32)],
        compiler_params=_cparams(("arbitrary",)),
        name="in_proj",
    )(x2, g, shift, scale, w_bf)


def _softplus(x):
    return jnp.maximum(x, 0.0) + jnp.log(1.0 + jnp.exp(-jnp.abs(x)))


def _prep_kernel(xr_ref, prev_ref, next_ref, qkv_ref,
                 mu_ref, kk_ref, ka_ref, rk_ref, w0_ref, a0_ref,
                 w1_ref, w2_ref, a1_ref, a2_ref, g1_ref, g2_ref, qg_ref, kg_ref, ob_ref,
                 nkk_o, r_o, v_o, lw0_o, b0_o, k0_o, lw1_o, b1_o, k1_o,
                 gate_o, bonus_o, naq_o, nak_o, nav_o):
    i = pl.program_id(1)
    nblk = pl.num_programs(1)
    x = xr_ref[0]
    tm = x.shape[0]
    prow = jnp.where(i == 0, 0.0, prev_ref[0][7:8, :])
    nrow = jnp.where(i == nblk - 1, 0.0, next_ref[0][0:1, :])
    rid = lax.broadcasted_iota(jnp.int32, x.shape, 0)
    prev = jnp.where(rid == 0, prow, pltpu.roll(x, 1, axis=0))
    nxt = jnp.where(rid == tm - 1, nrow, pltpu.roll(x, tm - 1, axis=0))
    xs = x + (0.5 * (prev + nxt) - x) * mu_ref[...]
    r = xs[:, 0:RW]
    k = xs[:, RW:2 * RW]
    v = xs[:, 2 * RW:3 * RW]
    z = xs[:, 3 * RW:4 * RW]
    ob = ob_ref[...]
    seg = lambda t: jnp.dot(t, ob, precision=HIGHEST, preferred_element_type=F32)
    zb = z.astype(BF16)
    bdot = lambda a, w: jnp.dot(a.astype(BF16), w, preferred_element_type=F32)
    gate_o[0] = bdot(jax.nn.sigmoid(bdot(zb, g1_ref[...])), g2_ref[...])
    kk = k * kk_ref[...]
    kk = kk / jnp.maximum(jnp.sqrt(seg(kk * kk)), 1e-12)
    nkk_o[0] = -kk
    r_o[0] = r
    v_o[0] = v
    bonus_o[0] = seg(r * k * rk_ref[...]) * v
    outs = ((lw0_o, b0_o, k0_o), (lw1_o, b1_o, k1_o))
    for d in range(2):
        wl = w0_ref[d:d + 1, :] + bdot(jnp.tanh(bdot(zb, w1_ref[d])), w2_ref[d])
        w = -_softplus(-wl) - 0.5
        a = jax.nn.sigmoid(a0_ref[d:d + 1, :] + bdot(bdot(zb, a1_ref[d]), a2_ref[d]))
        lw_o, b_o, k_o = outs[d]
        lw_o[0] = -jnp.exp(w)
        b_o[0] = kk * a
        k_o[0] = k * (1.0 + (a - 1.0) * ka_ref[...])
    qkv = qkv_ref[0]
    segm = lambda t: seg(t) * (1.0 / HEAD)
    qn = qkv[:, 0:NA_W]
    kn = qkv[:, NA_W:2 * NA_W]
    naq_o[0] = (qn * lax.rsqrt(segm(qn * qn) + RMS_EPS) * qg_ref[...] * (HEAD ** -0.5)).astype(BF16)
    nak_o[0] = (kn * lax.rsqrt(segm(kn * kn) + RMS_EPS) * kg_ref[...]).astype(BF16)
    nav_o[0] = qkv[:, 2 * NA_W:].astype(BF16)


def _prep(xr, qkv, p):
    bsz, seq, _ = xr.shape
    tm = 256
    nb = seq // tm
    h8 = tm // 8
    blk = lambda w: pl.BlockSpec((1, tm, w), lambda b, i: (b, i, 0))
    full = lambda a: pl.BlockSpec(a.shape, lambda b, i, _n=a.ndim: (0,) * _n)
    params = [p["mu"], p["k_k"], p["k_a"], p["r_k"], p["w0"], p["a0"], p["w1"], p["w2"], p["a1"], p["a2"],
              p["g1"], p["g2"], p["q_g"], p["k_g"], p["ob"]]
    f32o = jax.ShapeDtypeStruct((bsz, seq, RW), F32)
    bfo = jax.ShapeDtypeStruct((bsz, seq, NA_W), BF16)
    return pl.pallas_call(
        _prep_kernel,
        grid=(bsz, nb),
        in_specs=[blk(4 * RW),
                  pl.BlockSpec((1, 8, 4 * RW), lambda b, i: (b, jnp.maximum(i * h8 - 1, 0), 0)),
                  pl.BlockSpec((1, 8, 4 * RW), lambda b, i: (b, jnp.minimum((i + 1) * h8, seq // 8 - 1), 0)),
                  blk(3 * NA_W)] + [full(a) for a in params],
        out_specs=[blk(RW)] * 14,
        out_shape=[f32o] * 11 + [bfo] * 3,
        compiler_params=_cparams(("arbitrary", "arbitrary")),
        name="mixer_prep",
    )(xr, xr, xr, qkv, *params)


HEAD_PAIR = LANES // HEAD
WKV_PAIRS = RW // LANES
WKV_DOUBLINGS = WKV_CHUNK.bit_length() - 2


def _nt_dot(a, b):
    return lax.dot_general(a, b, (((1,), (1,)), ((), ())), preferred_element_type=F32)


def _wkv_kernel(*refs):
    f_in = refs[0:6]
    b_in = refs[6:12]
    tri_ref, msk_ref, eye_ref = refs[12:15]
    yf_ref, yb_ref = refs[15:17]
    s_ref = refs[17]
    c = pl.program_id(0)
    bi = pl.program_id(1)
    tc = WKV_CHUNK

    @pl.when(c == 0)
    def _():
        s_ref[bi] = jnp.zeros(s_ref.shape[1:], F32)

    first_head = lax.broadcasted_iota(jnp.int32, (tc, LANES), 1) < HEAD
    eye_bf = eye_ref[...]
    eye_f = eye_bf.astype(F32)

    def blk(z):
        return jnp.concatenate([jnp.where(first_head, z, 0.0), jnp.where(first_head, 0.0, z)], axis=0)

    bdot = lambda p, q: jnp.dot(p, q, preferred_element_type=F32)
    units = [(d, p) for d in range(2) for p in range(WKV_PAIRS)]
    every = lambda fn, *cols: [fn(*args) for args in zip(*cols)]
    states = [s_ref[bi, d, p] for d, p in units]
    masks = [(msk_ref[d, 0] > 0.5, msk_ref[d, 1] > 0.5) for d in range(2)]

    def load(d, p):
        src = f_in if d == 0 else b_in
        return [s[0, :, p * LANES:(p + 1) * LANES] for s in src]

    def decays(unit, data):
        d = unit[0]
        lw = data[3]
        cum = jnp.dot(tri_ref[d], lw, precision=HIGHEST, preferred_element_type=F32)
        cend = cum[tc - 1:tc] if d == 0 else cum[0:1]
        return cum, cend

    def operands(data, dec):
        a, r, v, lw, bb, kk = data
        cum, cend = dec
        e_neg = jnp.exp(-cum)
        e_end = jnp.exp(cend - cum)
        x = jnp.concatenate([blk(a * jnp.exp(cum - lw)), blk(r * jnp.exp(cum))], axis=0).astype(BF16)
        y = jnp.concatenate([blk(bb * e_neg), blk(kk * e_neg)], axis=0).astype(BF16)
        z = jnp.concatenate([blk(bb * e_end), blk(kk * e_end)], axis=0).astype(BF16)
        return x, y, z, blk(v)

    def causal(unit, g):
        strict, incl = masks[unit[0]]
        return (jnp.where(strict, g[:2 * tc, :2 * tc], 0.0), jnp.where(strict, g[:2 * tc, 2 * tc:], 0.0),
                jnp.concatenate([jnp.where(incl, g[2 * tc:, :2 * tc], 0.0),
                                 jnp.where(incl, g[2 * tc:, 2 * tc:], 0.0)], axis=1).astype(BF16))

    data = every(load, *zip(*units))
    dec = every(decays, units, data)
    ops = every(operands, data, dec)
    grams = every(lambda o: _nt_dot(o[0], o[1]), ops)
    nmat = every(causal, units, grams)
    ph = every(lambda o, st: _nt_dot(o[0], st.astype(BF16)), ops, states)
    vbf = every(lambda o: o[3].astype(BF16), ops)
    rhs = every(lambda q, n, vb: q[:2 * tc] + bdot(n[1].astype(BF16), vb), ph, nmat, vbf)
    inv = every(lambda n: eye_f + n[0], nmat)
    pw = every(lambda n: n[0].astype(BF16), nmat)
    for _ in range(WKV_DOUBLINGS):
        pw = every(lambda q: bdot(q, q).astype(BF16), pw)
        inv = every(lambda t, q: t + bdot(t.astype(BF16), q), inv, pw)
    u = every(lambda t, q: bdot(t.astype(BF16), q.astype(BF16)), inv, rhs)
    uv = every(lambda q, vb: jnp.concatenate([q.astype(BF16), vb], axis=0), u, vbf)
    yo = every(lambda q, n, w: q[2 * tc:] + bdot(n[2], w), ph, nmat, uv)
    uvt = every(lambda w: _nt_dot(eye_bf, w).astype(BF16), uv)
    new = every(lambda st, dc, w, o: st * jnp.exp(dc[1]) + bdot(w, o[2]), states, dec, uvt, ops)
    for d in range(2):
        out = yf_ref if d == 0 else yb_ref
        rows = [yo[units.index((d, p))] for p in range(WKV_PAIRS)]
        out[0] = jnp.concatenate([q[:tc] + q[tc:] for q in rows], axis=1)
    for (d, p), st in zip(units, new):
        s_ref[bi, d, p] = st


def _wkv_masks():
    tc = WKV_CHUNK
    t = np.arange(tc)
    tri = np.stack([t[None, :] <= t[:, None], t[None, :] >= t[:, None]]).astype(np.float32)
    head = np.arange(HEAD_PAIR * tc) // tc
    tt = np.arange(HEAD_PAIR * tc) % tc
    same = head[:, None] == head[None, :]
    m = np.stack([np.stack([same & (tt[None, :] < tt[:, None]), same & (tt[None, :] <= tt[:, None])]),
                  np.stack([same & (tt[None, :] > tt[:, None]), same & (tt[None, :] >= tt[:, None])])])
    return jnp.asarray(tri), jnp.asarray(m.astype(np.float32)), jnp.asarray(np.eye(LANES, dtype=np.float32), BF16)


def _wkv_scan(ins):
    bsz, seq, _ = ins["nkk"].shape
    tc = WKV_CHUNK
    nc = seq // tc
    tri, msk, eye = _wkv_masks()
    fwd = pl.BlockSpec((1, tc, RW), lambda c, b: (b, c, 0))
    bwd = pl.BlockSpec((1, tc, RW), lambda c, b: (b, nc - 1 - c, 0))
    full = lambda a: pl.BlockSpec(a.shape, lambda c, b, _n=a.ndim: (0,) * _n)
    f_args = [ins["nkk"], ins["r"], ins["v"], ins["lw0"], ins["b0"], ins["k0"]]
    b_args = [ins["nkk"], ins["r"], ins["v"], ins["lw1"], ins["b1"], ins["k1"]]
    o = jax.ShapeDtypeStruct((bsz, seq, RW), F32)
    return pl.pallas_call(
        _wkv_kernel,
        grid=(nc, bsz),
        in_specs=[fwd] * 6 + [bwd] * 6 + [full(tri), full(msk), full(eye)],
        out_specs=[fwd, bwd],
        out_shape=[o, o],
        scratch_shapes=[pltpu.VMEM((bsz, 2, WKV_PAIRS, LANES, LANES), F32)],
        compiler_params=_cparams(("arbitrary", "arbitrary")),
        name="wkv_scan",
    )(*f_args, *b_args, tri, msk, eye)


def _gelu_tanh(x):
    return 0.5 * x * (1.0 + jnp.tanh(math.sqrt(2.0 / math.pi) * (x + 0.044715 * (x * x * x))))


def _s5_kernel(ua_ref, ub_ref, bblk_ref, cblk_ref, lam_ref, lamc_ref, d_ref, glu_ref, o_ref,
               y_ref, st_ref, end_ref, carry_ref, *, seq):
    ch = S5_CHUNK
    nc = seq // ch
    n = S5_FLAT
    u_halves = (ua_ref, ub_ref)
    for hf in range(2):
        y_ref[hf] = u_halves[hf][0] * d_ref[:, hf * LANES:(hf + 1) * LANES]

    def cmul_add(lre, lim, s, add):
        sre = s[:, :n]
        sim = s[:, n:]
        return jnp.concatenate([lre * sre - lim * sim + add[:, :n],
                                lre * sim + lim * sre + add[:, n:]], axis=1)

    for d in range(2):
        lre = lam_ref[d, 0:1, :]
        lim = lam_ref[d, 1:2, :]
        lcre = lamc_ref[d, 0:1, :]
        lcim = lamc_ref[d, 1:2, :]
        tloc = (lambda i: i) if d == 0 else (lambda i: ch - 1 - i)
        cloc = (lambda i: i) if d == 0 else (lambda i: nc - 1 - i)

        def advance(tl):
            rows = jnp.concatenate([r[0, pl.ds(tl, nc, stride=ch), :] for r in u_halves], axis=1)
            bu = jnp.dot(rows.astype(BF16), bblk_ref[d], preferred_element_type=F32)
            st_ref[...] = cmul_add(lre, lim, st_ref[...], bu)

        st_ref[...] = jnp.zeros_like(st_ref)

        def p1(i, c):
            advance(tloc(i))
            return c

        lax.fori_loop(0, ch, p1, 0)
        end_ref[...] = st_ref[...]

        def cs(i, car):
            c = cloc(i)
            carry_ref[pl.ds(c, 1), :] = car
            return cmul_add(lcre, lcim, car, end_ref[pl.ds(c, 1), :])

        lax.fori_loop(0, nc, cs, jnp.zeros((1, 2 * n), F32))

        st_ref[...] = carry_ref[...]

        def p2(i, c):
            tl = tloc(i)
            advance(tl)
            yr = jnp.dot(st_ref[...].astype(BF16), cblk_ref[d], preferred_element_type=F32)
            idx = pl.ds(tl, nc, stride=ch)
            for hf in range(2):
                y_ref[hf, idx, :] = y_ref[hf, idx, :] + yr[:, hf * LANES:(hf + 1) * LANES]
            return c

        lax.fori_loop(0, ch, p2, 0)

    g = _gelu_tanh(jnp.concatenate([y_ref[0], y_ref[1]], axis=1))
    o_ref[0] = g * jax.nn.sigmoid(jnp.dot(g.astype(BF16), glu_ref[...], preferred_element_type=F32))


def _s5_params(lam_re, lam_im, log_dt, b_re, b_im, c_re, c_im):
    lre = lam_re.astype(F32)
    lim = lam_im.astype(F32)
    dt = jnp.exp(log_dt.astype(F32))[..., None]

    def cexp(scale):
        mag = jnp.exp(lre * dt * scale)
        return mag * jnp.cos(lim * dt * scale), mag * jnp.sin(lim * dt * scale)

    bar_re, bar_im = cexp(1.0)
    den = lre * lre + lim * lim
    f_re = ((bar_re - 1.0) * lre + bar_im * lim) / den
    f_im = (bar_im * lre - (bar_re - 1.0) * lim) / den
    bm_re = b_re.astype(F32)
    bm_im = b_im.astype(F32)
    bb_re = f_re[..., None] * bm_re - f_im[..., None] * bm_im
    bb_im = f_re[..., None] * bm_im + f_im[..., None] * bm_re
    eye_g = jnp.eye(S5_GROUPS, dtype=F32)

    def blockdiag_in(m):
        return jnp.einsum("dgph,gk->dghkp", m, eye_g).reshape(2, S5_WIDTH, S5_FLAT)

    def blockdiag_out(m):
        return jnp.einsum("dghp,gk->dgpkh", m, eye_g).reshape(2, S5_FLAT, S5_WIDTH)

    bblk = jnp.concatenate([blockdiag_in(bb_re), blockdiag_in(bb_im)], axis=2)
    cblk = jnp.concatenate([blockdiag_out(c_re.astype(F32)), -blockdiag_out(c_im.astype(F32))], axis=1)
    flat = lambda z: jnp.stack([z[0].reshape(2, S5_FLAT), z[1].reshape(2, S5_FLAT)], axis=1)
    return bblk.astype(BF16), cblk.astype(BF16), flat((bar_re, bar_im)), flat(cexp(float(S5_CHUNK)))


def _s5_mixer(u, bblk, cblk, lam, lamc, d_skip, glu_bf):
    bsz, seq, w = u.shape
    nc = seq // S5_CHUNK
    full = lambda a: pl.BlockSpec(a.shape, lambda b, _n=a.ndim: (0,) * _n)
    args = [bblk, cblk, lam, lamc, d_skip, glu_bf]
    return pl.pallas_call(
        functools.partial(_s5_kernel, seq=seq),
        grid=(bsz,),
        in_specs=[pl.BlockSpec((1, seq, LANES), lambda b: (b, 0, 0)),
                  pl.BlockSpec((1, seq, LANES), lambda b: (b, 0, 1))] + [full(a) for a in args],
        out_specs=pl.BlockSpec((1, seq, w), lambda b: (b, 0, 0)),
        out_shape=jax.ShapeDtypeStruct((bsz, seq, w), F32),
        scratch_shapes=[pltpu.VMEM((w // LANES, seq, LANES), F32),
                        pltpu.VMEM((nc, 2 * S5_FLAT), F32),
                        pltpu.VMEM((nc, 2 * S5_FLAT), F32),
                        pltpu.VMEM((nc, 2 * S5_FLAT), F32)],
        compiler_params=_cparams(("arbitrary",)),
        name="s5_mixer",
    )(u, u, *args)


def _na_bias_table(rpb):
    q_col = np.arange(GRID_W)
    c_start = np.clip(q_col - NA_KW // 2, 0, GRID_W - NA_KW)
    k_col = np.arange(GRID_W)
    valid = (k_col[None, :] >= c_start[:, None]) & (k_col[None, :] < c_start[:, None] + NA_KW)
    dx = np.clip(k_col[None, :] - q_col[:, None] + NA_KW - 1, 0, 2 * NA_KW - 2)
    pick = (np.arange(2 * NA_KW - 1)[:, None, None] == dx[None]).astype(np.float32)
    base = jnp.einsum("hyd,dqk->hyqk", rpb.astype(F32), jnp.asarray(pick), precision=HIGHEST)
    base = jnp.where(jnp.asarray(valid)[None, None], base, -jnp.inf)
    tab = jnp.stack([base[:, NA_KH - 1 - o:2 * NA_KH - 1 - o] for o in range(NA_KH)], axis=1)
    tab = jnp.transpose(tab, (0, 1, 3, 2, 4))
    return tab.reshape(rpb.shape[0], NA_KH, GRID_W, NA_KH * GRID_W)


def _na_kernel(q_ref, k_ref, v_ref, bias_ref, o_ref, *, rows, rblk):
    rb = pl.program_id(1)
    lane = lax.broadcasted_iota(jnp.int32, (GRID_W, LANES), 1)
    low = lane < HEAD

    def row(j, carry):
        r = rb * rblk + j
        rs = jnp.clip(r - NA_KH // 2, 0, rows - NA_KH)
        off = r - rs
        q = q_ref[0, j]
        kmat = k_ref[0, pl.ds(rs, NA_KH)].reshape(NA_KH * GRID_W, NA_W)
        vmat = v_ref[0, pl.ds(rs, NA_KH)].reshape(NA_KH * GRID_W, NA_W)
        outs = []
        for c in range(NA_W // LANES):
            sl = slice(c * LANES, (c + 1) * LANES)
            q2 = q[:, sl].astype(F32)
            lhs = jnp.concatenate([jnp.where(low, q2, 0.0), jnp.where(low, 0.0, q2)], axis=0).astype(BF16)
            s = lax.dot_general(lhs, kmat[:, sl], (((1,), (1,)), ((), ())), preferred_element_type=F32)
            s = s + jnp.concatenate([bias_ref[2 * c, off], bias_ref[2 * c + 1, off]], axis=0)
            m = jnp.max(s, axis=-1, keepdims=True)
            p = jnp.exp(s - m)
            l = jnp.sum(p, axis=-1, keepdims=True)
            o = jnp.dot(p.astype(BF16), vmat[:, sl], preferred_element_type=F32) / l
            outs.append(jnp.where(low, o[:GRID_W], o[GRID_W:]))
        o_ref[0, j] = jnp.concatenate(outs, axis=1)
        return carry

    lax.fori_loop(0, rblk, row, 0)


def _na_mixer(q, k, v, bias):
    bsz, seq, w = q.shape
    rows = seq // GRID_W
    rblk = 8
    g4 = lambda a: a.reshape(bsz, rows, GRID_W, w)
    img = pl.BlockSpec((1, rows, GRID_W, w), lambda b, i: (b, 0, 0, 0))
    blk = pl.BlockSpec((1, rblk, GRID_W, w), lambda b, i: (b, i, 0, 0))
    out = pl.pallas_call(
        functools.partial(_na_kernel, rows=rows, rblk=rblk),
        grid=(bsz, rows // rblk),
        in_specs=[blk, img, img, pl.BlockSpec(bias.shape, lambda b, i: (0, 0, 0, 0))],
        out_specs=blk,
        out_shape=jax.ShapeDtypeStruct((bsz, rows, GRID_W, w), F32),
        compiler_params=_cparams(("arbitrary", "arbitrary")),
        name="na_mixer",
    )(g4(q), g4(k), g4(v), bias)
    return out.reshape(bsz, seq, w)


def _outproj_kernel(x_ref, s5_ref, yf_ref, yb_ref, gate_ref, bonus_ref, na_ref,
                    lnw_ref, lnb_ref, obm_ref, w_ref, gm_ref, g2_ref, sh_ref, sc_ref,
                    xo_ref, h_ref):
    obm = obm_ref[...]
    segm = lambda t: jnp.dot(t, obm, precision=HIGHEST, preferred_element_type=F32)
    y = yf_ref[...] + yb_ref[...]
    yc = y - segm(y)
    yn = yc * lax.rsqrt(segm(yc * yc) + RWKV_GN_EPS) * lnw_ref[...] + lnb_ref[...]
    rw = (yn + bonus_ref[...]) * gate_ref[...]
    mixed = jnp.concatenate([s5_ref[...], rw, na_ref[...]], axis=1).astype(BF16)
    xo = x_ref[...] + gm_ref[0] * jnp.dot(mixed, w_ref[...], preferred_element_type=F32)
    xo_ref[...] = xo
    h_ref[...] = _rms_mod(xo, g2_ref[...], sh_ref[0], sc_ref[0])


def _out_proj(x2, s5o, yf, yb, gate, bonus, nao, lnw, lnb, obm, w_bf, gate_mix, g2, shift, scale, seq):
    t, d = x2.shape
    tm = 256
    per_b = seq // tm
    row = lambda w: pl.BlockSpec((tm, w), lambda i: (i, 0))
    full = lambda a: pl.BlockSpec(a.shape, lambda i, _n=a.ndim: (0,) * _n)
    bvec = pl.BlockSpec((1, 1, d), lambda i: (i // per_b, 0, 0))
    o = jax.ShapeDtypeStruct((t, d), F32)
    return pl.pallas_call(
        _outproj_kernel,
        grid=(t // tm,),
        in_specs=[row(d), row(S5_WIDTH)] + [row(RW)] * 5 +
                 [full(lnw), full(lnb), full(obm), full(w_bf), bvec, full(g2), bvec, bvec],
        out_specs=[row(d), row(d)],
        out_shape=[o, o],
        compiler_params=_cparams(("arbitrary",)),
        name="out_proj",
    )(x2, s5o, yf, yb, gate, bonus, nao, lnw, lnb, obm, w_bf, gate_mix, g2, shift, scale)


MOE_TILE = 256
SEG_ALIGN = 8
MOE_SLOTS = -(-(MOE_TILE * TOP_K + N_EXPERTS * (SEG_ALIGN - 1)) // LANES) * LANES


def _router_kernel(h_ref, w_ref, b_ref, tri_ref, upper_ref, slot_ref, gate_ref, cnt_ref, base_ref, loc_ref,
                   carry_ref):
    @pl.when(pl.program_id(0) == 0)
    def _():
        carry_ref[...] = jnp.zeros_like(carry_ref)

    logits = jnp.dot(h_ref[...], w_ref[...], precision=HIGHEST, preferred_element_type=F32) + b_ref[...]
    tm = logits.shape[0]
    lane = lax.broadcasted_iota(jnp.int32, (tm, LANES), 1)
    vals, hots = [], []
    cur = logits
    for _ in range(TOP_K):
        m = jnp.max(cur, axis=-1, keepdims=True)
        idx = jnp.min(jnp.where(cur == m, lane, LANES), axis=-1, keepdims=True)
        hot = lane == idx
        vals.append(m)
        hots.append(hot)
        cur = jnp.where(hot, -jnp.inf, cur)
    exps = [jnp.exp(v - vals[0]) for v in vals]
    den = exps[0] + exps[1] + exps[2] + exps[3]
    assign = sum(h.astype(F32) for h in hots)
    before = jnp.dot(tri_ref[...], assign.astype(BF16), preferred_element_type=F32)
    cnt = jnp.sum(assign, axis=0, keepdims=True)
    cnt = jnp.floor((cnt + (SEG_ALIGN - 1)) * (1.0 / SEG_ALIGN)) * SEG_ALIGN
    cnt8 = jnp.broadcast_to(cnt, (8, LANES)).astype(BF16)
    loc = jnp.dot(cnt8, upper_ref[...], preferred_element_type=F32)[0:1, :]
    place = before + loc
    s_out = jnp.zeros((tm, LANES), jnp.int32)
    g_out = jnp.zeros((tm, LANES), F32)
    for kk in range(TOP_K):
        slot = jnp.sum(jnp.where(hots[kk], place, 0.0), axis=-1, keepdims=True)
        sel = lane == kk
        s_out = jnp.where(sel, slot.astype(jnp.int32), s_out)
        g_out = jnp.where(sel, exps[kk] / den, g_out)
    slot_ref[...] = s_out
    gate_ref[...] = g_out
    cnt_ref[0] = cnt.astype(jnp.int32)
    base_ref[0] = carry_ref[...].astype(jnp.int32)
    loc_ref[0] = loc.astype(jnp.int32)
    carry_ref[...] = carry_ref[...] + cnt


def _router(h2, rw_pad, rb_pad):
    t, d = h2.shape
    tm = MOE_TILE
    nt = t // tm
    tri = jnp.asarray(np.tril(np.ones((tm, tm), np.float32), -1), BF16)
    upper = jnp.asarray(np.triu(np.ones((LANES, LANES), np.float32), 1), BF16)
    row = pl.BlockSpec((tm, LANES), lambda i: (i, 0))
    per_tile = pl.BlockSpec((1, 1, LANES), lambda i: (i, 0, 0))
    full = lambda a: pl.BlockSpec(a.shape, lambda i, _n=a.ndim: (0,) * _n)
    tile_i32 = jax.ShapeDtypeStruct((nt, 1, LANES), jnp.int32)
    return pl.pallas_call(
        _router_kernel,
        grid=(nt,),
        in_specs=[pl.BlockSpec((tm, d), lambda i: (i, 0)), full(rw_pad), full(rb_pad), full(tri), full(upper)],
        out_specs=[row, row, per_tile, per_tile, per_tile],
        out_shape=[jax.ShapeDtypeStruct((t, LANES), jnp.int32),
                   jax.ShapeDtypeStruct((t, LANES), F32),
                   tile_i32, tile_i32, tile_i32],
        scratch_shapes=[pltpu.VMEM((1, LANES), F32)],
        compiler_params=_cparams(("arbitrary",)),
        name="moe_router",
    )(h2, rw_pad, rb_pad, tri, upper)


SEG_PIECES = tuple(SEG_ALIGN << s for s in range((MOE_TILE // SEG_ALIGN).bit_length()))


def _segment_dmas(cnt_ref, loc_ref, row_ref, ne, make_copy, wait):
    tile = pl.program_id(0)

    def per_expert(e, carry):
        i = tile * ne + e
        n = cnt_ref[i]
        off = loc_ref[i]
        row = row_ref[i]
        for p in SEG_PIECES:
            has = (n & p) != 0

            @pl.when(has)
            def _(off=off, row=row, p=p):
                cp = make_copy(pl.multiple_of(off, SEG_ALIGN), pl.multiple_of(row, SEG_ALIGN), p)
                if wait:
                    cp.wait()
                else:
                    cp.start()

            step = jnp.where(has, p, 0)
            off = off + step
            row = row + step
        return carry

    lax.fori_loop(0, ne, per_expert, 0)


def _dispatch_kernel(cnt_ref, loc_ref, row_ref, slot_ref, h_ref, xb_in_ref, xb_ref, sorted_ref, sem, *, ne):
    del xb_in_ref
    tm = h_ref.shape[0]
    ns = MOE_SLOTS
    slot_t = jnp.transpose(slot_ref[...].astype(F32))
    srow = lax.broadcasted_iota(jnp.int32, (ns, tm), 0).astype(F32)
    pick = jnp.zeros((ns, tm), F32)
    for kk in range(TOP_K):
        pick = jnp.where(srow == slot_t[kk:kk + 1, :], 1.0, pick)
    sorted_ref[...] = jnp.dot(pick.astype(BF16), h_ref[...].astype(BF16), preferred_element_type=F32)

    def copy(off, row, p):
        return pltpu.make_async_copy(sorted_ref.at[pl.ds(off, p), :], xb_ref.at[pl.ds(row, p), :], sem)

    _segment_dmas(cnt_ref, loc_ref, row_ref, ne, copy, wait=False)
    _segment_dmas(cnt_ref, loc_ref, row_ref, ne, copy, wait=True)


def _dispatch(cnt, loc, rowstart, slot, h2, n_rows, ne):
    t, d = h2.shape
    tm = MOE_TILE
    zeros = jnp.zeros((n_rows, d), F32)
    grid_spec = pltpu.PrefetchScalarGridSpec(
        num_scalar_prefetch=3,
        grid=(t // tm,),
        in_specs=[pl.BlockSpec((tm, LANES), lambda i, *_: (i, 0)),
                  pl.BlockSpec((tm, d), lambda i, *_: (i, 0)),
                  pl.BlockSpec(memory_space=pl.ANY)],
        out_specs=pl.BlockSpec(memory_space=pl.ANY),
        scratch_shapes=[pltpu.VMEM((MOE_SLOTS, d), F32), pltpu.SemaphoreType.DMA(())],
    )
    return pl.pallas_call(
        functools.partial(_dispatch_kernel, ne=ne),
        grid_spec=grid_spec,
        out_shape=jax.ShapeDtypeStruct((n_rows, d), F32),
        input_output_aliases={5: 0},
        compiler_params=_cparams(("arbitrary",)),
        name="moe_dispatch",
    )(cnt, loc, rowstart, slot, h2, zeros)


PAIR_GROUP = 2 * LANES


def _pair_perm():
    p = np.zeros((PAIR_GROUP, PAIR_GROUP), np.float32)
    j = np.arange(LANES)
    p[2 * j, j] = 1.0
    p[2 * j + 1, LANES + j] = 1.0
    return jnp.asarray(p, BF16)


def _expert_kernel(be_ref, nu_ref, x_ref, w1_ref, b1_ref, w2_ref, b2_ref, perm_ref, y_ref, w1s_ref, w2s_ref):
    i = pl.program_id(0)
    f2 = w1_ref.shape[2]
    ngrp = f2 // PAIR_GROUP

    @pl.when(i >= nu_ref[0])
    def _():
        y_ref[...] = jnp.zeros_like(y_ref)

    @pl.when(i < nu_ref[0])
    def _():
        @pl.when((i == 0) | (be_ref[i] != be_ref[jnp.maximum(i - 1, 0)]))
        def _():
            for g in range(ngrp):
                sl = slice(g * PAIR_GROUP, (g + 1) * PAIR_GROUP)
                w1s_ref[:, sl] = jnp.dot(w1_ref[0, :, sl].astype(BF16), perm_ref[...],
                                         preferred_element_type=F32).astype(BF16)
            w2s_ref[...] = w2_ref[0].astype(BF16)

        hdn = jnp.dot(x_ref[...].astype(BF16), w1s_ref[...], preferred_element_type=F32) + b1_ref[0]
        glu = jnp.concatenate([hdn[:, g * PAIR_GROUP:g * PAIR_GROUP + LANES] for g in range(ngrp)], axis=1)
        lin = jnp.concatenate([hdn[:, g * PAIR_GROUP + LANES:(g + 1) * PAIR_GROUP] for g in range(ngrp)], axis=1)
        glu = jnp.minimum(glu, SWIGLU_LIMIT)
        lin = jnp.clip(lin, -SWIGLU_LIMIT, SWIGLU_LIMIT)
        act = glu * jax.nn.sigmoid(SWIGLU_ALPHA * glu) * (lin + 1.0)
        y_ref[...] = jnp.dot(act.astype(BF16), w2s_ref[...], preferred_element_type=F32) + b2_ref[0]


def _experts(block_e, n_used, xb, w1, b1_grp, w2, b2, layer):
    n_rows, d = xb.shape
    _, ne, _, f2 = w1.shape
    dff = w2.shape[2]
    nblk = n_rows // MOE_BLOCK
    perm = _pair_perm()
    blk = lambda i, be, nu: (jnp.minimum(i, nu[0] - 1), 0)
    wsel = lambda i, be, nu: (layer, be[i], 0, 0)
    grid_spec = pltpu.PrefetchScalarGridSpec(
        num_scalar_prefetch=2,
        grid=(nblk,),
        in_specs=[pl.BlockSpec((MOE_BLOCK, d), blk),
                  pl.BlockSpec((None, 1, d, f2), wsel),
                  pl.BlockSpec((None, 1, 1, f2), wsel),
                  pl.BlockSpec((None, 1, dff, d), wsel),
                  pl.BlockSpec((None, 1, 1, d), wsel),
                  pl.BlockSpec(perm.shape, lambda i, be, nu: (0, 0))],
        out_specs=pl.BlockSpec((MOE_BLOCK, d), lambda i, be, nu: (i, 0)),
        scratch_shapes=[pltpu.VMEM((d, f2), BF16), pltpu.VMEM((dff, d), BF16)],
    )
    nl = w1.shape[0]
    return pl.pallas_call(
        _expert_kernel,
        grid_spec=grid_spec,
        out_shape=jax.ShapeDtypeStruct((n_rows, d), F32),
        compiler_params=_cparams(("arbitrary",)),
        name="moe_experts",
    )(block_e, n_used, xb, w1, b1_grp.reshape(nl, ne, 1, f2), w2, b2.reshape(nl, ne, 1, d), perm)


def _combine_kernel(cnt_ref, loc_ref, row_ref, slot_ref, gates_ref, x_ref, gf_ref, yb_ref, o_ref, sorted_ref, sem,
                    *, ne):
    tm = x_ref.shape[0]
    ns = MOE_SLOTS

    def copy(off, row, p):
        return pltpu.make_async_copy(yb_ref.at[pl.ds(row, p), :], sorted_ref.at[pl.ds(off, p), :], sem)

    @pl.when(pl.program_id(0) == 0)
    def _():
        sorted_ref[...] = jnp.zeros_like(sorted_ref)

    _segment_dmas(cnt_ref, loc_ref, row_ref, ne, copy, wait=False)
    slot = slot_ref[...]
    gates = gates_ref[...]
    scol = lax.broadcasted_iota(jnp.int32, (tm, ns), 1)
    gmat = jnp.zeros((tm, ns), F32)
    for kk in range(TOP_K):
        gmat = jnp.where(scol == slot[:, kk:kk + 1], gates[:, kk:kk + 1], gmat)
    _segment_dmas(cnt_ref, loc_ref, row_ref, ne, copy, wait=True)
    acc = jnp.dot(gmat.astype(BF16), sorted_ref[...].astype(BF16), preferred_element_type=F32)
    o_ref[...] = x_ref[...] + gf_ref[0] * acc


def _combine(cnt, loc, rowstart, slot, gates, x2, gate_ffn, yb, seq, ne):
    t, d = x2.shape
    tm = MOE_TILE
    per_b = seq // tm
    grid_spec = pltpu.PrefetchScalarGridSpec(
        num_scalar_prefetch=3,
        grid=(t // tm,),
        in_specs=[pl.BlockSpec((tm, LANES), lambda i, *_: (i, 0)),
                  pl.BlockSpec((tm, LANES), lambda i, *_: (i, 0)),
                  pl.BlockSpec((tm, d), lambda i, *_: (i, 0)),
                  pl.BlockSpec((1, 1, d), lambda i, *_: (i // per_b, 0, 0)),
                  pl.BlockSpec(memory_space=pl.ANY)],
        out_specs=pl.BlockSpec((tm, d), lambda i, *_: (i, 0)),
        scratch_shapes=[pltpu.VMEM((MOE_SLOTS, d), F32), pltpu.SemaphoreType.DMA(())],
    )
    return pl.pallas_call(
        functools.partial(_combine_kernel, ne=ne),
        grid_spec=grid_spec,
        out_shape=jax.ShapeDtypeStruct((t, d), F32),
        compiler_params=_cparams(("arbitrary",)),
        name="moe_combine",
    )(cnt, loc, rowstart, slot, gates, x2, gate_ffn, yb)


def _group_pairs(b1):
    lead = b1.shape[:-1]
    g = b1.reshape(lead + (-1, LANES, 2))
    return jnp.swapaxes(g, -1, -2).reshape(b1.shape)


def _moe_layer(x2, h2, gate_ffn, router_w, router_b, w1, b1_grp, w2, b2, seq, layer):
    t, d = x2.shape
    ne = router_w.shape[1]
    rw_pad = jnp.zeros((d, LANES), F32).at[:, :ne].set(router_w.astype(F32))
    rb_pad = jnp.full((1, LANES), -jnp.inf, F32).at[0, :ne].set(router_b.astype(F32))
    slot, gates, cnt3, base3, loc3 = _router(h2, rw_pad, rb_pad)
    n_assign = t * TOP_K
    n_tiles = t // MOE_TILE
    n_blocks = -(-(n_assign + n_tiles * ne * (SEG_ALIGN - 1)) // MOE_BLOCK) + ne
    cnt = cnt3[:, 0, :ne]
    base = base3[:, 0, :ne]
    loc = loc3[:, 0, :ne]
    total = base[-1] + cnt[-1]
    padded = ((total + MOE_BLOCK - 1) // MOE_BLOCK) * MOE_BLOCK
    pad_end = jnp.cumsum(padded)
    pad_start = pad_end - padded
    rowstart = (pad_start[None, :] + base).reshape(-1).astype(jnp.int32)
    n_used = (pad_end[-1] // MOE_BLOCK).astype(jnp.int32)
    block_start = jnp.minimum(jnp.arange(n_blocks, dtype=jnp.int32), n_used - 1) * MOE_BLOCK
    block_e = jnp.minimum(jnp.sum(block_start[:, None] >= pad_end[None, :], axis=-1), ne - 1).astype(jnp.int32)
    cnt_f = cnt.reshape(-1)
    loc_f = loc.reshape(-1)
    xb = _dispatch(cnt_f, loc_f, rowstart, slot, h2, n_blocks * MOE_BLOCK, ne)
    yb = _experts(block_e, n_used.reshape(1), xb, w1, b1_grp, w2, b2, layer)
    return _combine(cnt_f, loc_f, rowstart, slot, gates, x2, gate_ffn, yb, seq, ne)


def kernel(x, c, ada_w, ada_b, norm1_g, norm2_g, w_in, w_out, s5_lam_re, s5_lam_im, s5_log_dt, s5_b_re, s5_b_im, s5_c_re, s5_c_im, s5_d, s5_glu_w, rwkv_mu, rwkv_w0, rwkv_w1, rwkv_w2, rwkv_a0, rwkv_a1, rwkv_a2, rwkv_g1, rwkv_g2, rwkv_k_k, rwkv_k_a, rwkv_r_k, rwkv_ln_w, rwkv_ln_b, na_q_g, na_k_g, na_rpb, router_w, router_b, exp_w1, exp_b1, exp_w2, exp_b2):
    bsz, seq, d = x.shape
    depth = ada_w.shape[0]
    t = bsz * seq
    mod = _ada_mod(c, ada_w, ada_b).reshape(depth, bsz, 6, 1, d)
    x2 = x.reshape(t, d)
    ob_f32 = _seg_ones(RW)
    obm_f32 = _seg_ones(RW, scale=1.0 / HEAD)
    row = lambda a: a.reshape(1, -1).astype(F32)
    b1_grp = _group_pairs(exp_b1.astype(F32))
    exp_b2f = exp_b2.astype(F32)
    for l in range(depth):
        m = lambda j: mod[l, :, j]
        s5u, xr, qkv = _in_proj(x2, row(norm1_g[l]), m(0), m(1), w_in[l].astype(BF16), seq)
        prep_params = dict(
            mu=row(rwkv_mu[l]), k_k=row(rwkv_k_k[l]), k_a=row(rwkv_k_a[l]), r_k=row(rwkv_r_k[l]),
            w0=rwkv_w0[l].astype(F32), a0=rwkv_a0[l].astype(F32),
            w1=rwkv_w1[l].astype(BF16), w2=rwkv_w2[l].astype(BF16),
            a1=rwkv_a1[l].astype(BF16), a2=rwkv_a2[l].astype(BF16),
            g1=rwkv_g1[l].astype(BF16), g2=rwkv_g2[l].astype(BF16),
            q_g=row(jnp.tile(na_q_g[l], NA_W // HEAD)), k_g=row(jnp.tile(na_k_g[l], NA_W // HEAD)), ob=ob_f32)
        (nkk, r, v, lw0, b0, k0, lw1, b1, k1, gate, bonus, naq, nak, nav) = _prep(
            xr.reshape(bsz, seq, 4 * RW), qkv.reshape(bsz, seq, 3 * NA_W), prep_params)
        yf, yb = _wkv_scan(dict(nkk=nkk, r=r, v=v, lw0=lw0, b0=b0, k0=k0, lw1=lw1, b1=b1, k1=k1))
        bblk, cblk, lam, lamc = _s5_params(s5_lam_re[l], s5_lam_im[l], s5_log_dt[l], s5_b_re[l], s5_b_im[l],
                                           s5_c_re[l], s5_c_im[l])
        s5o = _s5_mixer(s5u.reshape(bsz, seq, S5_WIDTH), bblk, cblk, lam, lamc, row(s5_d[l]),
                        s5_glu_w[l].astype(BF16))
        nao = _na_mixer(naq, nak, nav, _na_bias_table(na_rpb[l]))
        flat = lambda a: a.reshape(t, -1)
        x2, h2 = _out_proj(x2, flat(s5o), flat(yf), flat(yb), flat(gate), flat(bonus), flat(nao),
                           row(rwkv_ln_w[l]), row(rwkv_ln_b[l]), obm_f32, w_out[l].astype(BF16),
                           m(2), row(norm2_g[l]), m(3), m(4), seq)
        x2 = _moe_layer(x2, h2, m(5), router_w[l], router_b[l], exp_w1, b1_grp, exp_w2, exp_b2f, seq, l)
    return x2.reshape(bsz, seq, d)
```

```python
import functools
import math

import numpy as np
import jax
import jax.numpy as jnp
from jax import lax
from jax.experimental import pallas as pl
from jax.experimental.pallas import tpu as pltpu

F32 = jnp.float32
BF16 = jnp.bfloat16
HIGHEST = lax.Precision.HIGHEST

D_MODEL = 1024
S5_WIDTH = 256
S5_GROUP = 16
S5_GROUPS = 16
S5_STATE = 64
S5_CHUNK = 64
S5_FLAT = S5_GROUPS * S5_STATE
RW = 384
HEAD = 64
RWKV_GN_EPS = 64e-5
NA_W = 384
GRID_W = 64
NA_KH = 8
NA_KW = 16
N_EXPERTS = 32
TOP_K = 4
MOE_BLOCK = 256
SWIGLU_ALPHA = 1.702
SWIGLU_LIMIT = 7.0
RMS_EPS = 1e-6
LANES = 128
WKV_CHUNK = 64
VMEM_LIMIT = 56 * 1024 * 1024


def _cparams(sem):
    return pltpu.CompilerParams(dimension_semantics=sem, vmem_limit_bytes=VMEM_LIMIT)


def _seg_ones(n, seg=HEAD, dtype=F32, scale=1.0):
    idx = np.arange(n) // seg
    return jnp.asarray((idx[:, None] == idx[None, :]).astype(np.float32) * scale, dtype)


def _seg_sum(t, ones_bf):
    hi = t.astype(BF16)
    lo = (t - hi.astype(F32)).astype(BF16)
    return (jnp.dot(hi, ones_bf, preferred_element_type=F32) + jnp.dot(lo, ones_bf, preferred_element_type=F32))


def _ada_kernel(c_ref, w_ref, b_ref, o_ref):
    c = c_ref[...]
    cond = c * jax.nn.sigmoid(c)
    o_ref[0] = jnp.dot(cond, w_ref[0], preferred_element_type=F32) + b_ref[0]


def _ada_mod(c, ada_w, ada_b):
    nl, d, n6 = ada_w.shape
    bsz = c.shape[0]
    tn = 1536
    return pl.pallas_call(
        _ada_kernel,
        grid=(nl, n6 // tn),
        in_specs=[pl.BlockSpec((bsz, d), lambda l, j: (0, 0)),
                  pl.BlockSpec((1, d, tn), lambda l, j: (l, 0, j)),
                  pl.BlockSpec((1, 1, tn), lambda l, j: (l, 0, j))],
        out_specs=pl.BlockSpec((1, bsz, tn), lambda l, j: (l, 0, j)),
        out_shape=jax.ShapeDtypeStruct((nl, bsz, n6), F32),
        compiler_params=_cparams(("arbitrary", "arbitrary")),
        name="ada_mod",
    )(c, ada_w, ada_b.reshape(nl, 1, n6))


def _rms_mod(x, g, shift, scale):
    ms = jnp.mean(x * x, axis=-1, keepdims=True)
    h = x * lax.rsqrt(ms + RMS_EPS) * g
    return h * (1.0 + scale) + shift


def _proj_kernel(x_ref, g_ref, sh_ref, sc_ref, w_ref, o_s5, o_rw, o_na):
    h = _rms_mod(x_ref[...], g_ref[...], sh_ref[0], sc_ref[0])
    p = jnp.dot(h.astype(BF16), w_ref[...], preferred_element_type=F32)
    o_s5[...] = p[:, :S5_WIDTH]
    o_rw[...] = p[:, S5_WIDTH:S5_WIDTH + 4 * RW]
    o_na[...] = p[:, S5_WIDTH + 4 * RW:]


def _in_proj(x2, g, shift, scale, w_bf, seq):
    t, d = x2.shape
    n = w_bf.shape[1]
    tm = 256
    per_b = seq // tm
    row = lambda i: (i, 0)
    bvec = lambda i: (i // per_b, 0, 0)
    return pl.pallas_call(
        _proj_kernel,
        grid=(t // tm,),
        in_specs=[pl.BlockSpec((tm, d), row),
                  pl.BlockSpec((1, d), lambda i: (0, 0)),
                  pl.BlockSpec((1, 1, d), bvec),
                  pl.BlockSpec((1, 1, d), bvec),
                  pl.BlockSpec((d, n), lambda i: (0, 0))],
        out_specs=[pl.BlockSpec((tm, S5_WIDTH), row),
                   pl.BlockSpec((tm, 4 * RW), row),
                   pl.BlockSpec((tm, 3 * NA_W), row)],
        out_shape=[jax.ShapeDtypeStruct((t, S5_WIDTH), F32),
                   jax.ShapeDtypeStruct((t, 4 * RW), F32),
                   jax.ShapeDtypeStruct((t, 3 * NA_W), F32)],
        compiler_params=_cparams(("arbitrary",)),
        name="in_proj",
    )(x2, g, shift, scale, w_bf)


def _softplus(x):
    return jnp.maximum(x, 0.0) + jnp.log(1.0 + jnp.exp(-jnp.abs(x)))


def _prep_kernel(xr_ref, prev_ref, next_ref, qkv_ref,
                 mu_ref, kk_ref, ka_ref, rk_ref, w0_ref, a0_ref,
                 w1_ref, w2_ref, a1_ref, a2_ref, g1_ref, g2_ref, qg_ref, kg_ref, ob_ref,
                 nkk_o, r_o, v_o, lw0_o, b0_o, k0_o, lw1_o, b1_o, k1_o,
                 gate_o, bonus_o, naq_o, nak_o, nav_o):
    i = pl.program_id(1)
    nblk = pl.num_programs(1)
    x = xr_ref[0]
    tm = x.shape[0]
    prow = jnp.where(i == 0, 0.0, prev_ref[0][7:8, :])
    nrow = jnp.where(i == nblk - 1, 0.0, next_ref[0][0:1, :])
    rid = lax.broadcasted_iota(jnp.int32, x.shape, 0)
    prev = jnp.where(rid == 0, prow, pltpu.roll(x, 1, axis=0))
    nxt = jnp.where(rid == tm - 1, nrow, pltpu.roll(x, tm - 1, axis=0))
    xs = x + (0.5 * (prev + nxt) - x) * mu_ref[...]
    r = xs[:, 0:RW]
    k = xs[:, RW:2 * RW]
    v = xs[:, 2 * RW:3 * RW]
    z = xs[:, 3 * RW:4 * RW]
    seg = lambda t: _seg_sum(t, ob_ref[...])
    zb = z.astype(BF16)
    bdot = lambda a, w: jnp.dot(a.astype(BF16), w, preferred_element_type=F32)
    gate_o[0] = bdot(jax.nn.sigmoid(bdot(zb, g1_ref[...])), g2_ref[...])
    kk = k * kk_ref[...]
    kk = kk / jnp.maximum(jnp.sqrt(seg(kk * kk)), 1e-12)
    nkk_o[0] = -kk
    r_o[0] = r
    v_o[0] = v
    bonus_o[0] = seg(r * k * rk_ref[...]) * v
    outs = ((lw0_o, b0_o, k0_o), (lw1_o, b1_o, k1_o))
    for d in range(2):
        wl = w0_ref[d:d + 1, :] + bdot(jnp.tanh(bdot(zb, w1_ref[d])), w2_ref[d])
        w = -_softplus(-wl) - 0.5
        a = jax.nn.sigmoid(a0_ref[d:d + 1, :] + bdot(bdot(zb, a1_ref[d]), a2_ref[d]))
        lw_o, b_o, k_o = outs[d]
        lw_o[0] = -jnp.exp(w)
        b_o[0] = kk * a
        k_o[0] = k * (1.0 + (a - 1.0) * ka_ref[...])
    qkv = qkv_ref[0]
    segm = lambda t: seg(t) * (1.0 / HEAD)
    qn = qkv[:, 0:NA_W]
    kn = qkv[:, NA_W:2 * NA_W]
    naq_o[0] = (qn * lax.rsqrt(segm(qn * qn) + RMS_EPS) * qg_ref[...] * (HEAD ** -0.5)).astype(BF16)
    nak_o[0] = (kn * lax.rsqrt(segm(kn * kn) + RMS_EPS) * kg_ref[...]).astype(BF16)
    nav_o[0] = qkv[:, 2 * NA_W:].astype(BF16)


def _prep(xr, qkv, p):
    bsz, seq, _ = xr.shape
    tm = 256
    nb = seq // tm
    h8 = tm // 8
    blk = lambda w: pl.BlockSpec((1, tm, w), lambda b, i: (b, i, 0))
    full = lambda a: pl.BlockSpec(a.shape, lambda b, i, _n=a.ndim: (0,) * _n)
    params = [p["mu"], p["k_k"], p["k_a"], p["r_k"], p["w0"], p["a0"], p["w1"], p["w2"], p["a1"], p["a2"],
              p["g1"], p["g2"], p["q_g"], p["k_g"], p["ob"]]
    f32o = jax.ShapeDtypeStruct((bsz, seq, RW), F32)
    bfo = jax.ShapeDtypeStruct((bsz, seq, NA_W), BF16)
    return pl.pallas_call(
        _prep_kernel,
        grid=(bsz, nb),
        in_specs=[blk(4 * RW),
                  pl.BlockSpec((1, 8, 4 * RW), lambda b, i: (b, jnp.maximum(i * h8 - 1, 0), 0)),
                  pl.BlockSpec((1, 8, 4 * RW), lambda b, i: (b, jnp.minimum((i + 1) * h8, seq // 8 - 1), 0)),
                  blk(3 * NA_W)] + [full(a) for a in params],
        out_specs=[blk(RW)] * 14,
        out_shape=[f32o] * 11 + [bfo] * 3,
        compiler_params=_cparams(("arbitrary", "arbitrary")),
        name="mixer_prep",
    )(xr, xr, xr, qkv, *params)


HEAD_PAIR = LANES // HEAD
WKV_PAIRS = RW // LANES
WKV_DOUBLINGS = WKV_CHUNK.bit_length() - 2


def _nt_dot(a, b):
    return lax.dot_general(a, b, (((1,), (1,)), ((), ())), preferred_element_type=F32)


def _wkv_kernel(*refs):
    f_in = refs[0:6]
    b_in = refs[6:12]
    tri_ref, msk_ref, eye_ref = refs[12:15]
    yf_ref, yb_ref = refs[15:17]
    s_ref = refs[17]
    c = pl.program_id(0)
    bi = pl.program_id(1)
    tc = WKV_CHUNK

    @pl.when(c == 0)
    def _():
        s_ref[bi] = jnp.zeros(s_ref.shape[1:], F32)

    first_head = lax.broadcasted_iota(jnp.int32, (tc, LANES), 1) < HEAD
    eye_bf = eye_ref[...]
    eye_f = eye_bf.astype(F32)

    def blk(z):
        return jnp.concatenate([jnp.where(first_head, z, 0.0), jnp.where(first_head, 0.0, z)], axis=0)

    bdot = lambda p, q: jnp.dot(p, q, preferred_element_type=F32)
    units = [(d, p) for d in range(2) for p in range(WKV_PAIRS)]
    every = lambda fn, *cols: [fn(*args) for args in zip(*cols)]
    states = [s_ref[bi, d, p] for d, p in units]
    masks = [(msk_ref[d, 0] > 0.5, msk_ref[d, 1] > 0.5) for d in range(2)]

    def load(d, p):
        src = f_in if d == 0 else b_in
        return [s[0, :, p * LANES:(p + 1) * LANES] for s in src]

    def decays(unit, data):
        d = unit[0]
        lw = data[3]
        cum = jnp.dot(tri_ref[d], lw, precision=HIGHEST, preferred_element_type=F32)
        cend = cum[tc - 1:tc] if d == 0 else cum[0:1]
        return cum, cend

    def operands(data, dec):
        a, r, v, lw, bb, kk = data
        cum, cend = dec
        e_neg = jnp.exp(-cum)
        e_end = jnp.exp(cend - cum)
        x = jnp.concatenate([blk(a * jnp.exp(cum - lw)), blk(r * jnp.exp(cum))], axis=0).astype(BF16)
        y = jnp.concatenate([blk(bb * e_neg), blk(kk * e_neg)], axis=0).astype(BF16)
        z = jnp.concatenate([blk(bb * e_end), blk(kk * e_end)], axis=0).astype(BF16)
        return x, y, z, blk(v)

    def causal(unit, g):
        strict, incl = masks[unit[0]]
        return (jnp.where(strict, g[:2 * tc, :2 * tc], 0.0), jnp.where(strict, g[:2 * tc, 2 * tc:], 0.0),
                jnp.concatenate([jnp.where(incl, g[2 * tc:, :2 * tc], 0.0),
                                 jnp.where(incl, g[2 * tc:, 2 * tc:], 0.0)], axis=1).astype(BF16))

    data = every(load, *zip(*units))
    dec = every(decays, units, data)
    ops = every(operands, data, dec)
    grams = every(lambda o: _nt_dot(o[0], o[1]), ops)
    nmat = every(causal, units, grams)
    ph = every(lambda o, st: _nt_dot(o[0], st.astype(BF16)), ops, states)
    vbf = every(lambda o: o[3].astype(BF16), ops)
    rhs = every(lambda q, n, vb: q[:2 * tc] + bdot(n[1].astype(BF16), vb), ph, nmat, vbf)
    inv = every(lambda n: eye_f + n[0], nmat)
    pw = every(lambda n: n[0].astype(BF16), nmat)
    for _ in range(WKV_DOUBLINGS):
        pw = every(lambda q: bdot(q, q).astype(BF16), pw)
        inv = every(lambda t, q: t + bdot(t.astype(BF16), q), inv, pw)
    u = every(lambda t, q: bdot(t.astype(BF16), q.astype(BF16)), inv, rhs)
    uv = every(lambda q, vb: jnp.concatenate([q.astype(BF16), vb], axis=0), u, vbf)
    yo = every(lambda q, n, w: q[2 * tc:] + bdot(n[2], w), ph, nmat, uv)
    uvt = every(lambda w: _nt_dot(eye_bf, w).astype(BF16), uv)
    new = every(lambda st, dc, w, o: st * jnp.exp(dc[1]) + bdot(w, o[2]), states, dec, uvt, ops)
    for d in range(2):
        out = yf_ref if d == 0 else yb_ref
        rows = [yo[units.index((d, p))] for p in range(WKV_PAIRS)]
        out[0] = jnp.concatenate([q[:tc] + q[tc:] for q in rows], axis=1)
    for (d, p), st in zip(units, new):
        s_ref[bi, d, p] = st


def _wkv_masks():
    tc = WKV_CHUNK
    t = np.arange(tc)
    tri = np.stack([t[None, :] <= t[:, None], t[None, :] >= t[:, None]]).astype(np.float32)
    head = np.arange(HEAD_PAIR * tc) // tc
    tt = np.arange(HEAD_PAIR * tc) % tc
    same = head[:, None] == head[None, :]
    m = np.stack([np.stack([same & (tt[None, :] < tt[:, None]), same & (tt[None, :] <= tt[:, None])]),
                  np.stack([same & (tt[None, :] > tt[:, None]), same & (tt[None, :] >= tt[:, None])])])
    return jnp.asarray(tri), jnp.asarray(m.astype(np.float32)), jnp.asarray(np.eye(LANES, dtype=np.float32), BF16)


def _wkv_scan(ins):
    bsz, seq, _ = ins["nkk"].shape
    tc = WKV_CHUNK
    nc = seq // tc
    tri, msk, eye = _wkv_masks()
    fwd = pl.BlockSpec((1, tc, RW), lambda c, b: (b, c, 0))
    bwd = pl.BlockSpec((1, tc, RW), lambda c, b: (b, nc - 1 - c, 0))
    full = lambda a: pl.BlockSpec(a.shape, lambda c, b, _n=a.ndim: (0,) * _n)
    f_args = [ins["nkk"], ins["r"], ins["v"], ins["lw0"], ins["b0"], ins["k0"]]
    b_args = [ins["nkk"], ins["r"], ins["v"], ins["lw1"], ins["b1"], ins["k1"]]
    o = jax.ShapeDtypeStruct((bsz, seq, RW), F32)
    return pl.pallas_call(
        _wkv_kernel,
        grid=(nc, bsz),
        in_specs=[fwd] * 6 + [bwd] * 6 + [full(tri), full(msk), full(eye)],
        out_specs=[fwd, bwd],
        out_shape=[o, o],
        scratch_shapes=[pltpu.VMEM((bsz, 2, WKV_PAIRS, LANES, LANES), F32)],
        compiler_params=_cparams(("arbitrary", "arbitrary")),
        name="wkv_scan",
    )(*f_args, *b_args, tri, msk, eye)


def _gelu_tanh(x):
    return 0.5 * x * (1.0 + jnp.tanh(math.sqrt(2.0 / math.pi) * (x + 0.044715 * (x * x * x))))


def _s5_kernel(ua_ref, ub_ref, bblk_ref, cblk_ref, lam_ref, lamc_ref, d_ref, glu_ref, o_ref,
               y_ref, st_ref, bu_ref, end_ref, carry_ref, *, seq, nb):
    ch = S5_CHUNK
    nc = seq // ch
    n = S5_FLAT
    u_halves = (ua_ref, ub_ref)
    for b in range(nb):
        for hf in range(2):
            y_ref[b, hf] = u_halves[hf][b] * d_ref[:, hf * LANES:(hf + 1) * LANES]

    def cmul_add(lre, lim, s, add):
        sre = s[:, :n]
        sim = s[:, n:]
        return jnp.concatenate([lre * sre - lim * sim + add[:, :n],
                                lre * sim + lim * sre + add[:, n:]], axis=1)

    for d in range(2):
        lre = lam_ref[d, 0:1, :]
        lim = lam_ref[d, 1:2, :]
        lcre = lamc_ref[d, 0:1, :]
        lcim = lamc_ref[d, 1:2, :]
        tloc = (lambda i: i) if d == 0 else (lambda i: ch - 1 - i)
        cloc = (lambda i: i) if d == 0 else (lambda i: nc - 1 - i)

        def project(i, slot):
            tl = tloc(jnp.minimum(i, ch - 1))
            rows = jnp.concatenate(
                [jnp.concatenate([r[b, pl.ds(tl, nc, stride=ch), :] for r in u_halves], axis=1)
                 for b in range(nb)], axis=0)
            bu_ref[slot] = jnp.dot(rows.astype(BF16), bblk_ref[d], preferred_element_type=F32)

        def advance(slot):
            st_ref[...] = cmul_add(lre, lim, st_ref[...], bu_ref[slot])

        def emit(i):
            yr = jnp.dot(st_ref[...].astype(BF16), cblk_ref[d], preferred_element_type=F32)
            idx = pl.ds(tloc(i), nc, stride=ch)
            for b in range(nb):
                for hf in range(2):
                    y_ref[b, hf, idx, :] = (y_ref[b, hf, idx, :]
                                            + yr[b * nc:(b + 1) * nc, hf * LANES:(hf + 1) * LANES])

        st_ref[...] = jnp.zeros_like(st_ref)
        project(0, 0)

        def p1(j, c):
            project(2 * j + 1, 1)
            advance(0)
            project(2 * j + 2, 0)
            advance(1)
            return c

        lax.fori_loop(0, ch // 2, p1, 0)
        end_ref[...] = st_ref[...]

        def cs(i, car):
            c = cloc(i)
            for b in range(nb):
                carry_ref[pl.ds(b * nc + c, 1), :] = car[b:b + 1]
            ends = jnp.concatenate([end_ref[pl.ds(b * nc + c, 1), :] for b in range(nb)], axis=0)
            return cmul_add(lcre, lcim, car, ends)

        lax.fori_loop(0, nc, cs, jnp.zeros((nb, 2 * n), F32))

        st_ref[...] = carry_ref[...]
        project(0, 0)
        project(1, 1)
        advance(0)

        def p2(j, c):
            i = 2 * j + 1
            project(i + 1, 0)
            emit(i - 1)
            advance(1)
            project(i + 2, 1)
            emit(i)
            advance(0)
            return c

        lax.fori_loop(0, (ch - 2) // 2, p2, 0)
        emit(ch - 2)
        advance(1)
        emit(ch - 1)

    glu = glu_ref[...]
    for b in range(nb):
        g = _gelu_tanh(jnp.concatenate([y_ref[b, 0], y_ref[b, 1]], axis=1))
        o_ref[b] = (g * jax.nn.sigmoid(jnp.dot(g.astype(BF16), glu, preferred_element_type=F32))).astype(o_ref.dtype)


def _s5_params(lam_re, lam_im, log_dt, b_re, b_im, c_re, c_im):
    lre = lam_re.astype(F32)
    lim = lam_im.astype(F32)
    dt = jnp.exp(log_dt.astype(F32))[..., None]

    def cexp(scale):
        mag = jnp.exp(lre * dt * scale)
        return mag * jnp.cos(lim * dt * scale), mag * jnp.sin(lim * dt * scale)

    bar_re, bar_im = cexp(1.0)
    den = lre * lre + lim * lim
    f_re = ((bar_re - 1.0) * lre + bar_im * lim) / den
    f_im = (bar_im * lre - (bar_re - 1.0) * lim) / den
    bm_re = b_re.astype(F32)
    bm_im = b_im.astype(F32)
    bb_re = f_re[..., None] * bm_re - f_im[..., None] * bm_im
    bb_im = f_re[..., None] * bm_im + f_im[..., None] * bm_re
    eye_g = jnp.eye(S5_GROUPS, dtype=F32)

    def blockdiag_in(m):
        return jnp.einsum("dgph,gk->dghkp", m, eye_g).reshape(2, S5_WIDTH, S5_FLAT)

    def blockdiag_out(m):
        return jnp.einsum("dghp,gk->dgpkh", m, eye_g).reshape(2, S5_FLAT, S5_WIDTH)

    bblk = jnp.concatenate([blockdiag_in(bb_re), blockdiag_in(bb_im)], axis=2)
    cblk = jnp.concatenate([blockdiag_out(c_re.astype(F32)), -blockdiag_out(c_im.astype(F32))], axis=1)
    flat = lambda z: jnp.stack([z[0].reshape(2, S5_FLAT), z[1].reshape(2, S5_FLAT)], axis=1)
    return bblk.astype(BF16), cblk.astype(BF16), flat((bar_re, bar_im)), flat(cexp(float(S5_CHUNK)))


def _s5_mixer(u, bblk, cblk, lam, lamc, d_skip, glu_bf):
    bsz, seq, w = u.shape
    nc = seq // S5_CHUNK
    nb = 2 if bsz % 2 == 0 else 1
    full = lambda a: pl.BlockSpec(a.shape, lambda b, _n=a.ndim: (0,) * _n)
    args = [bblk, cblk, lam, lamc, d_skip, glu_bf]
    state = pltpu.VMEM((nb * nc, 2 * S5_FLAT), F32)
    return pl.pallas_call(
        functools.partial(_s5_kernel, seq=seq, nb=nb),
        grid=(bsz // nb,),
        in_specs=[pl.BlockSpec((nb, seq, LANES), lambda b: (b, 0, 0)),
                  pl.BlockSpec((nb, seq, LANES), lambda b: (b, 0, 1))] + [full(a) for a in args],
        out_specs=pl.BlockSpec((nb, seq, w), lambda b: (b, 0, 0)),
        out_shape=jax.ShapeDtypeStruct((bsz, seq, w), BF16),
        scratch_shapes=[pltpu.VMEM((nb, w // LANES, seq, LANES), F32),
                        state,
                        pltpu.VMEM((2, nb * nc, 2 * S5_FLAT), F32),
                        state,
                        state],
        compiler_params=_cparams(("arbitrary",)),
        name="s5_mixer",
    )(u, u, *args)


def _na_bias_table(rpb):
    q_col = np.arange(GRID_W)
    c_start = np.clip(q_col - NA_KW // 2, 0, GRID_W - NA_KW)
    k_col = np.arange(GRID_W)
    valid = (k_col[None, :] >= c_start[:, None]) & (k_col[None, :] < c_start[:, None] + NA_KW)
    dx = np.clip(k_col[None, :] - q_col[:, None] + NA_KW - 1, 0, 2 * NA_KW - 2)
    pick = (np.arange(2 * NA_KW - 1)[:, None, None] == dx[None]).astype(np.float32)
    base = jnp.einsum("hyd,dqk->hyqk", rpb.astype(F32), jnp.asarray(pick), precision=HIGHEST)
    base = jnp.where(jnp.asarray(valid)[None, None], base, -jnp.inf)
    tab = jnp.stack([base[:, NA_KH - 1 - o:2 * NA_KH - 1 - o] for o in range(NA_KH)], axis=1)
    tab = jnp.transpose(tab, (0, 1, 3, 2, 4))
    return tab.reshape(rpb.shape[0], NA_KH, GRID_W, NA_KH * GRID_W)


def _na_kernel(q_ref, k_ref, v_ref, bias_ref, o_ref, *, rows, rblk):
    rb = pl.program_id(1)
    lane = lax.broadcasted_iota(jnp.int32, (GRID_W, LANES), 1)
    low = lane < HEAD

    def row(j, carry):
        r = rb * rblk + j
        rs = jnp.clip(r - NA_KH // 2, 0, rows - NA_KH)
        off = r - rs
        q = q_ref[0, j]
        kmat = k_ref[0, pl.ds(rs, NA_KH)].reshape(NA_KH * GRID_W, NA_W)
        vmat = v_ref[0, pl.ds(rs, NA_KH)].reshape(NA_KH * GRID_W, NA_W)
        outs = []
        for c in range(NA_W // LANES):
            sl = slice(c * LANES, (c + 1) * LANES)
            q2 = q[:, sl].astype(F32)
            lhs = jnp.concatenate([jnp.where(low, q2, 0.0), jnp.where(low, 0.0, q2)], axis=0).astype(BF16)
            s = lax.dot_general(lhs, kmat[:, sl], (((1,), (1,)), ((), ())), preferred_element_type=F32)
            s = s + jnp.concatenate([bias_ref[2 * c, off], bias_ref[2 * c + 1, off]], axis=0)
            m = jnp.max(s, axis=-1, keepdims=True)
            p = jnp.exp(s - m)
            l = jnp.sum(p, axis=-1, keepdims=True)
            o = jnp.dot(p.astype(BF16), vmat[:, sl], preferred_element_type=F32) / l
            outs.append(jnp.where(low, o[:GRID_W], o[GRID_W:]))
        o_ref[0, j] = jnp.concatenate(outs, axis=1).astype(o_ref.dtype)
        return carry

    lax.fori_loop(0, rblk, row, 0)


def _na_mixer(q, k, v, bias):
    bsz, seq, w = q.shape
    rows = seq // GRID_W
    rblk = 8
    g4 = lambda a: a.reshape(bsz, rows, GRID_W, w)
    img = pl.BlockSpec((1, rows, GRID_W, w), lambda b, i: (b, 0, 0, 0))
    blk = pl.BlockSpec((1, rblk, GRID_W, w), lambda b, i: (b, i, 0, 0))
    out = pl.pallas_call(
        functools.partial(_na_kernel, rows=rows, rblk=rblk),
        grid=(bsz, rows // rblk),
        in_specs=[blk, img, img, pl.BlockSpec(bias.shape, lambda b, i: (0, 0, 0, 0))],
        out_specs=blk,
        out_shape=jax.ShapeDtypeStruct((bsz, rows, GRID_W, w), BF16),
        compiler_params=_cparams(("arbitrary", "arbitrary")),
        name="na_mixer",
    )(g4(q), g4(k), g4(v), bias)
    return out.reshape(bsz, seq, w)


def _outproj_kernel(x_ref, s5_ref, yf_ref, yb_ref, gate_ref, bonus_ref, na_ref,
                    lnw_ref, lnb_ref, obm_ref, w_ref, gm_ref, g2_ref, sh_ref, sc_ref,
                    xo_ref, h_ref):
    segm = lambda t: _seg_sum(t, obm_ref[...])
    y = yf_ref[...] + yb_ref[...]
    yc = y - segm(y)
    yn = yc * lax.rsqrt(segm(yc * yc) + RWKV_GN_EPS) * lnw_ref[...] + lnb_ref[...]
    rw = (yn + bonus_ref[...]) * gate_ref[...]
    mixed = jnp.concatenate([s5_ref[...].astype(BF16), rw.astype(BF16), na_ref[...].astype(BF16)], axis=1)
    xo = x_ref[...] + gm_ref[0] * jnp.dot(mixed, w_ref[...], preferred_element_type=F32)
    xo_ref[...] = xo
    h_ref[...] = _rms_mod(xo, g2_ref[...], sh_ref[0], sc_ref[0])


def _out_proj(x2, s5o, yf, yb, gate, bonus, nao, lnw, lnb, obm, w_bf, gate_mix, g2, shift, scale, seq):
    t, d = x2.shape
    tm = 256
    per_b = seq // tm
    row = lambda w: pl.BlockSpec((tm, w), lambda i: (i, 0))
    full = lambda a: pl.BlockSpec(a.shape, lambda i, _n=a.ndim: (0,) * _n)
    bvec = pl.BlockSpec((1, 1, d), lambda i: (i // per_b, 0, 0))
    o = jax.ShapeDtypeStruct((t, d), F32)
    return pl.pallas_call(
        _outproj_kernel,
        grid=(t // tm,),
        in_specs=[row(d), row(S5_WIDTH)] + [row(RW)] * 5 +
                 [full(lnw), full(lnb), full(obm), full(w_bf), bvec, full(g2), bvec, bvec],
        out_specs=[row(d), row(d)],
        out_shape=[o, o],
        compiler_params=_cparams(("arbitrary",)),
        name="out_proj",
    )(x2, s5o, yf, yb, gate, bonus, nao, lnw, lnb, obm, w_bf, gate_mix, g2, shift, scale)


MOE_TILE = 256
SEG_ALIGN = 8
MOE_SLOTS = -(-(MOE_TILE * TOP_K + N_EXPERTS * (SEG_ALIGN - 1)) // LANES) * LANES


def _router_kernel(h_ref, w_ref, b_ref, tri_ref, upper_ref, slot_ref, gate_ref, cnt_ref, base_ref, loc_ref,
                   carry_ref):
    @pl.when(pl.program_id(0) == 0)
    def _():
        carry_ref[...] = jnp.zeros_like(carry_ref)

    logits = jnp.dot(h_ref[...], w_ref[...], precision=HIGHEST, preferred_element_type=F32) + b_ref[...]
    tm = logits.shape[0]
    lane = lax.broadcasted_iota(jnp.int32, (tm, LANES), 1)
    vals, hots = [], []
    cur = logits
    for _ in range(TOP_K):
        m = jnp.max(cur, axis=-1, keepdims=True)
        idx = jnp.min(jnp.where(cur == m, lane, LANES), axis=-1, keepdims=True)
        hot = lane == idx
        vals.append(m)
        hots.append(hot)
        cur = jnp.where(hot, -jnp.inf, cur)
    exps = [jnp.exp(v - vals[0]) for v in vals]
    den = exps[0] + exps[1] + exps[2] + exps[3]
    assign = sum(h.astype(F32) for h in hots)
    before = jnp.dot(tri_ref[...], assign.astype(BF16), preferred_element_type=F32)
    cnt = jnp.sum(assign, axis=0, keepdims=True)
    cnt = jnp.floor((cnt + (SEG_ALIGN - 1)) * (1.0 / SEG_ALIGN)) * SEG_ALIGN
    cnt8 = jnp.broadcast_to(cnt, (8, LANES)).astype(BF16)
    loc = jnp.dot(cnt8, upper_ref[...], preferred_element_type=F32)[0:1, :]
    place = before + loc
    s_out = jnp.zeros((tm, LANES), jnp.int32)
    g_out = jnp.zeros((tm, LANES), F32)
    for kk in range(TOP_K):
        slot = jnp.sum(jnp.where(hots[kk], place, 0.0), axis=-1, keepdims=True)
        sel = lane == kk
        s_out = jnp.where(sel, slot.astype(jnp.int32), s_out)
        g_out = jnp.where(sel, exps[kk] / den, g_out)
    slot_ref[...] = s_out
    gate_ref[...] = g_out
    cnt_ref[0] = cnt.astype(jnp.int32)
    base_ref[0] = carry_ref[...].astype(jnp.int32)
    loc_ref[0] = loc.astype(jnp.int32)
    carry_ref[...] = carry_ref[...] + cnt


def _router(h2, rw_pad, rb_pad):
    t, d = h2.shape
    tm = MOE_TILE
    nt = t // tm
    tri = jnp.asarray(np.tril(np.ones((tm, tm), np.float32), -1), BF16)
    upper = jnp.asarray(np.triu(np.ones((LANES, LANES), np.float32), 1), BF16)
    row = pl.BlockSpec((tm, LANES), lambda i: (i, 0))
    per_tile = pl.BlockSpec((1, 1, LANES), lambda i: (i, 0, 0))
    full = lambda a: pl.BlockSpec(a.shape, lambda i, _n=a.ndim: (0,) * _n)
    tile_i32 = jax.ShapeDtypeStruct((nt, 1, LANES), jnp.int32)
    return pl.pallas_call(
        _router_kernel,
        grid=(nt,),
        in_specs=[pl.BlockSpec((tm, d), lambda i: (i, 0)), full(rw_pad), full(rb_pad), full(tri), full(upper)],
        out_specs=[row, row, per_tile, per_tile, per_tile],
        out_shape=[jax.ShapeDtypeStruct((t, LANES), jnp.int32),
                   jax.ShapeDtypeStruct((t, LANES), F32),
                   tile_i32, tile_i32, tile_i32],
        scratch_shapes=[pltpu.VMEM((1, LANES), F32)],
        compiler_params=_cparams(("arbitrary",)),
        name="moe_router",
    )(h2, rw_pad, rb_pad, tri, upper)


SEG_PIECES = tuple(SEG_ALIGN << s for s in range((MOE_TILE // SEG_ALIGN).bit_length()))


def _segment_dmas(cnt_ref, loc_ref, row_ref, ne, make_copy, wait):
    tile = pl.program_id(0)

    def per_expert(e, carry):
        i = tile * ne + e
        n = cnt_ref[i]
        off = loc_ref[i]
        row = row_ref[i]
        for p in SEG_PIECES:
            has = (n & p) != 0

            @pl.when(has)
            def _(off=off, row=row, p=p):
                cp = make_copy(pl.multiple_of(off, SEG_ALIGN), pl.multiple_of(row, SEG_ALIGN), p)
                if wait:
                    cp.wait()
                else:
                    cp.start()

            step = jnp.where(has, p, 0)
            off = off + step
            row = row + step
        return carry

    lax.fori_loop(0, ne, per_expert, 0)


def _dispatch_kernel(cnt_ref, loc_ref, row_ref, slot_ref, h_ref, xb_in_ref, xb_ref, sorted_ref, sem, *, ne):
    del xb_in_ref
    tm = h_ref.shape[0]
    ns = MOE_SLOTS
    slot_t = jnp.transpose(slot_ref[...].astype(F32))
    srow = lax.broadcasted_iota(jnp.int32, (ns, tm), 0).astype(F32)
    pick = jnp.zeros((ns, tm), F32)
    for kk in range(TOP_K):
        pick = jnp.where(srow == slot_t[kk:kk + 1, :], 1.0, pick)
    sorted_ref[...] = jnp.dot(pick.astype(BF16), h_ref[...].astype(BF16), preferred_element_type=F32)

    def copy(off, row, p):
        return pltpu.make_async_copy(sorted_ref.at[pl.ds(off, p), :], xb_ref.at[pl.ds(row, p), :], sem)

    _segment_dmas(cnt_ref, loc_ref, row_ref, ne, copy, wait=False)
    _segment_dmas(cnt_ref, loc_ref, row_ref, ne, copy, wait=True)


def _dispatch(cnt, loc, rowstart, slot, h2, n_rows, ne):
    t, d = h2.shape
    tm = MOE_TILE
    zeros = jnp.zeros((n_rows, d), F32)
    grid_spec = pltpu.PrefetchScalarGridSpec(
        num_scalar_prefetch=3,
        grid=(t // tm,),
        in_specs=[pl.BlockSpec((tm, LANES), lambda i, *_: (i, 0)),
                  pl.BlockSpec((tm, d), lambda i, *_: (i, 0)),
                  pl.BlockSpec(memory_space=pl.ANY)],
        out_specs=pl.BlockSpec(memory_space=pl.ANY),
        scratch_shapes=[pltpu.VMEM((MOE_SLOTS, d), F32), pltpu.SemaphoreType.DMA(())],
    )
    return pl.pallas_call(
        functools.partial(_dispatch_kernel, ne=ne),
        grid_spec=grid_spec,
        out_shape=jax.ShapeDtypeStruct((n_rows, d), F32),
        input_output_aliases={5: 0},
        compiler_params=_cparams(("arbitrary",)),
        name="moe_dispatch",
    )(cnt, loc, rowstart, slot, h2, zeros)


PAIR_GROUP = 2 * LANES


def _pair_perm():
    p = np.zeros((PAIR_GROUP, PAIR_GROUP), np.float32)
    j = np.arange(LANES)
    p[2 * j, j] = 1.0
    p[2 * j + 1, LANES + j] = 1.0
    return jnp.asarray(p, BF16)


def _expert_kernel(be_ref, nu_ref, x_ref, w1_ref, b1_ref, w2_ref, b2_ref, perm_ref, y_ref, w1s_ref, w2s_ref):
    i = pl.program_id(0)
    f2 = w1_ref.shape[2]
    ngrp = f2 // PAIR_GROUP

    @pl.when(i >= nu_ref[0])
    def _():
        y_ref[...] = jnp.zeros_like(y_ref)

    @pl.when(i < nu_ref[0])
    def _():
        @pl.when((i == 0) | (be_ref[i] != be_ref[jnp.maximum(i - 1, 0)]))
        def _():
            for g in range(ngrp):
                sl = slice(g * PAIR_GROUP, (g + 1) * PAIR_GROUP)
                w1s_ref[:, sl] = jnp.dot(w1_ref[0, :, sl].astype(BF16), perm_ref[...],
                                         preferred_element_type=F32).astype(BF16)
            w2s_ref[...] = w2_ref[0].astype(BF16)

        hdn = jnp.dot(x_ref[...].astype(BF16), w1s_ref[...], preferred_element_type=F32) + b1_ref[0]
        glu = jnp.concatenate([hdn[:, g * PAIR_GROUP:g * PAIR_GROUP + LANES] for g in range(ngrp)], axis=1)
        lin = jnp.concatenate([hdn[:, g * PAIR_GROUP + LANES:(g + 1) * PAIR_GROUP] for g in range(ngrp)], axis=1)
        glu = jnp.minimum(glu, SWIGLU_LIMIT)
        lin = jnp.clip(lin, -SWIGLU_LIMIT, SWIGLU_LIMIT)
        act = glu * jax.nn.sigmoid(SWIGLU_ALPHA * glu) * (lin + 1.0)
        y_ref[...] = jnp.dot(act.astype(BF16), w2s_ref[...], preferred_element_type=F32) + b2_ref[0]


def _experts(block_e, n_used, xb, w1, b1_grp, w2, b2, layer):
    n_rows, d = xb.shape
    _, ne, _, f2 = w1.shape
    dff = w2.shape[2]
    nblk = n_rows // MOE_BLOCK
    perm = _pair_perm()
    blk = lambda i, be, nu: (jnp.minimum(i, nu[0] - 1), 0)
    wsel = lambda i, be, nu: (layer, be[i], 0, 0)
    grid_spec = pltpu.PrefetchScalarGridSpec(
        num_scalar_prefetch=2,
        grid=(nblk,),
        in_specs=[pl.BlockSpec((MOE_BLOCK, d), blk),
                  pl.BlockSpec((None, 1, d, f2), wsel),
                  pl.BlockSpec((None, 1, 1, f2), wsel),
                  pl.BlockSpec((None, 1, dff, d), wsel),
                  pl.BlockSpec((None, 1, 1, d), wsel),
                  pl.BlockSpec(perm.shape, lambda i, be, nu: (0, 0))],
        out_specs=pl.BlockSpec((MOE_BLOCK, d), lambda i, be, nu: (i, 0)),
        scratch_shapes=[pltpu.VMEM((d, f2), BF16), pltpu.VMEM((dff, d), BF16)],
    )
    nl = w1.shape[0]
    return pl.pallas_call(
        _expert_kernel,
        grid_spec=grid_spec,
        out_shape=jax.ShapeDtypeStruct((n_rows, d), F32),
        compiler_params=_cparams(("arbitrary",)),
        name="moe_experts",
    )(block_e, n_used, xb, w1, b1_grp.reshape(nl, ne, 1, f2), w2, b2.reshape(nl, ne, 1, d), perm)


def _combine_kernel(cnt_ref, loc_ref, row_ref, slot_ref, gates_ref, x_ref, gf_ref, yb_ref, o_ref, sorted_ref, sem,
                    *, ne):
    tm = x_ref.shape[0]
    ns = MOE_SLOTS

    def copy(off, row, p):
        return pltpu.make_async_copy(yb_ref.at[pl.ds(row, p), :], sorted_ref.at[pl.ds(off, p), :], sem)

    @pl.when(pl.program_id(0) == 0)
    def _():
        sorted_ref[...] = jnp.zeros_like(sorted_ref)

    _segment_dmas(cnt_ref, loc_ref, row_ref, ne, copy, wait=False)
    slot = slot_ref[...]
    gates = gates_ref[...]
    scol = lax.broadcasted_iota(jnp.int32, (tm, ns), 1)
    gmat = jnp.zeros((tm, ns), F32)
    for kk in range(TOP_K):
        gmat = jnp.where(scol == slot[:, kk:kk + 1], gates[:, kk:kk + 1], gmat)
    _segment_dmas(cnt_ref, loc_ref, row_ref, ne, copy, wait=True)
    acc = jnp.dot(gmat.astype(BF16), sorted_ref[...].astype(BF16), preferred_element_type=F32)
    o_ref[...] = x_ref[...] + gf_ref[0] * acc


def _combine(cnt, loc, rowstart, slot, gates, x2, gate_ffn, yb, seq, ne):
    t, d = x2.shape
    tm = MOE_TILE
    per_b = seq // tm
    grid_spec = pltpu.PrefetchScalarGridSpec(
        num_scalar_prefetch=3,
        grid=(t // tm,),
        in_specs=[pl.BlockSpec((tm, LANES), lambda i, *_: (i, 0)),
                  pl.BlockSpec((tm, LANES), lambda i, *_: (i, 0)),
                  pl.BlockSpec((tm, d), lambda i, *_: (i, 0)),
                  pl.BlockSpec((1, 1, d), lambda i, *_: (i // per_b, 0, 0)),
                  pl.BlockSpec(memory_space=pl.ANY)],
        out_specs=pl.BlockSpec((tm, d), lambda i, *_: (i, 0)),
        scratch_shapes=[pltpu.VMEM((MOE_SLOTS, d), F32), pltpu.SemaphoreType.DMA(())],
    )
    return pl.pallas_call(
        functools.partial(_combine_kernel, ne=ne),
        grid_spec=grid_spec,
        out_shape=jax.ShapeDtypeStruct((t, d), F32),
        compiler_params=_cparams(("arbitrary",)),
        name="moe_combine",
    )(cnt, loc, rowstart, slot, gates, x2, gate_ffn, yb)


def _group_pairs(b1):
    lead = b1.shape[:-1]
    g = b1.reshape(lead + (-1, LANES, 2))
    return jnp.swapaxes(g, -1, -2).reshape(b1.shape)


def _moe_layer(x2, h2, gate_ffn, router_w, router_b, w1, b1_grp, w2, b2, seq, layer):
    t, d = x2.shape
    ne = router_w.shape[1]
    rw_pad = jnp.zeros((d, LANES), F32).at[:, :ne].set(router_w.astype(F32))
    rb_pad = jnp.full((1, LANES), -jnp.inf, F32).at[0, :ne].set(router_b.astype(F32))
    slot, gates, cnt3, base3, loc3 = _router(h2, rw_pad, rb_pad)
    n_assign = t * TOP_K
    n_tiles = t // MOE_TILE
    n_blocks = -(-(n_assign + n_tiles * ne * (SEG_ALIGN - 1)) // MOE_BLOCK) + ne
    cnt = cnt3[:, 0, :ne]
    base = base3[:, 0, :ne]
    loc = loc3[:, 0, :ne]
    total = base[-1] + cnt[-1]
    padded = ((total + MOE_BLOCK - 1) // MOE_BLOCK) * MOE_BLOCK
    pad_end = jnp.cumsum(padded)
    pad_start = pad_end - padded
    rowstart = (pad_start[None, :] + base).reshape(-1).astype(jnp.int32)
    n_used = (pad_end[-1] // MOE_BLOCK).astype(jnp.int32)
    block_start = jnp.minimum(jnp.arange(n_blocks, dtype=jnp.int32), n_used - 1) * MOE_BLOCK
    block_e = jnp.minimum(jnp.sum(block_start[:, None] >= pad_end[None, :], axis=-1), ne - 1).astype(jnp.int32)
    cnt_f = cnt.reshape(-1)
    loc_f = loc.reshape(-1)
    xb = _dispatch(cnt_f, loc_f, rowstart, slot, h2, n_blocks * MOE_BLOCK, ne)
    yb = _experts(block_e, n_used.reshape(1), xb, w1, b1_grp, w2, b2, layer)
    return _combine(cnt_f, loc_f, rowstart, slot, gates, x2, gate_ffn, yb, seq, ne)


def kernel(x, c, ada_w, ada_b, norm1_g, norm2_g, w_in, w_out, s5_lam_re, s5_lam_im, s5_log_dt, s5_b_re, s5_b_im, s5_c_re, s5_c_im, s5_d, s5_glu_w, rwkv_mu, rwkv_w0, rwkv_w1, rwkv_w2, rwkv_a0, rwkv_a1, rwkv_a2, rwkv_g1, rwkv_g2, rwkv_k_k, rwkv_k_a, rwkv_r_k, rwkv_ln_w, rwkv_ln_b, na_q_g, na_k_g, na_rpb, router_w, router_b, exp_w1, exp_b1, exp_w2, exp_b2):
    bsz, seq, d = x.shape
    depth = ada_w.shape[0]
    t = bsz * seq
    mod = _ada_mod(c, ada_w, ada_b).reshape(depth, bsz, 6, 1, d)
    x2 = x.reshape(t, d)
    seg_ones = _seg_ones(RW, dtype=BF16)
    seg_mean = _seg_ones(RW, dtype=BF16, scale=1.0 / HEAD)
    row = lambda a: a.reshape(1, -1).astype(F32)
    b1_grp = _group_pairs(exp_b1.astype(F32))
    exp_b2f = exp_b2.astype(F32)
    for l in range(depth):
        m = lambda j: mod[l, :, j]
        s5u, xr, qkv = _in_proj(x2, row(norm1_g[l]), m(0), m(1), w_in[l].astype(BF16), seq)
        prep_params = dict(
            mu=row(rwkv_mu[l]), k_k=row(rwkv_k_k[l]), k_a=row(rwkv_k_a[l]), r_k=row(rwkv_r_k[l]),
            w0=rwkv_w0[l].astype(F32), a0=rwkv_a0[l].astype(F32),
            w1=rwkv_w1[l].astype(BF16), w2=rwkv_w2[l].astype(BF16),
            a1=rwkv_a1[l].astype(BF16), a2=rwkv_a2[l].astype(BF16),
            g1=rwkv_g1[l].astype(BF16), g2=rwkv_g2[l].astype(BF16),
            q_g=row(jnp.tile(na_q_g[l], NA_W // HEAD)), k_g=row(jnp.tile(na_k_g[l], NA_W // HEAD)), ob=seg_ones)
        (nkk, r, v, lw0, b0, k0, lw1, b1, k1, gate, bonus, naq, nak, nav) = _prep(
            xr.reshape(bsz, seq, 4 * RW), qkv.reshape(bsz, seq, 3 * NA_W), prep_params)
        yf, yb = _wkv_scan(dict(nkk=nkk, r=r, v=v, lw0=lw0, b0=b0, k0=k0, lw1=lw1, b1=b1, k1=k1))
        bblk, cblk, lam, lamc = _s5_params(s5_lam_re[l], s5_lam_im[l], s5_log_dt[l], s5_b_re[l], s5_b_im[l],
                                           s5_c_re[l], s5_c_im[l])
        s5o = _s5_mixer(s5u.reshape(bsz, seq, S5_WIDTH), bblk, cblk, lam, lamc, row(s5_d[l]),
                        s5_glu_w[l].astype(BF16))
        nao = _na_mixer(naq, nak, nav, _na_bias_table(na_rpb[l]))
        flat = lambda a: a.reshape(t, -1)
        x2, h2 = _out_proj(x2, flat(s5o), flat(yf), flat(yb), flat(gate), flat(bonus), flat(nao),
                           row(rwkv_ln_w[l]), row(rwkv_ln_b[l]), seg_mean, w_out[l].astype(BF16),
                           m(2), row(norm2_g[l]), m(3), m(4), seq)
        x2 = _moe_layer(x2, h2, m(5), router_w[l], router_b[l], exp_w1, b1_grp, exp_w2, exp_b2f, seq, l)
    return x2.reshape(bsz, seq, d)
```

```python
import functools
import math

import numpy as np
import jax
import jax.numpy as jnp
from jax import lax
from jax.experimental import pallas as pl
from jax.experimental.pallas import tpu as pltpu

F32 = jnp.float32
BF16 = jnp.bfloat16
HIGHEST = lax.Precision.HIGHEST

D_MODEL = 1024
S5_WIDTH = 256
S5_GROUP = 16
S5_GROUPS = 16
S5_STATE = 64
S5_CHUNK = 64
S5_FLAT = S5_GROUPS * S5_STATE
RW = 384
HEAD = 64
RWKV_GN_EPS = 64e-5
NA_W = 384
GRID_W = 64
NA_KH = 8
NA_KW = 16
N_EXPERTS = 32
TOP_K = 4
MOE_BLOCK = 256
SWIGLU_ALPHA = 1.702
SWIGLU_LIMIT = 7.0
RMS_EPS = 1e-6
LANES = 128
WKV_CHUNK = 64
VMEM_LIMIT = 56 * 1024 * 1024


def _cparams(sem):
    return pltpu.CompilerParams(dimension_semantics=sem, vmem_limit_bytes=VMEM_LIMIT)


def _seg_ones(n, seg=HEAD, dtype=F32, scale=1.0):
    idx = np.arange(n) // seg
    return jnp.asarray((idx[:, None] == idx[None, :]).astype(np.float32) * scale, dtype)


def _seg_sum(t, ones_bf):
    hi = t.astype(BF16)
    lo = (t - hi.astype(F32)).astype(BF16)
    return (jnp.dot(hi, ones_bf, preferred_element_type=F32) + jnp.dot(lo, ones_bf, preferred_element_type=F32))


def _ada_kernel(c_ref, w_ref, b_ref, o_ref):
    c = c_ref[...]
    cond = c * jax.nn.sigmoid(c)
    o_ref[0] = jnp.dot(cond, w_ref[0], preferred_element_type=F32) + b_ref[0]


def _ada_mod(c, ada_w, ada_b):
    nl, d, n6 = ada_w.shape
    bsz = c.shape[0]
    tn = 1536
    return pl.pallas_call(
        _ada_kernel,
        grid=(nl, n6 // tn),
        in_specs=[pl.BlockSpec((bsz, d), lambda l, j: (0, 0)),
                  pl.BlockSpec((1, d, tn), lambda l, j: (l, 0, j)),
                  pl.BlockSpec((1, 1, tn), lambda l, j: (l, 0, j))],
        out_specs=pl.BlockSpec((1, bsz, tn), lambda l, j: (l, 0, j)),
        out_shape=jax.ShapeDtypeStruct((nl, bsz, n6), F32),
        compiler_params=_cparams(("arbitrary", "arbitrary")),
        name="ada_mod",
    )(c, ada_w, ada_b.reshape(nl, 1, n6))


def _rms_mod(x, g, shift, scale):
    ms = jnp.mean(x * x, axis=-1, keepdims=True)
    h = x * lax.rsqrt(ms + RMS_EPS) * g
    return h * (1.0 + scale) + shift


def _proj_kernel(x_ref, g_ref, sh_ref, sc_ref, w_ref, o_s5, o_rw, o_na):
    h = _rms_mod(x_ref[...], g_ref[...], sh_ref[0], sc_ref[0])
    p = jnp.dot(h.astype(BF16), w_ref[...], preferred_element_type=F32)
    o_s5[...] = p[:, :S5_WIDTH]
    o_rw[...] = p[:, S5_WIDTH:S5_WIDTH + 4 * RW]
    o_na[...] = p[:, S5_WIDTH + 4 * RW:]


def _in_proj(x2, g, shift, scale, w_bf, seq):
    t, d = x2.shape
    n = w_bf.shape[1]
    tm = 256
    per_b = seq // tm
    row = lambda i: (i, 0)
    bvec = lambda i: (i // per_b, 0, 0)
    return pl.pallas_call(
        _proj_kernel,
        grid=(t // tm,),
        in_specs=[pl.BlockSpec((tm, d), row),
                  pl.BlockSpec((1, d), lambda i: (0, 0)),
                  pl.BlockSpec((1, 1, d), bvec),
                  pl.BlockSpec((1, 1, d), bvec),
                  pl.BlockSpec((d, n), lambda i: (0, 0))],
        out_specs=[pl.BlockSpec((tm, S5_WIDTH), row),
                   pl.BlockSpec((tm, 4 * RW), row),
                   pl.BlockSpec((tm, 3 * NA_W), row)],
        out_shape=[jax.ShapeDtypeStruct((t, S5_WIDTH), F32),
                   jax.ShapeDtypeStruct((t, 4 * RW), F32),
                   jax.ShapeDtypeStruct((t, 3 * NA_W), F32)],
        compiler_params=_cparams(("arbitrary",)),
        name="in_proj",
    )(x2, g, shift, scale, w_bf)


def _softplus(x):
    return jnp.maximum(x, 0.0) + jnp.log(1.0 + jnp.exp(-jnp.abs(x)))


def _prep_kernel(xr_ref, prev_ref, next_ref, qkv_ref,
                 mu_ref, kk_ref, ka_ref, rk_ref, w0_ref, a0_ref,
                 w1_ref, w2_ref, a1_ref, a2_ref, g1_ref, g2_ref, qg_ref, kg_ref, ob_ref,
                 nkk_o, r_o, v_o, lw0_o, b0_o, k0_o, lw1_o, b1_o, k1_o,
                 gate_o, bonus_o, naq_o, nak_o, nav_o):
    i = pl.program_id(1)
    nblk = pl.num_programs(1)
    x = xr_ref[0]
    tm = x.shape[0]
    prow = jnp.where(i == 0, 0.0, prev_ref[0][7:8, :])
    nrow = jnp.where(i == nblk - 1, 0.0, next_ref[0][0:1, :])
    rid = lax.broadcasted_iota(jnp.int32, x.shape, 0)
    prev = jnp.where(rid == 0, prow, pltpu.roll(x, 1, axis=0))
    nxt = jnp.where(rid == tm - 1, nrow, pltpu.roll(x, tm - 1, axis=0))
    xs = x + (0.5 * (prev + nxt) - x) * mu_ref[...]
    r = xs[:, 0:RW]
    k = xs[:, RW:2 * RW]
    v = xs[:, 2 * RW:3 * RW]
    z = xs[:, 3 * RW:4 * RW]
    seg = lambda t: _seg_sum(t, ob_ref[...])
    zb = z.astype(BF16)
    bdot = lambda a, w: jnp.dot(a.astype(BF16), w, preferred_element_type=F32)
    gate_o[0] = bdot(jax.nn.sigmoid(bdot(zb, g1_ref[...])), g2_ref[...])
    kk = k * kk_ref[...]
    kk = kk / jnp.maximum(jnp.sqrt(seg(kk * kk)), 1e-12)
    nkk_o[0] = -kk
    r_o[0] = r
    v_o[0] = v
    bonus_o[0] = seg(r * k * rk_ref[...]) * v
    outs = ((lw0_o, b0_o, k0_o), (lw1_o, b1_o, k1_o))
    for d in range(2):
        wl = w0_ref[d:d + 1, :] + bdot(jnp.tanh(bdot(zb, w1_ref[d])), w2_ref[d])
        w = -_softplus(-wl) - 0.5
        a = jax.nn.sigmoid(a0_ref[d:d + 1, :] + bdot(bdot(zb, a1_ref[d]), a2_ref[d]))
        lw_o, b_o, k_o = outs[d]
        lw_o[0] = -jnp.exp(w)
        b_o[0] = kk * a
        k_o[0] = k * (1.0 + (a - 1.0) * ka_ref[...])
    qkv = qkv_ref[0]
    segm = lambda t: seg(t) * (1.0 / HEAD)
    qn = qkv[:, 0:NA_W]
    kn = qkv[:, NA_W:2 * NA_W]
    naq_o[0] = (qn * lax.rsqrt(segm(qn * qn) + RMS_EPS) * qg_ref[...] * (HEAD ** -0.5)).astype(BF16)
    nak_o[0] = (kn * lax.rsqrt(segm(kn * kn) + RMS_EPS) * kg_ref[...]).astype(BF16)
    nav_o[0] = qkv[:, 2 * NA_W:].astype(BF16)


def _prep(xr, qkv, p):
    bsz, seq, _ = xr.shape
    tm = 256
    nb = seq // tm
    h8 = tm // 8
    blk = lambda w: pl.BlockSpec((1, tm, w), lambda b, i: (b, i, 0))
    full = lambda a: pl.BlockSpec(a.shape, lambda b, i, _n=a.ndim: (0,) * _n)
    params = [p["mu"], p["k_k"], p["k_a"], p["r_k"], p["w0"], p["a0"], p["w1"], p["w2"], p["a1"], p["a2"],
              p["g1"], p["g2"], p["q_g"], p["k_g"], p["ob"]]
    f32o = jax.ShapeDtypeStruct((bsz, seq, RW), F32)
    bfo = jax.ShapeDtypeStruct((bsz, seq, NA_W), BF16)
    return pl.pallas_call(
        _prep_kernel,
        grid=(bsz, nb),
        in_specs=[blk(4 * RW),
                  pl.BlockSpec((1, 8, 4 * RW), lambda b, i: (b, jnp.maximum(i * h8 - 1, 0), 0)),
                  pl.BlockSpec((1, 8, 4 * RW), lambda b, i: (b, jnp.minimum((i + 1) * h8, seq // 8 - 1), 0)),
                  blk(3 * NA_W)] + [full(a) for a in params],
        out_specs=[blk(RW)] * 14,
        out_shape=[f32o] * 11 + [bfo] * 3,
        compiler_params=_cparams(("arbitrary", "arbitrary")),
        name="mixer_prep",
    )(xr, xr, xr, qkv, *params)


HEAD_PAIR = LANES // HEAD
WKV_PAIRS = RW // LANES
WKV_DOUBLINGS = WKV_CHUNK.bit_length() - 2


def _nt_dot(a, b):
    return lax.dot_general(a, b, (((1,), (1,)), ((), ())), preferred_element_type=F32)


def _wkv_kernel(*refs, nrows):
    f_in = refs[0:6]
    b_in = refs[6:12]
    tri_ref, msk_ref, eye_ref = refs[12:15]
    yf_ref, yb_ref = refs[15:17]
    s_ref = refs[17]
    c = pl.program_id(0)
    bi = pl.program_id(1)
    tc = WKV_CHUNK

    @pl.when(c == 0)
    def _():
        for row in range(nrows):
            s_ref[bi * nrows + row] = jnp.zeros(s_ref.shape[1:], F32)

    first_head = lax.broadcasted_iota(jnp.int32, (tc, LANES), 1) < HEAD
    eye_bf = eye_ref[...]
    eye_f = eye_bf.astype(F32)

    def blk(z):
        return jnp.concatenate([jnp.where(first_head, z, 0.0), jnp.where(first_head, 0.0, z)], axis=0)

    bdot = lambda p, q: jnp.dot(p, q, preferred_element_type=F32)
    units = [(row, d, p) for row in range(nrows) for d in range(2) for p in range(WKV_PAIRS)]
    every = lambda fn, *cols: [fn(*args) for args in zip(*cols)]
    states = [s_ref[bi * nrows + row, d, p] for row, d, p in units]
    masks = [(msk_ref[d, 0] > 0.5, msk_ref[d, 1] > 0.5) for d in range(2)]

    def load(row, d, p):
        src = f_in if d == 0 else b_in
        return [s[row, :, p * LANES:(p + 1) * LANES] for s in src]

    def decays(unit, data):
        d = unit[1]
        lw = data[3]
        cum = jnp.zeros_like(lw)
        rest = lw
        for _ in range(3):
            term = rest.astype(BF16)
            cum = cum + bdot(tri_ref[d], term)
            rest = rest - term.astype(F32)
        cend = cum[tc - 1:tc] if d == 0 else cum[0:1]
        return cum, cend

    def operands(data, dec):
        a, r, v, lw, bb, kk = data
        cum, cend = dec
        e_neg = jnp.exp(-cum)
        e_end = jnp.exp(cend - cum)
        x = jnp.concatenate([blk(a * jnp.exp(cum - lw)), blk(r * jnp.exp(cum))], axis=0).astype(BF16)
        y = jnp.concatenate([blk(bb * e_neg), blk(kk * e_neg)], axis=0).astype(BF16)
        z = jnp.concatenate([blk(bb * e_end), blk(kk * e_end)], axis=0).astype(BF16)
        return x, y, z, blk(v)

    def causal(unit, g):
        strict, incl = masks[unit[1]]
        return (jnp.where(strict, g[:2 * tc, :2 * tc], 0.0), jnp.where(strict, g[:2 * tc, 2 * tc:], 0.0),
                jnp.concatenate([jnp.where(incl, g[2 * tc:, :2 * tc], 0.0),
                                 jnp.where(incl, g[2 * tc:, 2 * tc:], 0.0)], axis=1).astype(BF16))

    data = every(load, *zip(*units))
    dec = every(decays, units, data)
    ops = every(operands, data, dec)
    grams = every(lambda o: _nt_dot(o[0], o[1]), ops)
    nmat = every(causal, units, grams)
    ph = every(lambda o, st: _nt_dot(o[0], st.astype(BF16)), ops, states)
    vbf = every(lambda o: o[3].astype(BF16), ops)
    rhs = every(lambda q, n, vb: q[:2 * tc] + bdot(n[1].astype(BF16), vb), ph, nmat, vbf)
    inv = every(lambda n: eye_f + n[0], nmat)
    pw = every(lambda n: n[0].astype(BF16), nmat)
    for _ in range(WKV_DOUBLINGS):
        pw = every(lambda q: bdot(q, q).astype(BF16), pw)
        inv = every(lambda t, q: t + bdot(t.astype(BF16), q), inv, pw)
    u = every(lambda t, q: bdot(t.astype(BF16), q.astype(BF16)), inv, rhs)
    uv = every(lambda q, vb: jnp.concatenate([q.astype(BF16), vb], axis=0), u, vbf)
    yo = every(lambda q, n, w: q[2 * tc:] + bdot(n[2], w), ph, nmat, uv)
    uvt = every(lambda w: _nt_dot(eye_bf, w).astype(BF16), uv)
    new = every(lambda st, dc, w, o: st * jnp.exp(dc[1]) + bdot(w, o[2]), states, dec, uvt, ops)
    for row in range(nrows):
        for d in range(2):
            out = yf_ref if d == 0 else yb_ref
            parts = [yo[units.index((row, d, p))] for p in range(WKV_PAIRS)]
            out[row] = jnp.concatenate([q[:tc] + q[tc:] for q in parts], axis=1)
    for (row, d, p), st in zip(units, new):
        s_ref[bi * nrows + row, d, p] = st


def _wkv_masks():
    tc = WKV_CHUNK
    t = np.arange(tc)
    tri = np.stack([t[None, :] <= t[:, None], t[None, :] >= t[:, None]]).astype(np.float32)
    head = np.arange(HEAD_PAIR * tc) // tc
    tt = np.arange(HEAD_PAIR * tc) % tc
    same = head[:, None] == head[None, :]
    m = np.stack([np.stack([same & (tt[None, :] < tt[:, None]), same & (tt[None, :] <= tt[:, None])]),
                  np.stack([same & (tt[None, :] > tt[:, None]), same & (tt[None, :] >= tt[:, None])])])
    return jnp.asarray(tri, BF16), jnp.asarray(m.astype(np.float32)), jnp.asarray(np.eye(LANES, dtype=np.float32), BF16)


def _wkv_scan(ins):
    bsz, seq, _ = ins["nkk"].shape
    tc = WKV_CHUNK
    nc = seq // tc
    tri, msk, eye = _wkv_masks()
    nrows = 2 if bsz % 2 == 0 else 1
    fwd = pl.BlockSpec((nrows, tc, RW), lambda c, b: (b, c, 0))
    bwd = pl.BlockSpec((nrows, tc, RW), lambda c, b: (b, nc - 1 - c, 0))
    full = lambda a: pl.BlockSpec(a.shape, lambda c, b, _n=a.ndim: (0,) * _n)
    f_args = [ins["nkk"], ins["r"], ins["v"], ins["lw0"], ins["b0"], ins["k0"]]
    b_args = [ins["nkk"], ins["r"], ins["v"], ins["lw1"], ins["b1"], ins["k1"]]
    o = jax.ShapeDtypeStruct((bsz, seq, RW), F32)
    return pl.pallas_call(
        functools.partial(_wkv_kernel, nrows=nrows),
        grid=(nc, bsz // nrows),
        in_specs=[fwd] * 6 + [bwd] * 6 + [full(tri), full(msk), full(eye)],
        out_specs=[fwd, bwd],
        out_shape=[o, o],
        scratch_shapes=[pltpu.VMEM((bsz, 2, WKV_PAIRS, LANES, LANES), F32)],
        compiler_params=_cparams(("arbitrary", "arbitrary")),
        name="wkv_scan",
    )(*f_args, *b_args, tri, msk, eye)


def _gelu_tanh(x):
    return 0.5 * x * (1.0 + jnp.tanh(math.sqrt(2.0 / math.pi) * (x + 0.044715 * (x * x * x))))


def _s5_kernel(ua_ref, ub_ref, bblk_ref, cblk_ref, lam_ref, lamc_ref, d_ref, glu_ref, o_ref,
               y_ref, st_ref, bu_ref, end_ref, carry_ref, *, seq, nb):
    ch = S5_CHUNK
    nc = seq // ch
    n = S5_FLAT
    u_halves = (ua_ref, ub_ref)
    for b in range(nb):
        for hf in range(2):
            y_ref[b, hf] = u_halves[hf][b] * d_ref[:, hf * LANES:(hf + 1) * LANES]

    def cmul_add(lre, lim, s, add):
        sre = s[:, :n]
        sim = s[:, n:]
        return jnp.concatenate([lre * sre - lim * sim + add[:, :n],
                                lre * sim + lim * sre + add[:, n:]], axis=1)

    for d in range(2):
        lre = lam_ref[d, 0:1, :]
        lim = lam_ref[d, 1:2, :]
        lcre = lamc_ref[d, 0:1, :]
        lcim = lamc_ref[d, 1:2, :]
        tloc = (lambda i: i) if d == 0 else (lambda i: ch - 1 - i)
        cloc = (lambda i: i) if d == 0 else (lambda i: nc - 1 - i)

        def project(i, slot):
            tl = tloc(jnp.minimum(i, ch - 1))
            rows = jnp.concatenate(
                [jnp.concatenate([r[b, pl.ds(tl, nc, stride=ch), :] for r in u_halves], axis=1)
                 for b in range(nb)], axis=0)
            bu_ref[slot] = jnp.dot(rows.astype(BF16), bblk_ref[d], preferred_element_type=F32)

        def advance(slot):
            st_ref[...] = cmul_add(lre, lim, st_ref[...], bu_ref[slot])

        def emit(i):
            yr = jnp.dot(st_ref[...].astype(BF16), cblk_ref[d], preferred_element_type=F32)
            idx = pl.ds(tloc(i), nc, stride=ch)
            for b in range(nb):
                for hf in range(2):
                    y_ref[b, hf, idx, :] = (y_ref[b, hf, idx, :]
                                            + yr[b * nc:(b + 1) * nc, hf * LANES:(hf + 1) * LANES])

        st_ref[...] = jnp.zeros_like(st_ref)
        project(0, 0)

        def p1(j, c):
            project(2 * j + 1, 1)
            advance(0)
            project(2 * j + 2, 0)
            advance(1)
            return c

        lax.fori_loop(0, ch // 2, p1, 0)
        end_ref[...] = st_ref[...]

        def cs(i, car):
            c = cloc(i)
            for b in range(nb):
                carry_ref[pl.ds(b * nc + c, 1), :] = car[b:b + 1]
            ends = jnp.concatenate([end_ref[pl.ds(b * nc + c, 1), :] for b in range(nb)], axis=0)
            return cmul_add(lcre, lcim, car, ends)

        lax.fori_loop(0, nc, cs, jnp.zeros((nb, 2 * n), F32))

        st_ref[...] = carry_ref[...]
        project(0, 0)
        project(1, 1)
        advance(0)

        def p2(j, c):
            i = 2 * j + 1
            project(i + 1, 0)
            emit(i - 1)
            advance(1)
            project(i + 2, 1)
            emit(i)
            advance(0)
            return c

        lax.fori_loop(0, (ch - 2) // 2, p2, 0)
        emit(ch - 2)
        advance(1)
        emit(ch - 1)

    glu = glu_ref[...]
    for b in range(nb):
        g = _gelu_tanh(jnp.concatenate([y_ref[b, 0], y_ref[b, 1]], axis=1))
        o_ref[b] = (g * jax.nn.sigmoid(jnp.dot(g.astype(BF16), glu, preferred_element_type=F32))).astype(o_ref.dtype)


def _s5_params(lam_re, lam_im, log_dt, b_re, b_im, c_re, c_im):
    lre = lam_re.astype(F32)
    lim = lam_im.astype(F32)
    dt = jnp.exp(log_dt.astype(F32))[..., None]

    def cexp(scale):
        mag = jnp.exp(lre * dt * scale)
        return mag * jnp.cos(lim * dt * scale), mag * jnp.sin(lim * dt * scale)

    bar_re, bar_im = cexp(1.0)
    den = lre * lre + lim * lim
    f_re = ((bar_re - 1.0) * lre + bar_im * lim) / den
    f_im = (bar_im * lre - (bar_re - 1.0) * lim) / den
    bm_re = b_re.astype(F32)
    bm_im = b_im.astype(F32)
    bb_re = f_re[..., None] * bm_re - f_im[..., None] * bm_im
    bb_im = f_re[..., None] * bm_im + f_im[..., None] * bm_re
    eye_g = jnp.eye(S5_GROUPS, dtype=F32)

    def blockdiag_in(m):
        return jnp.einsum("dgph,gk->dghkp", m, eye_g).reshape(2, S5_WIDTH, S5_FLAT)

    def blockdiag_out(m):
        return jnp.einsum("dghp,gk->dgpkh", m, eye_g).reshape(2, S5_FLAT, S5_WIDTH)

    bblk = jnp.concatenate([blockdiag_in(bb_re), blockdiag_in(bb_im)], axis=2)
    cblk = jnp.concatenate([blockdiag_out(c_re.astype(F32)), -blockdiag_out(c_im.astype(F32))], axis=1)
    flat = lambda z: jnp.stack([z[0].reshape(2, S5_FLAT), z[1].reshape(2, S5_FLAT)], axis=1)
    return bblk.astype(BF16), cblk.astype(BF16), flat((bar_re, bar_im)), flat(cexp(float(S5_CHUNK)))


def _s5_mixer(u, bblk, cblk, lam, lamc, d_skip, glu_bf):
    bsz, seq, w = u.shape
    nc = seq // S5_CHUNK
    nb = 2 if bsz % 2 == 0 else 1
    full = lambda a: pl.BlockSpec(a.shape, lambda b, _n=a.ndim: (0,) * _n)
    args = [bblk, cblk, lam, lamc, d_skip, glu_bf]
    state = pltpu.VMEM((nb * nc, 2 * S5_FLAT), F32)
    return pl.pallas_call(
        functools.partial(_s5_kernel, seq=seq, nb=nb),
        grid=(bsz // nb,),
        in_specs=[pl.BlockSpec((nb, seq, LANES), lambda b: (b, 0, 0)),
                  pl.BlockSpec((nb, seq, LANES), lambda b: (b, 0, 1))] + [full(a) for a in args],
        out_specs=pl.BlockSpec((nb, seq, w), lambda b: (b, 0, 0)),
        out_shape=jax.ShapeDtypeStruct((bsz, seq, w), BF16),
        scratch_shapes=[pltpu.VMEM((nb, w // LANES, seq, LANES), F32),
                        state,
                        pltpu.VMEM((2, nb * nc, 2 * S5_FLAT), F32),
                        state,
                        state],
        compiler_params=_cparams(("arbitrary",)),
        name="s5_mixer",
    )(u, u, *args)


def _na_bias_table(rpb):
    q_col = np.arange(GRID_W)
    c_start = np.clip(q_col - NA_KW // 2, 0, GRID_W - NA_KW)
    k_col = np.arange(GRID_W)
    valid = (k_col[None, :] >= c_start[:, None]) & (k_col[None, :] < c_start[:, None] + NA_KW)
    dx = np.clip(k_col[None, :] - q_col[:, None] + NA_KW - 1, 0, 2 * NA_KW - 2)
    pick = (np.arange(2 * NA_KW - 1)[:, None, None] == dx[None]).astype(np.float32)
    base = jnp.einsum("hyd,dqk->hyqk", rpb.astype(F32), jnp.asarray(pick), precision=HIGHEST)
    base = jnp.where(jnp.asarray(valid)[None, None], base, -jnp.inf)
    tab = jnp.stack([base[:, NA_KH - 1 - o:2 * NA_KH - 1 - o] for o in range(NA_KH)], axis=1)
    tab = jnp.transpose(tab, (0, 1, 3, 2, 4))
    return tab.reshape(rpb.shape[0], NA_KH, GRID_W, NA_KH * GRID_W)


def _na_kernel(q_ref, k_ref, v_ref, bias_ref, o_ref, *, rows, rblk):
    rb = pl.program_id(1)
    lane = lax.broadcasted_iota(jnp.int32, (GRID_W, LANES), 1)
    low = lane < HEAD

    def row(j, carry):
        r = rb * rblk + j
        rs = jnp.clip(r - NA_KH // 2, 0, rows - NA_KH)
        off = r - rs
        q = q_ref[0, j]
        kmat = k_ref[0, pl.ds(rs, NA_KH)].reshape(NA_KH * GRID_W, NA_W)
        vmat = v_ref[0, pl.ds(rs, NA_KH)].reshape(NA_KH * GRID_W, NA_W)
        outs = []
        for c in range(NA_W // LANES):
            sl = slice(c * LANES, (c + 1) * LANES)
            q2 = q[:, sl].astype(F32)
            lhs = jnp.concatenate([jnp.where(low, q2, 0.0), jnp.where(low, 0.0, q2)], axis=0).astype(BF16)
            s = lax.dot_general(lhs, kmat[:, sl], (((1,), (1,)), ((), ())), preferred_element_type=F32)
            s = s + jnp.concatenate([bias_ref[2 * c, off], bias_ref[2 * c + 1, off]], axis=0)
            m = jnp.max(s, axis=-1, keepdims=True)
            p = jnp.exp(s - m)
            l = jnp.sum(p, axis=-1, keepdims=True)
            o = jnp.dot(p.astype(BF16), vmat[:, sl], preferred_element_type=F32) / l
            outs.append(jnp.where(low, o[:GRID_W], o[GRID_W:]))
        o_ref[0, j] = jnp.concatenate(outs, axis=1).astype(o_ref.dtype)
        return carry

    lax.fori_loop(0, rblk, row, 0)


def _na_mixer(q, k, v, bias):
    bsz, seq, w = q.shape
    rows = seq // GRID_W
    rblk = 8
    g4 = lambda a: a.reshape(bsz, rows, GRID_W, w)
    img = pl.BlockSpec((1, rows, GRID_W, w), lambda b, i: (b, 0, 0, 0))
    blk = pl.BlockSpec((1, rblk, GRID_W, w), lambda b, i: (b, i, 0, 0))
    out = pl.pallas_call(
        functools.partial(_na_kernel, rows=rows, rblk=rblk),
        grid=(bsz, rows // rblk),
        in_specs=[blk, img, img, pl.BlockSpec(bias.shape, lambda b, i: (0, 0, 0, 0))],
        out_specs=blk,
        out_shape=jax.ShapeDtypeStruct((bsz, rows, GRID_W, w), BF16),
        compiler_params=_cparams(("arbitrary", "arbitrary")),
        name="na_mixer",
    )(g4(q), g4(k), g4(v), bias)
    return out.reshape(bsz, seq, w)


def _outproj_kernel(x_ref, s5_ref, yf_ref, yb_ref, gate_ref, bonus_ref, na_ref,
                    lnw_ref, lnb_ref, obm_ref, w_ref, gm_ref, g2_ref, sh_ref, sc_ref,
                    xo_ref, h_ref):
    segm = lambda t: _seg_sum(t, obm_ref[...])
    y = yf_ref[...] + yb_ref[...]
    yc = y - segm(y)
    yn = yc * lax.rsqrt(segm(yc * yc) + RWKV_GN_EPS) * lnw_ref[...] + lnb_ref[...]
    rw = (yn + bonus_ref[...]) * gate_ref[...]
    mixed = jnp.concatenate([s5_ref[...].astype(BF16), rw.astype(BF16), na_ref[...].astype(BF16)], axis=1)
    xo = x_ref[...] + gm_ref[0] * jnp.dot(mixed, w_ref[...], preferred_element_type=F32)
    xo_ref[...] = xo
    h_ref[...] = _rms_mod(xo, g2_ref[...], sh_ref[0], sc_ref[0])


def _out_proj(x2, s5o, yf, yb, gate, bonus, nao, lnw, lnb, obm, w_bf, gate_mix, g2, shift, scale, seq):
    t, d = x2.shape
    tm = 256
    per_b = seq // tm
    row = lambda w: pl.BlockSpec((tm, w), lambda i: (i, 0))
    full = lambda a: pl.BlockSpec(a.shape, lambda i, _n=a.ndim: (0,) * _n)
    bvec = pl.BlockSpec((1, 1, d), lambda i: (i // per_b, 0, 0))
    o = jax.ShapeDtypeStruct((t, d), F32)
    return pl.pallas_call(
        _outproj_kernel,
        grid=(t // tm,),
        in_specs=[row(d), row(S5_WIDTH)] + [row(RW)] * 5 +
                 [full(lnw), full(lnb), full(obm), full(w_bf), bvec, full(g2), bvec, bvec],
        out_specs=[row(d), row(d)],
        out_shape=[o, o],
        compiler_params=_cparams(("arbitrary",)),
        name="out_proj",
    )(x2, s5o, yf, yb, gate, bonus, nao, lnw, lnb, obm, w_bf, gate_mix, g2, shift, scale)


MOE_TILE = 256
SEG_ALIGN = 8
MOE_SLOTS = -(-(MOE_TILE * TOP_K + N_EXPERTS * (SEG_ALIGN - 1)) // LANES) * LANES


def _router_kernel(h_ref, w_ref, b_ref, tri_ref, upper_ref, slot_ref, gate_ref, cnt_ref, base_ref, loc_ref,
                   carry_ref):
    @pl.when(pl.program_id(0) == 0)
    def _():
        carry_ref[...] = jnp.zeros_like(carry_ref)

    logits = jnp.dot(h_ref[...], w_ref[...], precision=HIGHEST, preferred_element_type=F32) + b_ref[...]
    tm = logits.shape[0]
    lane = lax.broadcasted_iota(jnp.int32, (tm, LANES), 1)
    vals, hots = [], []
    cur = logits
    for _ in range(TOP_K):
        m = jnp.max(cur, axis=-1, keepdims=True)
        idx = jnp.min(jnp.where(cur == m, lane, LANES), axis=-1, keepdims=True)
        hot = lane == idx
        vals.append(m)
        hots.append(hot)
        cur = jnp.where(hot, -jnp.inf, cur)
    exps = [jnp.exp(v - vals[0]) for v in vals]
    den = exps[0] + exps[1] + exps[2] + exps[3]
    assign = sum(h.astype(F32) for h in hots)
    before = jnp.dot(tri_ref[...], assign.astype(BF16), preferred_element_type=F32)
    cnt = jnp.sum(assign, axis=0, keepdims=True)
    cnt = jnp.floor((cnt + (SEG_ALIGN - 1)) * (1.0 / SEG_ALIGN)) * SEG_ALIGN
    cnt8 = jnp.broadcast_to(cnt, (8, LANES)).astype(BF16)
    loc = jnp.dot(cnt8, upper_ref[...], preferred_element_type=F32)[0:1, :]
    place = before + loc
    s_out = jnp.zeros((tm, LANES), jnp.int32)
    g_out = jnp.zeros((tm, LANES), F32)
    for kk in range(TOP_K):
        slot = jnp.sum(jnp.where(hots[kk], place, 0.0), axis=-1, keepdims=True)
        sel = lane == kk
        s_out = jnp.where(sel, slot.astype(jnp.int32), s_out)
        g_out = jnp.where(sel, exps[kk] / den, g_out)
    slot_ref[...] = s_out
    gate_ref[...] = g_out
    cnt_ref[0] = cnt.astype(jnp.int32)
    base_ref[0] = carry_ref[...].astype(jnp.int32)
    loc_ref[0] = loc.astype(jnp.int32)
    carry_ref[...] = carry_ref[...] + cnt


def _router(h2, rw_pad, rb_pad):
    t, d = h2.shape
    tm = MOE_TILE
    nt = t // tm
    tri = jnp.asarray(np.tril(np.ones((tm, tm), np.float32), -1), BF16)
    upper = jnp.asarray(np.triu(np.ones((LANES, LANES), np.float32), 1), BF16)
    row = pl.BlockSpec((tm, LANES), lambda i: (i, 0))
    per_tile = pl.BlockSpec((1, 1, LANES), lambda i: (i, 0, 0))
    full = lambda a: pl.BlockSpec(a.shape, lambda i, _n=a.ndim: (0,) * _n)
    tile_i32 = jax.ShapeDtypeStruct((nt, 1, LANES), jnp.int32)
    return pl.pallas_call(
        _router_kernel,
        grid=(nt,),
        in_specs=[pl.BlockSpec((tm, d), lambda i: (i, 0)), full(rw_pad), full(rb_pad), full(tri), full(upper)],
        out_specs=[row, row, per_tile, per_tile, per_tile],
        out_shape=[jax.ShapeDtypeStruct((t, LANES), jnp.int32),
                   jax.ShapeDtypeStruct((t, LANES), F32),
                   tile_i32, tile_i32, tile_i32],
        scratch_shapes=[pltpu.VMEM((1, LANES), F32)],
        compiler_params=_cparams(("arbitrary",)),
        name="moe_router",
    )(h2, rw_pad, rb_pad, tri, upper)


SEG_PIECES = tuple(SEG_ALIGN << s for s in range((MOE_TILE // SEG_ALIGN).bit_length()))


def _segment_dmas(cnt_ref, loc_ref, row_ref, tile, ne, make_copy, wait):
    def per_expert(e, carry):
        i = tile * ne + e
        n = cnt_ref[i]
        off = loc_ref[i]
        row = row_ref[i]
        for p in SEG_PIECES:
            has = (n & p) != 0

            @pl.when(has)
            def _(off=off, row=row, p=p):
                cp = make_copy(pl.multiple_of(off, SEG_ALIGN), pl.multiple_of(row, SEG_ALIGN), p)
                if wait:
                    cp.wait()
                else:
                    cp.start()

            step = jnp.where(has, p, 0)
            off = off + step
            row = row + step
        return carry

    lax.fori_loop(0, ne, per_expert, 0)


def _dispatch_kernel(cnt_ref, loc_ref, row_ref, zcnt_ref, zoff_ref, zrow_ref, nu_ref,
                     slot_ref, h_ref, xb_ref, sorted_ref, zero_ref, sem, zsem, *, ne, nblk):
    i = pl.program_id(0)
    last = pl.num_programs(0) - 1
    cur = i % 2
    tm = h_ref.shape[0]
    ns = MOE_SLOTS

    def zero_copy(off, row, p):
        return pltpu.make_async_copy(zero_ref.at[pl.ds(off, p), :], xb_ref.at[pl.ds(row, p), :], zsem)

    def tail_copy(j):
        row = pl.multiple_of((nu_ref[0] + j) * MOE_BLOCK, MOE_BLOCK)
        return pltpu.make_async_copy(zero_ref, xb_ref.at[pl.ds(row, MOE_BLOCK), :], zsem)

    def zero_fill(wait):
        _segment_dmas(zcnt_ref, zoff_ref, zrow_ref, 0, ne, zero_copy, wait)

        def tail(j, c):
            if wait:
                tail_copy(j).wait()
            else:
                tail_copy(j).start()
            return c

        lax.fori_loop(0, nblk - nu_ref[0], tail, 0)

    @pl.when(i == 0)
    def _():
        zero_ref[...] = jnp.zeros_like(zero_ref)
        zero_fill(wait=False)

    slot_t = jnp.transpose(slot_ref[...].astype(F32))
    srow = lax.broadcasted_iota(jnp.int32, (ns, tm), 0).astype(F32)
    pick = jnp.zeros((ns, tm), F32)
    for kk in range(TOP_K):
        pick = jnp.where(srow == slot_t[kk:kk + 1, :], 1.0, pick)
    sorted_ref[cur] = jnp.dot(pick.astype(BF16), h_ref[...].astype(BF16), preferred_element_type=F32)

    def copy_from(buf):
        def copy(off, row, p):
            return pltpu.make_async_copy(sorted_ref.at[buf, pl.ds(off, p), :], xb_ref.at[pl.ds(row, p), :],
                                         sem.at[buf])
        return copy

    _segment_dmas(cnt_ref, loc_ref, row_ref, i, ne, copy_from(cur), wait=False)

    @pl.when(i > 0)
    def _():
        _segment_dmas(cnt_ref, loc_ref, row_ref, i - 1, ne, copy_from(1 - cur), wait=True)

    @pl.when(i == last)
    def _():
        _segment_dmas(cnt_ref, loc_ref, row_ref, i, ne, copy_from(cur), wait=True)
        zero_fill(wait=True)


def _dispatch(cnt, loc, rowstart, zcnt, zrow, n_used, slot, h2, n_rows, ne):
    t, d = h2.shape
    tm = MOE_TILE
    zoff = jnp.zeros_like(zcnt)
    grid_spec = pltpu.PrefetchScalarGridSpec(
        num_scalar_prefetch=7,
        grid=(t // tm,),
        in_specs=[pl.BlockSpec((tm, LANES), lambda i, *_: (i, 0)),
                  pl.BlockSpec((tm, d), lambda i, *_: (i, 0))],
        out_specs=pl.BlockSpec(memory_space=pl.ANY),
        scratch_shapes=[pltpu.VMEM((2, MOE_SLOTS, d), F32), pltpu.VMEM((MOE_BLOCK, d), F32),
                        pltpu.SemaphoreType.DMA((2,)), pltpu.SemaphoreType.DMA(())],
    )
    return pl.pallas_call(
        functools.partial(_dispatch_kernel, ne=ne, nblk=n_rows // MOE_BLOCK),
        grid_spec=grid_spec,
        out_shape=jax.ShapeDtypeStruct((n_rows, d), F32),
        compiler_params=_cparams(("arbitrary",)),
        name="moe_dispatch",
    )(cnt, loc, rowstart, zcnt, zoff, zrow, n_used, slot, h2)


PAIR_GROUP = 2 * LANES


def _pair_perm():
    p = np.zeros((PAIR_GROUP, PAIR_GROUP), np.float32)
    j = np.arange(LANES)
    p[2 * j, j] = 1.0
    p[2 * j + 1, LANES + j] = 1.0
    return jnp.asarray(p, BF16)


def _expert_kernel(be_ref, nu_ref, x_ref, w1_ref, b1_ref, w2_ref, b2_ref, perm_ref, y_ref, w1s_ref, w2s_ref):
    i = pl.program_id(0)
    f2 = w1_ref.shape[2]
    ngrp = f2 // PAIR_GROUP

    @pl.when(i >= nu_ref[0])
    def _():
        y_ref[...] = jnp.zeros_like(y_ref)

    @pl.when(i < nu_ref[0])
    def _():
        @pl.when((i == 0) | (be_ref[i] != be_ref[jnp.maximum(i - 1, 0)]))
        def _():
            for g in range(ngrp):
                sl = slice(g * PAIR_GROUP, (g + 1) * PAIR_GROUP)
                w1s_ref[:, sl] = jnp.dot(w1_ref[0, :, sl].astype(BF16), perm_ref[...],
                                         preferred_element_type=F32).astype(BF16)
            w2s_ref[...] = w2_ref[0].astype(BF16)

        hdn = jnp.dot(x_ref[...].astype(BF16), w1s_ref[...], preferred_element_type=F32) + b1_ref[0]
        glu = jnp.concatenate([hdn[:, g * PAIR_GROUP:g * PAIR_GROUP + LANES] for g in range(ngrp)], axis=1)
        lin = jnp.concatenate([hdn[:, g * PAIR_GROUP + LANES:(g + 1) * PAIR_GROUP] for g in range(ngrp)], axis=1)
        glu = jnp.minimum(glu, SWIGLU_LIMIT)
        lin = jnp.clip(lin, -SWIGLU_LIMIT, SWIGLU_LIMIT)
        act = glu * jax.nn.sigmoid(SWIGLU_ALPHA * glu) * (lin + 1.0)
        y_ref[...] = jnp.dot(act.astype(BF16), w2s_ref[...], preferred_element_type=F32) + b2_ref[0]


def _experts(block_e, n_used, xb, w1, b1_grp, w2, b2, layer):
    n_rows, d = xb.shape
    _, ne, _, f2 = w1.shape
    dff = w2.shape[2]
    nblk = n_rows // MOE_BLOCK
    perm = _pair_perm()
    blk = lambda i, be, nu: (jnp.minimum(i, nu[0] - 1), 0)
    wsel = lambda i, be, nu: (layer, be[i], 0, 0)
    grid_spec = pltpu.PrefetchScalarGridSpec(
        num_scalar_prefetch=2,
        grid=(nblk,),
        in_specs=[pl.BlockSpec((MOE_BLOCK, d), blk),
                  pl.BlockSpec((None, 1, d, f2), wsel),
                  pl.BlockSpec((None, 1, 1, f2), wsel),
                  pl.BlockSpec((None, 1, dff, d), wsel),
                  pl.BlockSpec((None, 1, 1, d), wsel),
                  pl.BlockSpec(perm.shape, lambda i, be, nu: (0, 0))],
        out_specs=pl.BlockSpec((MOE_BLOCK, d), lambda i, be, nu: (i, 0)),
        scratch_shapes=[pltpu.VMEM((d, f2), BF16), pltpu.VMEM((dff, d), BF16)],
    )
    nl = w1.shape[0]
    return pl.pallas_call(
        _expert_kernel,
        grid_spec=grid_spec,
        out_shape=jax.ShapeDtypeStruct((n_rows, d), F32),
        compiler_params=_cparams(("arbitrary",)),
        name="moe_experts",
    )(block_e, n_used, xb, w1, b1_grp.reshape(nl, ne, 1, f2), w2, b2.reshape(nl, ne, 1, d), perm)


def _combine_kernel(cnt_ref, loc_ref, row_ref, slot_ref, gates_ref, x_ref, gf_ref, yb_ref, o_ref, sorted_ref, sem,
                    *, ne):
    i = pl.program_id(0)
    last = pl.num_programs(0) - 1
    cur = i % 2
    tm = x_ref.shape[0]
    ns = MOE_SLOTS

    def copy_into(buf):
        def copy(off, row, p):
            return pltpu.make_async_copy(yb_ref.at[pl.ds(row, p), :], sorted_ref.at[buf, pl.ds(off, p), :],
                                         sem.at[buf])
        return copy

    @pl.when(i == 0)
    def _():
        sorted_ref[...] = jnp.zeros_like(sorted_ref)
        _segment_dmas(cnt_ref, loc_ref, row_ref, i, ne, copy_into(cur), wait=False)

    @pl.when(i < last)
    def _():
        _segment_dmas(cnt_ref, loc_ref, row_ref, i + 1, ne, copy_into(1 - cur), wait=False)

    slot = slot_ref[...]
    gates = gates_ref[...]
    scol = lax.broadcasted_iota(jnp.int32, (tm, ns), 1)
    gmat = jnp.zeros((tm, ns), F32)
    for kk in range(TOP_K):
        gmat = jnp.where(scol == slot[:, kk:kk + 1], gates[:, kk:kk + 1], gmat)
    _segment_dmas(cnt_ref, loc_ref, row_ref, i, ne, copy_into(cur), wait=True)
    acc = jnp.dot(gmat.astype(BF16), sorted_ref[cur].astype(BF16), preferred_element_type=F32)
    o_ref[...] = x_ref[...] + gf_ref[0] * acc


def _combine(cnt, loc, rowstart, slot, gates, x2, gate_ffn, yb, seq, ne):
    t, d = x2.shape
    tm = MOE_TILE
    per_b = seq // tm
    grid_spec = pltpu.PrefetchScalarGridSpec(
        num_scalar_prefetch=3,
        grid=(t // tm,),
        in_specs=[pl.BlockSpec((tm, LANES), lambda i, *_: (i, 0)),
                  pl.BlockSpec((tm, LANES), lambda i, *_: (i, 0)),
                  pl.BlockSpec((tm, d), lambda i, *_: (i, 0)),
                  pl.BlockSpec((1, 1, d), lambda i, *_: (i // per_b, 0, 0)),
                  pl.BlockSpec(memory_space=pl.ANY)],
        out_specs=pl.BlockSpec((tm, d), lambda i, *_: (i, 0)),
        scratch_shapes=[pltpu.VMEM((2, MOE_SLOTS, d), F32), pltpu.SemaphoreType.DMA((2,))],
    )
    return pl.pallas_call(
        functools.partial(_combine_kernel, ne=ne),
        grid_spec=grid_spec,
        out_shape=jax.ShapeDtypeStruct((t, d), F32),
        compiler_params=_cparams(("arbitrary",)),
        name="moe_combine",
    )(cnt, loc, rowstart, slot, gates, x2, gate_ffn, yb)


def _group_pairs(b1):
    lead = b1.shape[:-1]
    g = b1.reshape(lead + (-1, LANES, 2))
    return jnp.swapaxes(g, -1, -2).reshape(b1.shape)


def _moe_layer(x2, h2, gate_ffn, router_w, router_b, w1, b1_grp, w2, b2, seq, layer):
    t, d = x2.shape
    ne = router_w.shape[1]
    rw_pad = jnp.zeros((d, LANES), F32).at[:, :ne].set(router_w.astype(F32))
    rb_pad = jnp.full((1, LANES), -jnp.inf, F32).at[0, :ne].set(router_b.astype(F32))
    slot, gates, cnt3, base3, loc3 = _router(h2, rw_pad, rb_pad)
    n_assign = t * TOP_K
    n_tiles = t // MOE_TILE
    n_blocks = -(-(n_assign + n_tiles * ne * (SEG_ALIGN - 1)) // MOE_BLOCK) + ne
    cnt = cnt3[:, 0, :ne]
    base = base3[:, 0, :ne]
    loc = loc3[:, 0, :ne]
    total = base[-1] + cnt[-1]
    padded = ((total + MOE_BLOCK - 1) // MOE_BLOCK) * MOE_BLOCK
    pad_end = jnp.cumsum(padded)
    pad_start = pad_end - padded
    rowstart = (pad_start[None, :] + base).reshape(-1).astype(jnp.int32)
    n_used = (pad_end[-1] // MOE_BLOCK).astype(jnp.int32)
    block_start = jnp.minimum(jnp.arange(n_blocks, dtype=jnp.int32), n_used - 1) * MOE_BLOCK
    block_e = jnp.minimum(jnp.sum(block_start[:, None] >= pad_end[None, :], axis=-1), ne - 1).astype(jnp.int32)
    cnt_f = cnt.reshape(-1)
    loc_f = loc.reshape(-1)
    n_used = n_used.reshape(1)
    xb = _dispatch(cnt_f, loc_f, rowstart, (padded - total).astype(jnp.int32), (pad_start + total).astype(jnp.int32),
                   n_used, slot, h2, n_blocks * MOE_BLOCK, ne)
    yb = _experts(block_e, n_used, xb, w1, b1_grp, w2, b2, layer)
    return _combine(cnt_f, loc_f, rowstart, slot, gates, x2, gate_ffn, yb, seq, ne)


def kernel(x, c, ada_w, ada_b, norm1_g, norm2_g, w_in, w_out, s5_lam_re, s5_lam_im, s5_log_dt, s5_b_re, s5_b_im, s5_c_re, s5_c_im, s5_d, s5_glu_w, rwkv_mu, rwkv_w0, rwkv_w1, rwkv_w2, rwkv_a0, rwkv_a1, rwkv_a2, rwkv_g1, rwkv_g2, rwkv_k_k, rwkv_k_a, rwkv_r_k, rwkv_ln_w, rwkv_ln_b, na_q_g, na_k_g, na_rpb, router_w, router_b, exp_w1, exp_b1, exp_w2, exp_b2):
    bsz, seq, d = x.shape
    depth = ada_w.shape[0]
    t = bsz * seq
    mod = _ada_mod(c, ada_w, ada_b).reshape(depth, bsz, 6, 1, d)
    x2 = x.reshape(t, d)
    seg_ones = _seg_ones(RW, dtype=BF16)
    seg_mean = _seg_ones(RW, dtype=BF16, scale=1.0 / HEAD)
    row = lambda a: a.reshape(1, -1).astype(F32)
    b1_grp = _group_pairs(exp_b1.astype(F32))
    exp_b2f = exp_b2.astype(F32)
    for l in range(depth):
        m = lambda j: mod[l, :, j]
        s5u, xr, qkv = _in_proj(x2, row(norm1_g[l]), m(0), m(1), w_in[l].astype(BF16), seq)
        prep_params = dict(
            mu=row(rwkv_mu[l]), k_k=row(rwkv_k_k[l]), k_a=row(rwkv_k_a[l]), r_k=row(rwkv_r_k[l]),
            w0=rwkv_w0[l].astype(F32), a0=rwkv_a0[l].astype(F32),
            w1=rwkv_w1[l].astype(BF16), w2=rwkv_w2[l].astype(BF16),
            a1=rwkv_a1[l].astype(BF16), a2=rwkv_a2[l].astype(BF16),
            g1=rwkv_g1[l].astype(BF16), g2=rwkv_g2[l].astype(BF16),
            q_g=row(jnp.tile(na_q_g[l], NA_W // HEAD)), k_g=row(jnp.tile(na_k_g[l], NA_W // HEAD)), ob=seg_ones)
        (nkk, r, v, lw0, b0, k0, lw1, b1, k1, gate, bonus, naq, nak, nav) = _prep(
            xr.reshape(bsz, seq, 4 * RW), qkv.reshape(bsz, seq, 3 * NA_W), prep_params)
        yf, yb = _wkv_scan(dict(nkk=nkk, r=r, v=v, lw0=lw0, b0=b0, k0=k0, lw1=lw1, b1=b1, k1=k1))
        bblk, cblk, lam, lamc = _s5_params(s5_lam_re[l], s5_lam_im[l], s5_log_dt[l], s5_b_re[l], s5_b_im[l],
                                           s5_c_re[l], s5_c_im[l])
        s5o = _s5_mixer(s5u.reshape(bsz, seq, S5_WIDTH), bblk, cblk, lam, lamc, row(s5_d[l]),
                        s5_glu_w[l].astype(BF16))
        nao = _na_mixer(naq, nak, nav, _na_bias_table(na_rpb[l]))
        flat = lambda a: a.reshape(t, -1)
        x2, h2 = _out_proj(x2, flat(s5o), flat(yf), flat(yb), flat(gate), flat(bonus), flat(nao),
                           row(rwkv_ln_w[l]), row(rwkv_ln_b[l]), seg_mean, w_out[l].astype(BF16),
                           m(2), row(norm2_g[l]), m(3), m(4), seq)
        x2 = _moe_layer(x2, h2, m(5), router_w[l], router_b[l], exp_w1, b1_grp, exp_w2, exp_b2f, seq, l)
    return x2.reshape(bsz, seq, d)
```

```python
import functools
import math

import numpy as np
import jax
import jax.numpy as jnp
from jax import lax
from jax.experimental import pallas as pl
from jax.experimental.pallas import tpu as pltpu

F32 = jnp.float32
BF16 = jnp.bfloat16
HIGHEST = lax.Precision.HIGHEST

D_MODEL = 1024
S5_WIDTH = 256
S5_GROUP = 16
S5_GROUPS = 16
S5_STATE = 64
S5_CHUNK = 64
S5_FLAT = S5_GROUPS * S5_STATE
RW = 384
HEAD = 64
RWKV_GN_EPS = 64e-5
NA_W = 384
GRID_W = 64
NA_KH = 8
NA_KW = 16
N_EXPERTS = 32
TOP_K = 4
MOE_BLOCK = 256
SWIGLU_ALPHA = 1.702
SWIGLU_LIMIT = 7.0
RMS_EPS = 1e-6
LANES = 128
WKV_CHUNK = 64
VMEM_LIMIT = 56 * 1024 * 1024


def _cparams(sem):
    return pltpu.CompilerParams(dimension_semantics=sem, vmem_limit_bytes=VMEM_LIMIT)


def _seg_ones(n, seg=HEAD, dtype=F32, scale=1.0):
    idx = np.arange(n) // seg
    return jnp.asarray((idx[:, None] == idx[None, :]).astype(np.float32) * scale, dtype)


def _seg_sum(t, ones_bf):
    hi = t.astype(BF16)
    lo = (t - hi.astype(F32)).astype(BF16)
    return (jnp.dot(hi, ones_bf, preferred_element_type=F32) + jnp.dot(lo, ones_bf, preferred_element_type=F32))


def _ada_kernel(c_ref, w_ref, b_ref, o_ref):
    c = c_ref[...]
    cond = c * jax.nn.sigmoid(c)
    o_ref[0] = jnp.dot(cond, w_ref[0], preferred_element_type=F32) + b_ref[0]


def _ada_mod(c, ada_w, ada_b):
    nl, d, n6 = ada_w.shape
    bsz = c.shape[0]
    tn = 1536
    return pl.pallas_call(
        _ada_kernel,
        grid=(nl, n6 // tn),
        in_specs=[pl.BlockSpec((bsz, d), lambda l, j: (0, 0)),
                  pl.BlockSpec((1, d, tn), lambda l, j: (l, 0, j)),
                  pl.BlockSpec((1, 1, tn), lambda l, j: (l, 0, j))],
        out_specs=pl.BlockSpec((1, bsz, tn), lambda l, j: (l, 0, j)),
        out_shape=jax.ShapeDtypeStruct((nl, bsz, n6), F32),
        compiler_params=_cparams(("arbitrary", "arbitrary")),
        name="ada_mod",
    )(c, ada_w, ada_b.reshape(nl, 1, n6))


def _rms_mod(x, g, shift, scale):
    ms = jnp.mean(x * x, axis=-1, keepdims=True)
    h = x * lax.rsqrt(ms + RMS_EPS) * g
    return h * (1.0 + scale) + shift


def _proj_kernel(x_ref, g_ref, sh_ref, sc_ref, w_ref, o_s5, o_rw, o_na):
    h = _rms_mod(x_ref[...], g_ref[...], sh_ref[0], sc_ref[0])
    p = jnp.dot(h.astype(BF16), w_ref[...], preferred_element_type=F32)
    o_s5[...] = p[:, :S5_WIDTH]
    o_rw[...] = p[:, S5_WIDTH:S5_WIDTH + 4 * RW]
    o_na[...] = p[:, S5_WIDTH + 4 * RW:]


def _in_proj(x2, g, shift, scale, w_bf, seq):
    t, d = x2.shape
    n = w_bf.shape[1]
    tm = 256
    per_b = seq // tm
    row = lambda i: (i, 0)
    bvec = lambda i: (i // per_b, 0, 0)
    return pl.pallas_call(
        _proj_kernel,
        grid=(t // tm,),
        in_specs=[pl.BlockSpec((tm, d), row),
                  pl.BlockSpec((1, d), lambda i: (0, 0)),
                  pl.BlockSpec((1, 1, d), bvec),
                  pl.BlockSpec((1, 1, d), bvec),
                  pl.BlockSpec((d, n), lambda i: (0, 0))],
        out_specs=[pl.BlockSpec((tm, S5_WIDTH), row),
                   pl.BlockSpec((tm, 4 * RW), row),
                   pl.BlockSpec((tm, 3 * NA_W), row)],
        out_shape=[jax.ShapeDtypeStruct((t, S5_WIDTH), F32),
                   jax.ShapeDtypeStruct((t, 4 * RW), F32),
                   jax.ShapeDtypeStruct((t, 3 * NA_W), F32)],
        compiler_params=_cparams(("arbitrary",)),
        name="in_proj",
    )(x2, g, shift, scale, w_bf)


def _softplus(x):
    return jnp.maximum(x, 0.0) + jnp.log(1.0 + jnp.exp(-jnp.abs(x)))


def _prep_kernel(xr_ref, prev_ref, next_ref, qkv_ref,
                 mu_ref, kk_ref, ka_ref, rk_ref, w0_ref, a0_ref,
                 w1_ref, w2_ref, a1_ref, a2_ref, g1_ref, g2_ref, qg_ref, kg_ref, ob_ref,
                 nkk_o, r_o, v_o, lw0_o, b0_o, k0_o, lw1_o, b1_o, k1_o,
                 gate_o, bonus_o, naq_o, nak_o, nav_o):
    i = pl.program_id(1)
    nblk = pl.num_programs(1)
    x = xr_ref[0]
    tm = x.shape[0]
    prow = jnp.where(i == 0, 0.0, prev_ref[0][7:8, :])
    nrow = jnp.where(i == nblk - 1, 0.0, next_ref[0][0:1, :])
    rid = lax.broadcasted_iota(jnp.int32, x.shape, 0)
    prev = jnp.where(rid == 0, prow, pltpu.roll(x, 1, axis=0))
    nxt = jnp.where(rid == tm - 1, nrow, pltpu.roll(x, tm - 1, axis=0))
    xs = x + (0.5 * (prev + nxt) - x) * mu_ref[...]
    r = xs[:, 0:RW]
    k = xs[:, RW:2 * RW]
    v = xs[:, 2 * RW:3 * RW]
    z = xs[:, 3 * RW:4 * RW]
    seg = lambda t: _seg_sum(t, ob_ref[...])
    zb = z.astype(BF16)
    bdot = lambda a, w: jnp.dot(a.astype(BF16), w, preferred_element_type=F32)
    gate_o[0] = bdot(jax.nn.sigmoid(bdot(zb, g1_ref[...])), g2_ref[...])
    kk = k * kk_ref[...]
    kk = kk / jnp.maximum(jnp.sqrt(seg(kk * kk)), 1e-12)
    nkk_o[0] = -kk
    r_o[0] = r
    v_o[0] = v
    bonus_o[0] = seg(r * k * rk_ref[...]) * v
    outs = ((lw0_o, b0_o, k0_o), (lw1_o, b1_o, k1_o))
    for d in range(2):
        wl = w0_ref[d:d + 1, :] + bdot(jnp.tanh(bdot(zb, w1_ref[d])), w2_ref[d])
        w = -_softplus(-wl) - 0.5
        a = jax.nn.sigmoid(a0_ref[d:d + 1, :] + bdot(bdot(zb, a1_ref[d]), a2_ref[d]))
        lw_o, b_o, k_o = outs[d]
        lw_o[0] = -jnp.exp(w)
        b_o[0] = kk * a
        k_o[0] = k * (1.0 + (a - 1.0) * ka_ref[...])
    qkv = qkv_ref[0]
    segm = lambda t: seg(t) * (1.0 / HEAD)
    qn = qkv[:, 0:NA_W]
    kn = qkv[:, NA_W:2 * NA_W]
    naq_o[0] = (qn * lax.rsqrt(segm(qn * qn) + RMS_EPS) * qg_ref[...] * (HEAD ** -0.5)).astype(BF16)
    nak_o[0] = (kn * lax.rsqrt(segm(kn * kn) + RMS_EPS) * kg_ref[...]).astype(BF16)
    nav_o[0] = qkv[:, 2 * NA_W:].astype(BF16)


def _prep(xr, qkv, p):
    bsz, seq, _ = xr.shape
    tm = 256
    nb = seq // tm
    h8 = tm // 8
    blk = lambda w: pl.BlockSpec((1, tm, w), lambda b, i: (b, i, 0))
    full = lambda a: pl.BlockSpec(a.shape, lambda b, i, _n=a.ndim: (0,) * _n)
    params = [p["mu"], p["k_k"], p["k_a"], p["r_k"], p["w0"], p["a0"], p["w1"], p["w2"], p["a1"], p["a2"],
              p["g1"], p["g2"], p["q_g"], p["k_g"], p["ob"]]
    f32o = jax.ShapeDtypeStruct((bsz, seq, RW), F32)
    bfo = jax.ShapeDtypeStruct((bsz, seq, NA_W), BF16)
    return pl.pallas_call(
        _prep_kernel,
        grid=(bsz, nb),
        in_specs=[blk(4 * RW),
                  pl.BlockSpec((1, 8, 4 * RW), lambda b, i: (b, jnp.maximum(i * h8 - 1, 0), 0)),
                  pl.BlockSpec((1, 8, 4 * RW), lambda b, i: (b, jnp.minimum((i + 1) * h8, seq // 8 - 1), 0)),
                  blk(3 * NA_W)] + [full(a) for a in params],
        out_specs=[blk(RW)] * 14,
        out_shape=[f32o] * 11 + [bfo] * 3,
        compiler_params=_cparams(("arbitrary", "arbitrary")),
        name="mixer_prep",
    )(xr, xr, xr, qkv, *params)


HEAD_PAIR = LANES // HEAD
WKV_PAIRS = RW // LANES
WKV_DOUBLINGS = WKV_CHUNK.bit_length() - 2


def _nt_dot(a, b):
    return lax.dot_general(a, b, (((1,), (1,)), ((), ())), preferred_element_type=F32)


def _wkv_kernel(*refs, nrows):
    f_in = refs[0:6]
    b_in = refs[6:12]
    tri_ref, msk_ref, eye_ref = refs[12:15]
    yf_ref, yb_ref = refs[15:17]
    s_ref = refs[17]
    c = pl.program_id(0)
    bi = pl.program_id(1)
    tc = WKV_CHUNK

    @pl.when(c == 0)
    def _():
        for row in range(nrows):
            s_ref[bi * nrows + row] = jnp.zeros(s_ref.shape[1:], F32)

    first_head = lax.broadcasted_iota(jnp.int32, (tc, LANES), 1) < HEAD
    eye_bf = eye_ref[...]
    eye_f = eye_bf.astype(F32)

    def blk(z):
        return jnp.concatenate([jnp.where(first_head, z, 0.0), jnp.where(first_head, 0.0, z)], axis=0)

    bdot = lambda p, q: jnp.dot(p, q, preferred_element_type=F32)
    units = [(row, d, p) for row in range(nrows) for d in range(2) for p in range(WKV_PAIRS)]
    every = lambda fn, *cols: [fn(*args) for args in zip(*cols)]
    states = [s_ref[bi * nrows + row, d, p] for row, d, p in units]
    masks = [(msk_ref[d, 0] > 0.5, msk_ref[d, 1] > 0.5) for d in range(2)]

    def load(row, d, p):
        src = f_in if d == 0 else b_in
        return [s[row, :, p * LANES:(p + 1) * LANES] for s in src]

    def decays(unit, data):
        d = unit[1]
        lw = data[3]
        cum = jnp.zeros_like(lw)
        rest = lw
        for _ in range(3):
            term = rest.astype(BF16)
            cum = cum + bdot(tri_ref[d], term)
            rest = rest - term.astype(F32)
        cend = cum[tc - 1:tc] if d == 0 else cum[0:1]
        return cum, cend

    def operands(data, dec):
        a, r, v, lw, bb, kk = data
        cum, cend = dec
        e_neg = jnp.exp(-cum)
        e_end = jnp.exp(cend - cum)
        x = jnp.concatenate([blk(a * jnp.exp(cum - lw)), blk(r * jnp.exp(cum))], axis=0).astype(BF16)
        y = jnp.concatenate([blk(bb * e_neg), blk(kk * e_neg)], axis=0).astype(BF16)
        z = jnp.concatenate([blk(bb * e_end), blk(kk * e_end)], axis=0).astype(BF16)
        return x, y, z, blk(v)

    def causal(unit, g):
        strict, incl = masks[unit[1]]
        return (jnp.where(strict, g[:2 * tc, :2 * tc], 0.0), jnp.where(strict, g[:2 * tc, 2 * tc:], 0.0),
                jnp.concatenate([jnp.where(incl, g[2 * tc:, :2 * tc], 0.0),
                                 jnp.where(incl, g[2 * tc:, 2 * tc:], 0.0)], axis=1).astype(BF16))

    data = every(load, *zip(*units))
    dec = every(decays, units, data)
    ops = every(operands, data, dec)
    grams = every(lambda o: _nt_dot(o[0], o[1]), ops)
    nmat = every(causal, units, grams)
    ph = every(lambda o, st: _nt_dot(o[0], st.astype(BF16)), ops, states)
    vbf = every(lambda o: o[3].astype(BF16), ops)
    rhs = every(lambda q, n, vb: q[:2 * tc] + bdot(n[1].astype(BF16), vb), ph, nmat, vbf)
    inv = every(lambda n: eye_f + n[0], nmat)
    pw = every(lambda n: n[0].astype(BF16), nmat)
    for _ in range(WKV_DOUBLINGS):
        pw = every(lambda q: bdot(q, q).astype(BF16), pw)
        inv = every(lambda t, q: t + bdot(t.astype(BF16), q), inv, pw)
    u = every(lambda t, q: bdot(t.astype(BF16), q.astype(BF16)), inv, rhs)
    uv = every(lambda q, vb: jnp.concatenate([q.astype(BF16), vb], axis=0), u, vbf)
    yo = every(lambda q, n, w: q[2 * tc:] + bdot(n[2], w), ph, nmat, uv)
    uvt = every(lambda w: _nt_dot(eye_bf, w).astype(BF16), uv)
    new = every(lambda st, dc, w, o: st * jnp.exp(dc[1]) + bdot(w, o[2]), states, dec, uvt, ops)
    for row in range(nrows):
        for d in range(2):
            out = yf_ref if d == 0 else yb_ref
            parts = [yo[units.index((row, d, p))] for p in range(WKV_PAIRS)]
            out[row] = jnp.concatenate([q[:tc] + q[tc:] for q in parts], axis=1)
    for (row, d, p), st in zip(units, new):
        s_ref[bi * nrows + row, d, p] = st


def _wkv_masks():
    tc = WKV_CHUNK
    t = np.arange(tc)
    tri = np.stack([t[None, :] <= t[:, None], t[None, :] >= t[:, None]]).astype(np.float32)
    head = np.arange(HEAD_PAIR * tc) // tc
    tt = np.arange(HEAD_PAIR * tc) % tc
    same = head[:, None] == head[None, :]
    m = np.stack([np.stack([same & (tt[None, :] < tt[:, None]), same & (tt[None, :] <= tt[:, None])]),
                  np.stack([same & (tt[None, :] > tt[:, None]), same & (tt[None, :] >= tt[:, None])])])
    return jnp.asarray(tri, BF16), jnp.asarray(m.astype(np.float32)), jnp.asarray(np.eye(LANES, dtype=np.float32), BF16)


def _wkv_scan(ins):
    bsz, seq, _ = ins["nkk"].shape
    tc = WKV_CHUNK
    nc = seq // tc
    tri, msk, eye = _wkv_masks()
    nrows = 2 if bsz % 2 == 0 else 1
    fwd = pl.BlockSpec((nrows, tc, RW), lambda c, b: (b, c, 0))
    bwd = pl.BlockSpec((nrows, tc, RW), lambda c, b: (b, nc - 1 - c, 0))
    full = lambda a: pl.BlockSpec(a.shape, lambda c, b, _n=a.ndim: (0,) * _n)
    f_args = [ins["nkk"], ins["r"], ins["v"], ins["lw0"], ins["b0"], ins["k0"]]
    b_args = [ins["nkk"], ins["r"], ins["v"], ins["lw1"], ins["b1"], ins["k1"]]
    o = jax.ShapeDtypeStruct((bsz, seq, RW), F32)
    return pl.pallas_call(
        functools.partial(_wkv_kernel, nrows=nrows),
        grid=(nc, bsz // nrows),
        in_specs=[fwd] * 6 + [bwd] * 6 + [full(tri), full(msk), full(eye)],
        out_specs=[fwd, bwd],
        out_shape=[o, o],
        scratch_shapes=[pltpu.VMEM((bsz, 2, WKV_PAIRS, LANES, LANES), F32)],
        compiler_params=_cparams(("arbitrary", "arbitrary")),
        name="wkv_scan",
    )(*f_args, *b_args, tri, msk, eye)


def _gelu_tanh(x):
    return 0.5 * x * (1.0 + jnp.tanh(math.sqrt(2.0 / math.pi) * (x + 0.044715 * (x * x * x))))


def _s5_kernel(ua_ref, ub_ref, bblk_ref, cblk_ref, lam_ref, lamc_ref, d_ref, glu_ref, o_ref,
               y_ref, st_ref, bu_ref, end_ref, carry_ref, *, seq, nb):
    ch = S5_CHUNK
    nc = seq // ch
    n = S5_FLAT
    u_halves = (ua_ref, ub_ref)
    for b in range(nb):
        for hf in range(2):
            y_ref[b, hf] = u_halves[hf][b] * d_ref[:, hf * LANES:(hf + 1) * LANES]

    def cmul_add(lre, lim, s, add):
        sre = s[:, :n]
        sim = s[:, n:]
        return jnp.concatenate([lre * sre - lim * sim + add[:, :n],
                                lre * sim + lim * sre + add[:, n:]], axis=1)

    for d in range(2):
        lre = lam_ref[d, 0:1, :]
        lim = lam_ref[d, 1:2, :]
        lcre = lamc_ref[d, 0:1, :]
        lcim = lamc_ref[d, 1:2, :]
        tloc = (lambda i: i) if d == 0 else (lambda i: ch - 1 - i)
        cloc = (lambda i: i) if d == 0 else (lambda i: nc - 1 - i)

        def project(i, slot):
            tl = tloc(jnp.minimum(i, ch - 1))
            rows = jnp.concatenate(
                [jnp.concatenate([r[b, pl.ds(tl, nc, stride=ch), :] for r in u_halves], axis=1)
                 for b in range(nb)], axis=0)
            bu_ref[slot] = jnp.dot(rows.astype(BF16), bblk_ref[d], preferred_element_type=F32)

        def advance(slot):
            st_ref[...] = cmul_add(lre, lim, st_ref[...], bu_ref[slot])

        def emit(i):
            yr = jnp.dot(st_ref[...].astype(BF16), cblk_ref[d], preferred_element_type=F32)
            idx = pl.ds(tloc(i), nc, stride=ch)
            for b in range(nb):
                for hf in range(2):
                    y_ref[b, hf, idx, :] = (y_ref[b, hf, idx, :]
                                            + yr[b * nc:(b + 1) * nc, hf * LANES:(hf + 1) * LANES])

        st_ref[...] = jnp.zeros_like(st_ref)
        project(0, 0)

        def p1(j, c):
            project(2 * j + 1, 1)
            advance(0)
            project(2 * j + 2, 0)
            advance(1)
            return c

        lax.fori_loop(0, ch // 2, p1, 0)
        end_ref[...] = st_ref[...]

        def cs(i, car):
            c = cloc(i)
            for b in range(nb):
                carry_ref[pl.ds(b * nc + c, 1), :] = car[b:b + 1]
            ends = jnp.concatenate([end_ref[pl.ds(b * nc + c, 1), :] for b in range(nb)], axis=0)
            return cmul_add(lcre, lcim, car, ends)

        lax.fori_loop(0, nc, cs, jnp.zeros((nb, 2 * n), F32))

        st_ref[...] = carry_ref[...]
        project(0, 0)
        project(1, 1)
        advance(0)

        def p2(j, c):
            i = 2 * j + 1
            project(i + 1, 0)
            emit(i - 1)
            advance(1)
            project(i + 2, 1)
            emit(i)
            advance(0)
            return c

        lax.fori_loop(0, (ch - 2) // 2, p2, 0)
        emit(ch - 2)
        advance(1)
        emit(ch - 1)

    glu = glu_ref[...]
    for b in range(nb):
        g = _gelu_tanh(jnp.concatenate([y_ref[b, 0], y_ref[b, 1]], axis=1))
        o_ref[b] = (g * jax.nn.sigmoid(jnp.dot(g.astype(BF16), glu, preferred_element_type=F32))).astype(o_ref.dtype)


def _s5_params(lam_re, lam_im, log_dt, b_re, b_im, c_re, c_im):
    lre = lam_re.astype(F32)
    lim = lam_im.astype(F32)
    dt = jnp.exp(log_dt.astype(F32))[..., None]

    def cexp(scale):
        mag = jnp.exp(lre * dt * scale)
        return mag * jnp.cos(lim * dt * scale), mag * jnp.sin(lim * dt * scale)

    bar_re, bar_im = cexp(1.0)
    den = lre * lre + lim * lim
    f_re = ((bar_re - 1.0) * lre + bar_im * lim) / den
    f_im = (bar_im * lre - (bar_re - 1.0) * lim) / den
    bm_re = b_re.astype(F32)
    bm_im = b_im.astype(F32)
    bb_re = f_re[..., None] * bm_re - f_im[..., None] * bm_im
    bb_im = f_re[..., None] * bm_im + f_im[..., None] * bm_re
    eye_g = jnp.eye(S5_GROUPS, dtype=F32)

    def blockdiag_in(m):
        return jnp.einsum("dgph,gk->dghkp", m, eye_g).reshape(2, S5_WIDTH, S5_FLAT)

    def blockdiag_out(m):
        return jnp.einsum("dghp,gk->dgpkh", m, eye_g).reshape(2, S5_FLAT, S5_WIDTH)

    bblk = jnp.concatenate([blockdiag_in(bb_re), blockdiag_in(bb_im)], axis=2)
    cblk = jnp.concatenate([blockdiag_out(c_re.astype(F32)), -blockdiag_out(c_im.astype(F32))], axis=1)
    flat = lambda z: jnp.stack([z[0].reshape(2, S5_FLAT), z[1].reshape(2, S5_FLAT)], axis=1)
    return bblk.astype(BF16), cblk.astype(BF16), flat((bar_re, bar_im)), flat(cexp(float(S5_CHUNK)))


def _s5_mixer(u, bblk, cblk, lam, lamc, d_skip, glu_bf):
    bsz, seq, w = u.shape
    nc = seq // S5_CHUNK
    nb = 2 if bsz % 2 == 0 else 1
    full = lambda a: pl.BlockSpec(a.shape, lambda b, _n=a.ndim: (0,) * _n)
    args = [bblk, cblk, lam, lamc, d_skip, glu_bf]
    state = pltpu.VMEM((nb * nc, 2 * S5_FLAT), F32)
    return pl.pallas_call(
        functools.partial(_s5_kernel, seq=seq, nb=nb),
        grid=(bsz // nb,),
        in_specs=[pl.BlockSpec((nb, seq, LANES), lambda b: (b, 0, 0)),
                  pl.BlockSpec((nb, seq, LANES), lambda b: (b, 0, 1))] + [full(a) for a in args],
        out_specs=pl.BlockSpec((nb, seq, w), lambda b: (b, 0, 0)),
        out_shape=jax.ShapeDtypeStruct((bsz, seq, w), BF16),
        scratch_shapes=[pltpu.VMEM((nb, w // LANES, seq, LANES), F32),
                        state,
                        pltpu.VMEM((2, nb * nc, 2 * S5_FLAT), F32),
                        state,
                        state],
        compiler_params=_cparams(("arbitrary",)),
        name="s5_mixer",
    )(u, u, *args)


def _na_bias_table(rpb):
    q_col = np.arange(GRID_W)
    c_start = np.clip(q_col - NA_KW // 2, 0, GRID_W - NA_KW)
    k_col = np.arange(GRID_W)
    valid = (k_col[None, :] >= c_start[:, None]) & (k_col[None, :] < c_start[:, None] + NA_KW)
    dx = np.clip(k_col[None, :] - q_col[:, None] + NA_KW - 1, 0, 2 * NA_KW - 2)
    pick = (np.arange(2 * NA_KW - 1)[:, None, None] == dx[None]).astype(np.float32)
    base = jnp.einsum("hyd,dqk->hyqk", rpb.astype(F32), jnp.asarray(pick), precision=HIGHEST)
    base = jnp.where(jnp.asarray(valid)[None, None], base, -jnp.inf)
    tab = jnp.stack([base[:, NA_KH - 1 - o:2 * NA_KH - 1 - o] for o in range(NA_KH)], axis=1)
    tab = jnp.transpose(tab, (0, 1, 3, 2, 4))
    return tab.reshape(rpb.shape[0], NA_KH, GRID_W, NA_KH * GRID_W)


def _na_kernel(q_ref, k_ref, v_ref, bias_ref, o_ref, *, rows, rblk):
    rb = pl.program_id(1)
    lane = lax.broadcasted_iota(jnp.int32, (GRID_W, LANES), 1)
    low = lane < HEAD

    npair = NA_W // LANES
    every = lambda fn, *cols: [fn(*args) for args in zip(*cols)]

    def row_pair(jj, carry):
        js = [2 * jj, 2 * jj + 1]
        units = [(u, c) for u in range(2) for c in range(npair)]
        loaded = []
        for j in js:
            r = rb * rblk + j
            rs = jnp.clip(r - NA_KH // 2, 0, rows - NA_KH)
            loaded.append((q_ref[0, j],
                           k_ref[0, pl.ds(rs, NA_KH)].reshape(NA_KH * GRID_W, NA_W),
                           v_ref[0, pl.ds(rs, NA_KH)].reshape(NA_KH * GRID_W, NA_W),
                           r - rs))

        def scores(u, c):
            q, kmat, _, off = loaded[u]
            sl = slice(c * LANES, (c + 1) * LANES)
            q2 = q[:, sl].astype(F32)
            lhs = jnp.concatenate([jnp.where(low, q2, 0.0), jnp.where(low, 0.0, q2)], axis=0).astype(BF16)
            s = lax.dot_general(lhs, kmat[:, sl], (((1,), (1,)), ((), ())), preferred_element_type=F32)
            return s + jnp.concatenate([bias_ref[2 * c, off], bias_ref[2 * c + 1, off]], axis=0)

        s = every(scores, *zip(*units))
        m = every(lambda t: jnp.max(t, axis=-1, keepdims=True), s)
        p = every(lambda t, mx: jnp.exp(t - mx), s, m)
        l = every(lambda t: jnp.sum(t, axis=-1, keepdims=True), p)
        o = every(lambda t, den, uc: jnp.dot(t.astype(BF16), loaded[uc[0]][2][:, uc[1] * LANES:(uc[1] + 1) * LANES],
                                             preferred_element_type=F32) / den, p, l, units)
        o = every(lambda t: jnp.where(low, t[:GRID_W], t[GRID_W:]), o)
        for u, j in enumerate(js):
            o_ref[0, j] = jnp.concatenate(o[u * npair:(u + 1) * npair], axis=1).astype(o_ref.dtype)
        return carry

    lax.fori_loop(0, rblk // 2, row_pair, 0)


def _na_mixer(q, k, v, bias):
    bsz, seq, w = q.shape
    rows = seq // GRID_W
    rblk = 8
    g4 = lambda a: a.reshape(bsz, rows, GRID_W, w)
    img = pl.BlockSpec((1, rows, GRID_W, w), lambda b, i: (b, 0, 0, 0))
    blk = pl.BlockSpec((1, rblk, GRID_W, w), lambda b, i: (b, i, 0, 0))
    out = pl.pallas_call(
        functools.partial(_na_kernel, rows=rows, rblk=rblk),
        grid=(bsz, rows // rblk),
        in_specs=[blk, img, img, pl.BlockSpec(bias.shape, lambda b, i: (0, 0, 0, 0))],
        out_specs=blk,
        out_shape=jax.ShapeDtypeStruct((bsz, rows, GRID_W, w), BF16),
        compiler_params=_cparams(("arbitrary", "arbitrary")),
        name="na_mixer",
    )(g4(q), g4(k), g4(v), bias)
    return out.reshape(bsz, seq, w)


def _outproj_kernel(x_ref, s5_ref, yf_ref, yb_ref, gate_ref, bonus_ref, na_ref,
                    lnw_ref, lnb_ref, obm_ref, w_ref, gm_ref, g2_ref, sh_ref, sc_ref,
                    xo_ref, h_ref):
    segm = lambda t: _seg_sum(t, obm_ref[...])
    y = yf_ref[...] + yb_ref[...]
    yc = y - segm(y)
    yn = yc * lax.rsqrt(segm(yc * yc) + RWKV_GN_EPS) * lnw_ref[...] + lnb_ref[...]
    rw = (yn + bonus_ref[...]) * gate_ref[...]
    mixed = jnp.concatenate([s5_ref[...].astype(BF16), rw.astype(BF16), na_ref[...].astype(BF16)], axis=1)
    xo = x_ref[...] + gm_ref[0] * jnp.dot(mixed, w_ref[...], preferred_element_type=F32)
    xo_ref[...] = xo
    h_ref[...] = _rms_mod(xo, g2_ref[...], sh_ref[0], sc_ref[0])


def _out_proj(x2, s5o, yf, yb, gate, bonus, nao, lnw, lnb, obm, w_bf, gate_mix, g2, shift, scale, seq):
    t, d = x2.shape
    tm = 256
    per_b = seq // tm
    row = lambda w: pl.BlockSpec((tm, w), lambda i: (i, 0))
    full = lambda a: pl.BlockSpec(a.shape, lambda i, _n=a.ndim: (0,) * _n)
    bvec = pl.BlockSpec((1, 1, d), lambda i: (i // per_b, 0, 0))
    o = jax.ShapeDtypeStruct((t, d), F32)
    return pl.pallas_call(
        _outproj_kernel,
        grid=(t // tm,),
        in_specs=[row(d), row(S5_WIDTH)] + [row(RW)] * 5 +
                 [full(lnw), full(lnb), full(obm), full(w_bf), bvec, full(g2), bvec, bvec],
        out_specs=[row(d), row(d)],
        out_shape=[o, o],
        compiler_params=_cparams(("arbitrary",)),
        name="out_proj",
    )(x2, s5o, yf, yb, gate, bonus, nao, lnw, lnb, obm, w_bf, gate_mix, g2, shift, scale)


MOE_TILE = 256
SEG_ALIGN = 8
MOE_SLOTS = -(-(MOE_TILE * TOP_K + N_EXPERTS * (SEG_ALIGN - 1)) // LANES) * LANES


def _router_kernel(h_ref, w_ref, b_ref, tri_ref, upper_ref, slot_ref, gate_ref, cnt_ref, base_ref, loc_ref,
                   carry_ref):
    @pl.when(pl.program_id(0) == 0)
    def _():
        carry_ref[...] = jnp.zeros_like(carry_ref)

    logits = jnp.dot(h_ref[...], w_ref[...], precision=HIGHEST, preferred_element_type=F32) + b_ref[...]
    tm = logits.shape[0]
    lane = lax.broadcasted_iota(jnp.int32, (tm, LANES), 1)
    lane_f = lane.astype(F32)
    vals, hots = [], []
    cur = logits
    for _ in range(TOP_K):
        m = jnp.max(cur, axis=-1, keepdims=True)
        idx = jnp.min(jnp.where(cur == m, lane_f, float(LANES)), axis=-1, keepdims=True)
        hot = lane_f == idx
        vals.append(m)
        hots.append(hot)
        cur = jnp.where(hot, -jnp.inf, cur)
    exps = [jnp.exp(v - vals[0]) for v in vals]
    den = exps[0] + exps[1] + exps[2] + exps[3]
    assign = sum(h.astype(F32) for h in hots)
    before = jnp.dot(tri_ref[...], assign.astype(BF16), preferred_element_type=F32)
    cnt = jnp.sum(assign, axis=0, keepdims=True)
    cnt = jnp.floor((cnt + (SEG_ALIGN - 1)) * (1.0 / SEG_ALIGN)) * SEG_ALIGN
    cnt8 = jnp.broadcast_to(cnt, (8, LANES)).astype(BF16)
    loc = jnp.dot(cnt8, upper_ref[...], preferred_element_type=F32)[0:1, :]
    place = before + loc
    s_out = jnp.zeros((tm, LANES), jnp.int32)
    g_out = jnp.zeros((tm, LANES), F32)
    for kk in range(TOP_K):
        slot = jnp.sum(jnp.where(hots[kk], place, 0.0), axis=-1, keepdims=True)
        sel = lane == kk
        s_out = jnp.where(sel, slot.astype(jnp.int32), s_out)
        g_out = jnp.where(sel, exps[kk] / den, g_out)
    slot_ref[...] = s_out
    gate_ref[...] = g_out
    cnt_ref[0] = cnt.astype(jnp.int32)
    base_ref[0] = carry_ref[...].astype(jnp.int32)
    loc_ref[0] = loc.astype(jnp.int32)
    carry_ref[...] = carry_ref[...] + cnt


def _router(h2, rw_pad, rb_pad):
    t, d = h2.shape
    tm = MOE_TILE
    nt = t // tm
    tri = jnp.asarray(np.tril(np.ones((tm, tm), np.float32), -1), BF16)
    upper = jnp.asarray(np.triu(np.ones((LANES, LANES), np.float32), 1), BF16)
    row = pl.BlockSpec((tm, LANES), lambda i: (i, 0))
    per_tile = pl.BlockSpec((1, 1, LANES), lambda i: (i, 0, 0))
    full = lambda a: pl.BlockSpec(a.shape, lambda i, _n=a.ndim: (0,) * _n)
    tile_i32 = jax.ShapeDtypeStruct((nt, 1, LANES), jnp.int32)
    return pl.pallas_call(
        _router_kernel,
        grid=(nt,),
        in_specs=[pl.BlockSpec((tm, d), lambda i: (i, 0)), full(rw_pad), full(rb_pad), full(tri), full(upper)],
        out_specs=[row, row, per_tile, per_tile, per_tile],
        out_shape=[jax.ShapeDtypeStruct((t, LANES), jnp.int32),
                   jax.ShapeDtypeStruct((t, LANES), F32),
                   tile_i32, tile_i32, tile_i32],
        scratch_shapes=[pltpu.VMEM((1, LANES), F32)],
        compiler_params=_cparams(("arbitrary",)),
        name="moe_router",
    )(h2, rw_pad, rb_pad, tri, upper)


SEG_PIECES = tuple(SEG_ALIGN << s for s in range((MOE_TILE // SEG_ALIGN).bit_length()))


def _segment_dmas(cnt_ref, loc_ref, row_ref, tile, ne, make_copy, wait):
    def per_expert(e, carry):
        i = tile * ne + e
        n = cnt_ref[i]
        off = loc_ref[i]
        row = row_ref[i]
        for p in SEG_PIECES:
            has = (n & p) != 0

            @pl.when(has)
            def _(off=off, row=row, p=p):
                cp = make_copy(pl.multiple_of(off, SEG_ALIGN), pl.multiple_of(row, SEG_ALIGN), p)
                if wait:
                    cp.wait()
                else:
                    cp.start()

            step = jnp.where(has, p, 0)
            off = off + step
            row = row + step
        return carry

    lax.fori_loop(0, ne, per_expert, 0)


def _dispatch_kernel(cnt_ref, loc_ref, row_ref, zcnt_ref, zoff_ref, zrow_ref, nu_ref,
                     slot_ref, h_ref, xb_ref, sorted_ref, zero_ref, sem, zsem, *, ne, nblk):
    i = pl.program_id(0)
    last = pl.num_programs(0) - 1
    cur = i % 2
    tm = h_ref.shape[0]
    ns = MOE_SLOTS

    def zero_copy(off, row, p):
        return pltpu.make_async_copy(zero_ref.at[pl.ds(off, p), :], xb_ref.at[pl.ds(row, p), :], zsem)

    def tail_copy(j):
        row = pl.multiple_of((nu_ref[0] + j) * MOE_BLOCK, MOE_BLOCK)
        return pltpu.make_async_copy(zero_ref, xb_ref.at[pl.ds(row, MOE_BLOCK), :], zsem)

    def zero_fill(wait):
        _segment_dmas(zcnt_ref, zoff_ref, zrow_ref, 0, ne, zero_copy, wait)

        def tail(j, c):
            if wait:
                tail_copy(j).wait()
            else:
                tail_copy(j).start()
            return c

        lax.fori_loop(0, nblk - nu_ref[0], tail, 0)

    @pl.when(i == 0)
    def _():
        zero_ref[...] = jnp.zeros_like(zero_ref)
        zero_fill(wait=False)

    slot_t = jnp.transpose(slot_ref[...].astype(F32))
    srow = lax.broadcasted_iota(jnp.int32, (ns, tm), 0).astype(F32)
    pick = jnp.zeros((ns, tm), F32)
    for kk in range(TOP_K):
        pick = jnp.where(srow == slot_t[kk:kk + 1, :], 1.0, pick)
    sorted_ref[cur] = jnp.dot(pick.astype(BF16), h_ref[...].astype(BF16), preferred_element_type=F32)

    def copy_from(buf):
        def copy(off, row, p):
            return pltpu.make_async_copy(sorted_ref.at[buf, pl.ds(off, p), :], xb_ref.at[pl.ds(row, p), :],
                                         sem.at[buf])
        return copy

    _segment_dmas(cnt_ref, loc_ref, row_ref, i, ne, copy_from(cur), wait=False)

    @pl.when(i > 0)
    def _():
        _segment_dmas(cnt_ref, loc_ref, row_ref, i - 1, ne, copy_from(1 - cur), wait=True)

    @pl.when(i == last)
    def _():
        _segment_dmas(cnt_ref, loc_ref, row_ref, i, ne, copy_from(cur), wait=True)
        zero_fill(wait=True)


def _dispatch(cnt, loc, rowstart, zcnt, zrow, n_used, slot, h2, n_rows, ne):
    t, d = h2.shape
    tm = MOE_TILE
    zoff = jnp.zeros_like(zcnt)
    grid_spec = pltpu.PrefetchScalarGridSpec(
        num_scalar_prefetch=7,
        grid=(t // tm,),
        in_specs=[pl.BlockSpec((tm, LANES), lambda i, *_: (i, 0)),
                  pl.BlockSpec((tm, d), lambda i, *_: (i, 0))],
        out_specs=pl.BlockSpec(memory_space=pl.ANY),
        scratch_shapes=[pltpu.VMEM((2, MOE_SLOTS, d), F32), pltpu.VMEM((MOE_BLOCK, d), F32),
                        pltpu.SemaphoreType.DMA((2,)), pltpu.SemaphoreType.DMA(())],
    )
    return pl.pallas_call(
        functools.partial(_dispatch_kernel, ne=ne, nblk=n_rows // MOE_BLOCK),
        grid_spec=grid_spec,
        out_shape=jax.ShapeDtypeStruct((n_rows, d), F32),
        compiler_params=_cparams(("arbitrary",)),
        name="moe_dispatch",
    )(cnt, loc, rowstart, zcnt, zoff, zrow, n_used, slot, h2)


PAIR_GROUP = 2 * LANES


def _pair_perm():
    p = np.zeros((PAIR_GROUP, PAIR_GROUP), np.float32)
    j = np.arange(LANES)
    p[2 * j, j] = 1.0
    p[2 * j + 1, LANES + j] = 1.0
    return jnp.asarray(p, BF16)


def _expert_kernel(be_ref, nu_ref, x_ref, w1_ref, b1_ref, w2_ref, b2_ref, perm_ref, y_ref, w1s_ref, w2s_ref):
    i = pl.program_id(0)
    f2 = w1_ref.shape[2]
    ngrp = f2 // PAIR_GROUP

    @pl.when(i >= nu_ref[0])
    def _():
        y_ref[...] = jnp.zeros_like(y_ref)

    @pl.when(i < nu_ref[0])
    def _():
        @pl.when((i == 0) | (be_ref[i] != be_ref[jnp.maximum(i - 1, 0)]))
        def _():
            for g in range(ngrp):
                sl = slice(g * PAIR_GROUP, (g + 1) * PAIR_GROUP)
                w1s_ref[:, sl] = jnp.dot(w1_ref[0, :, sl].astype(BF16), perm_ref[...],
                                         preferred_element_type=F32).astype(BF16)
            w2s_ref[...] = w2_ref[0].astype(BF16)

        hdn = jnp.dot(x_ref[...].astype(BF16), w1s_ref[...], preferred_element_type=F32) + b1_ref[0]
        glu = jnp.concatenate([hdn[:, g * PAIR_GROUP:g * PAIR_GROUP + LANES] for g in range(ngrp)], axis=1)
        lin = jnp.concatenate([hdn[:, g * PAIR_GROUP + LANES:(g + 1) * PAIR_GROUP] for g in range(ngrp)], axis=1)
        glu = jnp.minimum(glu, SWIGLU_LIMIT)
        lin = jnp.clip(lin, -SWIGLU_LIMIT, SWIGLU_LIMIT)
        act = glu * jax.nn.sigmoid(SWIGLU_ALPHA * glu) * (lin + 1.0)
        y_ref[...] = jnp.dot(act.astype(BF16), w2s_ref[...], preferred_element_type=F32) + b2_ref[0]


def _experts(block_e, n_used, xb, w1, b1_grp, w2, b2, layer):
    n_rows, d = xb.shape
    _, ne, _, f2 = w1.shape
    dff = w2.shape[2]
    nblk = n_rows // MOE_BLOCK
    perm = _pair_perm()
    blk = lambda i, be, nu: (jnp.minimum(i, nu[0] - 1), 0)
    wsel = lambda i, be, nu: (layer, be[i], 0, 0)
    grid_spec = pltpu.PrefetchScalarGridSpec(
        num_scalar_prefetch=2,
        grid=(nblk,),
        in_specs=[pl.BlockSpec((MOE_BLOCK, d), blk),
                  pl.BlockSpec((None, 1, d, f2), wsel),
                  pl.BlockSpec((None, 1, 1, f2), wsel),
                  pl.BlockSpec((None, 1, dff, d), wsel),
                  pl.BlockSpec((None, 1, 1, d), wsel),
                  pl.BlockSpec(perm.shape, lambda i, be, nu: (0, 0))],
        out_specs=pl.BlockSpec((MOE_BLOCK, d), lambda i, be, nu: (i, 0)),
        scratch_shapes=[pltpu.VMEM((d, f2), BF16), pltpu.VMEM((dff, d), BF16)],
    )
    nl = w1.shape[0]
    return pl.pallas_call(
        _expert_kernel,
        grid_spec=grid_spec,
        out_shape=jax.ShapeDtypeStruct((n_rows, d), F32),
        compiler_params=_cparams(("arbitrary",)),
        name="moe_experts",
    )(block_e, n_used, xb, w1, b1_grp.reshape(nl, ne, 1, f2), w2, b2.reshape(nl, ne, 1, d), perm)


def _combine_kernel(cnt_ref, loc_ref, row_ref, slot_ref, gates_ref, x_ref, gf_ref, yb_ref, o_ref, sorted_ref, sem,
                    *, ne):
    i = pl.program_id(0)
    last = pl.num_programs(0) - 1
    cur = i % 2
    tm = x_ref.shape[0]
    ns = MOE_SLOTS

    def copy_into(buf):
        def copy(off, row, p):
            return pltpu.make_async_copy(yb_ref.at[pl.ds(row, p), :], sorted_ref.at[buf, pl.ds(off, p), :],
                                         sem.at[buf])
        return copy

    @pl.when(i == 0)
    def _():
        sorted_ref[...] = jnp.zeros_like(sorted_ref)
        _segment_dmas(cnt_ref, loc_ref, row_ref, i, ne, copy_into(cur), wait=False)

    @pl.when(i < last)
    def _():
        _segment_dmas(cnt_ref, loc_ref, row_ref, i + 1, ne, copy_into(1 - cur), wait=False)

    slot = slot_ref[...]
    gates = gates_ref[...]
    scol = lax.broadcasted_iota(jnp.int32, (tm, ns), 1)
    gmat = jnp.zeros((tm, ns), F32)
    for kk in range(TOP_K):
        gmat = jnp.where(scol == slot[:, kk:kk + 1], gates[:, kk:kk + 1], gmat)
    _segment_dmas(cnt_ref, loc_ref, row_ref, i, ne, copy_into(cur), wait=True)
    acc = jnp.dot(gmat.astype(BF16), sorted_ref[cur].astype(BF16), preferred_element_type=F32)
    o_ref[...] = x_ref[...] + gf_ref[0] * acc


def _combine(cnt, loc, rowstart, slot, gates, x2, gate_ffn, yb, seq, ne):
    t, d = x2.shape
    tm = MOE_TILE
    per_b = seq // tm
    grid_spec = pltpu.PrefetchScalarGridSpec(
        num_scalar_prefetch=3,
        grid=(t // tm,),
        in_specs=[pl.BlockSpec((tm, LANES), lambda i, *_: (i, 0)),
                  pl.BlockSpec((tm, LANES), lambda i, *_: (i, 0)),
                  pl.BlockSpec((tm, d), lambda i, *_: (i, 0)),
                  pl.BlockSpec((1, 1, d), lambda i, *_: (i // per_b, 0, 0)),
                  pl.BlockSpec(memory_space=pl.ANY)],
        out_specs=pl.BlockSpec((tm, d), lambda i, *_: (i, 0)),
        scratch_shapes=[pltpu.VMEM((2, MOE_SLOTS, d), F32), pltpu.SemaphoreType.DMA((2,))],
    )
    return pl.pallas_call(
        functools.partial(_combine_kernel, ne=ne),
        grid_spec=grid_spec,
        out_shape=jax.ShapeDtypeStruct((t, d), F32),
        compiler_params=_cparams(("arbitrary",)),
        name="moe_combine",
    )(cnt, loc, rowstart, slot, gates, x2, gate_ffn, yb)


def _group_pairs(b1):
    lead = b1.shape[:-1]
    g = b1.reshape(lead + (-1, LANES, 2))
    return jnp.swapaxes(g, -1, -2).reshape(b1.shape)


def _moe_layer(x2, h2, gate_ffn, router_w, router_b, w1, b1_grp, w2, b2, seq, layer):
    t, d = x2.shape
    ne = router_w.shape[1]
    rw_pad = jnp.zeros((d, LANES), F32).at[:, :ne].set(router_w.astype(F32))
    rb_pad = jnp.full((1, LANES), -jnp.inf, F32).at[0, :ne].set(router_b.astype(F32))
    slot, gates, cnt3, base3, loc3 = _router(h2, rw_pad, rb_pad)
    n_assign = t * TOP_K
    n_tiles = t // MOE_TILE
    n_blocks = -(-(n_assign + n_tiles * ne * (SEG_ALIGN - 1)) // MOE_BLOCK) + ne
    cnt = cnt3[:, 0, :ne]
    base = base3[:, 0, :ne]
    loc = loc3[:, 0, :ne]
    total = base[-1] + cnt[-1]
    padded = ((total + MOE_BLOCK - 1) // MOE_BLOCK) * MOE_BLOCK
    pad_end = jnp.cumsum(padded)
    pad_start = pad_end - padded
    rowstart = (pad_start[None, :] + base).reshape(-1).astype(jnp.int32)
    n_used = (pad_end[-1] // MOE_BLOCK).astype(jnp.int32)
    block_start = jnp.minimum(jnp.arange(n_blocks, dtype=jnp.int32), n_used - 1) * MOE_BLOCK
    block_e = jnp.minimum(jnp.sum(block_start[:, None] >= pad_end[None, :], axis=-1), ne - 1).astype(jnp.int32)
    cnt_f = cnt.reshape(-1)
    loc_f = loc.reshape(-1)
    n_used = n_used.reshape(1)
    xb = _dispatch(cnt_f, loc_f, rowstart, (padded - total).astype(jnp.int32), (pad_start + total).astype(jnp.int32),
                   n_used, slot, h2, n_blocks * MOE_BLOCK, ne)
    yb = _experts(block_e, n_used, xb, w1, b1_grp, w2, b2, layer)
    return _combine(cnt_f, loc_f, rowstart, slot, gates, x2, gate_ffn, yb, seq, ne)


def kernel(x, c, ada_w, ada_b, norm1_g, norm2_g, w_in, w_out, s5_lam_re, s5_lam_im, s5_log_dt, s5_b_re, s5_b_im, s5_c_re, s5_c_im, s5_d, s5_glu_w, rwkv_mu, rwkv_w0, rwkv_w1, rwkv_w2, rwkv_a0, rwkv_a1, rwkv_a2, rwkv_g1, rwkv_g2, rwkv_k_k, rwkv_k_a, rwkv_r_k, rwkv_ln_w, rwkv_ln_b, na_q_g, na_k_g, na_rpb, router_w, router_b, exp_w1, exp_b1, exp_w2, exp_b2):
    bsz, seq, d = x.shape
    depth = ada_w.shape[0]
    t = bsz * seq
    mod = _ada_mod(c, ada_w, ada_b).reshape(depth, bsz, 6, 1, d)
    x2 = x.reshape(t, d)
    seg_ones = _seg_ones(RW, dtype=BF16)
    seg_mean = _seg_ones(RW, dtype=BF16, scale=1.0 / HEAD)
    row = lambda a: a.reshape(1, -1).astype(F32)
    b1_grp = _group_pairs(exp_b1.astype(F32))
    exp_b2f = exp_b2.astype(F32)
    for l in range(depth):
        m = lambda j: mod[l, :, j]
        s5u, xr, qkv = _in_proj(x2, row(norm1_g[l]), m(0), m(1), w_in[l].astype(BF16), seq)
        prep_params = dict(
            mu=row(rwkv_mu[l]), k_k=row(rwkv_k_k[l]), k_a=row(rwkv_k_a[l]), r_k=row(rwkv_r_k[l]),
            w0=rwkv_w0[l].astype(F32), a0=rwkv_a0[l].astype(F32),
            w1=rwkv_w1[l].astype(BF16), w2=rwkv_w2[l].astype(BF16),
            a1=rwkv_a1[l].astype(BF16), a2=rwkv_a2[l].astype(BF16),
            g1=rwkv_g1[l].astype(BF16), g2=rwkv_g2[l].astype(BF16),
            q_g=row(jnp.tile(na_q_g[l], NA_W // HEAD)), k_g=row(jnp.tile(na_k_g[l], NA_W // HEAD)), ob=seg_ones)
        (nkk, r, v, lw0, b0, k0, lw1, b1, k1, gate, bonus, naq, nak, nav) = _prep(
            xr.reshape(bsz, seq, 4 * RW), qkv.reshape(bsz, seq, 3 * NA_W), prep_params)
        yf, yb = _wkv_scan(dict(nkk=nkk, r=r, v=v, lw0=lw0, b0=b0, k0=k0, lw1=lw1, b1=b1, k1=k1))
        bblk, cblk, lam, lamc = _s5_params(s5_lam_re[l], s5_lam_im[l], s5_log_dt[l], s5_b_re[l], s5_b_im[l],
                                           s5_c_re[l], s5_c_im[l])
        s5o = _s5_mixer(s5u.reshape(bsz, seq, S5_WIDTH), bblk, cblk, lam, lamc, row(s5_d[l]),
                        s5_glu_w[l].astype(BF16))
        nao = _na_mixer(naq, nak, nav, _na_bias_table(na_rpb[l]))
        flat = lambda a: a.reshape(t, -1)
        x2, h2 = _out_proj(x2, flat(s5o), flat(yf), flat(yb), flat(gate), flat(bonus), flat(nao),
                           row(rwkv_ln_w[l]), row(rwkv_ln_b[l]), seg_mean, w_out[l].astype(BF16),
                           m(2), row(norm2_g[l]), m(3), m(4), seq)
        x2 = _moe_layer(x2, h2, m(5), router_w[l], router_b[l], exp_w1, b1_grp, exp_w2, exp_b2f, seq, l)
    return x2.reshape(bsz, seq, d)
```

```python
import functools
import math

import numpy as np
import jax
import jax.numpy as jnp
from jax import lax
from jax.experimental import pallas as pl
from jax.experimental.pallas import tpu as pltpu

F32 = jnp.float32
BF16 = jnp.bfloat16
HIGHEST = lax.Precision.HIGHEST

D_MODEL = 1024
S5_WIDTH = 256
S5_GROUP = 16
S5_GROUPS = 16
S5_STATE = 64
S5_CHUNK = 64
S5_FLAT = S5_GROUPS * S5_STATE
RW = 384
HEAD = 64
RWKV_GN_EPS = 64e-5
NA_W = 384
GRID_W = 64
NA_KH = 8
NA_KW = 16
N_EXPERTS = 32
TOP_K = 4
MOE_BLOCK = 256
SWIGLU_ALPHA = 1.702
SWIGLU_LIMIT = 7.0
RMS_EPS = 1e-6
LANES = 128
WKV_CHUNK = 64
TOKEN_TILE = 512
VMEM_LIMIT = 56 * 1024 * 1024


def _cparams(sem):
    return pltpu.CompilerParams(dimension_semantics=sem, vmem_limit_bytes=VMEM_LIMIT)


def _seg_ones(n, seg=HEAD, dtype=F32, scale=1.0):
    idx = np.arange(n) // seg
    return jnp.asarray((idx[:, None] == idx[None, :]).astype(np.float32) * scale, dtype)


def _seg_sum(t, ones_bf):
    hi = t.astype(BF16)
    lo = (t - hi.astype(F32)).astype(BF16)
    return (jnp.dot(hi, ones_bf, preferred_element_type=F32) + jnp.dot(lo, ones_bf, preferred_element_type=F32))


def _ada_kernel(c_ref, w_ref, b_ref, o_ref):
    c = c_ref[...]
    cond = c * jax.nn.sigmoid(c)
    o_ref[0] = jnp.dot(cond, w_ref[0], preferred_element_type=F32) + b_ref[0]


def _ada_mod(c, ada_w, ada_b):
    nl, d, n6 = ada_w.shape
    bsz = c.shape[0]
    tn = 1536
    return pl.pallas_call(
        _ada_kernel,
        grid=(nl, n6 // tn),
        in_specs=[pl.BlockSpec((bsz, d), lambda l, j: (0, 0)),
                  pl.BlockSpec((1, d, tn), lambda l, j: (l, 0, j)),
                  pl.BlockSpec((1, 1, tn), lambda l, j: (l, 0, j))],
        out_specs=pl.BlockSpec((1, bsz, tn), lambda l, j: (l, 0, j)),
        out_shape=jax.ShapeDtypeStruct((nl, bsz, n6), F32),
        compiler_params=_cparams(("arbitrary", "arbitrary")),
        name="ada_mod",
    )(c, ada_w, ada_b.reshape(nl, 1, n6))


def _rms_mod(x, g, shift, scale):
    ms = jnp.mean(x * x, axis=-1, keepdims=True)
    h = x * lax.rsqrt(ms + RMS_EPS) * g
    return h * (1.0 + scale) + shift


def _proj_kernel(x_ref, g_ref, sh_ref, sc_ref, w_ref, o_s5, o_rw, o_na):
    h = _rms_mod(x_ref[...], g_ref[...], sh_ref[0], sc_ref[0])
    p = jnp.dot(h.astype(BF16), w_ref[...], preferred_element_type=F32)
    o_s5[...] = p[:, :S5_WIDTH]
    o_rw[...] = p[:, S5_WIDTH:S5_WIDTH + 4 * RW]
    o_na[...] = p[:, S5_WIDTH + 4 * RW:]


def _in_proj(x2, g, shift, scale, w_bf, seq):
    t, d = x2.shape
    n = w_bf.shape[1]
    tm = TOKEN_TILE
    per_b = seq // tm
    row = lambda i: (i, 0)
    bvec = lambda i: (i // per_b, 0, 0)
    return pl.pallas_call(
        _proj_kernel,
        grid=(t // tm,),
        in_specs=[pl.BlockSpec((tm, d), row),
                  pl.BlockSpec((1, d), lambda i: (0, 0)),
                  pl.BlockSpec((1, 1, d), bvec),
                  pl.BlockSpec((1, 1, d), bvec),
                  pl.BlockSpec((d, n), lambda i: (0, 0))],
        out_specs=[pl.BlockSpec((tm, S5_WIDTH), row),
                   pl.BlockSpec((tm, 4 * RW), row),
                   pl.BlockSpec((tm, 3 * NA_W), row)],
        out_shape=[jax.ShapeDtypeStruct((t, S5_WIDTH), F32),
                   jax.ShapeDtypeStruct((t, 4 * RW), F32),
                   jax.ShapeDtypeStruct((t, 3 * NA_W), F32)],
        compiler_params=_cparams(("arbitrary",)),
        name="in_proj",
    )(x2, g, shift, scale, w_bf)


def _softplus(x):
    return jnp.maximum(x, 0.0) + jnp.log(1.0 + jnp.exp(-jnp.abs(x)))


def _prep_kernel(xr_ref, prev_ref, next_ref, qkv_ref,
                 mu_ref, kk_ref, ka_ref, rk_ref, w0_ref, a0_ref,
                 w1_ref, w2_ref, a1_ref, a2_ref, g1_ref, g2_ref, qg_ref, kg_ref, ob_ref,
                 nkk_o, r_o, v_o, lw0_o, b0_o, k0_o, lw1_o, b1_o, k1_o,
                 gate_o, bonus_o, naq_o, nak_o, nav_o):
    i = pl.program_id(1)
    nblk = pl.num_programs(1)
    x = xr_ref[0]
    tm = x.shape[0]
    prow = jnp.where(i == 0, 0.0, prev_ref[0][7:8, :])
    nrow = jnp.where(i == nblk - 1, 0.0, next_ref[0][0:1, :])
    rid = lax.broadcasted_iota(jnp.int32, x.shape, 0)
    prev = jnp.where(rid == 0, prow, pltpu.roll(x, 1, axis=0))
    nxt = jnp.where(rid == tm - 1, nrow, pltpu.roll(x, tm - 1, axis=0))
    xs = x + (0.5 * (prev + nxt) - x) * mu_ref[...]
    r = xs[:, 0:RW]
    k = xs[:, RW:2 * RW]
    v = xs[:, 2 * RW:3 * RW]
    z = xs[:, 3 * RW:4 * RW]
    seg = lambda t: _seg_sum(t, ob_ref[...])
    zb = z.astype(BF16)
    bdot = lambda a, w: jnp.dot(a.astype(BF16), w, preferred_element_type=F32)
    gate_o[0] = bdot(jax.nn.sigmoid(bdot(zb, g1_ref[...])), g2_ref[...])
    kk = k * kk_ref[...]
    kk = kk / jnp.maximum(jnp.sqrt(seg(kk * kk)), 1e-12)
    nkk_o[0] = -kk
    r_o[0] = r
    v_o[0] = v
    bonus_o[0] = seg(r * k * rk_ref[...]) * v
    outs = ((lw0_o, b0_o, k0_o), (lw1_o, b1_o, k1_o))
    for d in range(2):
        wl = w0_ref[d:d + 1, :] + bdot(jnp.tanh(bdot(zb, w1_ref[d])), w2_ref[d])
        w = -_softplus(-wl) - 0.5
        a = jax.nn.sigmoid(a0_ref[d:d + 1, :] + bdot(bdot(zb, a1_ref[d]), a2_ref[d]))
        lw_o, b_o, k_o = outs[d]
        lw_o[0] = -jnp.exp(w)
        b_o[0] = kk * a
        k_o[0] = k * (1.0 + (a - 1.0) * ka_ref[...])
    qkv = qkv_ref[0]
    segm = lambda t: seg(t) * (1.0 / HEAD)
    qn = qkv[:, 0:NA_W]
    kn = qkv[:, NA_W:2 * NA_W]
    naq_o[0] = (qn * lax.rsqrt(segm(qn * qn) + RMS_EPS) * qg_ref[...] * (HEAD ** -0.5)).astype(BF16)
    nak_o[0] = (kn * lax.rsqrt(segm(kn * kn) + RMS_EPS) * kg_ref[...]).astype(BF16)
    nav_o[0] = qkv[:, 2 * NA_W:].astype(BF16)


def _prep(xr, qkv, p):
    bsz, seq, _ = xr.shape
    tm = TOKEN_TILE
    nb = seq // tm
    h8 = tm // 8
    blk = lambda w: pl.BlockSpec((1, tm, w), lambda b, i: (b, i, 0))
    full = lambda a: pl.BlockSpec(a.shape, lambda b, i, _n=a.ndim: (0,) * _n)
    params = [p["mu"], p["k_k"], p["k_a"], p["r_k"], p["w0"], p["a0"], p["w1"], p["w2"], p["a1"], p["a2"],
              p["g1"], p["g2"], p["q_g"], p["k_g"], p["ob"]]
    f32o = jax.ShapeDtypeStruct((bsz, seq, RW), F32)
    bfo = jax.ShapeDtypeStruct((bsz, seq, NA_W), BF16)
    return pl.pallas_call(
        _prep_kernel,
        grid=(bsz, nb),
        in_specs=[blk(4 * RW),
                  pl.BlockSpec((1, 8, 4 * RW), lambda b, i: (b, jnp.maximum(i * h8 - 1, 0), 0)),
                  pl.BlockSpec((1, 8, 4 * RW), lambda b, i: (b, jnp.minimum((i + 1) * h8, seq // 8 - 1), 0)),
                  blk(3 * NA_W)] + [full(a) for a in params],
        out_specs=[blk(RW)] * 14,
        out_shape=[f32o] * 11 + [bfo] * 3,
        compiler_params=_cparams(("arbitrary", "arbitrary")),
        name="mixer_prep",
    )(xr, xr, xr, qkv, *params)


HEAD_PAIR = LANES // HEAD
WKV_PAIRS = RW // LANES
WKV_DOUBLINGS = WKV_CHUNK.bit_length() - 2


def _nt_dot(a, b):
    return lax.dot_general(a, b, (((1,), (1,)), ((), ())), preferred_element_type=F32)


def _wkv_kernel(*refs, nrows):
    f_in = refs[0:6]
    b_in = refs[6:12]
    tri_ref, msk_ref, eye_ref = refs[12:15]
    yf_ref, yb_ref = refs[15:17]
    s_ref = refs[17]
    c = pl.program_id(0)
    bi = pl.program_id(1)
    tc = WKV_CHUNK

    @pl.when(c == 0)
    def _():
        for row in range(nrows):
            s_ref[bi * nrows + row] = jnp.zeros(s_ref.shape[1:], F32)

    first_head = lax.broadcasted_iota(jnp.int32, (tc, LANES), 1) < HEAD
    eye_bf = eye_ref[...]
    eye_f = eye_bf.astype(F32)

    def blk(z):
        return jnp.concatenate([jnp.where(first_head, z, 0.0), jnp.where(first_head, 0.0, z)], axis=0)

    bdot = lambda p, q: jnp.dot(p, q, preferred_element_type=F32)
    units = [(row, d, p) for row in range(nrows) for d in range(2) for p in range(WKV_PAIRS)]
    every = lambda fn, *cols: [fn(*args) for args in zip(*cols)]
    states = [s_ref[bi * nrows + row, d, p] for row, d, p in units]
    masks = [(msk_ref[d, 0] > 0.5, msk_ref[d, 1] > 0.5) for d in range(2)]

    def load(row, d, p):
        src = f_in if d == 0 else b_in
        return [s[row, :, p * LANES:(p + 1) * LANES] for s in src]

    def decays(unit, data):
        d = unit[1]
        lw = data[3]
        cum = jnp.zeros_like(lw)
        rest = lw
        for _ in range(3):
            term = rest.astype(BF16)
            cum = cum + bdot(tri_ref[d], term)
            rest = rest - term.astype(F32)
        cend = cum[tc - 1:tc] if d == 0 else cum[0:1]
        return cum, cend

    def operands(data, dec):
        a, r, v, lw, bb, kk = data
        cum, cend = dec
        e_neg = jnp.exp(-cum)
        e_end = jnp.exp(cend - cum)
        x = jnp.concatenate([blk(a * jnp.exp(cum - lw)), blk(r * jnp.exp(cum))], axis=0).astype(BF16)
        y = jnp.concatenate([blk(bb * e_neg), blk(kk * e_neg)], axis=0).astype(BF16)
        z = jnp.concatenate([blk(bb * e_end), blk(kk * e_end)], axis=0).astype(BF16)
        return x, y, z, blk(v)

    def causal(unit, g):
        strict, incl = masks[unit[1]]
        return (jnp.where(strict, g[:2 * tc, :2 * tc], 0.0), jnp.where(strict, g[:2 * tc, 2 * tc:], 0.0),
                jnp.concatenate([jnp.where(incl, g[2 * tc:, :2 * tc], 0.0),
                                 jnp.where(incl, g[2 * tc:, 2 * tc:], 0.0)], axis=1).astype(BF16))

    data = every(load, *zip(*units))
    dec = every(decays, units, data)
    ops = every(operands, data, dec)
    grams = every(lambda o: _nt_dot(o[0], o[1]), ops)
    nmat = every(causal, units, grams)
    ph = every(lambda o, st: _nt_dot(o[0], st.astype(BF16)), ops, states)
    vbf = every(lambda o: o[3].astype(BF16), ops)
    rhs = every(lambda q, n, vb: q[:2 * tc] + bdot(n[1].astype(BF16), vb), ph, nmat, vbf)
    inv = every(lambda n: eye_f + n[0], nmat)
    pw = every(lambda n: n[0].astype(BF16), nmat)
    for _ in range(WKV_DOUBLINGS):
        pw = every(lambda q: bdot(q, q).astype(BF16), pw)
        inv = every(lambda t, q: t + bdot(t.astype(BF16), q), inv, pw)
    u = every(lambda t, q: bdot(t.astype(BF16), q.astype(BF16)), inv, rhs)
    uv = every(lambda q, vb: jnp.concatenate([q.astype(BF16), vb], axis=0), u, vbf)
    yo = every(lambda q, n, w: q[2 * tc:] + bdot(n[2], w), ph, nmat, uv)
    uvt = every(lambda q, o: jnp.transpose(jnp.concatenate([q, o[3]], axis=0)).astype(BF16), u, ops)
    new = every(lambda st, dc, w, o: st * jnp.exp(dc[1]) + bdot(w, o[2]), states, dec, uvt, ops)
    for row in range(nrows):
        for d in range(2):
            out = yf_ref if d == 0 else yb_ref
            parts = [yo[units.index((row, d, p))] for p in range(WKV_PAIRS)]
            out[row] = jnp.concatenate([q[:tc] + q[tc:] for q in parts], axis=1)
    for (row, d, p), st in zip(units, new):
        s_ref[bi * nrows + row, d, p] = st


def _wkv_masks():
    tc = WKV_CHUNK
    t = np.arange(tc)
    tri = np.stack([t[None, :] <= t[:, None], t[None, :] >= t[:, None]]).astype(np.float32)
    head = np.arange(HEAD_PAIR * tc) // tc
    tt = np.arange(HEAD_PAIR * tc) % tc
    same = head[:, None] == head[None, :]
    m = np.stack([np.stack([same & (tt[None, :] < tt[:, None]), same & (tt[None, :] <= tt[:, None])]),
                  np.stack([same & (tt[None, :] > tt[:, None]), same & (tt[None, :] >= tt[:, None])])])
    return jnp.asarray(tri, BF16), jnp.asarray(m.astype(np.float32)), jnp.asarray(np.eye(LANES, dtype=np.float32), BF16)


def _wkv_scan(ins):
    bsz, seq, _ = ins["nkk"].shape
    tc = WKV_CHUNK
    nc = seq // tc
    tri, msk, eye = _wkv_masks()
    nrows = 2 if bsz % 2 == 0 else 1
    fwd = pl.BlockSpec((nrows, tc, RW), lambda c, b: (b, c, 0))
    bwd = pl.BlockSpec((nrows, tc, RW), lambda c, b: (b, nc - 1 - c, 0))
    full = lambda a: pl.BlockSpec(a.shape, lambda c, b, _n=a.ndim: (0,) * _n)
    f_args = [ins["nkk"], ins["r"], ins["v"], ins["lw0"], ins["b0"], ins["k0"]]
    b_args = [ins["nkk"], ins["r"], ins["v"], ins["lw1"], ins["b1"], ins["k1"]]
    o = jax.ShapeDtypeStruct((bsz, seq, RW), F32)
    return pl.pallas_call(
        functools.partial(_wkv_kernel, nrows=nrows),
        grid=(nc, bsz // nrows),
        in_specs=[fwd] * 6 + [bwd] * 6 + [full(tri), full(msk), full(eye)],
        out_specs=[fwd, bwd],
        out_shape=[o, o],
        scratch_shapes=[pltpu.VMEM((bsz, 2, WKV_PAIRS, LANES, LANES), F32)],
        compiler_params=_cparams(("arbitrary", "arbitrary")),
        name="wkv_scan",
    )(*f_args, *b_args, tri, msk, eye)


def _gelu_tanh(x):
    return 0.5 * x * (1.0 + jnp.tanh(math.sqrt(2.0 / math.pi) * (x + 0.044715 * (x * x * x))))


def _s5_kernel(ua_ref, ub_ref, bblk_ref, cblk_ref, lam_ref, lamc_ref, d_ref, glu_ref, o_ref,
               y_ref, st_ref, bu_ref, end_ref, carry_ref, *, seq, nb):
    ch = S5_CHUNK
    nc = seq // ch
    n = S5_FLAT
    u_halves = (ua_ref, ub_ref)
    for b in range(nb):
        for hf in range(2):
            y_ref[b, hf] = u_halves[hf][b] * d_ref[:, hf * LANES:(hf + 1) * LANES]

    def cmul_add(lre, lim, s, add):
        sre = s[:, :n]
        sim = s[:, n:]
        return jnp.concatenate([lre * sre - lim * sim + add[:, :n],
                                lre * sim + lim * sre + add[:, n:]], axis=1)

    for d in range(2):
        lre = lam_ref[d, 0:1, :]
        lim = lam_ref[d, 1:2, :]
        lcre = lamc_ref[d, 0:1, :]
        lcim = lamc_ref[d, 1:2, :]
        tloc = (lambda i: i) if d == 0 else (lambda i: ch - 1 - i)
        cloc = (lambda i: i) if d == 0 else (lambda i: nc - 1 - i)

        def drive(i):
            tl = tloc(jnp.minimum(i, ch - 1))
            rows = jnp.concatenate(
                [jnp.concatenate([r[b, pl.ds(tl, nc, stride=ch), :] for r in u_halves], axis=1)
                 for b in range(nb)], axis=0)
            return jnp.dot(rows.astype(BF16), bblk_ref[d], preferred_element_type=F32)

        def project(i, slot):
            bu_ref[slot] = drive(i)

        def advance(slot):
            st_ref[...] = cmul_add(lre, lim, st_ref[...], bu_ref[slot])

        def emit_from(st, i):
            yr = jnp.dot(st.astype(BF16), cblk_ref[d], preferred_element_type=F32)
            idx = pl.ds(tloc(i), nc, stride=ch)
            for b in range(nb):
                for hf in range(2):
                    y_ref[b, hf, idx, :] = (y_ref[b, hf, idx, :]
                                            + yr[b * nc:(b + 1) * nc, hf * LANES:(hf + 1) * LANES])

        def emit(i):
            emit_from(st_ref[...], i)

        st_ref[...] = jnp.zeros_like(st_ref)
        project(0, 0)

        def p1(j, c):
            nxt = drive(2 * j + 1)
            st = cmul_add(lre, lim, st_ref[...], bu_ref[0])
            st_ref[...] = cmul_add(lre, lim, st, nxt)
            project(2 * j + 2, 0)
            return c

        lax.fori_loop(0, ch // 2, p1, 0)
        end_ref[...] = st_ref[...]

        def cs(i, car):
            c = cloc(i)
            for b in range(nb):
                carry_ref[pl.ds(b * nc + c, 1), :] = car[b:b + 1]
            ends = jnp.concatenate([end_ref[pl.ds(b * nc + c, 1), :] for b in range(nb)], axis=0)
            return cmul_add(lcre, lcim, car, ends)

        lax.fori_loop(0, nc, cs, jnp.zeros((nb, 2 * n), F32))

        st_ref[...] = carry_ref[...]
        project(0, 0)
        project(1, 1)
        advance(0)

        def p2(j, c):
            i = 2 * j + 1
            nxt = drive(i + 1)
            st = st_ref[...]
            emit_from(st, i - 1)
            st = cmul_add(lre, lim, st, bu_ref[1])
            emit_from(st, i)
            st_ref[...] = cmul_add(lre, lim, st, nxt)
            project(i + 2, 1)
            return c

        lax.fori_loop(0, (ch - 2) // 2, p2, 0)
        emit(ch - 2)
        advance(1)
        emit(ch - 1)

    glu = glu_ref[...]
    for b in range(nb):
        g = _gelu_tanh(jnp.concatenate([y_ref[b, 0], y_ref[b, 1]], axis=1))
        o_ref[b] = (g * jax.nn.sigmoid(jnp.dot(g.astype(BF16), glu, preferred_element_type=F32))).astype(o_ref.dtype)


def _s5_params(lam_re, lam_im, log_dt, b_re, b_im, c_re, c_im):
    lre = lam_re.astype(F32)
    lim = lam_im.astype(F32)
    dt = jnp.exp(log_dt.astype(F32))[..., None]

    def cexp(scale):
        mag = jnp.exp(lre * dt * scale)
        return mag * jnp.cos(lim * dt * scale), mag * jnp.sin(lim * dt * scale)

    bar_re, bar_im = cexp(1.0)
    den = lre * lre + lim * lim
    f_re = ((bar_re - 1.0) * lre + bar_im * lim) / den
    f_im = (bar_im * lre - (bar_re - 1.0) * lim) / den
    bm_re = b_re.astype(F32)
    bm_im = b_im.astype(F32)
    bb_re = f_re[..., None] * bm_re - f_im[..., None] * bm_im
    bb_im = f_re[..., None] * bm_im + f_im[..., None] * bm_re
    eye_g = jnp.eye(S5_GROUPS, dtype=F32)

    def blockdiag_in(m):
        return jnp.einsum("dgph,gk->dghkp", m, eye_g).reshape(2, S5_WIDTH, S5_FLAT)

    def blockdiag_out(m):
        return jnp.einsum("dghp,gk->dgpkh", m, eye_g).reshape(2, S5_FLAT, S5_WIDTH)

    bblk = jnp.concatenate([blockdiag_in(bb_re), blockdiag_in(bb_im)], axis=2)
    cblk = jnp.concatenate([blockdiag_out(c_re.astype(F32)), -blockdiag_out(c_im.astype(F32))], axis=1)
    flat = lambda z: jnp.stack([z[0].reshape(2, S5_FLAT), z[1].reshape(2, S5_FLAT)], axis=1)
    return bblk.astype(BF16), cblk.astype(BF16), flat((bar_re, bar_im)), flat(cexp(float(S5_CHUNK)))


def _s5_mixer(u, bblk, cblk, lam, lamc, d_skip, glu_bf):
    bsz, seq, w = u.shape
    nc = seq // S5_CHUNK
    nb = 2 if bsz % 2 == 0 else 1
    full = lambda a: pl.BlockSpec(a.shape, lambda b, _n=a.ndim: (0,) * _n)
    args = [bblk, cblk, lam, lamc, d_skip, glu_bf]
    state = pltpu.VMEM((nb * nc, 2 * S5_FLAT), F32)
    return pl.pallas_call(
        functools.partial(_s5_kernel, seq=seq, nb=nb),
        grid=(bsz // nb,),
        in_specs=[pl.BlockSpec((nb, seq, LANES), lambda b: (b, 0, 0)),
                  pl.BlockSpec((nb, seq, LANES), lambda b: (b, 0, 1))] + [full(a) for a in args],
        out_specs=pl.BlockSpec((nb, seq, w), lambda b: (b, 0, 0)),
        out_shape=jax.ShapeDtypeStruct((bsz, seq, w), BF16),
        scratch_shapes=[pltpu.VMEM((nb, w // LANES, seq, LANES), F32),
                        state,
                        pltpu.VMEM((2, nb * nc, 2 * S5_FLAT), F32),
                        state,
                        state],
        compiler_params=_cparams(("arbitrary",)),
        name="s5_mixer",
    )(u, u, *args)


def _na_bias_table(rpb):
    q_col = np.arange(GRID_W)
    c_start = np.clip(q_col - NA_KW // 2, 0, GRID_W - NA_KW)
    k_col = np.arange(GRID_W)
    valid = (k_col[None, :] >= c_start[:, None]) & (k_col[None, :] < c_start[:, None] + NA_KW)
    dx = np.clip(k_col[None, :] - q_col[:, None] + NA_KW - 1, 0, 2 * NA_KW - 2)
    pick = (np.arange(2 * NA_KW - 1)[:, None, None] == dx[None]).astype(np.float32)
    base = jnp.einsum("hyd,dqk->hyqk", rpb.astype(F32), jnp.asarray(pick), precision=HIGHEST)
    base = jnp.where(jnp.asarray(valid)[None, None], base, -jnp.inf)
    tab = jnp.stack([base[:, NA_KH - 1 - o:2 * NA_KH - 1 - o] for o in range(NA_KH)], axis=1)
    tab = jnp.transpose(tab, (0, 1, 3, 2, 4))
    return tab.reshape(rpb.shape[0], NA_KH, GRID_W, NA_KH * GRID_W)


def _na_kernel(q_ref, k_ref, v_ref, bias_ref, o_ref, *, rows, rblk):
    rb = pl.program_id(1)
    lane = lax.broadcasted_iota(jnp.int32, (GRID_W, LANES), 1)
    low = lane < HEAD

    npair = NA_W // LANES
    every = lambda fn, *cols: [fn(*args) for args in zip(*cols)]

    def row_pair(jj, carry):
        js = [2 * jj, 2 * jj + 1]
        units = [(u, c) for u in range(2) for c in range(npair)]
        loaded = []
        for j in js:
            r = rb * rblk + j
            rs = jnp.clip(r - NA_KH // 2, 0, rows - NA_KH)
            loaded.append((q_ref[0, j],
                           k_ref[0, pl.ds(rs, NA_KH)].reshape(NA_KH * GRID_W, NA_W),
                           v_ref[0, pl.ds(rs, NA_KH)].reshape(NA_KH * GRID_W, NA_W),
                           r - rs))

        def scores(u, c):
            q, kmat, _, off = loaded[u]
            sl = slice(c * LANES, (c + 1) * LANES)
            q2 = q[:, sl].astype(F32)
            lhs = jnp.concatenate([jnp.where(low, q2, 0.0), jnp.where(low, 0.0, q2)], axis=0).astype(BF16)
            s = lax.dot_general(lhs, kmat[:, sl], (((1,), (1,)), ((), ())), preferred_element_type=F32)
            return s + jnp.concatenate([bias_ref[2 * c, off], bias_ref[2 * c + 1, off]], axis=0)

        s = every(scores, *zip(*units))
        m = every(lambda t: jnp.max(t, axis=-1, keepdims=True), s)
        p = every(lambda t, mx: jnp.exp(t - mx), s, m)
        l = every(lambda t: jnp.sum(t, axis=-1, keepdims=True), p)
        o = every(lambda t, den, uc: jnp.dot(t.astype(BF16), loaded[uc[0]][2][:, uc[1] * LANES:(uc[1] + 1) * LANES],
                                             preferred_element_type=F32) / den, p, l, units)
        o = every(lambda t: jnp.where(low, t[:GRID_W], t[GRID_W:]), o)
        for u, j in enumerate(js):
            o_ref[0, j] = jnp.concatenate(o[u * npair:(u + 1) * npair], axis=1).astype(o_ref.dtype)
        return carry

    lax.fori_loop(0, rblk // 2, row_pair, 0)


def _na_mixer(q, k, v, bias):
    bsz, seq, w = q.shape
    rows = seq // GRID_W
    rblk = 8
    g4 = lambda a: a.reshape(bsz, rows, GRID_W, w)
    img = pl.BlockSpec((1, rows, GRID_W, w), lambda b, i: (b, 0, 0, 0))
    blk = pl.BlockSpec((1, rblk, GRID_W, w), lambda b, i: (b, i, 0, 0))
    out = pl.pallas_call(
        functools.partial(_na_kernel, rows=rows, rblk=rblk),
        grid=(bsz, rows // rblk),
        in_specs=[blk, img, img, pl.BlockSpec(bias.shape, lambda b, i: (0, 0, 0, 0))],
        out_specs=blk,
        out_shape=jax.ShapeDtypeStruct((bsz, rows, GRID_W, w), BF16),
        compiler_params=_cparams(("arbitrary", "arbitrary")),
        name="na_mixer",
    )(g4(q), g4(k), g4(v), bias)
    return out.reshape(bsz, seq, w)


def _outproj_kernel(x_ref, s5_ref, yf_ref, yb_ref, gate_ref, bonus_ref, na_ref,
                    lnw_ref, lnb_ref, obm_ref, w_ref, gm_ref, g2_ref, sh_ref, sc_ref,
                    xo_ref, h_ref):
    segm = lambda t: _seg_sum(t, obm_ref[...])
    y = yf_ref[...] + yb_ref[...]
    yc = y - segm(y)
    yn = yc * lax.rsqrt(segm(yc * yc) + RWKV_GN_EPS) * lnw_ref[...] + lnb_ref[...]
    rw = (yn + bonus_ref[...]) * gate_ref[...]
    mixed = jnp.concatenate([s5_ref[...].astype(BF16), rw.astype(BF16), na_ref[...].astype(BF16)], axis=1)
    xo = x_ref[...] + gm_ref[0] * jnp.dot(mixed, w_ref[...], preferred_element_type=F32)
    xo_ref[...] = xo
    h_ref[...] = _rms_mod(xo, g2_ref[...], sh_ref[0], sc_ref[0])


def _out_proj(x2, s5o, yf, yb, gate, bonus, nao, lnw, lnb, obm, w_bf, gate_mix, g2, shift, scale, seq):
    t, d = x2.shape
    tm = TOKEN_TILE
    per_b = seq // tm
    row = lambda w: pl.BlockSpec((tm, w), lambda i: (i, 0))
    full = lambda a: pl.BlockSpec(a.shape, lambda i, _n=a.ndim: (0,) * _n)
    bvec = pl.BlockSpec((1, 1, d), lambda i: (i // per_b, 0, 0))
    o = jax.ShapeDtypeStruct((t, d), F32)
    return pl.pallas_call(
        _outproj_kernel,
        grid=(t // tm,),
        in_specs=[row(d), row(S5_WIDTH)] + [row(RW)] * 5 +
                 [full(lnw), full(lnb), full(obm), full(w_bf), bvec, full(g2), bvec, bvec],
        out_specs=[row(d), row(d)],
        out_shape=[o, o],
        compiler_params=_cparams(("arbitrary",)),
        name="out_proj",
    )(x2, s5o, yf, yb, gate, bonus, nao, lnw, lnb, obm, w_bf, gate_mix, g2, shift, scale)


MOE_TILE = 256
SEG_ALIGN = 8
MOE_SLOTS = -(-(MOE_TILE * TOP_K + N_EXPERTS * (SEG_ALIGN - 1)) // LANES) * LANES


def _router_kernel(h_ref, w_ref, b_ref, tri_ref, upper_ref, slot_ref, gate_ref, cnt_ref, base_ref, loc_ref,
                   carry_ref):
    @pl.when(pl.program_id(0) == 0)
    def _():
        carry_ref[...] = jnp.zeros_like(carry_ref)

    logits = jnp.dot(h_ref[...], w_ref[...], precision=HIGHEST, preferred_element_type=F32) + b_ref[...]
    tm = logits.shape[0]
    lane = lax.broadcasted_iota(jnp.int32, (tm, LANES), 1)
    lane_f = lane.astype(F32)
    vals, hots = [], []
    cur = logits
    for _ in range(TOP_K):
        m = jnp.max(cur, axis=-1, keepdims=True)
        idx = jnp.min(jnp.where(cur == m, lane_f, float(LANES)), axis=-1, keepdims=True)
        hot = lane_f == idx
        vals.append(m)
        hots.append(hot)
        cur = jnp.where(hot, -jnp.inf, cur)
    exps = [jnp.exp(v - vals[0]) for v in vals]
    den = exps[0] + exps[1] + exps[2] + exps[3]
    assign = sum(h.astype(F32) for h in hots)
    before = jnp.dot(tri_ref[...], assign.astype(BF16), preferred_element_type=F32)
    cnt = jnp.sum(assign, axis=0, keepdims=True)
    cnt = jnp.floor((cnt + (SEG_ALIGN - 1)) * (1.0 / SEG_ALIGN)) * SEG_ALIGN
    cnt8 = jnp.broadcast_to(cnt, (8, LANES)).astype(BF16)
    loc = jnp.dot(cnt8, upper_ref[...], preferred_element_type=F32)[0:1, :]
    place = before + loc
    s_out = jnp.zeros((tm, LANES), jnp.int32)
    g_out = jnp.zeros((tm, LANES), F32)
    for kk in range(TOP_K):
        slot = jnp.sum(jnp.where(hots[kk], place, 0.0), axis=-1, keepdims=True)
        sel = lane == kk
        s_out = jnp.where(sel, slot.astype(jnp.int32), s_out)
        g_out = jnp.where(sel, exps[kk] / den, g_out)
    slot_ref[...] = s_out
    gate_ref[...] = g_out
    cnt_ref[0] = cnt.astype(jnp.int32)
    base_ref[0] = carry_ref[...].astype(jnp.int32)
    loc_ref[0] = loc.astype(jnp.int32)
    carry_ref[...] = carry_ref[...] + cnt


def _router(h2, rw_pad, rb_pad):
    t, d = h2.shape
    tm = MOE_TILE
    nt = t // tm
    tri = jnp.asarray(np.tril(np.ones((tm, tm), np.float32), -1), BF16)
    upper = jnp.asarray(np.triu(np.ones((LANES, LANES), np.float32), 1), BF16)
    row = pl.BlockSpec((tm, LANES), lambda i: (i, 0))
    per_tile = pl.BlockSpec((1, 1, LANES), lambda i: (i, 0, 0))
    full = lambda a: pl.BlockSpec(a.shape, lambda i, _n=a.ndim: (0,) * _n)
    tile_i32 = jax.ShapeDtypeStruct((nt, 1, LANES), jnp.int32)
    return pl.pallas_call(
        _router_kernel,
        grid=(nt,),
        in_specs=[pl.BlockSpec((tm, d), lambda i: (i, 0)), full(rw_pad), full(rb_pad), full(tri), full(upper)],
        out_specs=[row, row, per_tile, per_tile, per_tile],
        out_shape=[jax.ShapeDtypeStruct((t, LANES), jnp.int32),
                   jax.ShapeDtypeStruct((t, LANES), F32),
                   tile_i32, tile_i32, tile_i32],
        scratch_shapes=[pltpu.VMEM((1, LANES), F32)],
        compiler_params=_cparams(("arbitrary",)),
        name="moe_router",
    )(h2, rw_pad, rb_pad, tri, upper)


SEG_PIECES = tuple(SEG_ALIGN << s for s in range((MOE_TILE // SEG_ALIGN).bit_length()))


def _segment_dmas(cnt_ref, loc_ref, row_ref, tile, ne, make_copy, wait):
    def per_expert(e, carry):
        i = tile * ne + e
        n = cnt_ref[i]
        off = loc_ref[i]
        row = row_ref[i]
        for p in SEG_PIECES:
            has = (n & p) != 0

            @pl.when(has)
            def _(off=off, row=row, p=p):
                cp = make_copy(pl.multiple_of(off, SEG_ALIGN), pl.multiple_of(row, SEG_ALIGN), p)
                if wait:
                    cp.wait()
                else:
                    cp.start()

            step = jnp.where(has, p, 0)
            off = off + step
            row = row + step
        return carry

    lax.fori_loop(0, ne, per_expert, 0)


def _dispatch_kernel(cnt_ref, loc_ref, row_ref, zcnt_ref, zoff_ref, zrow_ref, nu_ref,
                     slot_ref, h_ref, xb_ref, sorted_ref, zero_ref, sem, zsem, *, ne, nblk):
    i = pl.program_id(0)
    last = pl.num_programs(0) - 1
    cur = i % 2
    tm = h_ref.shape[0]
    ns = MOE_SLOTS

    def zero_copy(off, row, p):
        return pltpu.make_async_copy(zero_ref.at[pl.ds(off, p), :], xb_ref.at[pl.ds(row, p), :], zsem)

    def tail_copy(j):
        row = pl.multiple_of((nu_ref[0] + j) * MOE_BLOCK, MOE_BLOCK)
        return pltpu.make_async_copy(zero_ref, xb_ref.at[pl.ds(row, MOE_BLOCK), :], zsem)

    def zero_fill(wait):
        _segment_dmas(zcnt_ref, zoff_ref, zrow_ref, 0, ne, zero_copy, wait)

        def tail(j, c):
            if wait:
                tail_copy(j).wait()
            else:
                tail_copy(j).start()
            return c

        lax.fori_loop(0, nblk - nu_ref[0], tail, 0)

    @pl.when(i == 0)
    def _():
        zero_ref[...] = jnp.zeros_like(zero_ref)
        zero_fill(wait=False)

    slot_t = jnp.transpose(slot_ref[...].astype(F32))
    srow = lax.broadcasted_iota(jnp.int32, (ns, tm), 0).astype(F32)
    pick = jnp.zeros((ns, tm), F32)
    for kk in range(TOP_K):
        pick = jnp.where(srow == slot_t[kk:kk + 1, :], 1.0, pick)
    sorted_ref[cur] = jnp.dot(pick.astype(BF16), h_ref[...].astype(BF16), preferred_element_type=F32)

    def copy_from(buf):
        def copy(off, row, p):
            return pltpu.make_async_copy(sorted_ref.at[buf, pl.ds(off, p), :], xb_ref.at[pl.ds(row, p), :],
                                         sem.at[buf])
        return copy

    _segment_dmas(cnt_ref, loc_ref, row_ref, i, ne, copy_from(cur), wait=False)

    @pl.when(i > 0)
    def _():
        _segment_dmas(cnt_ref, loc_ref, row_ref, i - 1, ne, copy_from(1 - cur), wait=True)

    @pl.when(i == last)
    def _():
        _segment_dmas(cnt_ref, loc_ref, row_ref, i, ne, copy_from(cur), wait=True)
        zero_fill(wait=True)


def _dispatch(cnt, loc, rowstart, zcnt, zrow, n_used, slot, h2, n_rows, ne):
    t, d = h2.shape
    tm = MOE_TILE
    zoff = jnp.zeros_like(zcnt)
    grid_spec = pltpu.PrefetchScalarGridSpec(
        num_scalar_prefetch=7,
        grid=(t // tm,),
        in_specs=[pl.BlockSpec((tm, LANES), lambda i, *_: (i, 0)),
                  pl.BlockSpec((tm, d), lambda i, *_: (i, 0))],
        out_specs=pl.BlockSpec(memory_space=pl.ANY),
        scratch_shapes=[pltpu.VMEM((2, MOE_SLOTS, d), F32), pltpu.VMEM((MOE_BLOCK, d), F32),
                        pltpu.SemaphoreType.DMA((2,)), pltpu.SemaphoreType.DMA(())],
    )
    return pl.pallas_call(
        functools.partial(_dispatch_kernel, ne=ne, nblk=n_rows // MOE_BLOCK),
        grid_spec=grid_spec,
        out_shape=jax.ShapeDtypeStruct((n_rows, d), F32),
        compiler_params=_cparams(("arbitrary",)),
        name="moe_dispatch",
    )(cnt, loc, rowstart, zcnt, zoff, zrow, n_used, slot, h2)


PAIR_GROUP = 2 * LANES


def _pair_perm():
    p = np.zeros((PAIR_GROUP, PAIR_GROUP), np.float32)
    j = np.arange(LANES)
    p[2 * j, j] = 1.0
    p[2 * j + 1, LANES + j] = 1.0
    return jnp.asarray(p, BF16)


def _expert_kernel(be_ref, nu_ref, x_ref, w1_ref, b1_ref, w2_ref, b2_ref, perm_ref, y_ref, w1s_ref, w2s_ref):
    i = pl.program_id(0)
    f2 = w1_ref.shape[2]
    ngrp = f2 // PAIR_GROUP

    @pl.when(i >= nu_ref[0])
    def _():
        y_ref[...] = jnp.zeros_like(y_ref)

    @pl.when(i < nu_ref[0])
    def _():
        @pl.when((i == 0) | (be_ref[i] != be_ref[jnp.maximum(i - 1, 0)]))
        def _():
            for g in range(ngrp):
                sl = slice(g * PAIR_GROUP, (g + 1) * PAIR_GROUP)
                w1s_ref[:, sl] = jnp.dot(w1_ref[0, :, sl].astype(BF16), perm_ref[...],
                                         preferred_element_type=F32).astype(BF16)
            w2s_ref[...] = w2_ref[0].astype(BF16)

        hdn = jnp.dot(x_ref[...].astype(BF16), w1s_ref[...], preferred_element_type=F32) + b1_ref[0]
        glu = jnp.concatenate([hdn[:, g * PAIR_GROUP:g * PAIR_GROUP + LANES] for g in range(ngrp)], axis=1)
        lin = jnp.concatenate([hdn[:, g * PAIR_GROUP + LANES:(g + 1) * PAIR_GROUP] for g in range(ngrp)], axis=1)
        glu = jnp.minimum(glu, SWIGLU_LIMIT)
        lin = jnp.clip(lin, -SWIGLU_LIMIT, SWIGLU_LIMIT)
        act = glu * jax.nn.sigmoid(SWIGLU_ALPHA * glu) * (lin + 1.0)
        y_ref[...] = jnp.dot(act.astype(BF16), w2s_ref[...], preferred_element_type=F32) + b2_ref[0]


def _experts(block_e, n_used, xb, w1, b1_grp, w2, b2, layer):
    n_rows, d = xb.shape
    _, ne, _, f2 = w1.shape
    dff = w2.shape[2]
    nblk = n_rows // MOE_BLOCK
    perm = _pair_perm()
    blk = lambda i, be, nu: (jnp.minimum(i, nu[0] - 1), 0)
    wsel = lambda i, be, nu: (layer, be[i], 0, 0)
    grid_spec = pltpu.PrefetchScalarGridSpec(
        num_scalar_prefetch=2,
        grid=(nblk,),
        in_specs=[pl.BlockSpec((MOE_BLOCK, d), blk),
                  pl.BlockSpec((None, 1, d, f2), wsel),
                  pl.BlockSpec((None, 1, 1, f2), wsel),
                  pl.BlockSpec((None, 1, dff, d), wsel),
                  pl.BlockSpec((None, 1, 1, d), wsel),
                  pl.BlockSpec(perm.shape, lambda i, be, nu: (0, 0))],
        out_specs=pl.BlockSpec((MOE_BLOCK, d), lambda i, be, nu: (i, 0)),
        scratch_shapes=[pltpu.VMEM((d, f2), BF16), pltpu.VMEM((dff, d), BF16)],
    )
    nl = w1.shape[0]
    return pl.pallas_call(
        _expert_kernel,
        grid_spec=grid_spec,
        out_shape=jax.ShapeDtypeStruct((n_rows, d), F32),
        compiler_params=_cparams(("arbitrary",)),
        name="moe_experts",
    )(block_e, n_used, xb, w1, b1_grp.reshape(nl, ne, 1, f2), w2, b2.reshape(nl, ne, 1, d), perm)


def _combine_kernel(cnt_ref, loc_ref, row_ref, slot_ref, gates_ref, x_ref, gf_ref, yb_ref, o_ref, sorted_ref, sem,
                    *, ne):
    i = pl.program_id(0)
    last = pl.num_programs(0) - 1
    cur = i % 2
    tm = x_ref.shape[0]
    ns = MOE_SLOTS

    def copy_into(buf):
        def copy(off, row, p):
            return pltpu.make_async_copy(yb_ref.at[pl.ds(row, p), :], sorted_ref.at[buf, pl.ds(off, p), :],
                                         sem.at[buf])
        return copy

    @pl.when(i == 0)
    def _():
        sorted_ref[...] = jnp.zeros_like(sorted_ref)
        _segment_dmas(cnt_ref, loc_ref, row_ref, i, ne, copy_into(cur), wait=False)

    @pl.when(i < last)
    def _():
        _segment_dmas(cnt_ref, loc_ref, row_ref, i + 1, ne, copy_into(1 - cur), wait=False)

    slot = slot_ref[...]
    gates = gates_ref[...]
    scol = lax.broadcasted_iota(jnp.int32, (tm, ns), 1)
    gmat = jnp.zeros((tm, ns), F32)
    for kk in range(TOP_K):
        gmat = jnp.where(scol == slot[:, kk:kk + 1], gates[:, kk:kk + 1], gmat)
    _segment_dmas(cnt_ref, loc_ref, row_ref, i, ne, copy_into(cur), wait=True)
    acc = jnp.dot(gmat.astype(BF16), sorted_ref[cur].astype(BF16), preferred_element_type=F32)
    o_ref[...] = x_ref[...] + gf_ref[0] * acc


def _combine(cnt, loc, rowstart, slot, gates, x2, gate_ffn, yb, seq, ne):
    t, d = x2.shape
    tm = MOE_TILE
    per_b = seq // tm
    grid_spec = pltpu.PrefetchScalarGridSpec(
        num_scalar_prefetch=3,
        grid=(t // tm,),
        in_specs=[pl.BlockSpec((tm, LANES), lambda i, *_: (i, 0)),
                  pl.BlockSpec((tm, LANES), lambda i, *_: (i, 0)),
                  pl.BlockSpec((tm, d), lambda i, *_: (i, 0)),
                  pl.BlockSpec((1, 1, d), lambda i, *_: (i // per_b, 0, 0)),
                  pl.BlockSpec(memory_space=pl.ANY)],
        out_specs=pl.BlockSpec((tm, d), lambda i, *_: (i, 0)),
        scratch_shapes=[pltpu.VMEM((2, MOE_SLOTS, d), F32), pltpu.SemaphoreType.DMA((2,))],
    )
    return pl.pallas_call(
        functools.partial(_combine_kernel, ne=ne),
        grid_spec=grid_spec,
        out_shape=jax.ShapeDtypeStruct((t, d), F32),
        compiler_params=_cparams(("arbitrary",)),
        name="moe_combine",
    )(cnt, loc, rowstart, slot, gates, x2, gate_ffn, yb)


def _group_pairs(b1):
    lead = b1.shape[:-1]
    g = b1.reshape(lead + (-1, LANES, 2))
    return jnp.swapaxes(g, -1, -2).reshape(b1.shape)


def _moe_layer(x2, h2, gate_ffn, router_w, router_b, w1, b1_grp, w2, b2, seq, layer):
    t, d = x2.shape
    ne = router_w.shape[1]
    rw_pad = jnp.zeros((d, LANES), F32).at[:, :ne].set(router_w.astype(F32))
    rb_pad = jnp.full((1, LANES), -jnp.inf, F32).at[0, :ne].set(router_b.astype(F32))
    slot, gates, cnt3, base3, loc3 = _router(h2, rw_pad, rb_pad)
    n_assign = t * TOP_K
    n_tiles = t // MOE_TILE
    n_blocks = -(-(n_assign + n_tiles * ne * (SEG_ALIGN - 1)) // MOE_BLOCK) + ne
    cnt = cnt3[:, 0, :ne]
    base = base3[:, 0, :ne]
    loc = loc3[:, 0, :ne]
    total = base[-1] + cnt[-1]
    padded = ((total + MOE_BLOCK - 1) // MOE_BLOCK) * MOE_BLOCK
    pad_end = jnp.cumsum(padded)
    pad_start = pad_end - padded
    rowstart = (pad_start[None, :] + base).reshape(-1).astype(jnp.int32)
    n_used = (pad_end[-1] // MOE_BLOCK).astype(jnp.int32)
    block_start = jnp.minimum(jnp.arange(n_blocks, dtype=jnp.int32), n_used - 1) * MOE_BLOCK
    block_e = jnp.minimum(jnp.sum(block_start[:, None] >= pad_end[None, :], axis=-1), ne - 1).astype(jnp.int32)
    cnt_f = cnt.reshape(-1)
    loc_f = loc.reshape(-1)
    n_used = n_used.reshape(1)
    xb = _dispatch(cnt_f, loc_f, rowstart, (padded - total).astype(jnp.int32), (pad_start + total).astype(jnp.int32),
                   n_used, slot, h2, n_blocks * MOE_BLOCK, ne)
    yb = _experts(block_e, n_used, xb, w1, b1_grp, w2, b2, layer)
    return _combine(cnt_f, loc_f, rowstart, slot, gates, x2, gate_ffn, yb, seq, ne)


def kernel(x, c, ada_w, ada_b, norm1_g, norm2_g, w_in, w_out, s5_lam_re, s5_lam_im, s5_log_dt, s5_b_re, s5_b_im, s5_c_re, s5_c_im, s5_d, s5_glu_w, rwkv_mu, rwkv_w0, rwkv_w1, rwkv_w2, rwkv_a0, rwkv_a1, rwkv_a2, rwkv_g1, rwkv_g2, rwkv_k_k, rwkv_k_a, rwkv_r_k, rwkv_ln_w, rwkv_ln_b, na_q_g, na_k_g, na_rpb, router_w, router_b, exp_w1, exp_b1, exp_w2, exp_b2):
    bsz, seq, d = x.shape
    depth = ada_w.shape[0]
    t = bsz * seq
    mod = _ada_mod(c, ada_w, ada_b).reshape(depth, bsz, 6, 1, d)
    x2 = x.reshape(t, d)
    seg_ones = _seg_ones(RW, dtype=BF16)
    seg_mean = _seg_ones(RW, dtype=BF16, scale=1.0 / HEAD)
    row = lambda a: a.reshape(1, -1).astype(F32)
    b1_grp = _group_pairs(exp_b1.astype(F32))
    exp_b2f = exp_b2.astype(F32)
    for l in range(depth):
        m = lambda j: mod[l, :, j]
        s5u, xr, qkv = _in_proj(x2, row(norm1_g[l]), m(0), m(1), w_in[l].astype(BF16), seq)
        prep_params = dict(
            mu=row(rwkv_mu[l]), k_k=row(rwkv_k_k[l]), k_a=row(rwkv_k_a[l]), r_k=row(rwkv_r_k[l]),
            w0=rwkv_w0[l].astype(F32), a0=rwkv_a0[l].astype(F32),
            w1=rwkv_w1[l].astype(BF16), w2=rwkv_w2[l].astype(BF16),
            a1=rwkv_a1[l].astype(BF16), a2=rwkv_a2[l].astype(BF16),
            g1=rwkv_g1[l].astype(BF16), g2=rwkv_g2[l].astype(BF16),
            q_g=row(jnp.tile(na_q_g[l], NA_W // HEAD)), k_g=row(jnp.tile(na_k_g[l], NA_W // HEAD)), ob=seg_ones)
        (nkk, r, v, lw0, b0, k0, lw1, b1, k1, gate, bonus, naq, nak, nav) = _prep(
            xr.reshape(bsz, seq, 4 * RW), qkv.reshape(bsz, seq, 3 * NA_W), prep_params)
        yf, yb = _wkv_scan(dict(nkk=nkk, r=r, v=v, lw0=lw0, b0=b0, k0=k0, lw1=lw1, b1=b1, k1=k1))
        bblk, cblk, lam, lamc = _s5_params(s5_lam_re[l], s5_lam_im[l], s5_log_dt[l], s5_b_re[l], s5_b_im[l],
                                           s5_c_re[l], s5_c_im[l])
        s5o = _s5_mixer(s5u.reshape(bsz, seq, S5_WIDTH), bblk, cblk, lam, lamc, row(s5_d[l]),
                        s5_glu_w[l].astype(BF16))
        nao = _na_mixer(naq, nak, nav, _na_bias_table(na_rpb[l]))
        flat = lambda a: a.reshape(t, -1)
        x2, h2 = _out_proj(x2, flat(s5o), flat(yf), flat(yb), flat(gate), flat(bonus), flat(nao),
                           row(rwkv_ln_w[l]), row(rwkv_ln_b[l]), seg_mean, w_out[l].astype(BF16),
                           m(2), row(norm2_g[l]), m(3), m(4), seq)
        x2 = _moe_layer(x2, h2, m(5), router_w[l], router_b[l], exp_w1, b1_grp, exp_w2, exp_b2f, seq, l)
    return x2.reshape(bsz, seq, d)
```

```python
import functools
import math

import numpy as np
import jax
import jax.numpy as jnp
from jax import lax
from jax.experimental import pallas as pl
from jax.experimental.pallas import tpu as pltpu

F32 = jnp.float32
BF16 = jnp.bfloat16
HIGHEST = lax.Precision.HIGHEST

D_MODEL = 1024
S5_WIDTH = 256
S5_GROUP = 16
S5_GROUPS = 16
S5_STATE = 64
S5_CHUNK = 64
S5_FLAT = S5_GROUPS * S5_STATE
RW = 384
HEAD = 64
RWKV_GN_EPS = 64e-5
NA_W = 384
GRID_W = 64
NA_KH = 8
NA_KW = 16
N_EXPERTS = 32
TOP_K = 4
MOE_BLOCK = 256
SWIGLU_ALPHA = 1.702
SWIGLU_LIMIT = 7.0
RMS_EPS = 1e-6
LANES = 128
WKV_CHUNK = 64
TOKEN_TILE = 512
VMEM_LIMIT = 56 * 1024 * 1024


def _cparams(sem):
    return pltpu.CompilerParams(dimension_semantics=sem, vmem_limit_bytes=VMEM_LIMIT)


def _seg_ones(n, seg=HEAD, dtype=F32, scale=1.0):
    idx = np.arange(n) // seg
    return jnp.asarray((idx[:, None] == idx[None, :]).astype(np.float32) * scale, dtype)


def _seg_sum(t, ones_bf):
    hi = t.astype(BF16)
    lo = (t - hi.astype(F32)).astype(BF16)
    return (jnp.dot(hi, ones_bf, preferred_element_type=F32) + jnp.dot(lo, ones_bf, preferred_element_type=F32))


def _ada_kernel(c_ref, w_ref, b_ref, o_ref):
    c = c_ref[...]
    cond = c * jax.nn.sigmoid(c)
    o_ref[0] = jnp.dot(cond, w_ref[0], preferred_element_type=F32) + b_ref[0]


def _ada_mod(c, ada_w, ada_b):
    nl, d, n6 = ada_w.shape
    bsz = c.shape[0]
    tn = 1536
    return pl.pallas_call(
        _ada_kernel,
        grid=(nl, n6 // tn),
        in_specs=[pl.BlockSpec((bsz, d), lambda l, j: (0, 0)),
                  pl.BlockSpec((1, d, tn), lambda l, j: (l, 0, j)),
                  pl.BlockSpec((1, 1, tn), lambda l, j: (l, 0, j))],
        out_specs=pl.BlockSpec((1, bsz, tn), lambda l, j: (l, 0, j)),
        out_shape=jax.ShapeDtypeStruct((nl, bsz, n6), F32),
        compiler_params=_cparams(("arbitrary", "arbitrary")),
        name="ada_mod",
    )(c, ada_w, ada_b.reshape(nl, 1, n6))


def _rms_mod(x, g, shift, scale):
    ms = jnp.mean(x * x, axis=-1, keepdims=True)
    h = x * lax.rsqrt(ms + RMS_EPS) * g
    return h * (1.0 + scale) + shift


def _proj_kernel(x_ref, g_ref, sh_ref, sc_ref, w_ref, o_s5, o_rw, o_na):
    h = _rms_mod(x_ref[...], g_ref[...], sh_ref[0], sc_ref[0])
    p = jnp.dot(h.astype(BF16), w_ref[...], preferred_element_type=F32)
    o_s5[...] = p[:, :S5_WIDTH]
    o_rw[...] = p[:, S5_WIDTH:S5_WIDTH + 4 * RW]
    o_na[...] = p[:, S5_WIDTH + 4 * RW:]


def _in_proj(x2, g, shift, scale, w_bf, seq):
    t, d = x2.shape
    n = w_bf.shape[1]
    tm = TOKEN_TILE
    per_b = seq // tm
    row = lambda i: (i, 0)
    bvec = lambda i: (i // per_b, 0, 0)
    return pl.pallas_call(
        _proj_kernel,
        grid=(t // tm,),
        in_specs=[pl.BlockSpec((tm, d), row),
                  pl.BlockSpec((1, d), lambda i: (0, 0)),
                  pl.BlockSpec((1, 1, d), bvec),
                  pl.BlockSpec((1, 1, d), bvec),
                  pl.BlockSpec((d, n), lambda i: (0, 0))],
        out_specs=[pl.BlockSpec((tm, S5_WIDTH), row),
                   pl.BlockSpec((tm, 4 * RW), row),
                   pl.BlockSpec((tm, 3 * NA_W), row)],
        out_shape=[jax.ShapeDtypeStruct((t, S5_WIDTH), F32),
                   jax.ShapeDtypeStruct((t, 4 * RW), F32),
                   jax.ShapeDtypeStruct((t, 3 * NA_W), F32)],
        compiler_params=_cparams(("arbitrary",)),
        name="in_proj",
    )(x2, g, shift, scale, w_bf)


def _softplus(x):
    return jnp.maximum(x, 0.0) + jnp.log(1.0 + jnp.exp(-jnp.abs(x)))


def _prep_kernel(xr_ref, prev_ref, next_ref, qkv_ref,
                 mu_ref, kk_ref, ka_ref, rk_ref, w0_ref, a0_ref,
                 w1_ref, w2_ref, a1_ref, a2_ref, g1_ref, g2_ref, qg_ref, kg_ref, ob_ref,
                 nkk_o, r_o, v_o, lw0_o, b0_o, k0_o, lw1_o, b1_o, k1_o,
                 gate_o, bonus_o, naq_o, nak_o, nav_o):
    i = pl.program_id(1)
    nblk = pl.num_programs(1)
    x = xr_ref[0]
    tm = x.shape[0]
    prow = jnp.where(i == 0, 0.0, prev_ref[0][7:8, :])
    nrow = jnp.where(i == nblk - 1, 0.0, next_ref[0][0:1, :])
    rid = lax.broadcasted_iota(jnp.int32, x.shape, 0)
    prev = jnp.where(rid == 0, prow, pltpu.roll(x, 1, axis=0))
    nxt = jnp.where(rid == tm - 1, nrow, pltpu.roll(x, tm - 1, axis=0))
    xs = x + (0.5 * (prev + nxt) - x) * mu_ref[...]
    r = xs[:, 0:RW]
    k = xs[:, RW:2 * RW]
    v = xs[:, 2 * RW:3 * RW]
    z = xs[:, 3 * RW:4 * RW]
    seg = lambda t: _seg_sum(t, ob_ref[...])
    zb = z.astype(BF16)
    bdot = lambda a, w: jnp.dot(a.astype(BF16), w, preferred_element_type=F32)
    gate_o[0] = bdot(jax.nn.sigmoid(bdot(zb, g1_ref[...])), g2_ref[...])
    kk = k * kk_ref[...]
    kk = kk / jnp.maximum(jnp.sqrt(seg(kk * kk)), 1e-12)
    nkk_o[0] = -kk
    r_o[0] = r
    v_o[0] = v
    bonus_o[0] = seg(r * k * rk_ref[...]) * v
    outs = ((lw0_o, b0_o, k0_o), (lw1_o, b1_o, k1_o))
    for d in range(2):
        wl = w0_ref[d:d + 1, :] + bdot(jnp.tanh(bdot(zb, w1_ref[d])), w2_ref[d])
        w = -_softplus(-wl) - 0.5
        a = jax.nn.sigmoid(a0_ref[d:d + 1, :] + bdot(bdot(zb, a1_ref[d]), a2_ref[d]))
        lw_o, b_o, k_o = outs[d]
        lw_o[0] = -jnp.exp(w)
        b_o[0] = kk * a
        k_o[0] = k * (1.0 + (a - 1.0) * ka_ref[...])
    qkv = qkv_ref[0]
    segm = lambda t: seg(t) * (1.0 / HEAD)
    qn = qkv[:, 0:NA_W]
    kn = qkv[:, NA_W:2 * NA_W]
    naq_o[0] = (qn * lax.rsqrt(segm(qn * qn) + RMS_EPS) * qg_ref[...] * (HEAD ** -0.5)).astype(BF16)
    nak_o[0] = (kn * lax.rsqrt(segm(kn * kn) + RMS_EPS) * kg_ref[...]).astype(BF16)
    nav_o[0] = qkv[:, 2 * NA_W:].astype(BF16)


def _prep(xr, qkv, p):
    bsz, seq, _ = xr.shape
    tm = TOKEN_TILE
    nb = seq // tm
    h8 = tm // 8
    blk = lambda w: pl.BlockSpec((1, tm, w), lambda b, i: (b, i, 0))
    full = lambda a: pl.BlockSpec(a.shape, lambda b, i, _n=a.ndim: (0,) * _n)
    params = [p["mu"], p["k_k"], p["k_a"], p["r_k"], p["w0"], p["a0"], p["w1"], p["w2"], p["a1"], p["a2"],
              p["g1"], p["g2"], p["q_g"], p["k_g"], p["ob"]]
    f32o = jax.ShapeDtypeStruct((bsz, seq, RW), F32)
    bfo = jax.ShapeDtypeStruct((bsz, seq, NA_W), BF16)
    return pl.pallas_call(
        _prep_kernel,
        grid=(bsz, nb),
        in_specs=[blk(4 * RW),
                  pl.BlockSpec((1, 8, 4 * RW), lambda b, i: (b, jnp.maximum(i * h8 - 1, 0), 0)),
                  pl.BlockSpec((1, 8, 4 * RW), lambda b, i: (b, jnp.minimum((i + 1) * h8, seq // 8 - 1), 0)),
                  blk(3 * NA_W)] + [full(a) for a in params],
        out_specs=[blk(RW)] * 14,
        out_shape=[f32o] * 11 + [bfo] * 3,
        compiler_params=_cparams(("arbitrary", "arbitrary")),
        name="mixer_prep",
    )(xr, xr, xr, qkv, *params)


HEAD_PAIR = LANES // HEAD
WKV_PAIRS = RW // LANES
WKV_DOUBLINGS = WKV_CHUNK.bit_length() - 2


def _nt_dot(a, b):
    return lax.dot_general(a, b, (((1,), (1,)), ((), ())), preferred_element_type=F32)


def _wkv_kernel(*refs, nrows):
    f_in = refs[0:6]
    b_in = refs[6:12]
    tri_ref, msk_ref, eye_ref = refs[12:15]
    yf_ref, yb_ref = refs[15:17]
    s_ref = refs[17]
    c = pl.program_id(0)
    bi = pl.program_id(1)
    tc = WKV_CHUNK

    @pl.when(c == 0)
    def _():
        for row in range(nrows):
            s_ref[bi * nrows + row] = jnp.zeros(s_ref.shape[1:], F32)

    first_head = lax.broadcasted_iota(jnp.int32, (tc, LANES), 1) < HEAD
    eye_bf = eye_ref[...]
    eye_f = eye_bf.astype(F32)

    def blk(z):
        return jnp.concatenate([jnp.where(first_head, z, 0.0), jnp.where(first_head, 0.0, z)], axis=0)

    bdot = lambda p, q: jnp.dot(p, q, preferred_element_type=F32)
    units = [(row, d, p) for row in range(nrows) for d in range(2) for p in range(WKV_PAIRS)]
    every = lambda fn, *cols: [fn(*args) for args in zip(*cols)]
    states = [s_ref[bi * nrows + row, d, p] for row, d, p in units]
    masks = [(msk_ref[d, 0] > 0.5, msk_ref[d, 1] > 0.5) for d in range(2)]

    def load(row, d, p):
        src = f_in if d == 0 else b_in
        return [s[row, :, p * LANES:(p + 1) * LANES] for s in src]

    def decays(unit, data):
        d = unit[1]
        lw = data[3]
        cum = jnp.zeros_like(lw)
        rest = lw
        for _ in range(3):
            term = rest.astype(BF16)
            cum = cum + bdot(tri_ref[d], term)
            rest = rest - term.astype(F32)
        cend = cum[tc - 1:tc] if d == 0 else cum[0:1]
        return cum, cend

    def operands(data, dec):
        a, r, v, lw, bb, kk = data
        cum, cend = dec
        e_neg = jnp.exp(-cum)
        e_end = jnp.exp(cend - cum)
        x = jnp.concatenate([blk(a * jnp.exp(cum - lw)), blk(r * jnp.exp(cum))], axis=0).astype(BF16)
        y = jnp.concatenate([blk(bb * e_neg), blk(kk * e_neg)], axis=0).astype(BF16)
        z = jnp.concatenate([blk(bb * e_end), blk(kk * e_end)], axis=0).astype(BF16)
        return x, y, z, blk(v)

    def causal(unit, g):
        strict, incl = masks[unit[1]]
        return (jnp.where(strict, g[:2 * tc, :2 * tc], 0.0), jnp.where(strict, g[:2 * tc, 2 * tc:], 0.0),
                jnp.concatenate([jnp.where(incl, g[2 * tc:, :2 * tc], 0.0),
                                 jnp.where(incl, g[2 * tc:, 2 * tc:], 0.0)], axis=1).astype(BF16))

    data = every(load, *zip(*units))
    dec = every(decays, units, data)
    ops = every(operands, data, dec)
    grams = every(lambda o: _nt_dot(o[0], o[1]), ops)
    nmat = every(causal, units, grams)
    ph = every(lambda o, st: _nt_dot(o[0], st.astype(BF16)), ops, states)
    vbf = every(lambda o: o[3].astype(BF16), ops)
    rhs = every(lambda q, n, vb: q[:2 * tc] + bdot(n[1].astype(BF16), vb), ph, nmat, vbf)
    inv = every(lambda n: eye_f + n[0], nmat)
    pw = every(lambda n: n[0].astype(BF16), nmat)
    for _ in range(WKV_DOUBLINGS):
        pw = every(lambda q: bdot(q, q).astype(BF16), pw)
        inv = every(lambda t, q: t + bdot(t.astype(BF16), q), inv, pw)
    u = every(lambda t, q: bdot(t.astype(BF16), q.astype(BF16)), inv, rhs)
    uv = every(lambda q, vb: jnp.concatenate([q.astype(BF16), vb], axis=0), u, vbf)
    yo = every(lambda q, n, w: q[2 * tc:] + bdot(n[2], w), ph, nmat, uv)
    uvt = every(lambda q, o: jnp.transpose(jnp.concatenate([q, o[3]], axis=0)).astype(BF16), u, ops)
    new = every(lambda st, dc, w, o: st * jnp.exp(dc[1]) + bdot(w, o[2]), states, dec, uvt, ops)
    for row in range(nrows):
        for d in range(2):
            out = yf_ref if d == 0 else yb_ref
            parts = [yo[units.index((row, d, p))] for p in range(WKV_PAIRS)]
            out[row] = jnp.concatenate([q[:tc] + q[tc:] for q in parts], axis=1)
    for (row, d, p), st in zip(units, new):
        s_ref[bi * nrows + row, d, p] = st


def _wkv_masks():
    tc = WKV_CHUNK
    t = np.arange(tc)
    tri = np.stack([t[None, :] <= t[:, None], t[None, :] >= t[:, None]]).astype(np.float32)
    head = np.arange(HEAD_PAIR * tc) // tc
    tt = np.arange(HEAD_PAIR * tc) % tc
    same = head[:, None] == head[None, :]
    m = np.stack([np.stack([same & (tt[None, :] < tt[:, None]), same & (tt[None, :] <= tt[:, None])]),
                  np.stack([same & (tt[None, :] > tt[:, None]), same & (tt[None, :] >= tt[:, None])])])
    return jnp.asarray(tri, BF16), jnp.asarray(m.astype(np.float32)), jnp.asarray(np.eye(LANES, dtype=np.float32), BF16)


def _wkv_scan(ins):
    bsz, seq, _ = ins["nkk"].shape
    tc = WKV_CHUNK
    nc = seq // tc
    tri, msk, eye = _wkv_masks()
    nrows = 2 if bsz % 2 == 0 else 1
    fwd = pl.BlockSpec((nrows, tc, RW), lambda c, b: (b, c, 0))
    bwd = pl.BlockSpec((nrows, tc, RW), lambda c, b: (b, nc - 1 - c, 0))
    full = lambda a: pl.BlockSpec(a.shape, lambda c, b, _n=a.ndim: (0,) * _n)
    f_args = [ins["nkk"], ins["r"], ins["v"], ins["lw0"], ins["b0"], ins["k0"]]
    b_args = [ins["nkk"], ins["r"], ins["v"], ins["lw1"], ins["b1"], ins["k1"]]
    o = jax.ShapeDtypeStruct((bsz, seq, RW), F32)
    return pl.pallas_call(
        functools.partial(_wkv_kernel, nrows=nrows),
        grid=(nc, bsz // nrows),
        in_specs=[fwd] * 6 + [bwd] * 6 + [full(tri), full(msk), full(eye)],
        out_specs=[fwd, bwd],
        out_shape=[o, o],
        scratch_shapes=[pltpu.VMEM((bsz, 2, WKV_PAIRS, LANES, LANES), F32)],
        compiler_params=_cparams(("arbitrary", "arbitrary")),
        name="wkv_scan",
    )(*f_args, *b_args, tri, msk, eye)


def _gelu_tanh(x):
    return 0.5 * x * (1.0 + jnp.tanh(math.sqrt(2.0 / math.pi) * (x + 0.044715 * (x * x * x))))


def _s5_kernel(ua_ref, ub_ref, bblk_ref, cblk_ref, lam_ref, lamc_ref, d_ref, glu_ref, o_ref,
               y_ref, st_ref, bu_ref, end_ref, carry_ref, *, seq, nb):
    ch = S5_CHUNK
    nc = seq // ch
    n = S5_FLAT
    u_halves = (ua_ref, ub_ref)
    for b in range(nb):
        for hf in range(2):
            y_ref[b, hf] = u_halves[hf][b] * d_ref[:, hf * LANES:(hf + 1) * LANES]

    def cmul_add(lre, lim, s, add):
        sre = s[:, :n]
        sim = s[:, n:]
        return jnp.concatenate([lre * sre - lim * sim + add[:, :n],
                                lre * sim + lim * sre + add[:, n:]], axis=1)

    for d in range(2):
        lre = lam_ref[d, 0:1, :]
        lim = lam_ref[d, 1:2, :]
        lcre = lamc_ref[d, 0:1, :]
        lcim = lamc_ref[d, 1:2, :]
        tloc = (lambda i: i) if d == 0 else (lambda i: ch - 1 - i)
        cloc = (lambda i: i) if d == 0 else (lambda i: nc - 1 - i)

        def drive(i):
            tl = tloc(jnp.minimum(i, ch - 1))
            rows = jnp.concatenate(
                [jnp.concatenate([r[b, pl.ds(tl, nc, stride=ch), :] for r in u_halves], axis=1)
                 for b in range(nb)], axis=0)
            return jnp.dot(rows.astype(BF16), bblk_ref[d], preferred_element_type=F32)

        def project(i, slot):
            bu_ref[slot] = drive(i)

        def advance(slot):
            st_ref[...] = cmul_add(lre, lim, st_ref[...], bu_ref[slot])

        def emit_from(st, i):
            yr = jnp.dot(st.astype(BF16), cblk_ref[d], preferred_element_type=F32)
            idx = pl.ds(tloc(i), nc, stride=ch)
            for b in range(nb):
                for hf in range(2):
                    y_ref[b, hf, idx, :] = (y_ref[b, hf, idx, :]
                                            + yr[b * nc:(b + 1) * nc, hf * LANES:(hf + 1) * LANES])

        def emit(i):
            emit_from(st_ref[...], i)

        st_ref[...] = jnp.zeros_like(st_ref)
        project(0, 0)

        def p1(j, c):
            nxt = drive(2 * j + 1)
            st = cmul_add(lre, lim, st_ref[...], bu_ref[0])
            st_ref[...] = cmul_add(lre, lim, st, nxt)
            project(2 * j + 2, 0)
            return c

        lax.fori_loop(0, ch // 2, p1, 0)
        end_ref[...] = st_ref[...]

        def cs(i, car):
            c = cloc(i)
            for b in range(nb):
                carry_ref[pl.ds(b * nc + c, 1), :] = car[b:b + 1]
            ends = jnp.concatenate([end_ref[pl.ds(b * nc + c, 1), :] for b in range(nb)], axis=0)
            return cmul_add(lcre, lcim, car, ends)

        lax.fori_loop(0, nc, cs, jnp.zeros((nb, 2 * n), F32))

        st_ref[...] = carry_ref[...]
        project(0, 0)
        project(1, 1)
        advance(0)

        def p2(j, c):
            i = 2 * j + 1
            nxt = drive(i + 1)
            st = st_ref[...]
            emit_from(st, i - 1)
            st = cmul_add(lre, lim, st, bu_ref[1])
            emit_from(st, i)
            st_ref[...] = cmul_add(lre, lim, st, nxt)
            project(i + 2, 1)
            return c

        lax.fori_loop(0, (ch - 2) // 2, p2, 0)
        emit(ch - 2)
        advance(1)
        emit(ch - 1)

    glu = glu_ref[...]
    for b in range(nb):
        g = _gelu_tanh(jnp.concatenate([y_ref[b, 0], y_ref[b, 1]], axis=1))
        o_ref[b] = (g * jax.nn.sigmoid(jnp.dot(g.astype(BF16), glu, preferred_element_type=F32))).astype(o_ref.dtype)


def _s5_params(lam_re, lam_im, log_dt, b_re, b_im, c_re, c_im):
    lre = lam_re.astype(F32)
    lim = lam_im.astype(F32)
    dt = jnp.exp(log_dt.astype(F32))[..., None]

    def cexp(scale):
        mag = jnp.exp(lre * dt * scale)
        return mag * jnp.cos(lim * dt * scale), mag * jnp.sin(lim * dt * scale)

    bar_re, bar_im = cexp(1.0)
    den = lre * lre + lim * lim
    f_re = ((bar_re - 1.0) * lre + bar_im * lim) / den
    f_im = (bar_im * lre - (bar_re - 1.0) * lim) / den
    bm_re = b_re.astype(F32)
    bm_im = b_im.astype(F32)
    bb_re = f_re[..., None] * bm_re - f_im[..., None] * bm_im
    bb_im = f_re[..., None] * bm_im + f_im[..., None] * bm_re
    eye_g = jnp.eye(S5_GROUPS, dtype=F32)

    def blockdiag_in(m):
        return jnp.einsum("dgph,gk->dghkp", m, eye_g).reshape(2, S5_WIDTH, S5_FLAT)

    def blockdiag_out(m):
        return jnp.einsum("dghp,gk->dgpkh", m, eye_g).reshape(2, S5_FLAT, S5_WIDTH)

    bblk = jnp.concatenate([blockdiag_in(bb_re), blockdiag_in(bb_im)], axis=2)
    cblk = jnp.concatenate([blockdiag_out(c_re.astype(F32)), -blockdiag_out(c_im.astype(F32))], axis=1)
    flat = lambda z: jnp.stack([z[0].reshape(2, S5_FLAT), z[1].reshape(2, S5_FLAT)], axis=1)
    return bblk.astype(BF16), cblk.astype(BF16), flat((bar_re, bar_im)), flat(cexp(float(S5_CHUNK)))


def _s5_mixer(u, bblk, cblk, lam, lamc, d_skip, glu_bf):
    bsz, seq, w = u.shape
    nc = seq // S5_CHUNK
    nb = 2 if bsz % 2 == 0 else 1
    full = lambda a: pl.BlockSpec(a.shape, lambda b, _n=a.ndim: (0,) * _n)
    args = [bblk, cblk, lam, lamc, d_skip, glu_bf]
    state = pltpu.VMEM((nb * nc, 2 * S5_FLAT), F32)
    return pl.pallas_call(
        functools.partial(_s5_kernel, seq=seq, nb=nb),
        grid=(bsz // nb,),
        in_specs=[pl.BlockSpec((nb, seq, LANES), lambda b: (b, 0, 0)),
                  pl.BlockSpec((nb, seq, LANES), lambda b: (b, 0, 1))] + [full(a) for a in args],
        out_specs=pl.BlockSpec((nb, seq, w), lambda b: (b, 0, 0)),
        out_shape=jax.ShapeDtypeStruct((bsz, seq, w), BF16),
        scratch_shapes=[pltpu.VMEM((nb, w // LANES, seq, LANES), F32),
                        state,
                        pltpu.VMEM((2, nb * nc, 2 * S5_FLAT), F32),
                        state,
                        state],
        compiler_params=_cparams(("arbitrary",)),
        name="s5_mixer",
    )(u, u, *args)


def _na_bias_table(rpb):
    q_col = np.arange(GRID_W)
    c_start = np.clip(q_col - NA_KW // 2, 0, GRID_W - NA_KW)
    k_col = np.arange(GRID_W)
    valid = (k_col[None, :] >= c_start[:, None]) & (k_col[None, :] < c_start[:, None] + NA_KW)
    dx = np.clip(k_col[None, :] - q_col[:, None] + NA_KW - 1, 0, 2 * NA_KW - 2)
    pick = (np.arange(2 * NA_KW - 1)[:, None, None] == dx[None]).astype(np.float32)
    base = jnp.einsum("hyd,dqk->hyqk", rpb.astype(F32), jnp.asarray(pick), precision=HIGHEST)
    base = jnp.where(jnp.asarray(valid)[None, None], base, -jnp.inf)
    tab = jnp.stack([base[:, NA_KH - 1 - o:2 * NA_KH - 1 - o] for o in range(NA_KH)], axis=1)
    tab = jnp.transpose(tab, (0, 1, 3, 2, 4))
    return tab.reshape(rpb.shape[0], NA_KH, GRID_W, NA_KH * GRID_W)


def _na_kernel(q_ref, k_ref, v_ref, bias_ref, o_ref, *, rows, rblk):
    rb = pl.program_id(1)
    lane = lax.broadcasted_iota(jnp.int32, (GRID_W, LANES), 1)
    low = lane < HEAD

    npair = NA_W // LANES
    every = lambda fn, *cols: [fn(*args) for args in zip(*cols)]

    def row_pair(jj, carry):
        js = [2 * jj, 2 * jj + 1]
        units = [(u, c) for u in range(2) for c in range(npair)]
        loaded = []
        for j in js:
            r = rb * rblk + j
            rs = jnp.clip(r - NA_KH // 2, 0, rows - NA_KH)
            loaded.append((q_ref[0, j],
                           k_ref[0, pl.ds(rs, NA_KH)].reshape(NA_KH * GRID_W, NA_W),
                           v_ref[0, pl.ds(rs, NA_KH)].reshape(NA_KH * GRID_W, NA_W),
                           r - rs))

        def scores(u, c):
            q, kmat, _, off = loaded[u]
            sl = slice(c * LANES, (c + 1) * LANES)
            q2 = q[:, sl].astype(F32)
            lhs = jnp.concatenate([jnp.where(low, q2, 0.0), jnp.where(low, 0.0, q2)], axis=0).astype(BF16)
            s = lax.dot_general(lhs, kmat[:, sl], (((1,), (1,)), ((), ())), preferred_element_type=F32)
            return s + jnp.concatenate([bias_ref[2 * c, off], bias_ref[2 * c + 1, off]], axis=0)

        s = every(scores, *zip(*units))
        m = every(lambda t: jnp.max(t, axis=-1, keepdims=True), s)
        p = every(lambda t, mx: jnp.exp(t - mx), s, m)
        l = every(lambda t: jnp.sum(t, axis=-1, keepdims=True), p)
        o = every(lambda t, den, uc: jnp.dot(t.astype(BF16), loaded[uc[0]][2][:, uc[1] * LANES:(uc[1] + 1) * LANES],
                                             preferred_element_type=F32) / den, p, l, units)
        o = every(lambda t: jnp.where(low, t[:GRID_W], t[GRID_W:]), o)
        for u, j in enumerate(js):
            o_ref[0, j] = jnp.concatenate(o[u * npair:(u + 1) * npair], axis=1).astype(o_ref.dtype)
        return carry

    lax.fori_loop(0, rblk // 2, row_pair, 0)


def _na_mixer(q, k, v, bias):
    bsz, seq, w = q.shape
    rows = seq // GRID_W
    rblk = 8
    g4 = lambda a: a.reshape(bsz, rows, GRID_W, w)
    img = pl.BlockSpec((1, rows, GRID_W, w), lambda b, i: (b, 0, 0, 0))
    blk = pl.BlockSpec((1, rblk, GRID_W, w), lambda b, i: (b, i, 0, 0))
    out = pl.pallas_call(
        functools.partial(_na_kernel, rows=rows, rblk=rblk),
        grid=(bsz, rows // rblk),
        in_specs=[blk, img, img, pl.BlockSpec(bias.shape, lambda b, i: (0, 0, 0, 0))],
        out_specs=blk,
        out_shape=jax.ShapeDtypeStruct((bsz, rows, GRID_W, w), BF16),
        compiler_params=_cparams(("arbitrary", "arbitrary")),
        name="na_mixer",
    )(g4(q), g4(k), g4(v), bias)
    return out.reshape(bsz, seq, w)


def _outproj_kernel(x_ref, s5_ref, yf_ref, yb_ref, gate_ref, bonus_ref, na_ref,
                    lnw_ref, lnb_ref, obm_ref, w_ref, gm_ref, g2_ref, sh_ref, sc_ref,
                    xo_ref, h_ref):
    segm = lambda t: _seg_sum(t, obm_ref[...])
    y = yf_ref[...] + yb_ref[...]
    yc = y - segm(y)
    yn = yc * lax.rsqrt(segm(yc * yc) + RWKV_GN_EPS) * lnw_ref[...] + lnb_ref[...]
    rw = (yn + bonus_ref[...]) * gate_ref[...]
    mixed = jnp.concatenate([s5_ref[...].astype(BF16), rw.astype(BF16), na_ref[...].astype(BF16)], axis=1)
    xo = x_ref[...] + gm_ref[0] * jnp.dot(mixed, w_ref[...], preferred_element_type=F32)
    xo_ref[...] = xo
    h_ref[...] = _rms_mod(xo, g2_ref[...], sh_ref[0], sc_ref[0])


def _out_proj(x2, s5o, yf, yb, gate, bonus, nao, lnw, lnb, obm, w_bf, gate_mix, g2, shift, scale, seq):
    t, d = x2.shape
    tm = TOKEN_TILE
    per_b = seq // tm
    row = lambda w: pl.BlockSpec((tm, w), lambda i: (i, 0))
    full = lambda a: pl.BlockSpec(a.shape, lambda i, _n=a.ndim: (0,) * _n)
    bvec = pl.BlockSpec((1, 1, d), lambda i: (i // per_b, 0, 0))
    o = jax.ShapeDtypeStruct((t, d), F32)
    return pl.pallas_call(
        _outproj_kernel,
        grid=(t // tm,),
        in_specs=[row(d), row(S5_WIDTH)] + [row(RW)] * 5 +
                 [full(lnw), full(lnb), full(obm), full(w_bf), bvec, full(g2), bvec, bvec],
        out_specs=[row(d), row(d)],
        out_shape=[o, o],
        compiler_params=_cparams(("arbitrary",)),
        name="out_proj",
    )(x2, s5o, yf, yb, gate, bonus, nao, lnw, lnb, obm, w_bf, gate_mix, g2, shift, scale)


MOE_TILE = 256
SEG_ALIGN = 8
MOE_SLOTS = -(-(MOE_TILE * TOP_K + N_EXPERTS * (SEG_ALIGN - 1)) // LANES) * LANES


def _router_kernel(h_ref, w_ref, b_ref, tri_ref, upper_ref, slot_ref, gate_ref, cnt_ref, base_ref, loc_ref,
                   carry_ref):
    @pl.when(pl.program_id(0) == 0)
    def _():
        carry_ref[...] = jnp.zeros_like(carry_ref)

    logits = jnp.dot(h_ref[...], w_ref[...], precision=HIGHEST, preferred_element_type=F32) + b_ref[...]
    tm = logits.shape[0]
    lane = lax.broadcasted_iota(jnp.int32, (tm, LANES), 1)
    lane_f = lane.astype(F32)
    vals, hots = [], []
    cur = logits
    for _ in range(TOP_K):
        m = jnp.max(cur, axis=-1, keepdims=True)
        idx = jnp.min(jnp.where(cur == m, lane_f, float(LANES)), axis=-1, keepdims=True)
        hot = lane_f == idx
        vals.append(m)
        hots.append(hot)
        cur = jnp.where(hot, -jnp.inf, cur)
    exps = [jnp.exp(v - vals[0]) for v in vals]
    den = exps[0] + exps[1] + exps[2] + exps[3]
    assign = sum(h.astype(F32) for h in hots)
    before = jnp.dot(tri_ref[...], assign.astype(BF16), preferred_element_type=F32)
    cnt = jnp.sum(assign, axis=0, keepdims=True)
    cnt = jnp.floor((cnt + (SEG_ALIGN - 1)) * (1.0 / SEG_ALIGN)) * SEG_ALIGN
    cnt8 = jnp.broadcast_to(cnt, (8, LANES)).astype(BF16)
    loc = jnp.dot(cnt8, upper_ref[...], preferred_element_type=F32)[0:1, :]
    place = before + loc
    s_out = jnp.zeros((tm, LANES), jnp.int32)
    g_out = jnp.zeros((tm, LANES), F32)
    for kk in range(TOP_K):
        slot = jnp.sum(jnp.where(hots[kk], place, 0.0), axis=-1, keepdims=True)
        sel = lane == kk
        s_out = jnp.where(sel, slot.astype(jnp.int32), s_out)
        g_out = jnp.where(sel, exps[kk] / den, g_out)
    slot_ref[...] = s_out
    gate_ref[...] = g_out
    cnt_ref[0] = cnt.astype(jnp.int32)
    base_ref[0] = carry_ref[...].astype(jnp.int32)
    loc_ref[0] = loc.astype(jnp.int32)
    carry_ref[...] = carry_ref[...] + cnt


def _router(h2, rw_pad, rb_pad):
    t, d = h2.shape
    tm = MOE_TILE
    nt = t // tm
    tri = jnp.asarray(np.tril(np.ones((tm, tm), np.float32), -1), BF16)
    upper = jnp.asarray(np.triu(np.ones((LANES, LANES), np.float32), 1), BF16)
    row = pl.BlockSpec((tm, LANES), lambda i: (i, 0))
    per_tile = pl.BlockSpec((1, 1, LANES), lambda i: (i, 0, 0))
    full = lambda a: pl.BlockSpec(a.shape, lambda i, _n=a.ndim: (0,) * _n)
    tile_i32 = jax.ShapeDtypeStruct((nt, 1, LANES), jnp.int32)
    return pl.pallas_call(
        _router_kernel,
        grid=(nt,),
        in_specs=[pl.BlockSpec((tm, d), lambda i: (i, 0)), full(rw_pad), full(rb_pad), full(tri), full(upper)],
        out_specs=[row, row, per_tile, per_tile, per_tile],
        out_shape=[jax.ShapeDtypeStruct((t, LANES), jnp.int32),
                   jax.ShapeDtypeStruct((t, LANES), F32),
                   tile_i32, tile_i32, tile_i32],
        scratch_shapes=[pltpu.VMEM((1, LANES), F32)],
        compiler_params=_cparams(("arbitrary",)),
        name="moe_router",
    )(h2, rw_pad, rb_pad, tri, upper)


def _segment_dmas(cnt_ref, loc_ref, row_ref, tile, ne, make_copy, wait):
    def per_expert(e, carry):
        i = tile * ne + e
        n = cnt_ref[i]

        @pl.when(n > 0)
        def _():
            cp = make_copy(pl.multiple_of(loc_ref[i], SEG_ALIGN), pl.multiple_of(row_ref[i], SEG_ALIGN),
                           pl.multiple_of(n, SEG_ALIGN))
            if wait:
                cp.wait()
            else:
                cp.start()

        return carry

    lax.fori_loop(0, ne, per_expert, 0)


def _dispatch_kernel(cnt_ref, loc_ref, row_ref, zcnt_ref, zoff_ref, zrow_ref, nu_ref,
                     slot_ref, h_ref, xb_ref, sorted_ref, zero_ref, sem, zsem, *, ne, nblk):
    i = pl.program_id(0)
    last = pl.num_programs(0) - 1
    cur = i % 2
    tm = h_ref.shape[0]
    ns = MOE_SLOTS

    def zero_copy(off, row, p):
        return pltpu.make_async_copy(zero_ref.at[pl.ds(off, p), :], xb_ref.at[pl.ds(row, p), :], zsem)

    def tail_copy(j):
        row = pl.multiple_of((nu_ref[0] + j) * MOE_BLOCK, MOE_BLOCK)
        return pltpu.make_async_copy(zero_ref, xb_ref.at[pl.ds(row, MOE_BLOCK), :], zsem)

    def zero_fill(wait):
        _segment_dmas(zcnt_ref, zoff_ref, zrow_ref, 0, ne, zero_copy, wait)

        def tail(j, c):
            if wait:
                tail_copy(j).wait()
            else:
                tail_copy(j).start()
            return c

        lax.fori_loop(0, nblk - nu_ref[0], tail, 0)

    @pl.when(i == 0)
    def _():
        zero_ref[...] = jnp.zeros_like(zero_ref)
        zero_fill(wait=False)

    slot_t = jnp.transpose(slot_ref[...].astype(F32))
    srow = lax.broadcasted_iota(jnp.int32, (ns, tm), 0).astype(F32)
    pick = jnp.zeros((ns, tm), F32)
    for kk in range(TOP_K):
        pick = jnp.where(srow == slot_t[kk:kk + 1, :], 1.0, pick)
    sorted_ref[cur] = jnp.dot(pick.astype(BF16), h_ref[...].astype(BF16), preferred_element_type=F32)

    def copy_from(buf):
        def copy(off, row, p):
            return pltpu.make_async_copy(sorted_ref.at[buf, pl.ds(off, p), :], xb_ref.at[pl.ds(row, p), :],
                                         sem.at[buf])
        return copy

    _segment_dmas(cnt_ref, loc_ref, row_ref, i, ne, copy_from(cur), wait=False)

    @pl.when(i > 0)
    def _():
        _segment_dmas(cnt_ref, loc_ref, row_ref, i - 1, ne, copy_from(1 - cur), wait=True)

    @pl.when(i == last)
    def _():
        _segment_dmas(cnt_ref, loc_ref, row_ref, i, ne, copy_from(cur), wait=True)
        zero_fill(wait=True)


def _dispatch(cnt, loc, rowstart, zcnt, zrow, n_used, slot, h2, n_rows, ne):
    t, d = h2.shape
    tm = MOE_TILE
    zoff = jnp.zeros_like(zcnt)
    grid_spec = pltpu.PrefetchScalarGridSpec(
        num_scalar_prefetch=7,
        grid=(t // tm,),
        in_specs=[pl.BlockSpec((tm, LANES), lambda i, *_: (i, 0)),
                  pl.BlockSpec((tm, d), lambda i, *_: (i, 0))],
        out_specs=pl.BlockSpec(memory_space=pl.ANY),
        scratch_shapes=[pltpu.VMEM((2, MOE_SLOTS, d), F32), pltpu.VMEM((MOE_BLOCK, d), F32),
                        pltpu.SemaphoreType.DMA((2,)), pltpu.SemaphoreType.DMA(())],
    )
    return pl.pallas_call(
        functools.partial(_dispatch_kernel, ne=ne, nblk=n_rows // MOE_BLOCK),
        grid_spec=grid_spec,
        out_shape=jax.ShapeDtypeStruct((n_rows, d), F32),
        compiler_params=_cparams(("arbitrary",)),
        name="moe_dispatch",
    )(cnt, loc, rowstart, zcnt, zoff, zrow, n_used, slot, h2)


PAIR_GROUP = 2 * LANES


def _pair_perm():
    p = np.zeros((PAIR_GROUP, PAIR_GROUP), np.float32)
    j = np.arange(LANES)
    p[2 * j, j] = 1.0
    p[2 * j + 1, LANES + j] = 1.0
    return jnp.asarray(p, BF16)


def _expert_kernel(be_ref, nu_ref, x_ref, w1_ref, b1_ref, w2_ref, b2_ref, perm_ref, y_ref, w1s_ref, w2s_ref):
    i = pl.program_id(0)
    f2 = w1_ref.shape[2]
    ngrp = f2 // PAIR_GROUP

    @pl.when(i >= nu_ref[0])
    def _():
        y_ref[...] = jnp.zeros_like(y_ref)

    @pl.when(i < nu_ref[0])
    def _():
        @pl.when((i == 0) | (be_ref[i] != be_ref[jnp.maximum(i - 1, 0)]))
        def _():
            for g in range(ngrp):
                sl = slice(g * PAIR_GROUP, (g + 1) * PAIR_GROUP)
                w1s_ref[:, sl] = jnp.dot(w1_ref[0, :, sl].astype(BF16), perm_ref[...],
                                         preferred_element_type=F32).astype(BF16)
            w2s_ref[...] = w2_ref[0].astype(BF16)

        hdn = jnp.dot(x_ref[...].astype(BF16), w1s_ref[...], preferred_element_type=F32) + b1_ref[0]
        glu = jnp.concatenate([hdn[:, g * PAIR_GROUP:g * PAIR_GROUP + LANES] for g in range(ngrp)], axis=1)
        lin = jnp.concatenate([hdn[:, g * PAIR_GROUP + LANES:(g + 1) * PAIR_GROUP] for g in range(ngrp)], axis=1)
        glu = jnp.minimum(glu, SWIGLU_LIMIT)
        lin = jnp.clip(lin, -SWIGLU_LIMIT, SWIGLU_LIMIT)
        act = glu * jax.nn.sigmoid(SWIGLU_ALPHA * glu) * (lin + 1.0)
        y_ref[...] = jnp.dot(act.astype(BF16), w2s_ref[...], preferred_element_type=F32) + b2_ref[0]


def _experts(block_e, n_used, xb, w1, b1_grp, w2, b2, layer):
    n_rows, d = xb.shape
    _, ne, _, f2 = w1.shape
    dff = w2.shape[2]
    nblk = n_rows // MOE_BLOCK
    perm = _pair_perm()
    blk = lambda i, be, nu: (jnp.minimum(i, nu[0] - 1), 0)
    wsel = lambda i, be, nu: (layer, be[i], 0, 0)
    grid_spec = pltpu.PrefetchScalarGridSpec(
        num_scalar_prefetch=2,
        grid=(nblk,),
        in_specs=[pl.BlockSpec((MOE_BLOCK, d), blk),
                  pl.BlockSpec((None, 1, d, f2), wsel),
                  pl.BlockSpec((None, 1, 1, f2), wsel),
                  pl.BlockSpec((None, 1, dff, d), wsel),
                  pl.BlockSpec((None, 1, 1, d), wsel),
                  pl.BlockSpec(perm.shape, lambda i, be, nu: (0, 0))],
        out_specs=pl.BlockSpec((MOE_BLOCK, d), lambda i, be, nu: (i, 0)),
        scratch_shapes=[pltpu.VMEM((d, f2), BF16), pltpu.VMEM((dff, d), BF16)],
    )
    nl = w1.shape[0]
    return pl.pallas_call(
        _expert_kernel,
        grid_spec=grid_spec,
        out_shape=jax.ShapeDtypeStruct((n_rows, d), F32),
        compiler_params=_cparams(("arbitrary",)),
        name="moe_experts",
    )(block_e, n_used, xb, w1, b1_grp.reshape(nl, ne, 1, f2), w2, b2.reshape(nl, ne, 1, d), perm)


def _combine_kernel(cnt_ref, loc_ref, row_ref, slot_ref, gates_ref, x_ref, gf_ref, yb_ref, o_ref, sorted_ref, sem,
                    *, ne):
    i = pl.program_id(0)
    last = pl.num_programs(0) - 1
    cur = i % 2
    tm = x_ref.shape[0]
    ns = MOE_SLOTS

    def copy_into(buf):
        def copy(off, row, p):
            return pltpu.make_async_copy(yb_ref.at[pl.ds(row, p), :], sorted_ref.at[buf, pl.ds(off, p), :],
                                         sem.at[buf])
        return copy

    @pl.when(i == 0)
    def _():
        sorted_ref[...] = jnp.zeros_like(sorted_ref)
        _segment_dmas(cnt_ref, loc_ref, row_ref, i, ne, copy_into(cur), wait=False)

    @pl.when(i < last)
    def _():
        _segment_dmas(cnt_ref, loc_ref, row_ref, i + 1, ne, copy_into(1 - cur), wait=False)

    slot = slot_ref[...]
    gates = gates_ref[...]
    scol = lax.broadcasted_iota(jnp.int32, (tm, ns), 1)
    gmat = jnp.zeros((tm, ns), F32)
    for kk in range(TOP_K):
        gmat = jnp.where(scol == slot[:, kk:kk + 1], gates[:, kk:kk + 1], gmat)
    _segment_dmas(cnt_ref, loc_ref, row_ref, i, ne, copy_into(cur), wait=True)
    acc = jnp.dot(gmat.astype(BF16), sorted_ref[cur].astype(BF16), preferred_element_type=F32)
    o_ref[...] = x_ref[...] + gf_ref[0] * acc


def _combine(cnt, loc, rowstart, slot, gates, x2, gate_ffn, yb, seq, ne):
    t, d = x2.shape
    tm = MOE_TILE
    per_b = seq // tm
    grid_spec = pltpu.PrefetchScalarGridSpec(
        num_scalar_prefetch=3,
        grid=(t // tm,),
        in_specs=[pl.BlockSpec((tm, LANES), lambda i, *_: (i, 0)),
                  pl.BlockSpec((tm, LANES), lambda i, *_: (i, 0)),
                  pl.BlockSpec((tm, d), lambda i, *_: (i, 0)),
                  pl.BlockSpec((1, 1, d), lambda i, *_: (i // per_b, 0, 0)),
                  pl.BlockSpec(memory_space=pl.ANY)],
        out_specs=pl.BlockSpec((tm, d), lambda i, *_: (i, 0)),
        scratch_shapes=[pltpu.VMEM((2, MOE_SLOTS, d), F32), pltpu.SemaphoreType.DMA((2,))],
    )
    return pl.pallas_call(
        functools.partial(_combine_kernel, ne=ne),
        grid_spec=grid_spec,
        out_shape=jax.ShapeDtypeStruct((t, d), F32),
        compiler_params=_cparams(("arbitrary",)),
        name="moe_combine",
    )(cnt, loc, rowstart, slot, gates, x2, gate_ffn, yb)


def _group_pairs(b1):
    lead = b1.shape[:-1]
    g = b1.reshape(lead + (-1, LANES, 2))
    return jnp.swapaxes(g, -1, -2).reshape(b1.shape)


def _moe_layer(x2, h2, gate_ffn, router_w, router_b, w1, b1_grp, w2, b2, seq, layer):
    t, d = x2.shape
    ne = router_w.shape[1]
    rw_pad = jnp.zeros((d, LANES), F32).at[:, :ne].set(router_w.astype(F32))
    rb_pad = jnp.full((1, LANES), -jnp.inf, F32).at[0, :ne].set(router_b.astype(F32))
    slot, gates, cnt3, base3, loc3 = _router(h2, rw_pad, rb_pad)
    n_assign = t * TOP_K
    n_tiles = t // MOE_TILE
    n_blocks = -(-(n_assign + n_tiles * ne * (SEG_ALIGN - 1)) // MOE_BLOCK) + ne
    cnt = cnt3[:, 0, :ne]
    base = base3[:, 0, :ne]
    loc = loc3[:, 0, :ne]
    total = base[-1] + cnt[-1]
    padded = ((total + MOE_BLOCK - 1) // MOE_BLOCK) * MOE_BLOCK
    pad_end = jnp.cumsum(padded)
    pad_start = pad_end - padded
    rowstart = (pad_start[None, :] + base).reshape(-1).astype(jnp.int32)
    n_used = (pad_end[-1] // MOE_BLOCK).astype(jnp.int32)
    block_start = jnp.minimum(jnp.arange(n_blocks, dtype=jnp.int32), n_used - 1) * MOE_BLOCK
    block_e = jnp.minimum(jnp.sum(block_start[:, None] >= pad_end[None, :], axis=-1), ne - 1).astype(jnp.int32)
    cnt_f = cnt.reshape(-1)
    loc_f = loc.reshape(-1)
    n_used = n_used.reshape(1)
    xb = _dispatch(cnt_f, loc_f, rowstart, (padded - total).astype(jnp.int32), (pad_start + total).astype(jnp.int32),
                   n_used, slot, h2, n_blocks * MOE_BLOCK, ne)
    yb = _experts(block_e, n_used, xb, w1, b1_grp, w2, b2, layer)
    return _combine(cnt_f, loc_f, rowstart, slot, gates, x2, gate_ffn, yb, seq, ne)


def kernel(x, c, ada_w, ada_b, norm1_g, norm2_g, w_in, w_out, s5_lam_re, s5_lam_im, s5_log_dt, s5_b_re, s5_b_im, s5_c_re, s5_c_im, s5_d, s5_glu_w, rwkv_mu, rwkv_w0, rwkv_w1, rwkv_w2, rwkv_a0, rwkv_a1, rwkv_a2, rwkv_g1, rwkv_g2, rwkv_k_k, rwkv_k_a, rwkv_r_k, rwkv_ln_w, rwkv_ln_b, na_q_g, na_k_g, na_rpb, router_w, router_b, exp_w1, exp_b1, exp_w2, exp_b2):
    bsz, seq, d = x.shape
    depth = ada_w.shape[0]
    t = bsz * seq
    mod = _ada_mod(c, ada_w, ada_b).reshape(depth, bsz, 6, 1, d)
    x2 = x.reshape(t, d)
    seg_ones = _seg_ones(RW, dtype=BF16)
    seg_mean = _seg_ones(RW, dtype=BF16, scale=1.0 / HEAD)
    row = lambda a: a.reshape(1, -1).astype(F32)
    b1_grp = _group_pairs(exp_b1.astype(F32))
    exp_b2f = exp_b2.astype(F32)
    for l in range(depth):
        m = lambda j: mod[l, :, j]
        s5u, xr, qkv = _in_proj(x2, row(norm1_g[l]), m(0), m(1), w_in[l].astype(BF16), seq)
        prep_params = dict(
            mu=row(rwkv_mu[l]), k_k=row(rwkv_k_k[l]), k_a=row(rwkv_k_a[l]), r_k=row(rwkv_r_k[l]),
            w0=rwkv_w0[l].astype(F32), a0=rwkv_a0[l].astype(F32),
            w1=rwkv_w1[l].astype(BF16), w2=rwkv_w2[l].astype(BF16),
            a1=rwkv_a1[l].astype(BF16), a2=rwkv_a2[l].astype(BF16),
            g1=rwkv_g1[l].astype(BF16), g2=rwkv_g2[l].astype(BF16),
            q_g=row(jnp.tile(na_q_g[l], NA_W // HEAD)), k_g=row(jnp.tile(na_k_g[l], NA_W // HEAD)), ob=seg_ones)
        (nkk, r, v, lw0, b0, k0, lw1, b1, k1, gate, bonus, naq, nak, nav) = _prep(
            xr.reshape(bsz, seq, 4 * RW), qkv.reshape(bsz, seq, 3 * NA_W), prep_params)
        yf, yb = _wkv_scan(dict(nkk=nkk, r=r, v=v, lw0=lw0, b0=b0, k0=k0, lw1=lw1, b1=b1, k1=k1))
        bblk, cblk, lam, lamc = _s5_params(s5_lam_re[l], s5_lam_im[l], s5_log_dt[l], s5_b_re[l], s5_b_im[l],
                                           s5_c_re[l], s5_c_im[l])
        s5o = _s5_mixer(s5u.reshape(bsz, seq, S5_WIDTH), bblk, cblk, lam, lamc, row(s5_d[l]),
                        s5_glu_w[l].astype(BF16))
        nao = _na_mixer(naq, nak, nav, _na_bias_table(na_rpb[l]))
        flat = lambda a: a.reshape(t, -1)
        x2, h2 = _out_proj(x2, flat(s5o), flat(yf), flat(yb), flat(gate), flat(bonus), flat(nao),
                           row(rwkv_ln_w[l]), row(rwkv_ln_b[l]), seg_mean, w_out[l].astype(BF16),
                           m(2), row(norm2_g[l]), m(3), m(4), seq)
        x2 = _moe_layer(x2, h2, m(5), router_w[l], router_b[l], exp_w1, b1_grp, exp_w2, exp_b2f, seq, l)
    return x2.reshape(bsz, seq, d)
```

```python
import functools
import math

import numpy as np
import jax
import jax.numpy as jnp
from jax import lax
from jax.experimental import pallas as pl
from jax.experimental.pallas import tpu as pltpu

F32 = jnp.float32
BF16 = jnp.bfloat16
HIGHEST = lax.Precision.HIGHEST

D_MODEL = 1024
S5_WIDTH = 256
S5_GROUP = 16
S5_GROUPS = 16
S5_STATE = 64
S5_CHUNK = 64
S5_FLAT = S5_GROUPS * S5_STATE
RW = 384
HEAD = 64
RWKV_GN_EPS = 64e-5
NA_W = 384
GRID_W = 64
NA_KH = 8
NA_KW = 16
N_EXPERTS = 32
TOP_K = 4
MOE_BLOCK = 256
SWIGLU_ALPHA = 1.702
SWIGLU_LIMIT = 7.0
RMS_EPS = 1e-6
LANES = 128
WKV_CHUNK = 64
TOKEN_TILE = 512
VMEM_LIMIT = 56 * 1024 * 1024


def _cparams(sem):
    return pltpu.CompilerParams(dimension_semantics=sem, vmem_limit_bytes=VMEM_LIMIT)


def _seg_ones(n, seg=HEAD, dtype=F32, scale=1.0):
    idx = np.arange(n) // seg
    return jnp.asarray((idx[:, None] == idx[None, :]).astype(np.float32) * scale, dtype)


def _seg_sum(t, ones_bf):
    hi = t.astype(BF16)
    lo = (t - hi.astype(F32)).astype(BF16)
    return (jnp.dot(hi, ones_bf, preferred_element_type=F32) + jnp.dot(lo, ones_bf, preferred_element_type=F32))


def _ada_kernel(c_ref, w_ref, b_ref, o_ref):
    c = c_ref[...]
    cond = c * jax.nn.sigmoid(c)
    o_ref[0] = jnp.dot(cond, w_ref[0], preferred_element_type=F32) + b_ref[0]


def _ada_mod(c, ada_w, ada_b):
    nl, d, n6 = ada_w.shape
    bsz = c.shape[0]
    tn = 1536
    return pl.pallas_call(
        _ada_kernel,
        grid=(nl, n6 // tn),
        in_specs=[pl.BlockSpec((bsz, d), lambda l, j: (0, 0)),
                  pl.BlockSpec((1, d, tn), lambda l, j: (l, 0, j)),
                  pl.BlockSpec((1, 1, tn), lambda l, j: (l, 0, j))],
        out_specs=pl.BlockSpec((1, bsz, tn), lambda l, j: (l, 0, j)),
        out_shape=jax.ShapeDtypeStruct((nl, bsz, n6), F32),
        compiler_params=_cparams(("arbitrary", "arbitrary")),
        name="ada_mod",
    )(c, ada_w, ada_b.reshape(nl, 1, n6))


def _rms_mod(x, g, shift, scale):
    ms = jnp.mean(x * x, axis=-1, keepdims=True)
    h = x * lax.rsqrt(ms + RMS_EPS) * g
    return h * (1.0 + scale) + shift


def _proj_kernel(x_ref, g_ref, sh_ref, sc_ref, w_ref, o_s5, o_rw, o_na):
    h = _rms_mod(x_ref[...], g_ref[...], sh_ref[0], sc_ref[0])
    p = jnp.dot(h.astype(BF16), w_ref[...], preferred_element_type=F32)
    o_s5[...] = p[:, :S5_WIDTH]
    o_rw[...] = p[:, S5_WIDTH:S5_WIDTH + 4 * RW]
    o_na[...] = p[:, S5_WIDTH + 4 * RW:]


def _in_proj(x2, g, shift, scale, w_bf, seq):
    t, d = x2.shape
    n = w_bf.shape[1]
    tm = TOKEN_TILE
    per_b = seq // tm
    row = lambda i: (i, 0)
    bvec = lambda i: (i // per_b, 0, 0)
    return pl.pallas_call(
        _proj_kernel,
        grid=(t // tm,),
        in_specs=[pl.BlockSpec((tm, d), row),
                  pl.BlockSpec((1, d), lambda i: (0, 0)),
                  pl.BlockSpec((1, 1, d), bvec),
                  pl.BlockSpec((1, 1, d), bvec),
                  pl.BlockSpec((d, n), lambda i: (0, 0))],
        out_specs=[pl.BlockSpec((tm, S5_WIDTH), row),
                   pl.BlockSpec((tm, 4 * RW), row),
                   pl.BlockSpec((tm, 3 * NA_W), row)],
        out_shape=[jax.ShapeDtypeStruct((t, S5_WIDTH), F32),
                   jax.ShapeDtypeStruct((t, 4 * RW), F32),
                   jax.ShapeDtypeStruct((t, 3 * NA_W), F32)],
        compiler_params=_cparams(("arbitrary",)),
        name="in_proj",
    )(x2, g, shift, scale, w_bf)


def _softplus(x):
    return jnp.maximum(x, 0.0) + jnp.log(1.0 + jnp.exp(-jnp.abs(x)))


def _prep_kernel(xr_ref, prev_ref, next_ref, qkv_ref,
                 mu_ref, kk_ref, ka_ref, rk_ref, w0_ref, a0_ref,
                 w1_ref, w2_ref, a1_ref, a2_ref, g1_ref, g2_ref, qg_ref, kg_ref, ob_ref,
                 nkk_o, r_o, v_o, lw0_o, b0_o, k0_o, lw1_o, b1_o, k1_o,
                 gate_o, bonus_o, naq_o, nak_o, nav_o):
    i = pl.program_id(1)
    nblk = pl.num_programs(1)
    x = xr_ref[0]
    tm = x.shape[0]
    prow = jnp.where(i == 0, 0.0, prev_ref[0][7:8, :])
    nrow = jnp.where(i == nblk - 1, 0.0, next_ref[0][0:1, :])
    rid = lax.broadcasted_iota(jnp.int32, x.shape, 0)
    prev = jnp.where(rid == 0, prow, pltpu.roll(x, 1, axis=0))
    nxt = jnp.where(rid == tm - 1, nrow, pltpu.roll(x, tm - 1, axis=0))
    xs = x + (0.5 * (prev + nxt) - x) * mu_ref[...]
    r = xs[:, 0:RW]
    k = xs[:, RW:2 * RW]
    v = xs[:, 2 * RW:3 * RW]
    z = xs[:, 3 * RW:4 * RW]
    seg = lambda t: _seg_sum(t, ob_ref[...])
    zb = z.astype(BF16)
    bdot = lambda a, w: jnp.dot(a.astype(BF16), w, preferred_element_type=F32)
    gate_o[0] = bdot(jax.nn.sigmoid(bdot(zb, g1_ref[...])), g2_ref[...])
    kk = k * kk_ref[...]
    kk = kk / jnp.maximum(jnp.sqrt(seg(kk * kk)), 1e-12)
    nkk_o[0] = -kk
    r_o[0] = r
    v_o[0] = v
    bonus_o[0] = seg(r * k * rk_ref[...]) * v
    outs = ((lw0_o, b0_o, k0_o), (lw1_o, b1_o, k1_o))
    for d in range(2):
        wl = w0_ref[d:d + 1, :] + bdot(jnp.tanh(bdot(zb, w1_ref[d])), w2_ref[d])
        w = -_softplus(-wl) - 0.5
        a = jax.nn.sigmoid(a0_ref[d:d + 1, :] + bdot(bdot(zb, a1_ref[d]), a2_ref[d]))
        lw_o, b_o, k_o = outs[d]
        lw_o[0] = -jnp.exp(w)
        b_o[0] = kk * a
        k_o[0] = k * (1.0 + (a - 1.0) * ka_ref[...])
    qkv = qkv_ref[0]
    segm = lambda t: seg(t) * (1.0 / HEAD)
    qn = qkv[:, 0:NA_W]
    kn = qkv[:, NA_W:2 * NA_W]
    naq_o[0] = (qn * lax.rsqrt(segm(qn * qn) + RMS_EPS) * qg_ref[...] * (HEAD ** -0.5)).astype(BF16)
    nak_o[0] = (kn * lax.rsqrt(segm(kn * kn) + RMS_EPS) * kg_ref[...]).astype(BF16)
    nav_o[0] = qkv[:, 2 * NA_W:].astype(BF16)


def _prep(xr, qkv, p):
    bsz, seq, _ = xr.shape
    tm = TOKEN_TILE
    nb = seq // tm
    h8 = tm // 8
    blk = lambda w: pl.BlockSpec((1, tm, w), lambda b, i: (b, i, 0))
    full = lambda a: pl.BlockSpec(a.shape, lambda b, i, _n=a.ndim: (0,) * _n)
    params = [p["mu"], p["k_k"], p["k_a"], p["r_k"], p["w0"], p["a0"], p["w1"], p["w2"], p["a1"], p["a2"],
              p["g1"], p["g2"], p["q_g"], p["k_g"], p["ob"]]
    f32o = jax.ShapeDtypeStruct((bsz, seq, RW), F32)
    bfo = jax.ShapeDtypeStruct((bsz, seq, NA_W), BF16)
    return pl.pallas_call(
        _prep_kernel,
        grid=(bsz, nb),
        in_specs=[blk(4 * RW),
                  pl.BlockSpec((1, 8, 4 * RW), lambda b, i: (b, jnp.maximum(i * h8 - 1, 0), 0)),
                  pl.BlockSpec((1, 8, 4 * RW), lambda b, i: (b, jnp.minimum((i + 1) * h8, seq // 8 - 1), 0)),
                  blk(3 * NA_W)] + [full(a) for a in params],
        out_specs=[blk(RW)] * 14,
        out_shape=[f32o] * 11 + [bfo] * 3,
        compiler_params=_cparams(("arbitrary", "arbitrary")),
        name="mixer_prep",
    )(xr, xr, xr, qkv, *params)


HEAD_PAIR = LANES // HEAD
WKV_PAIRS = RW // LANES
WKV_DOUBLINGS = WKV_CHUNK.bit_length() - 2


def _nt_dot(a, b):
    return lax.dot_general(a, b, (((1,), (1,)), ((), ())), preferred_element_type=F32)


def _wkv_kernel(*refs, nrows):
    f_in = refs[0:6]
    b_in = refs[6:12]
    tri_ref, msk_ref, eye_ref = refs[12:15]
    yf_ref, yb_ref = refs[15:17]
    s_ref = refs[17]
    c = pl.program_id(0)
    bi = pl.program_id(1)
    tc = WKV_CHUNK

    @pl.when(c == 0)
    def _():
        for row in range(nrows):
            s_ref[bi * nrows + row] = jnp.zeros(s_ref.shape[1:], F32)

    first_head = lax.broadcasted_iota(jnp.int32, (tc, LANES), 1) < HEAD
    eye_bf = eye_ref[...]
    eye_f = eye_bf.astype(F32)

    def blk(z):
        return jnp.concatenate([jnp.where(first_head, z, 0.0), jnp.where(first_head, 0.0, z)], axis=0)

    bdot = lambda p, q: jnp.dot(p, q, preferred_element_type=F32)
    units = [(row, d, p) for row in range(nrows) for d in range(2) for p in range(WKV_PAIRS)]
    every = lambda fn, *cols: [fn(*args) for args in zip(*cols)]
    states = [s_ref[bi * nrows + row, d, p] for row, d, p in units]
    masks = [(msk_ref[d, 0] > 0.5, msk_ref[d, 1] > 0.5) for d in range(2)]

    def load(row, d, p):
        src = f_in if d == 0 else b_in
        return [s[row, :, p * LANES:(p + 1) * LANES] for s in src]

    def decays(unit, data):
        d = unit[1]
        lw = data[3]
        cum = jnp.zeros_like(lw)
        rest = lw
        for _ in range(3):
            term = rest.astype(BF16)
            cum = cum + bdot(tri_ref[d], term)
            rest = rest - term.astype(F32)
        cend = cum[tc - 1:tc] if d == 0 else cum[0:1]
        return cum, cend

    def operands(data, dec):
        a, r, v, lw, bb, kk = data
        cum, cend = dec
        e_neg = jnp.exp(-cum)
        e_end = jnp.exp(cend - cum)
        x = jnp.concatenate([blk(a * jnp.exp(cum - lw)), blk(r * jnp.exp(cum))], axis=0).astype(BF16)
        y = jnp.concatenate([blk(bb * e_neg), blk(kk * e_neg)], axis=0).astype(BF16)
        z = jnp.concatenate([blk(bb * e_end), blk(kk * e_end)], axis=0).astype(BF16)
        return x, y, z, blk(v)

    def causal(unit, g):
        strict, incl = masks[unit[1]]
        return (jnp.where(strict, g[:2 * tc, :2 * tc], 0.0), jnp.where(strict, g[:2 * tc, 2 * tc:], 0.0),
                jnp.concatenate([jnp.where(incl, g[2 * tc:, :2 * tc], 0.0),
                                 jnp.where(incl, g[2 * tc:, 2 * tc:], 0.0)], axis=1).astype(BF16))

    data = every(load, *zip(*units))
    dec = every(decays, units, data)
    ops = every(operands, data, dec)
    grams = every(lambda o: _nt_dot(o[0], o[1]), ops)
    nmat = every(causal, units, grams)
    ph = every(lambda o, st: _nt_dot(o[0], st.astype(BF16)), ops, states)
    vbf = every(lambda o: o[3].astype(BF16), ops)
    rhs = every(lambda q, n, vb: q[:2 * tc] + bdot(n[1].astype(BF16), vb), ph, nmat, vbf)
    inv = every(lambda n: eye_f + n[0], nmat)
    pw = every(lambda n: n[0].astype(BF16), nmat)
    for _ in range(WKV_DOUBLINGS):
        pw = every(lambda q: bdot(q, q).astype(BF16), pw)
        inv = every(lambda t, q: t + bdot(t.astype(BF16), q), inv, pw)
    u = every(lambda t, q: bdot(t.astype(BF16), q.astype(BF16)), inv, rhs)
    uv = every(lambda q, vb: jnp.concatenate([q.astype(BF16), vb], axis=0), u, vbf)
    yo = every(lambda q, n, w: q[2 * tc:] + bdot(n[2], w), ph, nmat, uv)
    uvt = every(lambda q, o: jnp.transpose(jnp.concatenate([q, o[3]], axis=0)).astype(BF16), u, ops)
    new = every(lambda st, dc, w, o: st * jnp.exp(dc[1]) + bdot(w, o[2]), states, dec, uvt, ops)
    for row in range(nrows):
        for d in range(2):
            out = yf_ref if d == 0 else yb_ref
            parts = [yo[units.index((row, d, p))] for p in range(WKV_PAIRS)]
            out[row] = jnp.concatenate([q[:tc] + q[tc:] for q in parts], axis=1)
    for (row, d, p), st in zip(units, new):
        s_ref[bi * nrows + row, d, p] = st


def _wkv_masks():
    tc = WKV_CHUNK
    t = np.arange(tc)
    tri = np.stack([t[None, :] <= t[:, None], t[None, :] >= t[:, None]]).astype(np.float32)
    head = np.arange(HEAD_PAIR * tc) // tc
    tt = np.arange(HEAD_PAIR * tc) % tc
    same = head[:, None] == head[None, :]
    m = np.stack([np.stack([same & (tt[None, :] < tt[:, None]), same & (tt[None, :] <= tt[:, None])]),
                  np.stack([same & (tt[None, :] > tt[:, None]), same & (tt[None, :] >= tt[:, None])])])
    return jnp.asarray(tri, BF16), jnp.asarray(m.astype(np.float32)), jnp.asarray(np.eye(LANES, dtype=np.float32), BF16)


def _wkv_scan(ins):
    bsz, seq, _ = ins["nkk"].shape
    tc = WKV_CHUNK
    nc = seq // tc
    tri, msk, eye = _wkv_masks()
    nrows = 4 if bsz % 4 == 0 else (2 if bsz % 2 == 0 else 1)
    fwd = pl.BlockSpec((nrows, tc, RW), lambda c, b: (b, c, 0))
    bwd = pl.BlockSpec((nrows, tc, RW), lambda c, b: (b, nc - 1 - c, 0))
    full = lambda a: pl.BlockSpec(a.shape, lambda c, b, _n=a.ndim: (0,) * _n)
    f_args = [ins["nkk"], ins["r"], ins["v"], ins["lw0"], ins["b0"], ins["k0"]]
    b_args = [ins["nkk"], ins["r"], ins["v"], ins["lw1"], ins["b1"], ins["k1"]]
    o = jax.ShapeDtypeStruct((bsz, seq, RW), F32)
    return pl.pallas_call(
        functools.partial(_wkv_kernel, nrows=nrows),
        grid=(nc, bsz // nrows),
        in_specs=[fwd] * 6 + [bwd] * 6 + [full(tri), full(msk), full(eye)],
        out_specs=[fwd, bwd],
        out_shape=[o, o],
        scratch_shapes=[pltpu.VMEM((bsz, 2, WKV_PAIRS, LANES, LANES), F32)],
        compiler_params=_cparams(("arbitrary", "arbitrary")),
        name="wkv_scan",
    )(*f_args, *b_args, tri, msk, eye)


def _gelu_tanh(x):
    return 0.5 * x * (1.0 + jnp.tanh(math.sqrt(2.0 / math.pi) * (x + 0.044715 * (x * x * x))))


def _s5_kernel(ua_ref, ub_ref, bblk_ref, cblk_ref, lam_ref, lamc_ref, d_ref, glu_ref, o_ref,
               y_ref, st_ref, bu_ref, end_ref, carry_ref, *, seq, nb):
    ch = S5_CHUNK
    nc = seq // ch
    n = S5_FLAT
    u_halves = (ua_ref, ub_ref)
    for b in range(nb):
        for hf in range(2):
            y_ref[b, hf] = u_halves[hf][b] * d_ref[:, hf * LANES:(hf + 1) * LANES]

    def cmul_add(lre, lim, s, add):
        sre = s[:, :n]
        sim = s[:, n:]
        return jnp.concatenate([lre * sre - lim * sim + add[:, :n],
                                lre * sim + lim * sre + add[:, n:]], axis=1)

    for d in range(2):
        lre = lam_ref[d, 0:1, :]
        lim = lam_ref[d, 1:2, :]
        lcre = lamc_ref[d, 0:1, :]
        lcim = lamc_ref[d, 1:2, :]
        tloc = (lambda i: i) if d == 0 else (lambda i: ch - 1 - i)
        cloc = (lambda i: i) if d == 0 else (lambda i: nc - 1 - i)

        def drive(i):
            tl = tloc(jnp.minimum(i, ch - 1))
            rows = jnp.concatenate(
                [jnp.concatenate([r[b, pl.ds(tl, nc, stride=ch), :] for r in u_halves], axis=1)
                 for b in range(nb)], axis=0)
            return jnp.dot(rows.astype(BF16), bblk_ref[d], preferred_element_type=F32)

        def project(i, slot):
            bu_ref[slot] = drive(i)

        def advance(slot):
            st_ref[...] = cmul_add(lre, lim, st_ref[...], bu_ref[slot])

        def emit_from(st, i):
            yr = jnp.dot(st.astype(BF16), cblk_ref[d], preferred_element_type=F32)
            idx = pl.ds(tloc(i), nc, stride=ch)
            for b in range(nb):
                for hf in range(2):
                    y_ref[b, hf, idx, :] = (y_ref[b, hf, idx, :]
                                            + yr[b * nc:(b + 1) * nc, hf * LANES:(hf + 1) * LANES])

        def emit(i):
            emit_from(st_ref[...], i)

        st_ref[...] = jnp.zeros_like(st_ref)
        project(0, 0)

        def p1(j, c):
            nxt = drive(2 * j + 1)
            st = cmul_add(lre, lim, st_ref[...], bu_ref[0])
            st_ref[...] = cmul_add(lre, lim, st, nxt)
            project(2 * j + 2, 0)
            return c

        lax.fori_loop(0, ch // 2, p1, 0)
        end_ref[...] = st_ref[...]

        def cs(i, car):
            c = cloc(i)
            for b in range(nb):
                carry_ref[pl.ds(b * nc + c, 1), :] = car[b:b + 1]
            ends = jnp.concatenate([end_ref[pl.ds(b * nc + c, 1), :] for b in range(nb)], axis=0)
            return cmul_add(lcre, lcim, car, ends)

        lax.fori_loop(0, nc, cs, jnp.zeros((nb, 2 * n), F32))

        st_ref[...] = carry_ref[...]
        project(0, 0)
        project(1, 1)
        advance(0)

        def p2(j, c):
            i = 2 * j + 1
            nxt = drive(i + 1)
            st = st_ref[...]
            emit_from(st, i - 1)
            st = cmul_add(lre, lim, st, bu_ref[1])
            emit_from(st, i)
            st_ref[...] = cmul_add(lre, lim, st, nxt)
            project(i + 2, 1)
            return c

        lax.fori_loop(0, (ch - 2) // 2, p2, 0)
        emit(ch - 2)
        advance(1)
        emit(ch - 1)

    glu = glu_ref[...]
    for b in range(nb):
        g = _gelu_tanh(jnp.concatenate([y_ref[b, 0], y_ref[b, 1]], axis=1))
        o_ref[b] = (g * jax.nn.sigmoid(jnp.dot(g.astype(BF16), glu, preferred_element_type=F32))).astype(o_ref.dtype)


def _s5_params(lam_re, lam_im, log_dt, b_re, b_im, c_re, c_im):
    lre = lam_re.astype(F32)
    lim = lam_im.astype(F32)
    dt = jnp.exp(log_dt.astype(F32))[..., None]

    def cexp(scale):
        mag = jnp.exp(lre * dt * scale)
        return mag * jnp.cos(lim * dt * scale), mag * jnp.sin(lim * dt * scale)

    bar_re, bar_im = cexp(1.0)
    den = lre * lre + lim * lim
    f_re = ((bar_re - 1.0) * lre + bar_im * lim) / den
    f_im = (bar_im * lre - (bar_re - 1.0) * lim) / den
    bm_re = b_re.astype(F32)
    bm_im = b_im.astype(F32)
    bb_re = f_re[..., None] * bm_re - f_im[..., None] * bm_im
    bb_im = f_re[..., None] * bm_im + f_im[..., None] * bm_re
    eye_g = jnp.eye(S5_GROUPS, dtype=F32)

    def blockdiag_in(m):
        return jnp.einsum("dgph,gk->dghkp", m, eye_g).reshape(2, S5_WIDTH, S5_FLAT)

    def blockdiag_out(m):
        return jnp.einsum("dghp,gk->dgpkh", m, eye_g).reshape(2, S5_FLAT, S5_WIDTH)

    bblk = jnp.concatenate([blockdiag_in(bb_re), blockdiag_in(bb_im)], axis=2)
    cblk = jnp.concatenate([blockdiag_out(c_re.astype(F32)), -blockdiag_out(c_im.astype(F32))], axis=1)
    flat = lambda z: jnp.stack([z[0].reshape(2, S5_FLAT), z[1].reshape(2, S5_FLAT)], axis=1)
    return bblk.astype(BF16), cblk.astype(BF16), flat((bar_re, bar_im)), flat(cexp(float(S5_CHUNK)))


def _s5_mixer(u, bblk, cblk, lam, lamc, d_skip, glu_bf):
    bsz, seq, w = u.shape
    nc = seq // S5_CHUNK
    nb = 2 if bsz % 2 == 0 else 1
    full = lambda a: pl.BlockSpec(a.shape, lambda b, _n=a.ndim: (0,) * _n)
    args = [bblk, cblk, lam, lamc, d_skip, glu_bf]
    state = pltpu.VMEM((nb * nc, 2 * S5_FLAT), F32)
    return pl.pallas_call(
        functools.partial(_s5_kernel, seq=seq, nb=nb),
        grid=(bsz // nb,),
        in_specs=[pl.BlockSpec((nb, seq, LANES), lambda b: (b, 0, 0)),
                  pl.BlockSpec((nb, seq, LANES), lambda b: (b, 0, 1))] + [full(a) for a in args],
        out_specs=pl.BlockSpec((nb, seq, w), lambda b: (b, 0, 0)),
        out_shape=jax.ShapeDtypeStruct((bsz, seq, w), BF16),
        scratch_shapes=[pltpu.VMEM((nb, w // LANES, seq, LANES), F32),
                        state,
                        pltpu.VMEM((2, nb * nc, 2 * S5_FLAT), F32),
                        state,
                        state],
        compiler_params=_cparams(("arbitrary",)),
        name="s5_mixer",
    )(u, u, *args)


def _na_bias_table(rpb):
    q_col = np.arange(GRID_W)
    c_start = np.clip(q_col - NA_KW // 2, 0, GRID_W - NA_KW)
    k_col = np.arange(GRID_W)
    valid = (k_col[None, :] >= c_start[:, None]) & (k_col[None, :] < c_start[:, None] + NA_KW)
    dx = np.clip(k_col[None, :] - q_col[:, None] + NA_KW - 1, 0, 2 * NA_KW - 2)
    pick = (np.arange(2 * NA_KW - 1)[:, None, None] == dx[None]).astype(np.float32)
    base = jnp.einsum("hyd,dqk->hyqk", rpb.astype(F32), jnp.asarray(pick), precision=HIGHEST)
    base = jnp.where(jnp.asarray(valid)[None, None], base, -jnp.inf)
    tab = jnp.stack([base[:, NA_KH - 1 - o:2 * NA_KH - 1 - o] for o in range(NA_KH)], axis=1)
    tab = jnp.transpose(tab, (0, 1, 3, 2, 4))
    return tab.reshape(rpb.shape[0], NA_KH, GRID_W, NA_KH * GRID_W)


def _na_kernel(q_ref, k_ref, v_ref, bias_ref, o_ref, *, rows, rblk):
    rb = pl.program_id(1)
    lane = lax.broadcasted_iota(jnp.int32, (GRID_W, LANES), 1)
    low = lane < HEAD

    npair = NA_W // LANES
    every = lambda fn, *cols: [fn(*args) for args in zip(*cols)]

    def row_pair(jj, carry):
        js = [2 * jj, 2 * jj + 1]
        units = [(u, c) for u in range(2) for c in range(npair)]
        loaded = []
        for j in js:
            r = rb * rblk + j
            rs = jnp.clip(r - NA_KH // 2, 0, rows - NA_KH)
            loaded.append((q_ref[0, j],
                           k_ref[0, pl.ds(rs, NA_KH)].reshape(NA_KH * GRID_W, NA_W),
                           v_ref[0, pl.ds(rs, NA_KH)].reshape(NA_KH * GRID_W, NA_W),
                           r - rs))

        def scores(u, c):
            q, kmat, _, off = loaded[u]
            sl = slice(c * LANES, (c + 1) * LANES)
            q2 = q[:, sl].astype(F32)
            lhs = jnp.concatenate([jnp.where(low, q2, 0.0), jnp.where(low, 0.0, q2)], axis=0).astype(BF16)
            s = lax.dot_general(lhs, kmat[:, sl], (((1,), (1,)), ((), ())), preferred_element_type=F32)
            return s + jnp.concatenate([bias_ref[2 * c, off], bias_ref[2 * c + 1, off]], axis=0)

        s = every(scores, *zip(*units))
        m = every(lambda t: jnp.max(t, axis=-1, keepdims=True), s)
        p = every(lambda t, mx: jnp.exp(t - mx), s, m)
        l = every(lambda t: jnp.sum(t, axis=-1, keepdims=True), p)
        o = every(lambda t, den, uc: jnp.dot(t.astype(BF16), loaded[uc[0]][2][:, uc[1] * LANES:(uc[1] + 1) * LANES],
                                             preferred_element_type=F32) / den, p, l, units)
        o = every(lambda t: jnp.where(low, t[:GRID_W], t[GRID_W:]), o)
        for u, j in enumerate(js):
            o_ref[0, j] = jnp.concatenate(o[u * npair:(u + 1) * npair], axis=1).astype(o_ref.dtype)
        return carry

    lax.fori_loop(0, rblk // 2, row_pair, 0)


def _na_mixer(q, k, v, bias):
    bsz, seq, w = q.shape
    rows = seq // GRID_W
    rblk = 8
    g4 = lambda a: a.reshape(bsz, rows, GRID_W, w)
    img = pl.BlockSpec((1, rows, GRID_W, w), lambda b, i: (b, 0, 0, 0))
    blk = pl.BlockSpec((1, rblk, GRID_W, w), lambda b, i: (b, i, 0, 0))
    out = pl.pallas_call(
        functools.partial(_na_kernel, rows=rows, rblk=rblk),
        grid=(bsz, rows // rblk),
        in_specs=[blk, img, img, pl.BlockSpec(bias.shape, lambda b, i: (0, 0, 0, 0))],
        out_specs=blk,
        out_shape=jax.ShapeDtypeStruct((bsz, rows, GRID_W, w), BF16),
        compiler_params=_cparams(("arbitrary", "arbitrary")),
        name="na_mixer",
    )(g4(q), g4(k), g4(v), bias)
    return out.reshape(bsz, seq, w)


def _outproj_kernel(x_ref, s5_ref, yf_ref, yb_ref, gate_ref, bonus_ref, na_ref,
                    lnw_ref, lnb_ref, obm_ref, w_ref, gm_ref, g2_ref, sh_ref, sc_ref,
                    xo_ref, h_ref):
    segm = lambda t: _seg_sum(t, obm_ref[...])
    y = yf_ref[...] + yb_ref[...]
    yc = y - segm(y)
    yn = yc * lax.rsqrt(segm(yc * yc) + RWKV_GN_EPS) * lnw_ref[...] + lnb_ref[...]
    rw = (yn + bonus_ref[...]) * gate_ref[...]
    mixed = jnp.concatenate([s5_ref[...].astype(BF16), rw.astype(BF16), na_ref[...].astype(BF16)], axis=1)
    xo = x_ref[...] + gm_ref[0] * jnp.dot(mixed, w_ref[...], preferred_element_type=F32)
    xo_ref[...] = xo
    h_ref[...] = _rms_mod(xo, g2_ref[...], sh_ref[0], sc_ref[0])


def _out_proj(x2, s5o, yf, yb, gate, bonus, nao, lnw, lnb, obm, w_bf, gate_mix, g2, shift, scale, seq):
    t, d = x2.shape
    tm = TOKEN_TILE
    per_b = seq // tm
    row = lambda w: pl.BlockSpec((tm, w), lambda i: (i, 0))
    full = lambda a: pl.BlockSpec(a.shape, lambda i, _n=a.ndim: (0,) * _n)
    bvec = pl.BlockSpec((1, 1, d), lambda i: (i // per_b, 0, 0))
    o = jax.ShapeDtypeStruct((t, d), F32)
    return pl.pallas_call(
        _outproj_kernel,
        grid=(t // tm,),
        in_specs=[row(d), row(S5_WIDTH)] + [row(RW)] * 5 +
                 [full(lnw), full(lnb), full(obm), full(w_bf), bvec, full(g2), bvec, bvec],
        out_specs=[row(d), row(d)],
        out_shape=[o, o],
        compiler_params=_cparams(("arbitrary",)),
        name="out_proj",
    )(x2, s5o, yf, yb, gate, bonus, nao, lnw, lnb, obm, w_bf, gate_mix, g2, shift, scale)


MOE_TILE = 256
SEG_ALIGN = 8
MOE_SLOTS = -(-(MOE_TILE * TOP_K + N_EXPERTS * (SEG_ALIGN - 1)) // LANES) * LANES


def _router_kernel(h_ref, w_ref, b_ref, tri_ref, upper_ref, slot_ref, gate_ref, cnt_ref, base_ref, loc_ref,
                   carry_ref):
    @pl.when(pl.program_id(0) == 0)
    def _():
        carry_ref[...] = jnp.zeros_like(carry_ref)

    logits = jnp.dot(h_ref[...], w_ref[...], precision=HIGHEST, preferred_element_type=F32) + b_ref[...]
    tm = logits.shape[0]
    lane = lax.broadcasted_iota(jnp.int32, (tm, LANES), 1)
    lane_f = lane.astype(F32)
    vals, hots = [], []
    cur = logits
    for _ in range(TOP_K):
        m = jnp.max(cur, axis=-1, keepdims=True)
        idx = jnp.min(jnp.where(cur == m, lane_f, float(LANES)), axis=-1, keepdims=True)
        hot = lane_f == idx
        vals.append(m)
        hots.append(hot)
        cur = jnp.where(hot, -jnp.inf, cur)
    exps = [jnp.exp(v - vals[0]) for v in vals]
    den = exps[0] + exps[1] + exps[2] + exps[3]
    assign = sum(h.astype(F32) for h in hots)
    before = jnp.dot(tri_ref[...], assign.astype(BF16), preferred_element_type=F32)
    cnt = jnp.sum(assign, axis=0, keepdims=True)
    cnt = jnp.floor((cnt + (SEG_ALIGN - 1)) * (1.0 / SEG_ALIGN)) * SEG_ALIGN
    cnt8 = jnp.broadcast_to(cnt, (8, LANES)).astype(BF16)
    loc = jnp.dot(cnt8, upper_ref[...], preferred_element_type=F32)[0:1, :]
    place = before + loc
    s_out = jnp.zeros((tm, LANES), jnp.int32)
    g_out = jnp.zeros((tm, LANES), F32)
    for kk in range(TOP_K):
        slot = jnp.sum(jnp.where(hots[kk], place, 0.0), axis=-1, keepdims=True)
        sel = lane == kk
        s_out = jnp.where(sel, slot.astype(jnp.int32), s_out)
        g_out = jnp.where(sel, exps[kk] / den, g_out)
    slot_ref[...] = s_out
    gate_ref[...] = g_out
    cnt_ref[0] = cnt.astype(jnp.int32)
    base_ref[0] = carry_ref[...].astype(jnp.int32)
    loc_ref[0] = loc.astype(jnp.int32)
    carry_ref[...] = carry_ref[...] + cnt


def _router(h2, rw_pad, rb_pad):
    t, d = h2.shape
    tm = MOE_TILE
    nt = t // tm
    tri = jnp.asarray(np.tril(np.ones((tm, tm), np.float32), -1), BF16)
    upper = jnp.asarray(np.triu(np.ones((LANES, LANES), np.float32), 1), BF16)
    row = pl.BlockSpec((tm, LANES), lambda i: (i, 0))
    per_tile = pl.BlockSpec((1, 1, LANES), lambda i: (i, 0, 0))
    full = lambda a: pl.BlockSpec(a.shape, lambda i, _n=a.ndim: (0,) * _n)
    tile_i32 = jax.ShapeDtypeStruct((nt, 1, LANES), jnp.int32)
    return pl.pallas_call(
        _router_kernel,
        grid=(nt,),
        in_specs=[pl.BlockSpec((tm, d), lambda i: (i, 0)), full(rw_pad), full(rb_pad), full(tri), full(upper)],
        out_specs=[row, row, per_tile, per_tile, per_tile],
        out_shape=[jax.ShapeDtypeStruct((t, LANES), jnp.int32),
                   jax.ShapeDtypeStruct((t, LANES), F32),
                   tile_i32, tile_i32, tile_i32],
        scratch_shapes=[pltpu.VMEM((1, LANES), F32)],
        compiler_params=_cparams(("arbitrary",)),
        name="moe_router",
    )(h2, rw_pad, rb_pad, tri, upper)


def _segment_dmas(cnt_ref, loc_ref, row_ref, tile, ne, make_copy, wait):
    def per_expert(e, carry):
        i = tile * ne + e
        n = cnt_ref[i]

        @pl.when(n > 0)
        def _():
            cp = make_copy(pl.multiple_of(loc_ref[i], SEG_ALIGN), pl.multiple_of(row_ref[i], SEG_ALIGN),
                           pl.multiple_of(n, SEG_ALIGN))
            if wait:
                cp.wait()
            else:
                cp.start()

        return carry

    lax.fori_loop(0, ne, per_expert, 0)


def _dispatch_kernel(cnt_ref, loc_ref, row_ref, zcnt_ref, zoff_ref, zrow_ref, nu_ref,
                     slot_ref, h_ref, xb_ref, sorted_ref, zero_ref, sem, zsem, *, ne, nblk):
    i = pl.program_id(0)
    last = pl.num_programs(0) - 1
    cur = i % 2
    tm = h_ref.shape[0]
    ns = MOE_SLOTS

    def zero_copy(off, row, p):
        return pltpu.make_async_copy(zero_ref.at[pl.ds(off, p), :], xb_ref.at[pl.ds(row, p), :], zsem)

    def tail_copy(j):
        row = pl.multiple_of((nu_ref[0] + j) * MOE_BLOCK, MOE_BLOCK)
        return pltpu.make_async_copy(zero_ref, xb_ref.at[pl.ds(row, MOE_BLOCK), :], zsem)

    def zero_fill(wait):
        _segment_dmas(zcnt_ref, zoff_ref, zrow_ref, 0, ne, zero_copy, wait)

        def tail(j, c):
            if wait:
                tail_copy(j).wait()
            else:
                tail_copy(j).start()
            return c

        lax.fori_loop(0, nblk - nu_ref[0], tail, 0)

    @pl.when(i == 0)
    def _():
        zero_ref[...] = jnp.zeros_like(zero_ref)
        zero_fill(wait=False)

    slot_t = jnp.transpose(slot_ref[...].astype(F32))
    srow = lax.broadcasted_iota(jnp.int32, (ns, tm), 0).astype(F32)
    pick = jnp.zeros((ns, tm), F32)
    for kk in range(TOP_K):
        pick = jnp.where(srow == slot_t[kk:kk + 1, :], 1.0, pick)
    sorted_ref[cur] = jnp.dot(pick.astype(BF16), h_ref[...].astype(BF16), preferred_element_type=F32)

    def copy_from(buf):
        def copy(off, row, p):
            return pltpu.make_async_copy(sorted_ref.at[buf, pl.ds(off, p), :], xb_ref.at[pl.ds(row, p), :],
                                         sem.at[buf])
        return copy

    _segment_dmas(cnt_ref, loc_ref, row_ref, i, ne, copy_from(cur), wait=False)

    @pl.when(i > 0)
    def _():
        _segment_dmas(cnt_ref, loc_ref, row_ref, i - 1, ne, copy_from(1 - cur), wait=True)

    @pl.when(i == last)
    def _():
        _segment_dmas(cnt_ref, loc_ref, row_ref, i, ne, copy_from(cur), wait=True)
        zero_fill(wait=True)


def _dispatch(cnt, loc, rowstart, zcnt, zrow, n_used, slot, h2, n_rows, ne):
    t, d = h2.shape
    tm = MOE_TILE
    zoff = jnp.zeros_like(zcnt)
    grid_spec = pltpu.PrefetchScalarGridSpec(
        num_scalar_prefetch=7,
        grid=(t // tm,),
        in_specs=[pl.BlockSpec((tm, LANES), lambda i, *_: (i, 0)),
                  pl.BlockSpec((tm, d), lambda i, *_: (i, 0))],
        out_specs=pl.BlockSpec(memory_space=pl.ANY),
        scratch_shapes=[pltpu.VMEM((2, MOE_SLOTS, d), F32), pltpu.VMEM((MOE_BLOCK, d), F32),
                        pltpu.SemaphoreType.DMA((2,)), pltpu.SemaphoreType.DMA(())],
    )
    return pl.pallas_call(
        functools.partial(_dispatch_kernel, ne=ne, nblk=n_rows // MOE_BLOCK),
        grid_spec=grid_spec,
        out_shape=jax.ShapeDtypeStruct((n_rows, d), F32),
        compiler_params=_cparams(("arbitrary",)),
        name="moe_dispatch",
    )(cnt, loc, rowstart, zcnt, zoff, zrow, n_used, slot, h2)


PAIR_GROUP = 2 * LANES


def _pair_perm():
    p = np.zeros((PAIR_GROUP, PAIR_GROUP), np.float32)
    j = np.arange(LANES)
    p[2 * j, j] = 1.0
    p[2 * j + 1, LANES + j] = 1.0
    return jnp.asarray(p, BF16)


def _expert_kernel(be_ref, nu_ref, x_ref, w1_ref, b1_ref, w2_ref, b2_ref, perm_ref, y_ref, w1s_ref, w2s_ref):
    i = pl.program_id(0)
    f2 = w1_ref.shape[2]
    ngrp = f2 // PAIR_GROUP

    @pl.when(i >= nu_ref[0])
    def _():
        y_ref[...] = jnp.zeros_like(y_ref)

    @pl.when(i < nu_ref[0])
    def _():
        @pl.when((i == 0) | (be_ref[i] != be_ref[jnp.maximum(i - 1, 0)]))
        def _():
            for g in range(ngrp):
                sl = slice(g * PAIR_GROUP, (g + 1) * PAIR_GROUP)
                w1s_ref[:, sl] = jnp.dot(w1_ref[0, :, sl].astype(BF16), perm_ref[...],
                                         preferred_element_type=F32).astype(BF16)
            w2s_ref[...] = w2_ref[0].astype(BF16)

        hdn = jnp.dot(x_ref[...].astype(BF16), w1s_ref[...], preferred_element_type=F32) + b1_ref[0]
        glu = jnp.concatenate([hdn[:, g * PAIR_GROUP:g * PAIR_GROUP + LANES] for g in range(ngrp)], axis=1)
        lin = jnp.concatenate([hdn[:, g * PAIR_GROUP + LANES:(g + 1) * PAIR_GROUP] for g in range(ngrp)], axis=1)
        glu = jnp.minimum(glu, SWIGLU_LIMIT)
        lin = jnp.clip(lin, -SWIGLU_LIMIT, SWIGLU_LIMIT)
        act = glu * jax.nn.sigmoid(SWIGLU_ALPHA * glu) * (lin + 1.0)
        y_ref[...] = jnp.dot(act.astype(BF16), w2s_ref[...], preferred_element_type=F32) + b2_ref[0]


def _experts(block_e, n_used, xb, w1, b1_grp, w2, b2, layer):
    n_rows, d = xb.shape
    _, ne, _, f2 = w1.shape
    dff = w2.shape[2]
    nblk = n_rows // MOE_BLOCK
    perm = _pair_perm()
    blk = lambda i, be, nu: (jnp.minimum(i, nu[0] - 1), 0)
    wsel = lambda i, be, nu: (layer, be[i], 0, 0)
    grid_spec = pltpu.PrefetchScalarGridSpec(
        num_scalar_prefetch=2,
        grid=(nblk,),
        in_specs=[pl.BlockSpec((MOE_BLOCK, d), blk),
                  pl.BlockSpec((None, 1, d, f2), wsel),
                  pl.BlockSpec((None, 1, 1, f2), wsel),
                  pl.BlockSpec((None, 1, dff, d), wsel),
                  pl.BlockSpec((None, 1, 1, d), wsel),
                  pl.BlockSpec(perm.shape, lambda i, be, nu: (0, 0))],
        out_specs=pl.BlockSpec((MOE_BLOCK, d), lambda i, be, nu: (i, 0)),
        scratch_shapes=[pltpu.VMEM((d, f2), BF16), pltpu.VMEM((dff, d), BF16)],
    )
    nl = w1.shape[0]
    return pl.pallas_call(
        _expert_kernel,
        grid_spec=grid_spec,
        out_shape=jax.ShapeDtypeStruct((n_rows, d), F32),
        compiler_params=_cparams(("arbitrary",)),
        name="moe_experts",
    )(block_e, n_used, xb, w1, b1_grp.reshape(nl, ne, 1, f2), w2, b2.reshape(nl, ne, 1, d), perm)


def _combine_kernel(cnt_ref, loc_ref, row_ref, slot_ref, gates_ref, x_ref, gf_ref, yb_ref, o_ref, sorted_ref, sem,
                    *, ne):
    i = pl.program_id(0)
    last = pl.num_programs(0) - 1
    cur = i % 2
    tm = x_ref.shape[0]
    ns = MOE_SLOTS

    def copy_into(buf):
        def copy(off, row, p):
            return pltpu.make_async_copy(yb_ref.at[pl.ds(row, p), :], sorted_ref.at[buf, pl.ds(off, p), :],
                                         sem.at[buf])
        return copy

    @pl.when(i == 0)
    def _():
        sorted_ref[...] = jnp.zeros_like(sorted_ref)
        _segment_dmas(cnt_ref, loc_ref, row_ref, i, ne, copy_into(cur), wait=False)

    @pl.when(i < last)
    def _():
        _segment_dmas(cnt_ref, loc_ref, row_ref, i + 1, ne, copy_into(1 - cur), wait=False)

    slot = slot_ref[...]
    gates = gates_ref[...]
    scol = lax.broadcasted_iota(jnp.int32, (tm, ns), 1)
    gmat = jnp.zeros((tm, ns), F32)
    for kk in range(TOP_K):
        gmat = jnp.where(scol == slot[:, kk:kk + 1], gates[:, kk:kk + 1], gmat)
    _segment_dmas(cnt_ref, loc_ref, row_ref, i, ne, copy_into(cur), wait=True)
    acc = jnp.dot(gmat.astype(BF16), sorted_ref[cur].astype(BF16), preferred_element_type=F32)
    o_ref[...] = x_ref[...] + gf_ref[0] * acc


def _combine(cnt, loc, rowstart, slot, gates, x2, gate_ffn, yb, seq, ne):
    t, d = x2.shape
    tm = MOE_TILE
    per_b = seq // tm
    grid_spec = pltpu.PrefetchScalarGridSpec(
        num_scalar_prefetch=3,
        grid=(t // tm,),
        in_specs=[pl.BlockSpec((tm, LANES), lambda i, *_: (i, 0)),
                  pl.BlockSpec((tm, LANES), lambda i, *_: (i, 0)),
                  pl.BlockSpec((tm, d), lambda i, *_: (i, 0)),
                  pl.BlockSpec((1, 1, d), lambda i, *_: (i // per_b, 0, 0)),
                  pl.BlockSpec(memory_space=pl.ANY)],
        out_specs=pl.BlockSpec((tm, d), lambda i, *_: (i, 0)),
        scratch_shapes=[pltpu.VMEM((2, MOE_SLOTS, d), F32), pltpu.SemaphoreType.DMA((2,))],
    )
    return pl.pallas_call(
        functools.partial(_combine_kernel, ne=ne),
        grid_spec=grid_spec,
        out_shape=jax.ShapeDtypeStruct((t, d), F32),
        compiler_params=_cparams(("arbitrary",)),
        name="moe_combine",
    )(cnt, loc, rowstart, slot, gates, x2, gate_ffn, yb)


def _group_pairs(b1):
    lead = b1.shape[:-1]
    g = b1.reshape(lead + (-1, LANES, 2))
    return jnp.swapaxes(g, -1, -2).reshape(b1.shape)


def _moe_layer(x2, h2, gate_ffn, router_w, router_b, w1, b1_grp, w2, b2, seq, layer):
    t, d = x2.shape
    ne = router_w.shape[1]
    rw_pad = jnp.zeros((d, LANES), F32).at[:, :ne].set(router_w.astype(F32))
    rb_pad = jnp.full((1, LANES), -jnp.inf, F32).at[0, :ne].set(router_b.astype(F32))
    slot, gates, cnt3, base3, loc3 = _router(h2, rw_pad, rb_pad)
    n_assign = t * TOP_K
    n_tiles = t // MOE_TILE
    n_blocks = -(-(n_assign + n_tiles * ne * (SEG_ALIGN - 1)) // MOE_BLOCK) + ne
    cnt = cnt3[:, 0, :ne]
    base = base3[:, 0, :ne]
    loc = loc3[:, 0, :ne]
    total = base[-1] + cnt[-1]
    padded = ((total + MOE_BLOCK - 1) // MOE_BLOCK) * MOE_BLOCK
    pad_end = jnp.cumsum(padded)
    pad_start = pad_end - padded
    rowstart = (pad_start[None, :] + base).reshape(-1).astype(jnp.int32)
    n_used = (pad_end[-1] // MOE_BLOCK).astype(jnp.int32)
    block_start = jnp.minimum(jnp.arange(n_blocks, dtype=jnp.int32), n_used - 1) * MOE_BLOCK
    block_e = jnp.minimum(jnp.sum(block_start[:, None] >= pad_end[None, :], axis=-1), ne - 1).astype(jnp.int32)
    cnt_f = cnt.reshape(-1)
    loc_f = loc.reshape(-1)
    n_used = n_used.reshape(1)
    xb = _dispatch(cnt_f, loc_f, rowstart, (padded - total).astype(jnp.int32), (pad_start + total).astype(jnp.int32),
                   n_used, slot, h2, n_blocks * MOE_BLOCK, ne)
    yb = _experts(block_e, n_used, xb, w1, b1_grp, w2, b2, layer)
    return _combine(cnt_f, loc_f, rowstart, slot, gates, x2, gate_ffn, yb, seq, ne)


def kernel(x, c, ada_w, ada_b, norm1_g, norm2_g, w_in, w_out, s5_lam_re, s5_lam_im, s5_log_dt, s5_b_re, s5_b_im, s5_c_re, s5_c_im, s5_d, s5_glu_w, rwkv_mu, rwkv_w0, rwkv_w1, rwkv_w2, rwkv_a0, rwkv_a1, rwkv_a2, rwkv_g1, rwkv_g2, rwkv_k_k, rwkv_k_a, rwkv_r_k, rwkv_ln_w, rwkv_ln_b, na_q_g, na_k_g, na_rpb, router_w, router_b, exp_w1, exp_b1, exp_w2, exp_b2):
    bsz, seq, d = x.shape
    depth = ada_w.shape[0]
    t = bsz * seq
    mod = _ada_mod(c, ada_w, ada_b).reshape(depth, bsz, 6, 1, d)
    x2 = x.reshape(t, d)
    seg_ones = _seg_ones(RW, dtype=BF16)
    seg_mean = _seg_ones(RW, dtype=BF16, scale=1.0 / HEAD)
    row = lambda a: a.reshape(1, -1).astype(F32)
    b1_grp = _group_pairs(exp_b1.astype(F32))
    exp_b2f = exp_b2.astype(F32)
    for l in range(depth):
        m = lambda j: mod[l, :, j]
        s5u, xr, qkv = _in_proj(x2, row(norm1_g[l]), m(0), m(1), w_in[l].astype(BF16), seq)
        prep_params = dict(
            mu=row(rwkv_mu[l]), k_k=row(rwkv_k_k[l]), k_a=row(rwkv_k_a[l]), r_k=row(rwkv_r_k[l]),
            w0=rwkv_w0[l].astype(F32), a0=rwkv_a0[l].astype(F32),
            w1=rwkv_w1[l].astype(BF16), w2=rwkv_w2[l].astype(BF16),
            a1=rwkv_a1[l].astype(BF16), a2=rwkv_a2[l].astype(BF16),
            g1=rwkv_g1[l].astype(BF16), g2=rwkv_g2[l].astype(BF16),
            q_g=row(jnp.tile(na_q_g[l], NA_W // HEAD)), k_g=row(jnp.tile(na_k_g[l], NA_W // HEAD)), ob=seg_ones)
        (nkk, r, v, lw0, b0, k0, lw1, b1, k1, gate, bonus, naq, nak, nav) = _prep(
            xr.reshape(bsz, seq, 4 * RW), qkv.reshape(bsz, seq, 3 * NA_W), prep_params)
        yf, yb = _wkv_scan(dict(nkk=nkk, r=r, v=v, lw0=lw0, b0=b0, k0=k0, lw1=lw1, b1=b1, k1=k1))
        bblk, cblk, lam, lamc = _s5_params(s5_lam_re[l], s5_lam_im[l], s5_log_dt[l], s5_b_re[l], s5_b_im[l],
                                           s5_c_re[l], s5_c_im[l])
        s5o = _s5_mixer(s5u.reshape(bsz, seq, S5_WIDTH), bblk, cblk, lam, lamc, row(s5_d[l]),
                        s5_glu_w[l].astype(BF16))
        nao = _na_mixer(naq, nak, nav, _na_bias_table(na_rpb[l]))
        flat = lambda a: a.reshape(t, -1)
        x2, h2 = _out_proj(x2, flat(s5o), flat(yf), flat(yb), flat(gate), flat(bonus), flat(nao),
                           row(rwkv_ln_w[l]), row(rwkv_ln_b[l]), seg_mean, w_out[l].astype(BF16),
                           m(2), row(norm2_g[l]), m(3), m(4), seq)
        x2 = _moe_layer(x2, h2, m(5), router_w[l], router_b[l], exp_w1, b1_grp, exp_w2, exp_b2f, seq, l)
    return x2.reshape(bsz, seq, d)
```

```python
import functools
import math

import numpy as np
import jax
import jax.numpy as jnp
from jax import lax
from jax.experimental import pallas as pl
from jax.experimental.pallas import tpu as pltpu

F32 = jnp.float32
BF16 = jnp.bfloat16
HIGHEST = lax.Precision.HIGHEST

D_MODEL = 1024
S5_WIDTH = 256
S5_GROUP = 16
S5_GROUPS = 16
S5_STATE = 64
S5_CHUNK = 64
S5_FLAT = S5_GROUPS * S5_STATE
RW = 384
HEAD = 64
RWKV_GN_EPS = 64e-5
NA_W = 384
GRID_W = 64
NA_KH = 8
NA_KW = 16
N_EXPERTS = 32
TOP_K = 4
MOE_BLOCK = 256
SWIGLU_ALPHA = 1.702
SWIGLU_LIMIT = 7.0
RMS_EPS = 1e-6
LANES = 128
WKV_CHUNK = 64
TOKEN_TILE = 512
VMEM_LIMIT = 56 * 1024 * 1024


def _cparams(sem):
    return pltpu.CompilerParams(dimension_semantics=sem, vmem_limit_bytes=VMEM_LIMIT)


def _seg_ones(n, seg=HEAD, dtype=F32, scale=1.0):
    idx = np.arange(n) // seg
    return jnp.asarray((idx[:, None] == idx[None, :]).astype(np.float32) * scale, dtype)


def _seg_sum(t, ones_bf):
    hi = t.astype(BF16)
    lo = (t - hi.astype(F32)).astype(BF16)
    return (jnp.dot(hi, ones_bf, preferred_element_type=F32) + jnp.dot(lo, ones_bf, preferred_element_type=F32))


def _ada_kernel(c_ref, w_ref, b_ref, o_ref):
    c = c_ref[...]
    cond = c * jax.nn.sigmoid(c)
    o_ref[0] = jnp.dot(cond, w_ref[0], preferred_element_type=F32) + b_ref[0]


def _ada_mod(c, ada_w, ada_b):
    nl, d, n6 = ada_w.shape
    bsz = c.shape[0]
    tn = 1536
    return pl.pallas_call(
        _ada_kernel,
        grid=(nl, n6 // tn),
        in_specs=[pl.BlockSpec((bsz, d), lambda l, j: (0, 0)),
                  pl.BlockSpec((1, d, tn), lambda l, j: (l, 0, j)),
                  pl.BlockSpec((1, 1, tn), lambda l, j: (l, 0, j))],
        out_specs=pl.BlockSpec((1, bsz, tn), lambda l, j: (l, 0, j)),
        out_shape=jax.ShapeDtypeStruct((nl, bsz, n6), F32),
        compiler_params=_cparams(("arbitrary", "arbitrary")),
        name="ada_mod",
    )(c, ada_w, ada_b.reshape(nl, 1, n6))


def _rms_mod(x, g, shift, scale):
    ms = jnp.mean(x * x, axis=-1, keepdims=True)
    h = x * lax.rsqrt(ms + RMS_EPS) * g
    return h * (1.0 + scale) + shift


def _proj_kernel(x_ref, g_ref, sh_ref, sc_ref, w_ref, o_s5, o_rw, o_na):
    h = _rms_mod(x_ref[...], g_ref[...], sh_ref[0], sc_ref[0])
    p = jnp.dot(h.astype(BF16), w_ref[...], preferred_element_type=F32)
    o_s5[...] = p[:, :S5_WIDTH]
    o_rw[...] = p[:, S5_WIDTH:S5_WIDTH + 4 * RW]
    o_na[...] = p[:, S5_WIDTH + 4 * RW:]


def _in_proj(x2, g, shift, scale, w_bf, seq):
    t, d = x2.shape
    n = w_bf.shape[1]
    tm = TOKEN_TILE
    per_b = seq // tm
    row = lambda i: (i, 0)
    bvec = lambda i: (i // per_b, 0, 0)
    return pl.pallas_call(
        _proj_kernel,
        grid=(t // tm,),
        in_specs=[pl.BlockSpec((tm, d), row),
                  pl.BlockSpec((1, d), lambda i: (0, 0)),
                  pl.BlockSpec((1, 1, d), bvec),
                  pl.BlockSpec((1, 1, d), bvec),
                  pl.BlockSpec((d, n), lambda i: (0, 0))],
        out_specs=[pl.BlockSpec((tm, S5_WIDTH), row),
                   pl.BlockSpec((tm, 4 * RW), row),
                   pl.BlockSpec((tm, 3 * NA_W), row)],
        out_shape=[jax.ShapeDtypeStruct((t, S5_WIDTH), F32),
                   jax.ShapeDtypeStruct((t, 4 * RW), F32),
                   jax.ShapeDtypeStruct((t, 3 * NA_W), F32)],
        compiler_params=_cparams(("arbitrary",)),
        name="in_proj",
    )(x2, g, shift, scale, w_bf)


def _softplus(x):
    return jnp.maximum(x, 0.0) + jnp.log(1.0 + jnp.exp(-jnp.abs(x)))


def _prep_kernel(xr_ref, prev_ref, next_ref, qkv_ref,
                 mu_ref, kk_ref, ka_ref, rk_ref, w0_ref, a0_ref,
                 w1_ref, w2_ref, a1_ref, a2_ref, g1_ref, g2_ref, qg_ref, kg_ref, ob_ref,
                 nkk_o, r_o, v_o, lw0_o, b0_o, k0_o, lw1_o, b1_o, k1_o,
                 gate_o, bonus_o, naq_o, nak_o, nav_o):
    i = pl.program_id(1)
    nblk = pl.num_programs(1)
    x = xr_ref[0]
    tm = x.shape[0]
    prow = jnp.where(i == 0, 0.0, prev_ref[0][7:8, :])
    nrow = jnp.where(i == nblk - 1, 0.0, next_ref[0][0:1, :])
    rid = lax.broadcasted_iota(jnp.int32, x.shape, 0)
    prev = jnp.where(rid == 0, prow, pltpu.roll(x, 1, axis=0))
    nxt = jnp.where(rid == tm - 1, nrow, pltpu.roll(x, tm - 1, axis=0))
    xs = x + (0.5 * (prev + nxt) - x) * mu_ref[...]
    r = xs[:, 0:RW]
    k = xs[:, RW:2 * RW]
    v = xs[:, 2 * RW:3 * RW]
    z = xs[:, 3 * RW:4 * RW]
    seg = lambda t: _seg_sum(t, ob_ref[...])
    zb = z.astype(BF16)
    bdot = lambda a, w: jnp.dot(a.astype(BF16), w, preferred_element_type=F32)
    gate_o[0] = bdot(jax.nn.sigmoid(bdot(zb, g1_ref[...])), g2_ref[...])
    kk = k * kk_ref[...]
    kk = kk / jnp.maximum(jnp.sqrt(seg(kk * kk)), 1e-12)
    nkk_o[0] = -kk
    r_o[0] = r
    v_o[0] = v
    bonus_o[0] = seg(r * k * rk_ref[...]) * v
    outs = ((lw0_o, b0_o, k0_o), (lw1_o, b1_o, k1_o))
    for d in range(2):
        wl = w0_ref[d:d + 1, :] + bdot(jnp.tanh(bdot(zb, w1_ref[d])), w2_ref[d])
        w = -_softplus(-wl) - 0.5
        a = jax.nn.sigmoid(a0_ref[d:d + 1, :] + bdot(bdot(zb, a1_ref[d]), a2_ref[d]))
        lw_o, b_o, k_o = outs[d]
        lw_o[0] = -jnp.exp(w)
        b_o[0] = kk * a
        k_o[0] = k * (1.0 + (a - 1.0) * ka_ref[...])
    qkv = qkv_ref[0]
    segm = lambda t: seg(t) * (1.0 / HEAD)
    qn = qkv[:, 0:NA_W]
    kn = qkv[:, NA_W:2 * NA_W]
    naq_o[0] = (qn * lax.rsqrt(segm(qn * qn) + RMS_EPS) * qg_ref[...] * (HEAD ** -0.5)).astype(BF16)
    nak_o[0] = (kn * lax.rsqrt(segm(kn * kn) + RMS_EPS) * kg_ref[...]).astype(BF16)
    nav_o[0] = qkv[:, 2 * NA_W:].astype(BF16)


def _prep(xr, qkv, p):
    bsz, seq, _ = xr.shape
    tm = TOKEN_TILE
    nb = seq // tm
    h8 = tm // 8
    blk = lambda w: pl.BlockSpec((1, tm, w), lambda b, i: (b, i, 0))
    full = lambda a: pl.BlockSpec(a.shape, lambda b, i, _n=a.ndim: (0,) * _n)
    params = [p["mu"], p["k_k"], p["k_a"], p["r_k"], p["w0"], p["a0"], p["w1"], p["w2"], p["a1"], p["a2"],
              p["g1"], p["g2"], p["q_g"], p["k_g"], p["ob"]]
    f32o = jax.ShapeDtypeStruct((bsz, seq, RW), F32)
    bfo = jax.ShapeDtypeStruct((bsz, seq, NA_W), BF16)
    return pl.pallas_call(
        _prep_kernel,
        grid=(bsz, nb),
        in_specs=[blk(4 * RW),
                  pl.BlockSpec((1, 8, 4 * RW), lambda b, i: (b, jnp.maximum(i * h8 - 1, 0), 0)),
                  pl.BlockSpec((1, 8, 4 * RW), lambda b, i: (b, jnp.minimum((i + 1) * h8, seq // 8 - 1), 0)),
                  blk(3 * NA_W)] + [full(a) for a in params],
        out_specs=[blk(RW)] * 14,
        out_shape=[f32o] * 11 + [bfo] * 3,
        compiler_params=_cparams(("arbitrary", "arbitrary")),
        name="mixer_prep",
    )(xr, xr, xr, qkv, *params)


HEAD_PAIR = LANES // HEAD
WKV_PAIRS = RW // LANES
WKV_DOUBLINGS = WKV_CHUNK.bit_length() - 2


def _nt_dot(a, b):
    return lax.dot_general(a, b, (((1,), (1,)), ((), ())), preferred_element_type=F32)


def _wkv_kernel(*refs, nrows):
    f_in = refs[0:6]
    b_in = refs[6:12]
    tri_ref, msk_ref, eye_ref = refs[12:15]
    yf_ref, yb_ref = refs[15:17]
    s_ref = refs[17]
    c = pl.program_id(0)
    bi = pl.program_id(1)
    tc = WKV_CHUNK

    @pl.when(c == 0)
    def _():
        for row in range(nrows):
            s_ref[bi * nrows + row] = jnp.zeros(s_ref.shape[1:], F32)

    first_head = lax.broadcasted_iota(jnp.int32, (tc, LANES), 1) < HEAD
    eye_bf = eye_ref[...]
    eye_f = eye_bf.astype(F32)

    def blk(z):
        return jnp.concatenate([jnp.where(first_head, z, 0.0), jnp.where(first_head, 0.0, z)], axis=0)

    bdot = lambda p, q: jnp.dot(p, q, preferred_element_type=F32)
    units = [(row, d, p) for row in range(nrows) for d in range(2) for p in range(WKV_PAIRS)]
    every = lambda fn, *cols: [fn(*args) for args in zip(*cols)]
    states = [s_ref[bi * nrows + row, d, p] for row, d, p in units]
    masks = [(msk_ref[d, 0] > 0.5, msk_ref[d, 1] > 0.5) for d in range(2)]

    def load(row, d, p):
        src = f_in if d == 0 else b_in
        return [s[row, :, p * LANES:(p + 1) * LANES] for s in src]

    def decays(unit, data):
        d = unit[1]
        lw = data[3]
        cum = jnp.zeros_like(lw)
        rest = lw
        for _ in range(3):
            term = rest.astype(BF16)
            cum = cum + bdot(tri_ref[d], term)
            rest = rest - term.astype(F32)
        cend = cum[tc - 1:tc] if d == 0 else cum[0:1]
        return cum, cend

    def operands(data, dec):
        a, r, v, lw, bb, kk = data
        cum, cend = dec
        e_neg = jnp.exp(-cum)
        e_end = jnp.exp(cend - cum)
        x = jnp.concatenate([blk(a * jnp.exp(cum - lw)), blk(r * jnp.exp(cum))], axis=0).astype(BF16)
        y = jnp.concatenate([blk(bb * e_neg), blk(kk * e_neg)], axis=0).astype(BF16)
        z = jnp.concatenate([blk(bb * e_end), blk(kk * e_end)], axis=0).astype(BF16)
        return x, y, z, blk(v)

    def causal(unit, g):
        strict, incl = masks[unit[1]]
        return (jnp.where(strict, g[:2 * tc, :2 * tc], 0.0), jnp.where(strict, g[:2 * tc, 2 * tc:], 0.0),
                jnp.concatenate([jnp.where(incl, g[2 * tc:, :2 * tc], 0.0),
                                 jnp.where(incl, g[2 * tc:, 2 * tc:], 0.0)], axis=1).astype(BF16))

    data = every(load, *zip(*units))
    dec = every(decays, units, data)
    ops = every(operands, data, dec)
    grams = every(lambda o: _nt_dot(o[0], o[1]), ops)
    nmat = every(causal, units, grams)
    ph = every(lambda o, st: _nt_dot(o[0], st.astype(BF16)), ops, states)
    vbf = every(lambda o: o[3].astype(BF16), ops)
    rhs = every(lambda q, n, vb: q[:2 * tc] + bdot(n[1].astype(BF16), vb), ph, nmat, vbf)
    inv = every(lambda n: eye_f + n[0], nmat)
    pw = every(lambda n: n[0].astype(BF16), nmat)
    for _ in range(WKV_DOUBLINGS):
        pw = every(lambda q: bdot(q, q).astype(BF16), pw)
        inv = every(lambda t, q: t + bdot(t.astype(BF16), q), inv, pw)
    u = every(lambda t, q: bdot(t.astype(BF16), q.astype(BF16)), inv, rhs)
    uv = every(lambda q, vb: jnp.concatenate([q.astype(BF16), vb], axis=0), u, vbf)
    yo = every(lambda q, n, w: q[2 * tc:] + bdot(n[2], w), ph, nmat, uv)
    uvt = every(lambda q, o: jnp.transpose(jnp.concatenate([q, o[3]], axis=0)).astype(BF16), u, ops)
    new = every(lambda st, dc, w, o: st * jnp.exp(dc[1]) + bdot(w, o[2]), states, dec, uvt, ops)
    for row in range(nrows):
        for d in range(2):
            out = yf_ref if d == 0 else yb_ref
            parts = [yo[units.index((row, d, p))] for p in range(WKV_PAIRS)]
            out[row] = jnp.concatenate([q[:tc] + q[tc:] for q in parts], axis=1)
    for (row, d, p), st in zip(units, new):
        s_ref[bi * nrows + row, d, p] = st


def _wkv_masks():
    tc = WKV_CHUNK
    t = np.arange(tc)
    tri = np.stack([t[None, :] <= t[:, None], t[None, :] >= t[:, None]]).astype(np.float32)
    head = np.arange(HEAD_PAIR * tc) // tc
    tt = np.arange(HEAD_PAIR * tc) % tc
    same = head[:, None] == head[None, :]
    m = np.stack([np.stack([same & (tt[None, :] < tt[:, None]), same & (tt[None, :] <= tt[:, None])]),
                  np.stack([same & (tt[None, :] > tt[:, None]), same & (tt[None, :] >= tt[:, None])])])
    return jnp.asarray(tri, BF16), jnp.asarray(m.astype(np.float32)), jnp.asarray(np.eye(LANES, dtype=np.float32), BF16)


def _wkv_scan(ins):
    bsz, seq, _ = ins["nkk"].shape
    tc = WKV_CHUNK
    nc = seq // tc
    tri, msk, eye = _wkv_masks()
    nrows = 4 if bsz % 4 == 0 else (2 if bsz % 2 == 0 else 1)
    fwd = pl.BlockSpec((nrows, tc, RW), lambda c, b: (b, c, 0))
    bwd = pl.BlockSpec((nrows, tc, RW), lambda c, b: (b, nc - 1 - c, 0))
    full = lambda a: pl.BlockSpec(a.shape, lambda c, b, _n=a.ndim: (0,) * _n)
    f_args = [ins["nkk"], ins["r"], ins["v"], ins["lw0"], ins["b0"], ins["k0"]]
    b_args = [ins["nkk"], ins["r"], ins["v"], ins["lw1"], ins["b1"], ins["k1"]]
    o = jax.ShapeDtypeStruct((bsz, seq, RW), F32)
    return pl.pallas_call(
        functools.partial(_wkv_kernel, nrows=nrows),
        grid=(nc, bsz // nrows),
        in_specs=[fwd] * 6 + [bwd] * 6 + [full(tri), full(msk), full(eye)],
        out_specs=[fwd, bwd],
        out_shape=[o, o],
        scratch_shapes=[pltpu.VMEM((bsz, 2, WKV_PAIRS, LANES, LANES), F32)],
        compiler_params=_cparams(("arbitrary", "arbitrary")),
        name="wkv_scan",
    )(*f_args, *b_args, tri, msk, eye)


def _gelu_tanh(x):
    return 0.5 * x * (1.0 + jnp.tanh(math.sqrt(2.0 / math.pi) * (x + 0.044715 * (x * x * x))))


def _s5_kernel(ua_ref, ub_ref, bblk_ref, cblk_ref, lam_ref, lamc_ref, d_ref, glu_ref, o_ref,
               y_ref, st_ref, bu_ref, end_ref, carry_ref, *, seq, nb):
    ch = S5_CHUNK
    nc = seq // ch
    n = S5_FLAT
    u_halves = (ua_ref, ub_ref)
    for b in range(nb):
        for hf in range(2):
            y_ref[b, hf] = u_halves[hf][b] * d_ref[:, hf * LANES:(hf + 1) * LANES]

    def cmul_add(lre, lim, s, add):
        sre = s[:, :n]
        sim = s[:, n:]
        return jnp.concatenate([lre * sre - lim * sim + add[:, :n],
                                lre * sim + lim * sre + add[:, n:]], axis=1)

    for d in range(2):
        lre = lam_ref[d, 0:1, :]
        lim = lam_ref[d, 1:2, :]
        lcre = lamc_ref[d, 0:1, :]
        lcim = lamc_ref[d, 1:2, :]
        tloc = (lambda i: i) if d == 0 else (lambda i: ch - 1 - i)
        cloc = (lambda i: i) if d == 0 else (lambda i: nc - 1 - i)

        def drive(i):
            tl = tloc(jnp.minimum(i, ch - 1))
            rows = jnp.concatenate(
                [jnp.concatenate([r[b, pl.ds(tl, nc, stride=ch), :] for r in u_halves], axis=1)
                 for b in range(nb)], axis=0)
            return jnp.dot(rows.astype(BF16), bblk_ref[d], preferred_element_type=F32)

        def project(i, slot):
            bu_ref[slot] = drive(i)

        def advance(slot):
            st_ref[...] = cmul_add(lre, lim, st_ref[...], bu_ref[slot])

        def emit_from(st, i):
            yr = jnp.dot(st.astype(BF16), cblk_ref[d], preferred_element_type=F32)
            idx = pl.ds(tloc(i), nc, stride=ch)
            for b in range(nb):
                for hf in range(2):
                    y_ref[b, hf, idx, :] = (y_ref[b, hf, idx, :]
                                            + yr[b * nc:(b + 1) * nc, hf * LANES:(hf + 1) * LANES])

        def emit(i):
            emit_from(st_ref[...], i)

        st_ref[...] = jnp.zeros_like(st_ref)
        project(0, 0)

        def p1(j, c):
            nxt = drive(2 * j + 1)
            st = cmul_add(lre, lim, st_ref[...], bu_ref[0])
            st_ref[...] = cmul_add(lre, lim, st, nxt)
            project(2 * j + 2, 0)
            return c

        lax.fori_loop(0, ch // 2, p1, 0)
        end_ref[...] = st_ref[...]

        def cs(i, car):
            c = cloc(i)
            for b in range(nb):
                carry_ref[pl.ds(b * nc + c, 1), :] = car[b:b + 1]
            ends = jnp.concatenate([end_ref[pl.ds(b * nc + c, 1), :] for b in range(nb)], axis=0)
            return cmul_add(lcre, lcim, car, ends)

        lax.fori_loop(0, nc, cs, jnp.zeros((nb, 2 * n), F32))

        st_ref[...] = carry_ref[...]
        project(0, 0)
        project(1, 1)
        advance(0)

        def p2(j, c):
            i = 2 * j + 1
            nxt = drive(i + 1)
            st = st_ref[...]
            emit_from(st, i - 1)
            st = cmul_add(lre, lim, st, bu_ref[1])
            emit_from(st, i)
            st_ref[...] = cmul_add(lre, lim, st, nxt)
            project(i + 2, 1)
            return c

        lax.fori_loop(0, (ch - 2) // 2, p2, 0)
        emit(ch - 2)
        advance(1)
        emit(ch - 1)

    glu = glu_ref[...]
    for b in range(nb):
        g = _gelu_tanh(jnp.concatenate([y_ref[b, 0], y_ref[b, 1]], axis=1))
        o_ref[b] = (g * jax.nn.sigmoid(jnp.dot(g.astype(BF16), glu, preferred_element_type=F32))).astype(o_ref.dtype)


def _s5_params(lam_re, lam_im, log_dt, b_re, b_im, c_re, c_im):
    lre = lam_re.astype(F32)
    lim = lam_im.astype(F32)
    dt = jnp.exp(log_dt.astype(F32))[..., None]

    def cexp(scale):
        mag = jnp.exp(lre * dt * scale)
        return mag * jnp.cos(lim * dt * scale), mag * jnp.sin(lim * dt * scale)

    bar_re, bar_im = cexp(1.0)
    den = lre * lre + lim * lim
    f_re = ((bar_re - 1.0) * lre + bar_im * lim) / den
    f_im = (bar_im * lre - (bar_re - 1.0) * lim) / den
    bm_re = b_re.astype(F32)
    bm_im = b_im.astype(F32)
    bb_re = f_re[..., None] * bm_re - f_im[..., None] * bm_im
    bb_im = f_re[..., None] * bm_im + f_im[..., None] * bm_re
    eye_g = jnp.eye(S5_GROUPS, dtype=F32)

    def blockdiag_in(m):
        return jnp.einsum("dgph,gk->dghkp", m, eye_g).reshape(2, S5_WIDTH, S5_FLAT)

    def blockdiag_out(m):
        return jnp.einsum("dghp,gk->dgpkh", m, eye_g).reshape(2, S5_FLAT, S5_WIDTH)

    bblk = jnp.concatenate([blockdiag_in(bb_re), blockdiag_in(bb_im)], axis=2)
    cblk = jnp.concatenate([blockdiag_out(c_re.astype(F32)), -blockdiag_out(c_im.astype(F32))], axis=1)
    flat = lambda z: jnp.stack([z[0].reshape(2, S5_FLAT), z[1].reshape(2, S5_FLAT)], axis=1)
    return bblk.astype(BF16), cblk.astype(BF16), flat((bar_re, bar_im)), flat(cexp(float(S5_CHUNK)))


def _s5_mixer(u, bblk, cblk, lam, lamc, d_skip, glu_bf):
    bsz, seq, w = u.shape
    nc = seq // S5_CHUNK
    nb = 2 if bsz % 2 == 0 else 1
    full = lambda a: pl.BlockSpec(a.shape, lambda b, _n=a.ndim: (0,) * _n)
    args = [bblk, cblk, lam, lamc, d_skip, glu_bf]
    state = pltpu.VMEM((nb * nc, 2 * S5_FLAT), F32)
    return pl.pallas_call(
        functools.partial(_s5_kernel, seq=seq, nb=nb),
        grid=(bsz // nb,),
        in_specs=[pl.BlockSpec((nb, seq, LANES), lambda b: (b, 0, 0)),
                  pl.BlockSpec((nb, seq, LANES), lambda b: (b, 0, 1))] + [full(a) for a in args],
        out_specs=pl.BlockSpec((nb, seq, w), lambda b: (b, 0, 0)),
        out_shape=jax.ShapeDtypeStruct((bsz, seq, w), BF16),
        scratch_shapes=[pltpu.VMEM((nb, w // LANES, seq, LANES), F32),
                        state,
                        pltpu.VMEM((2, nb * nc, 2 * S5_FLAT), F32),
                        state,
                        state],
        compiler_params=_cparams(("arbitrary",)),
        name="s5_mixer",
    )(u, u, *args)


def _na_bias_table(rpb):
    q_col = np.arange(GRID_W)
    c_start = np.clip(q_col - NA_KW // 2, 0, GRID_W - NA_KW)
    k_col = np.arange(GRID_W)
    valid = (k_col[None, :] >= c_start[:, None]) & (k_col[None, :] < c_start[:, None] + NA_KW)
    dx = np.clip(k_col[None, :] - q_col[:, None] + NA_KW - 1, 0, 2 * NA_KW - 2)
    pick = (np.arange(2 * NA_KW - 1)[:, None, None] == dx[None]).astype(np.float32)
    base = jnp.einsum("hyd,dqk->hyqk", rpb.astype(F32), jnp.asarray(pick), precision=HIGHEST)
    base = jnp.where(jnp.asarray(valid)[None, None], base, -jnp.inf)
    tab = jnp.stack([base[:, NA_KH - 1 - o:2 * NA_KH - 1 - o] for o in range(NA_KH)], axis=1)
    tab = jnp.transpose(tab, (0, 1, 3, 2, 4))
    return tab.reshape(rpb.shape[0], NA_KH, GRID_W, NA_KH * GRID_W)


def _na_kernel(q_ref, k_ref, v_ref, bias_ref, o_ref, *, rows, rblk):
    rb = pl.program_id(1)
    lane = lax.broadcasted_iota(jnp.int32, (GRID_W, LANES), 1)
    low = lane < HEAD

    npair = NA_W // LANES
    every = lambda fn, *cols: [fn(*args) for args in zip(*cols)]

    def row_pair(jj, carry):
        js = [2 * jj, 2 * jj + 1]
        units = [(u, c) for u in range(2) for c in range(npair)]
        loaded = []
        for j in js:
            r = rb * rblk + j
            rs = jnp.clip(r - NA_KH // 2, 0, rows - NA_KH)
            loaded.append((q_ref[0, j],
                           k_ref[0, pl.ds(rs, NA_KH)].reshape(NA_KH * GRID_W, NA_W),
                           v_ref[0, pl.ds(rs, NA_KH)].reshape(NA_KH * GRID_W, NA_W),
                           r - rs))

        def scores(u, c):
            q, kmat, _, off = loaded[u]
            sl = slice(c * LANES, (c + 1) * LANES)
            q2 = q[:, sl].astype(F32)
            lhs = jnp.concatenate([jnp.where(low, q2, 0.0), jnp.where(low, 0.0, q2)], axis=0).astype(BF16)
            s = lax.dot_general(lhs, kmat[:, sl], (((1,), (1,)), ((), ())), preferred_element_type=F32)
            return s + jnp.concatenate([bias_ref[2 * c, off], bias_ref[2 * c + 1, off]], axis=0)

        s = every(scores, *zip(*units))
        m = every(lambda t: jnp.max(t, axis=-1, keepdims=True), s)
        p = every(lambda t, mx: jnp.exp(t - mx), s, m)
        l = every(lambda t: jnp.sum(t, axis=-1, keepdims=True), p)
        o = every(lambda t, den, uc: jnp.dot(t.astype(BF16), loaded[uc[0]][2][:, uc[1] * LANES:(uc[1] + 1) * LANES],
                                             preferred_element_type=F32) / den, p, l, units)
        o = every(lambda t: jnp.where(low, t[:GRID_W], t[GRID_W:]), o)
        for u, j in enumerate(js):
            o_ref[0, j] = jnp.concatenate(o[u * npair:(u + 1) * npair], axis=1).astype(o_ref.dtype)
        return carry

    lax.fori_loop(0, rblk // 2, row_pair, 0)


def _na_mixer(q, k, v, bias):
    bsz, seq, w = q.shape
    rows = seq // GRID_W
    rblk = 8
    g4 = lambda a: a.reshape(bsz, rows, GRID_W, w)
    img = pl.BlockSpec((1, rows, GRID_W, w), lambda b, i: (b, 0, 0, 0))
    blk = pl.BlockSpec((1, rblk, GRID_W, w), lambda b, i: (b, i, 0, 0))
    out = pl.pallas_call(
        functools.partial(_na_kernel, rows=rows, rblk=rblk),
        grid=(bsz, rows // rblk),
        in_specs=[blk, img, img, pl.BlockSpec(bias.shape, lambda b, i: (0, 0, 0, 0))],
        out_specs=blk,
        out_shape=jax.ShapeDtypeStruct((bsz, rows, GRID_W, w), BF16),
        compiler_params=_cparams(("arbitrary", "arbitrary")),
        name="na_mixer",
    )(g4(q), g4(k), g4(v), bias)
    return out.reshape(bsz, seq, w)


def _outproj_kernel(x_ref, s5_ref, yf_ref, yb_ref, gate_ref, bonus_ref, na_ref,
                    lnw_ref, lnb_ref, obm_ref, w_ref, gm_ref, g2_ref, sh_ref, sc_ref,
                    xo_ref, h_ref):
    segm = lambda t: _seg_sum(t, obm_ref[...])
    y = yf_ref[...] + yb_ref[...]
    yc = y - segm(y)
    yn = yc * lax.rsqrt(segm(yc * yc) + RWKV_GN_EPS) * lnw_ref[...] + lnb_ref[...]
    rw = (yn + bonus_ref[...]) * gate_ref[...]
    mixed = jnp.concatenate([s5_ref[...].astype(BF16), rw.astype(BF16), na_ref[...].astype(BF16)], axis=1)
    xo = x_ref[...] + gm_ref[0] * jnp.dot(mixed, w_ref[...], preferred_element_type=F32)
    xo_ref[...] = xo
    h_ref[...] = _rms_mod(xo, g2_ref[...], sh_ref[0], sc_ref[0])


def _out_proj(x2, s5o, yf, yb, gate, bonus, nao, lnw, lnb, obm, w_bf, gate_mix, g2, shift, scale, seq):
    t, d = x2.shape
    tm = TOKEN_TILE
    per_b = seq // tm
    row = lambda w: pl.BlockSpec((tm, w), lambda i: (i, 0))
    full = lambda a: pl.BlockSpec(a.shape, lambda i, _n=a.ndim: (0,) * _n)
    bvec = pl.BlockSpec((1, 1, d), lambda i: (i // per_b, 0, 0))
    o = jax.ShapeDtypeStruct((t, d), F32)
    return pl.pallas_call(
        _outproj_kernel,
        grid=(t // tm,),
        in_specs=[row(d), row(S5_WIDTH)] + [row(RW)] * 5 +
                 [full(lnw), full(lnb), full(obm), full(w_bf), bvec, full(g2), bvec, bvec],
        out_specs=[row(d), row(d)],
        out_shape=[o, o],
        compiler_params=_cparams(("arbitrary",)),
        name="out_proj",
    )(x2, s5o, yf, yb, gate, bonus, nao, lnw, lnb, obm, w_bf, gate_mix, g2, shift, scale)


MOE_TILE = 256
SEG_ALIGN = 8
MOE_SLOTS = -(-(MOE_TILE * TOP_K + N_EXPERTS * (SEG_ALIGN - 1)) // LANES) * LANES


def _router_kernel(h_ref, w_ref, b_ref, tri_ref, upper_ref, slot_ref, gate_ref, cnt_ref, base_ref, loc_ref,
                   carry_ref):
    @pl.when(pl.program_id(0) == 0)
    def _():
        carry_ref[...] = jnp.zeros_like(carry_ref)

    logits = jnp.dot(h_ref[...], w_ref[...], precision=HIGHEST, preferred_element_type=F32) + b_ref[...]
    tm = logits.shape[0]
    lane = lax.broadcasted_iota(jnp.int32, (tm, LANES), 1)
    lane_f = lane.astype(F32)
    vals, hots = [], []
    cur = logits
    for _ in range(TOP_K):
        m = jnp.max(cur, axis=-1, keepdims=True)
        idx = jnp.min(jnp.where(cur == m, lane_f, float(LANES)), axis=-1, keepdims=True)
        hot = lane_f == idx
        vals.append(m)
        hots.append(hot)
        cur = jnp.where(hot, -jnp.inf, cur)
    exps = [jnp.exp(v - vals[0]) for v in vals]
    den = exps[0] + exps[1] + exps[2] + exps[3]
    assign = sum(h.astype(F32) for h in hots)
    before = jnp.dot(tri_ref[...], assign.astype(BF16), preferred_element_type=F32)
    cnt = jnp.sum(assign, axis=0, keepdims=True)
    cnt = jnp.floor((cnt + (SEG_ALIGN - 1)) * (1.0 / SEG_ALIGN)) * SEG_ALIGN
    cnt8 = jnp.broadcast_to(cnt, (8, LANES)).astype(BF16)
    loc = jnp.dot(cnt8, upper_ref[...], preferred_element_type=F32)[0:1, :]
    place = before + loc
    s_out = jnp.zeros((tm, LANES), jnp.int32)
    g_out = jnp.zeros((tm, LANES), F32)
    for kk in range(TOP_K):
        slot = jnp.sum(jnp.where(hots[kk], place, 0.0), axis=-1, keepdims=True)
        sel = lane == kk
        s_out = jnp.where(sel, slot.astype(jnp.int32), s_out)
        g_out = jnp.where(sel, exps[kk] / den, g_out)
    slot_ref[...] = s_out
    gate_ref[...] = g_out
    cnt_ref[0] = cnt.astype(jnp.int32)
    base_ref[0] = carry_ref[...].astype(jnp.int32)
    loc_ref[0] = loc.astype(jnp.int32)
    carry_ref[...] = carry_ref[...] + cnt


def _router(h2, rw_pad, rb_pad):
    t, d = h2.shape
    tm = MOE_TILE
    nt = t // tm
    tri = jnp.asarray(np.tril(np.ones((tm, tm), np.float32), -1), BF16)
    upper = jnp.asarray(np.triu(np.ones((LANES, LANES), np.float32), 1), BF16)
    row = pl.BlockSpec((tm, LANES), lambda i: (i, 0))
    per_tile = pl.BlockSpec((1, 1, LANES), lambda i: (i, 0, 0))
    full = lambda a: pl.BlockSpec(a.shape, lambda i, _n=a.ndim: (0,) * _n)
    tile_i32 = jax.ShapeDtypeStruct((nt, 1, LANES), jnp.int32)
    return pl.pallas_call(
        _router_kernel,
        grid=(nt,),
        in_specs=[pl.BlockSpec((tm, d), lambda i: (i, 0)), full(rw_pad), full(rb_pad), full(tri), full(upper)],
        out_specs=[row, row, per_tile, per_tile, per_tile],
        out_shape=[jax.ShapeDtypeStruct((t, LANES), jnp.int32),
                   jax.ShapeDtypeStruct((t, LANES), F32),
                   tile_i32, tile_i32, tile_i32],
        scratch_shapes=[pltpu.VMEM((1, LANES), F32)],
        compiler_params=_cparams(("arbitrary",)),
        name="moe_router",
    )(h2, rw_pad, rb_pad, tri, upper)


def _segment_dmas(cnt_ref, loc_ref, row_ref, tile, ne, make_copy, wait):
    def per_expert(e, carry):
        i = tile * ne + e
        n = cnt_ref[i]

        @pl.when(n > 0)
        def _():
            cp = make_copy(pl.multiple_of(loc_ref[i], SEG_ALIGN), pl.multiple_of(row_ref[i], SEG_ALIGN),
                           pl.multiple_of(n, SEG_ALIGN))
            if wait:
                cp.wait()
            else:
                cp.start()

        return carry

    lax.fori_loop(0, ne, per_expert, 0)


def _dispatch_kernel(cnt_ref, loc_ref, row_ref, zcnt_ref, zoff_ref, zrow_ref, nu_ref,
                     slot_ref, h_ref, xb_ref, sorted_ref, zero_ref, sem, zsem, *, ne, nblk):
    i = pl.program_id(0)
    last = pl.num_programs(0) - 1
    cur = i % 2
    tm = h_ref.shape[0]
    ns = MOE_SLOTS

    def zero_copy(off, row, p):
        return pltpu.make_async_copy(zero_ref.at[pl.ds(off, p), :], xb_ref.at[pl.ds(row, p), :], zsem)

    def tail_copy(j):
        row = pl.multiple_of((nu_ref[0] + j) * MOE_BLOCK, MOE_BLOCK)
        return pltpu.make_async_copy(zero_ref, xb_ref.at[pl.ds(row, MOE_BLOCK), :], zsem)

    def zero_fill(wait):
        _segment_dmas(zcnt_ref, zoff_ref, zrow_ref, 0, ne, zero_copy, wait)

        def tail(j, c):
            if wait:
                tail_copy(j).wait()
            else:
                tail_copy(j).start()
            return c

        lax.fori_loop(0, nblk - nu_ref[0], tail, 0)

    @pl.when(i == 0)
    def _():
        zero_ref[...] = jnp.zeros_like(zero_ref)
        zero_fill(wait=False)

    slot_t = jnp.transpose(slot_ref[...].astype(F32))
    srow = lax.broadcasted_iota(jnp.int32, (ns, tm), 0).astype(F32)
    pick = jnp.zeros((ns, tm), F32)
    for kk in range(TOP_K):
        pick = jnp.where(srow == slot_t[kk:kk + 1, :], 1.0, pick)
    sorted_ref[cur] = jnp.dot(pick.astype(BF16), h_ref[...].astype(BF16), preferred_element_type=F32)

    def copy_from(buf):
        def copy(off, row, p):
            return pltpu.make_async_copy(sorted_ref.at[buf, pl.ds(off, p), :], xb_ref.at[pl.ds(row, p), :],
                                         sem.at[buf])
        return copy

    _segment_dmas(cnt_ref, loc_ref, row_ref, i, ne, copy_from(cur), wait=False)

    @pl.when(i > 0)
    def _():
        _segment_dmas(cnt_ref, loc_ref, row_ref, i - 1, ne, copy_from(1 - cur), wait=True)

    @pl.when(i == last)
    def _():
        _segment_dmas(cnt_ref, loc_ref, row_ref, i, ne, copy_from(cur), wait=True)
        zero_fill(wait=True)


def _dispatch(cnt, loc, rowstart, zcnt, zrow, n_used, slot, h2, n_rows, ne):
    t, d = h2.shape
    tm = MOE_TILE
    zoff = jnp.zeros_like(zcnt)
    grid_spec = pltpu.PrefetchScalarGridSpec(
        num_scalar_prefetch=7,
        grid=(t // tm,),
        in_specs=[pl.BlockSpec((tm, LANES), lambda i, *_: (i, 0)),
                  pl.BlockSpec((tm, d), lambda i, *_: (i, 0))],
        out_specs=pl.BlockSpec(memory_space=pl.ANY),
        scratch_shapes=[pltpu.VMEM((2, MOE_SLOTS, d), F32), pltpu.VMEM((MOE_BLOCK, d), F32),
                        pltpu.SemaphoreType.DMA((2,)), pltpu.SemaphoreType.DMA(())],
    )
    return pl.pallas_call(
        functools.partial(_dispatch_kernel, ne=ne, nblk=n_rows // MOE_BLOCK),
        grid_spec=grid_spec,
        out_shape=jax.ShapeDtypeStruct((n_rows, d), F32),
        compiler_params=_cparams(("arbitrary",)),
        name="moe_dispatch",
    )(cnt, loc, rowstart, zcnt, zoff, zrow, n_used, slot, h2)


PAIR_GROUP = 2 * LANES


def _pair_perm():
    p = np.zeros((PAIR_GROUP, PAIR_GROUP), np.float32)
    j = np.arange(LANES)
    p[2 * j, j] = 1.0
    p[2 * j + 1, LANES + j] = 1.0
    return jnp.asarray(p, BF16)


def _expert_kernel(be_ref, nu_ref, x_ref, w1_ref, b1_ref, w2_ref, b2_ref, perm_ref, y_ref, w1s_ref, w2s_ref):
    i = pl.program_id(0)
    f2 = w1_ref.shape[2]
    ngrp = f2 // PAIR_GROUP

    @pl.when(i >= nu_ref[0])
    def _():
        y_ref[...] = jnp.zeros_like(y_ref)

    @pl.when(i < nu_ref[0])
    def _():
        @pl.when((i == 0) | (be_ref[i] != be_ref[jnp.maximum(i - 1, 0)]))
        def _():
            for g in range(ngrp):
                sl = slice(g * PAIR_GROUP, (g + 1) * PAIR_GROUP)
                w1s_ref[:, sl] = jnp.dot(w1_ref[0, :, sl].astype(BF16), perm_ref[...],
                                         preferred_element_type=F32).astype(BF16)
            w2s_ref[...] = w2_ref[0].astype(BF16)

        hdn = jnp.dot(x_ref[...].astype(BF16), w1s_ref[...], preferred_element_type=F32) + b1_ref[0]
        glu = jnp.concatenate([hdn[:, g * PAIR_GROUP:g * PAIR_GROUP + LANES] for g in range(ngrp)], axis=1)
        lin = jnp.concatenate([hdn[:, g * PAIR_GROUP + LANES:(g + 1) * PAIR_GROUP] for g in range(ngrp)], axis=1)
        glu = jnp.minimum(glu, SWIGLU_LIMIT)
        lin = jnp.clip(lin, -SWIGLU_LIMIT, SWIGLU_LIMIT)
        act = glu * jax.nn.sigmoid(SWIGLU_ALPHA * glu) * (lin + 1.0)
        y_ref[...] = jnp.dot(act.astype(BF16), w2s_ref[...], preferred_element_type=F32) + b2_ref[0]


def _experts(block_e, n_used, xb, w1, b1_grp, w2, b2, layer):
    n_rows, d = xb.shape
    _, ne, _, f2 = w1.shape
    dff = w2.shape[2]
    nblk = n_rows // MOE_BLOCK
    perm = _pair_perm()
    blk = lambda i, be, nu: (jnp.minimum(i, nu[0] - 1), 0)
    wsel = lambda i, be, nu: (layer, be[i], 0, 0)
    grid_spec = pltpu.PrefetchScalarGridSpec(
        num_scalar_prefetch=2,
        grid=(nblk,),
        in_specs=[pl.BlockSpec((MOE_BLOCK, d), blk),
                  pl.BlockSpec((None, 1, d, f2), wsel),
                  pl.BlockSpec((None, 1, 1, f2), wsel),
                  pl.BlockSpec((None, 1, dff, d), wsel),
                  pl.BlockSpec((None, 1, 1, d), wsel),
                  pl.BlockSpec(perm.shape, lambda i, be, nu: (0, 0))],
        out_specs=pl.BlockSpec((MOE_BLOCK, d), lambda i, be, nu: (i, 0)),
        scratch_shapes=[pltpu.VMEM((d, f2), BF16), pltpu.VMEM((dff, d), BF16)],
    )
    nl = w1.shape[0]
    return pl.pallas_call(
        _expert_kernel,
        grid_spec=grid_spec,
        out_shape=jax.ShapeDtypeStruct((n_rows, d), F32),
        compiler_params=_cparams(("arbitrary",)),
        name="moe_experts",
    )(block_e, n_used, xb, w1, b1_grp.reshape(nl, ne, 1, f2), w2, b2.reshape(nl, ne, 1, d), perm)


def _combine_kernel(cnt_ref, loc_ref, row_ref, slot_ref, gates_ref, x_ref, gf_ref, yb_ref, o_ref, sorted_ref, sem,
                    *, ne):
    i = pl.program_id(0)
    last = pl.num_programs(0) - 1
    cur = i % 2
    tm = x_ref.shape[0]
    ns = MOE_SLOTS

    def copy_into(buf):
        def copy(off, row, p):
            return pltpu.make_async_copy(yb_ref.at[pl.ds(row, p), :], sorted_ref.at[buf, pl.ds(off, p), :],
                                         sem.at[buf])
        return copy

    @pl.when(i == 0)
    def _():
        sorted_ref[...] = jnp.zeros_like(sorted_ref)
        _segment_dmas(cnt_ref, loc_ref, row_ref, i, ne, copy_into(cur), wait=False)

    @pl.when(i < last)
    def _():
        _segment_dmas(cnt_ref, loc_ref, row_ref, i + 1, ne, copy_into(1 - cur), wait=False)

    slot = slot_ref[...]
    gates = gates_ref[...]
    scol = lax.broadcasted_iota(jnp.int32, (tm, ns), 1)
    gmat = jnp.zeros((tm, ns), F32)
    for kk in range(TOP_K):
        gmat = jnp.where(scol == slot[:, kk:kk + 1], gates[:, kk:kk + 1], gmat)
    _segment_dmas(cnt_ref, loc_ref, row_ref, i, ne, copy_into(cur), wait=True)
    acc = jnp.dot(gmat.astype(BF16), sorted_ref[cur].astype(BF16), preferred_element_type=F32)
    o_ref[...] = x_ref[...] + gf_ref[0] * acc


def _combine(cnt, loc, rowstart, slot, gates, x2, gate_ffn, yb, seq, ne):
    t, d = x2.shape
    tm = MOE_TILE
    per_b = seq // tm
    grid_spec = pltpu.PrefetchScalarGridSpec(
        num_scalar_prefetch=3,
        grid=(t // tm,),
        in_specs=[pl.BlockSpec((tm, LANES), lambda i, *_: (i, 0)),
                  pl.BlockSpec((tm, LANES), lambda i, *_: (i, 0)),
                  pl.BlockSpec((tm, d), lambda i, *_: (i, 0)),
                  pl.BlockSpec((1, 1, d), lambda i, *_: (i // per_b, 0, 0)),
                  pl.BlockSpec(memory_space=pl.ANY)],
        out_specs=pl.BlockSpec((tm, d), lambda i, *_: (i, 0)),
        scratch_shapes=[pltpu.VMEM((2, MOE_SLOTS, d), F32), pltpu.SemaphoreType.DMA((2,))],
    )
    return pl.pallas_call(
        functools.partial(_combine_kernel, ne=ne),
        grid_spec=grid_spec,
        out_shape=jax.ShapeDtypeStruct((t, d), F32),
        compiler_params=_cparams(("arbitrary",)),
        name="moe_combine",
    )(cnt, loc, rowstart, slot, gates, x2, gate_ffn, yb)


def _group_pairs(b1):
    lead = b1.shape[:-1]
    g = b1.reshape(lead + (-1, LANES, 2))
    return jnp.swapaxes(g, -1, -2).reshape(b1.shape)


def _moe_layer(x2, h2, gate_ffn, router_w, router_b, w1, b1_grp, w2, b2, seq, layer):
    t, d = x2.shape
    ne = router_w.shape[1]
    rw_pad = jnp.zeros((d, LANES), F32).at[:, :ne].set(router_w.astype(F32))
    rb_pad = jnp.full((1, LANES), -jnp.inf, F32).at[0, :ne].set(router_b.astype(F32))
    slot, gates, cnt3, base3, loc3 = _router(h2, rw_pad, rb_pad)
    n_assign = t * TOP_K
    n_tiles = t // MOE_TILE
    n_blocks = -(-(n_assign + n_tiles * ne * (SEG_ALIGN - 1)) // MOE_BLOCK) + ne
    cnt = cnt3[:, 0, :ne]
    base = base3[:, 0, :ne]
    loc = loc3[:, 0, :ne]
    total = base[-1] + cnt[-1]
    padded = ((total + MOE_BLOCK - 1) // MOE_BLOCK) * MOE_BLOCK
    pad_end = jnp.cumsum(padded)
    pad_start = pad_end - padded
    rowstart = (pad_start[None, :] + base).reshape(-1).astype(jnp.int32)
    n_used = (pad_end[-1] // MOE_BLOCK).astype(jnp.int32)
    block_start = jnp.minimum(jnp.arange(n_blocks, dtype=jnp.int32), n_used - 1) * MOE_BLOCK
    block_e = jnp.minimum(jnp.sum(block_start[:, None] >= pad_end[None, :], axis=-1), ne - 1).astype(jnp.int32)
    cnt_f = cnt.reshape(-1)
    loc_f = loc.reshape(-1)
    n_used = n_used.reshape(1)
    xb = _dispatch(cnt_f, loc_f, rowstart, (padded - total).astype(jnp.int32), (pad_start + total).astype(jnp.int32),
                   n_used, slot, h2, n_blocks * MOE_BLOCK, ne)
    yb = _experts(block_e, n_used, xb, w1, b1_grp, w2, b2, layer)
    return _combine(cnt_f, loc_f, rowstart, slot, gates, x2, gate_ffn, yb, seq, ne)


def kernel(x, c, ada_w, ada_b, norm1_g, norm2_g, w_in, w_out, s5_lam_re, s5_lam_im, s5_log_dt, s5_b_re, s5_b_im, s5_c_re, s5_c_im, s5_d, s5_glu_w, rwkv_mu, rwkv_w0, rwkv_w1, rwkv_w2, rwkv_a0, rwkv_a1, rwkv_a2, rwkv_g1, rwkv_g2, rwkv_k_k, rwkv_k_a, rwkv_r_k, rwkv_ln_w, rwkv_ln_b, na_q_g, na_k_g, na_rpb, router_w, router_b, exp_w1, exp_b1, exp_w2, exp_b2):
    bsz, seq, d = x.shape
    depth = ada_w.shape[0]
    t = bsz * seq
    mod = _ada_mod(c, ada_w, ada_b).reshape(depth, bsz, 6, 1, d)
    x2 = x.reshape(t, d)
    seg_ones = _seg_ones(RW, dtype=BF16)
    seg_mean = _seg_ones(RW, dtype=BF16, scale=1.0 / HEAD)
    row = lambda a: a.reshape(1, -1).astype(F32)
    b1_grp = _group_pairs(exp_b1.astype(F32))
    exp_b2f = exp_b2.astype(F32)
    bf = lambda a: a.astype(BF16)
    w_in_bf, w_out_bf, glu_bf = bf(w_in), bf(w_out), bf(s5_glu_w)
    w1_bf, w2_bf, a1_bf, a2_bf, g1_bf, g2_bf = (bf(rwkv_w1), bf(rwkv_w2), bf(rwkv_a1), bf(rwkv_a2),
                                                  bf(rwkv_g1), bf(rwkv_g2))
    s5_bblk, s5_cblk, s5_lam, s5_lamc = jax.vmap(_s5_params)(s5_lam_re, s5_lam_im, s5_log_dt, s5_b_re, s5_b_im,
                                                             s5_c_re, s5_c_im)
    na_bias = jax.vmap(_na_bias_table)(na_rpb)
    for l in range(depth):
        m = lambda j: mod[l, :, j]
        s5u, xr, qkv = _in_proj(x2, row(norm1_g[l]), m(0), m(1), w_in_bf[l], seq)
        prep_params = dict(
            mu=row(rwkv_mu[l]), k_k=row(rwkv_k_k[l]), k_a=row(rwkv_k_a[l]), r_k=row(rwkv_r_k[l]),
            w0=rwkv_w0[l].astype(F32), a0=rwkv_a0[l].astype(F32),
            w1=w1_bf[l], w2=w2_bf[l], a1=a1_bf[l], a2=a2_bf[l], g1=g1_bf[l], g2=g2_bf[l],
            q_g=row(jnp.tile(na_q_g[l], NA_W // HEAD)), k_g=row(jnp.tile(na_k_g[l], NA_W // HEAD)), ob=seg_ones)
        (nkk, r, v, lw0, b0, k0, lw1, b1, k1, gate, bonus, naq, nak, nav) = _prep(
            xr.reshape(bsz, seq, 4 * RW), qkv.reshape(bsz, seq, 3 * NA_W), prep_params)
        yf, yb = _wkv_scan(dict(nkk=nkk, r=r, v=v, lw0=lw0, b0=b0, k0=k0, lw1=lw1, b1=b1, k1=k1))
        s5o = _s5_mixer(s5u.reshape(bsz, seq, S5_WIDTH), s5_bblk[l], s5_cblk[l], s5_lam[l], s5_lamc[l],
                        row(s5_d[l]), glu_bf[l])
        nao = _na_mixer(naq, nak, nav, na_bias[l])
        flat = lambda a: a.reshape(t, -1)
        x2, h2 = _out_proj(x2, flat(s5o), flat(yf), flat(yb), flat(gate), flat(bonus), flat(nao),
                           row(rwkv_ln_w[l]), row(rwkv_ln_b[l]), seg_mean, w_out_bf[l],
                           m(2), row(norm2_g[l]), m(3), m(4), seq)
        x2 = _moe_layer(x2, h2, m(5), router_w[l], router_b[l], exp_w1, b1_grp, exp_w2, exp_b2f, seq, l)
    return x2.reshape(bsz, seq, d)
```

```python
import functools
import math

import numpy as np
import jax
import jax.numpy as jnp
from jax import lax
from jax.experimental import pallas as pl
from jax.experimental.pallas import tpu as pltpu

F32 = jnp.float32
BF16 = jnp.bfloat16
HIGHEST = lax.Precision.HIGHEST

D_MODEL = 1024
S5_WIDTH = 256
S5_GROUP = 16
S5_GROUPS = 16
S5_STATE = 64
S5_CHUNK = 64
S5_FLAT = S5_GROUPS * S5_STATE
RW = 384
HEAD = 64
RWKV_GN_EPS = 64e-5
NA_W = 384
GRID_W = 64
NA_KH = 8
NA_KW = 16
N_EXPERTS = 32
TOP_K = 4
MOE_BLOCK = 256
SWIGLU_ALPHA = 1.702
SWIGLU_LIMIT = 7.0
RMS_EPS = 1e-6
LANES = 128
WKV_CHUNK = 64
TOKEN_TILE = 512
VMEM_LIMIT = 56 * 1024 * 1024


def _cparams(sem):
    return pltpu.CompilerParams(dimension_semantics=sem, vmem_limit_bytes=VMEM_LIMIT)


def _seg_ones(n, seg=HEAD, dtype=F32, scale=1.0):
    idx = np.arange(n) // seg
    return jnp.asarray((idx[:, None] == idx[None, :]).astype(np.float32) * scale, dtype)


def _seg_sum(t, ones_bf):
    hi = t.astype(BF16)
    lo = (t - hi.astype(F32)).astype(BF16)
    return (jnp.dot(hi, ones_bf, preferred_element_type=F32) + jnp.dot(lo, ones_bf, preferred_element_type=F32))


def _ada_kernel(c_ref, w_ref, b_ref, o_ref):
    c = c_ref[...]
    cond = c * jax.nn.sigmoid(c)
    o_ref[0] = jnp.dot(cond, w_ref[0], preferred_element_type=F32) + b_ref[0]


def _ada_mod(c, ada_w, ada_b):
    nl, d, n6 = ada_w.shape
    bsz = c.shape[0]
    tn = 1536
    return pl.pallas_call(
        _ada_kernel,
        grid=(nl, n6 // tn),
        in_specs=[pl.BlockSpec((bsz, d), lambda l, j: (0, 0)),
                  pl.BlockSpec((1, d, tn), lambda l, j: (l, 0, j)),
                  pl.BlockSpec((1, 1, tn), lambda l, j: (l, 0, j))],
        out_specs=pl.BlockSpec((1, bsz, tn), lambda l, j: (l, 0, j)),
        out_shape=jax.ShapeDtypeStruct((nl, bsz, n6), F32),
        compiler_params=_cparams(("arbitrary", "arbitrary")),
        name="ada_mod",
    )(c, ada_w, ada_b.reshape(nl, 1, n6))


def _rms_mod(x, g, shift, scale):
    ms = jnp.mean(x * x, axis=-1, keepdims=True)
    h = x * lax.rsqrt(ms + RMS_EPS) * g
    return h * (1.0 + scale) + shift


def _proj_kernel(x_ref, g_ref, sh_ref, sc_ref, w_ref, o_s5, o_rw, o_na):
    h = _rms_mod(x_ref[...], g_ref[...], sh_ref[0], sc_ref[0])
    p = jnp.dot(h.astype(BF16), w_ref[...], preferred_element_type=F32)
    o_s5[...] = p[:, :S5_WIDTH]
    o_rw[...] = p[:, S5_WIDTH:S5_WIDTH + 4 * RW]
    o_na[...] = p[:, S5_WIDTH + 4 * RW:]


def _in_proj(x2, g, shift, scale, w_bf, seq):
    t, d = x2.shape
    n = w_bf.shape[1]
    tm = TOKEN_TILE
    per_b = seq // tm
    row = lambda i: (i, 0)
    bvec = lambda i: (i // per_b, 0, 0)
    return pl.pallas_call(
        _proj_kernel,
        grid=(t // tm,),
        in_specs=[pl.BlockSpec((tm, d), row),
                  pl.BlockSpec((1, d), lambda i: (0, 0)),
                  pl.BlockSpec((1, 1, d), bvec),
                  pl.BlockSpec((1, 1, d), bvec),
                  pl.BlockSpec((d, n), lambda i: (0, 0))],
        out_specs=[pl.BlockSpec((tm, S5_WIDTH), row),
                   pl.BlockSpec((tm, 4 * RW), row),
                   pl.BlockSpec((tm, 3 * NA_W), row)],
        out_shape=[jax.ShapeDtypeStruct((t, S5_WIDTH), F32),
                   jax.ShapeDtypeStruct((t, 4 * RW), F32),
                   jax.ShapeDtypeStruct((t, 3 * NA_W), F32)],
        compiler_params=_cparams(("arbitrary",)),
        name="in_proj",
    )(x2, g, shift, scale, w_bf)


def _softplus(x):
    return jnp.maximum(x, 0.0) + jnp.log(1.0 + jnp.exp(-jnp.abs(x)))


def _prep_kernel(xr_ref, prev_ref, next_ref, qkv_ref,
                 mu_ref, kk_ref, ka_ref, rk_ref, w0_ref, a0_ref,
                 w1_ref, w2_ref, a1_ref, a2_ref, g1_ref, g2_ref, qg_ref, kg_ref, ob_ref,
                 nkk_o, r_o, v_o, lw0_o, b0_o, k0_o, lw1_o, b1_o, k1_o,
                 gate_o, bonus_o, naq_o, nak_o, nav_o):
    i = pl.program_id(1)
    nblk = pl.num_programs(1)
    x = xr_ref[0]
    tm = x.shape[0]
    prow = jnp.where(i == 0, 0.0, prev_ref[0][7:8, :])
    nrow = jnp.where(i == nblk - 1, 0.0, next_ref[0][0:1, :])
    rid = lax.broadcasted_iota(jnp.int32, x.shape, 0)
    prev = jnp.where(rid == 0, prow, pltpu.roll(x, 1, axis=0))
    nxt = jnp.where(rid == tm - 1, nrow, pltpu.roll(x, tm - 1, axis=0))
    xs = x + (0.5 * (prev + nxt) - x) * mu_ref[...]
    r = xs[:, 0:RW]
    k = xs[:, RW:2 * RW]
    v = xs[:, 2 * RW:3 * RW]
    z = xs[:, 3 * RW:4 * RW]
    seg = lambda t: _seg_sum(t, ob_ref[...])
    zb = z.astype(BF16)
    bdot = lambda a, w: jnp.dot(a.astype(BF16), w, preferred_element_type=F32)
    gate_o[0] = bdot(jax.nn.sigmoid(bdot(zb, g1_ref[...])), g2_ref[...])
    kk = k * kk_ref[...]
    kk = kk / jnp.maximum(jnp.sqrt(seg(kk * kk)), 1e-12)
    nkk_o[0] = -kk
    r_o[0] = r
    v_o[0] = v
    bonus_o[0] = seg(r * k * rk_ref[...]) * v
    outs = ((lw0_o, b0_o, k0_o), (lw1_o, b1_o, k1_o))
    for d in range(2):
        wl = w0_ref[d:d + 1, :] + bdot(jnp.tanh(bdot(zb, w1_ref[d])), w2_ref[d])
        w = -_softplus(-wl) - 0.5
        a = jax.nn.sigmoid(a0_ref[d:d + 1, :] + bdot(bdot(zb, a1_ref[d]), a2_ref[d]))
        lw_o, b_o, k_o = outs[d]
        lw_o[0] = -jnp.exp(w)
        b_o[0] = kk * a
        k_o[0] = k * (1.0 + (a - 1.0) * ka_ref[...])
    qkv = qkv_ref[0]
    segm = lambda t: seg(t) * (1.0 / HEAD)
    qn = qkv[:, 0:NA_W]
    kn = qkv[:, NA_W:2 * NA_W]
    naq_o[0] = (qn * lax.rsqrt(segm(qn * qn) + RMS_EPS) * qg_ref[...] * (HEAD ** -0.5)).astype(BF16)
    nak_o[0] = (kn * lax.rsqrt(segm(kn * kn) + RMS_EPS) * kg_ref[...]).astype(BF16)
    nav_o[0] = qkv[:, 2 * NA_W:].astype(BF16)


def _prep(xr, qkv, p):
    bsz, seq, _ = xr.shape
    tm = TOKEN_TILE
    nb = seq // tm
    h8 = tm // 8
    blk = lambda w: pl.BlockSpec((1, tm, w), lambda b, i: (b, i, 0))
    full = lambda a: pl.BlockSpec(a.shape, lambda b, i, _n=a.ndim: (0,) * _n)
    params = [p["mu"], p["k_k"], p["k_a"], p["r_k"], p["w0"], p["a0"], p["w1"], p["w2"], p["a1"], p["a2"],
              p["g1"], p["g2"], p["q_g"], p["k_g"], p["ob"]]
    f32o = jax.ShapeDtypeStruct((bsz, seq, RW), F32)
    bfo = jax.ShapeDtypeStruct((bsz, seq, NA_W), BF16)
    return pl.pallas_call(
        _prep_kernel,
        grid=(bsz, nb),
        in_specs=[blk(4 * RW),
                  pl.BlockSpec((1, 8, 4 * RW), lambda b, i: (b, jnp.maximum(i * h8 - 1, 0), 0)),
                  pl.BlockSpec((1, 8, 4 * RW), lambda b, i: (b, jnp.minimum((i + 1) * h8, seq // 8 - 1), 0)),
                  blk(3 * NA_W)] + [full(a) for a in params],
        out_specs=[blk(RW)] * 14,
        out_shape=[f32o] * 11 + [bfo] * 3,
        compiler_params=_cparams(("arbitrary", "arbitrary")),
        name="mixer_prep",
    )(xr, xr, xr, qkv, *params)


HEAD_PAIR = LANES // HEAD
WKV_PAIRS = RW // LANES
WKV_DOUBLINGS = WKV_CHUNK.bit_length() - 2


def _nt_dot(a, b):
    return lax.dot_general(a, b, (((1,), (1,)), ((), ())), preferred_element_type=F32)


def _wkv_kernel(*refs, nrows):
    f_in = refs[0:6]
    b_in = refs[6:12]
    tri_ref, msk_ref, eye_ref = refs[12:15]
    yf_ref, yb_ref = refs[15:17]
    s_ref = refs[17]
    c = pl.program_id(0)
    bi = pl.program_id(1)
    tc = WKV_CHUNK

    @pl.when(c == 0)
    def _():
        for row in range(nrows):
            s_ref[bi * nrows + row] = jnp.zeros(s_ref.shape[1:], F32)

    first_head = lax.broadcasted_iota(jnp.int32, (tc, LANES), 1) < HEAD
    eye_bf = eye_ref[...]
    eye_f = eye_bf.astype(F32)

    def blk(z):
        return jnp.concatenate([jnp.where(first_head, z, 0.0), jnp.where(first_head, 0.0, z)], axis=0)

    bdot = lambda p, q: jnp.dot(p, q, preferred_element_type=F32)
    units = [(row, d, p) for row in range(nrows) for d in range(2) for p in range(WKV_PAIRS)]
    every = lambda fn, *cols: [fn(*args) for args in zip(*cols)]
    states = [s_ref[bi * nrows + row, d, p] for row, d, p in units]
    masks = [(msk_ref[d, 0] > 0.5, msk_ref[d, 1] > 0.5) for d in range(2)]

    def load(row, d, p):
        src = f_in if d == 0 else b_in
        return [s[row, :, p * LANES:(p + 1) * LANES] for s in src]

    def decays(unit, data):
        d = unit[1]
        lw = data[3]
        cum = jnp.zeros_like(lw)
        rest = lw
        for _ in range(3):
            term = rest.astype(BF16)
            cum = cum + bdot(tri_ref[d], term)
            rest = rest - term.astype(F32)
        cend = cum[tc - 1:tc] if d == 0 else cum[0:1]
        return cum, cend

    def operands(data, dec):
        a, r, v, lw, bb, kk = data
        cum, cend = dec
        e_neg = jnp.exp(-cum)
        e_end = jnp.exp(cend - cum)
        x = jnp.concatenate([blk(a * jnp.exp(cum - lw)), blk(r * jnp.exp(cum))], axis=0).astype(BF16)
        y = jnp.concatenate([blk(bb * e_neg), blk(kk * e_neg)], axis=0).astype(BF16)
        z = jnp.concatenate([blk(bb * e_end), blk(kk * e_end)], axis=0).astype(BF16)
        return x, y, z, blk(v)

    def causal(unit, g):
        strict, incl = masks[unit[1]]
        return (jnp.where(strict, g[:2 * tc, :2 * tc], 0.0), jnp.where(strict, g[:2 * tc, 2 * tc:], 0.0),
                jnp.concatenate([jnp.where(incl, g[2 * tc:, :2 * tc], 0.0),
                                 jnp.where(incl, g[2 * tc:, 2 * tc:], 0.0)], axis=1).astype(BF16))

    data = every(load, *zip(*units))
    dec = every(decays, units, data)
    ops = every(operands, data, dec)
    grams = every(lambda o: _nt_dot(o[0], o[1]), ops)
    nmat = every(causal, units, grams)
    ph = every(lambda o, st: _nt_dot(o[0], st.astype(BF16)), ops, states)
    vbf = every(lambda o: o[3].astype(BF16), ops)
    rhs = every(lambda q, n, vb: q[:2 * tc] + bdot(n[1].astype(BF16), vb), ph, nmat, vbf)
    inv = every(lambda n: eye_f + n[0], nmat)
    pw = every(lambda n: n[0].astype(BF16), nmat)
    for _ in range(WKV_DOUBLINGS):
        pw = every(lambda q: bdot(q, q).astype(BF16), pw)
        inv = every(lambda t, q: t + bdot(t.astype(BF16), q), inv, pw)
    u = every(lambda t, q: bdot(t.astype(BF16), q.astype(BF16)), inv, rhs)
    uv = every(lambda q, vb: jnp.concatenate([q.astype(BF16), vb], axis=0), u, vbf)
    yo = every(lambda q, n, w: q[2 * tc:] + bdot(n[2], w), ph, nmat, uv)
    uvt = every(lambda q, o: jnp.transpose(jnp.concatenate([q, o[3]], axis=0)).astype(BF16), u, ops)
    new = every(lambda st, dc, w, o: st * jnp.exp(dc[1]) + bdot(w, o[2]), states, dec, uvt, ops)
    for row in range(nrows):
        for d in range(2):
            out = yf_ref if d == 0 else yb_ref
            parts = [yo[units.index((row, d, p))] for p in range(WKV_PAIRS)]
            out[row] = jnp.concatenate([q[:tc] + q[tc:] for q in parts], axis=1)
    for (row, d, p), st in zip(units, new):
        s_ref[bi * nrows + row, d, p] = st


def _wkv_masks():
    tc = WKV_CHUNK
    t = np.arange(tc)
    tri = np.stack([t[None, :] <= t[:, None], t[None, :] >= t[:, None]]).astype(np.float32)
    head = np.arange(HEAD_PAIR * tc) // tc
    tt = np.arange(HEAD_PAIR * tc) % tc
    same = head[:, None] == head[None, :]
    m = np.stack([np.stack([same & (tt[None, :] < tt[:, None]), same & (tt[None, :] <= tt[:, None])]),
                  np.stack([same & (tt[None, :] > tt[:, None]), same & (tt[None, :] >= tt[:, None])])])
    return jnp.asarray(tri, BF16), jnp.asarray(m.astype(np.float32)), jnp.asarray(np.eye(LANES, dtype=np.float32), BF16)


def _wkv_scan(ins):
    bsz, seq, _ = ins["nkk"].shape
    tc = WKV_CHUNK
    nc = seq // tc
    tri, msk, eye = _wkv_masks()
    nrows = 4 if bsz % 4 == 0 else (2 if bsz % 2 == 0 else 1)
    fwd = pl.BlockSpec((nrows, tc, RW), lambda c, b: (b, c, 0))
    bwd = pl.BlockSpec((nrows, tc, RW), lambda c, b: (b, nc - 1 - c, 0))
    full = lambda a: pl.BlockSpec(a.shape, lambda c, b, _n=a.ndim: (0,) * _n)
    f_args = [ins["nkk"], ins["r"], ins["v"], ins["lw0"], ins["b0"], ins["k0"]]
    b_args = [ins["nkk"], ins["r"], ins["v"], ins["lw1"], ins["b1"], ins["k1"]]
    o = jax.ShapeDtypeStruct((bsz, seq, RW), F32)
    return pl.pallas_call(
        functools.partial(_wkv_kernel, nrows=nrows),
        grid=(nc, bsz // nrows),
        in_specs=[fwd] * 6 + [bwd] * 6 + [full(tri), full(msk), full(eye)],
        out_specs=[fwd, bwd],
        out_shape=[o, o],
        scratch_shapes=[pltpu.VMEM((bsz, 2, WKV_PAIRS, LANES, LANES), F32)],
        compiler_params=_cparams(("arbitrary", "arbitrary")),
        name="wkv_scan",
    )(*f_args, *b_args, tri, msk, eye)


def _gelu_tanh(x):
    return 0.5 * x * (1.0 + jnp.tanh(math.sqrt(2.0 / math.pi) * (x + 0.044715 * (x * x * x))))


def _s5_kernel(ua_ref, ub_ref, bblk_ref, cblk_ref, lam_ref, lamc_ref, d_ref, glu_ref, o_ref,
               y_ref, st_ref, bu_ref, end_ref, carry_ref, *, seq, nb):
    ch = S5_CHUNK
    nc = seq // ch
    n = S5_FLAT
    u_halves = (ua_ref, ub_ref)
    for b in range(nb):
        for hf in range(2):
            y_ref[b, hf] = u_halves[hf][b] * d_ref[:, hf * LANES:(hf + 1) * LANES]

    def cmul_add(lre, lim, s, add):
        sre = s[:, :n]
        sim = s[:, n:]
        return jnp.concatenate([lre * sre - lim * sim + add[:, :n],
                                lre * sim + lim * sre + add[:, n:]], axis=1)

    for d in range(2):
        lre = lam_ref[d, 0:1, :]
        lim = lam_ref[d, 1:2, :]
        lcre = lamc_ref[d, 0:1, :]
        lcim = lamc_ref[d, 1:2, :]
        tloc = (lambda i: i) if d == 0 else (lambda i: ch - 1 - i)
        cloc = (lambda i: i) if d == 0 else (lambda i: nc - 1 - i)

        def drive(i):
            tl = tloc(jnp.minimum(i, ch - 1))
            rows = jnp.concatenate(
                [jnp.concatenate([r[b, pl.ds(tl, nc, stride=ch), :] for r in u_halves], axis=1)
                 for b in range(nb)], axis=0)
            return jnp.dot(rows.astype(BF16), bblk_ref[d], preferred_element_type=F32)

        def project(i, slot):
            bu_ref[slot] = drive(i)

        def advance(slot):
            st_ref[...] = cmul_add(lre, lim, st_ref[...], bu_ref[slot])

        def emit_from(st, i):
            yr = jnp.dot(st.astype(BF16), cblk_ref[d], preferred_element_type=F32)
            idx = pl.ds(tloc(i), nc, stride=ch)
            for b in range(nb):
                for hf in range(2):
                    y_ref[b, hf, idx, :] = (y_ref[b, hf, idx, :]
                                            + yr[b * nc:(b + 1) * nc, hf * LANES:(hf + 1) * LANES])

        def emit(i):
            emit_from(st_ref[...], i)

        st_ref[...] = jnp.zeros_like(st_ref)
        project(0, 0)

        def p1(j, c):
            nxt = drive(2 * j + 1)
            st = cmul_add(lre, lim, st_ref[...], bu_ref[0])
            st_ref[...] = cmul_add(lre, lim, st, nxt)
            project(2 * j + 2, 0)
            return c

        lax.fori_loop(0, ch // 2, p1, 0)
        end_ref[...] = st_ref[...]

        def cs(i, car):
            c = cloc(i)
            for b in range(nb):
                carry_ref[pl.ds(b * nc + c, 1), :] = car[b:b + 1]
            ends = jnp.concatenate([end_ref[pl.ds(b * nc + c, 1), :] for b in range(nb)], axis=0)
            return cmul_add(lcre, lcim, car, ends)

        lax.fori_loop(0, nc, cs, jnp.zeros((nb, 2 * n), F32))

        st_ref[...] = carry_ref[...]
        project(0, 0)
        project(1, 1)
        advance(0)

        def p2(j, c):
            i = 2 * j + 1
            nxt = drive(i + 1)
            st = st_ref[...]
            emit_from(st, i - 1)
            st = cmul_add(lre, lim, st, bu_ref[1])
            emit_from(st, i)
            st_ref[...] = cmul_add(lre, lim, st, nxt)
            project(i + 2, 1)
            return c

        lax.fori_loop(0, (ch - 2) // 2, p2, 0)
        emit(ch - 2)
        advance(1)
        emit(ch - 1)

    glu = glu_ref[...]
    for b in range(nb):
        g = _gelu_tanh(jnp.concatenate([y_ref[b, 0], y_ref[b, 1]], axis=1))
        o_ref[b] = (g * jax.nn.sigmoid(jnp.dot(g.astype(BF16), glu, preferred_element_type=F32))).astype(o_ref.dtype)


def _s5_params(lam_re, lam_im, log_dt, b_re, b_im, c_re, c_im):
    lre = lam_re.astype(F32)
    lim = lam_im.astype(F32)
    dt = jnp.exp(log_dt.astype(F32))[..., None]

    def cexp(scale):
        mag = jnp.exp(lre * dt * scale)
        return mag * jnp.cos(lim * dt * scale), mag * jnp.sin(lim * dt * scale)

    bar_re, bar_im = cexp(1.0)
    den = lre * lre + lim * lim
    f_re = ((bar_re - 1.0) * lre + bar_im * lim) / den
    f_im = (bar_im * lre - (bar_re - 1.0) * lim) / den
    bm_re = b_re.astype(F32)
    bm_im = b_im.astype(F32)
    bb_re = f_re[..., None] * bm_re - f_im[..., None] * bm_im
    bb_im = f_re[..., None] * bm_im + f_im[..., None] * bm_re
    eye_g = jnp.eye(S5_GROUPS, dtype=F32)

    def blockdiag_in(m):
        return jnp.einsum("dgph,gk->dghkp", m, eye_g).reshape(2, S5_WIDTH, S5_FLAT)

    def blockdiag_out(m):
        return jnp.einsum("dghp,gk->dgpkh", m, eye_g).reshape(2, S5_FLAT, S5_WIDTH)

    bblk = jnp.concatenate([blockdiag_in(bb_re), blockdiag_in(bb_im)], axis=2)
    cblk = jnp.concatenate([blockdiag_out(c_re.astype(F32)), -blockdiag_out(c_im.astype(F32))], axis=1)
    flat = lambda z: jnp.stack([z[0].reshape(2, S5_FLAT), z[1].reshape(2, S5_FLAT)], axis=1)
    return bblk.astype(BF16), cblk.astype(BF16), flat((bar_re, bar_im)), flat(cexp(float(S5_CHUNK)))


def _s5_mixer(u, bblk, cblk, lam, lamc, d_skip, glu_bf):
    bsz, seq, w = u.shape
    nc = seq // S5_CHUNK
    nb = 2 if bsz % 2 == 0 else 1
    full = lambda a: pl.BlockSpec(a.shape, lambda b, _n=a.ndim: (0,) * _n)
    args = [bblk, cblk, lam, lamc, d_skip, glu_bf]
    state = pltpu.VMEM((nb * nc, 2 * S5_FLAT), F32)
    return pl.pallas_call(
        functools.partial(_s5_kernel, seq=seq, nb=nb),
        grid=(bsz // nb,),
        in_specs=[pl.BlockSpec((nb, seq, LANES), lambda b: (b, 0, 0)),
                  pl.BlockSpec((nb, seq, LANES), lambda b: (b, 0, 1))] + [full(a) for a in args],
        out_specs=pl.BlockSpec((nb, seq, w), lambda b: (b, 0, 0)),
        out_shape=jax.ShapeDtypeStruct((bsz, seq, w), BF16),
        scratch_shapes=[pltpu.VMEM((nb, w // LANES, seq, LANES), F32),
                        state,
                        pltpu.VMEM((2, nb * nc, 2 * S5_FLAT), F32),
                        state,
                        state],
        compiler_params=_cparams(("arbitrary",)),
        name="s5_mixer",
    )(u, u, *args)


def _na_bias_table(rpb):
    q_col = np.arange(GRID_W)
    c_start = np.clip(q_col - NA_KW // 2, 0, GRID_W - NA_KW)
    k_col = np.arange(GRID_W)
    valid = (k_col[None, :] >= c_start[:, None]) & (k_col[None, :] < c_start[:, None] + NA_KW)
    dx = np.clip(k_col[None, :] - q_col[:, None] + NA_KW - 1, 0, 2 * NA_KW - 2)
    pick = (np.arange(2 * NA_KW - 1)[:, None, None] == dx[None]).astype(np.float32)
    base = jnp.einsum("hyd,dqk->hyqk", rpb.astype(F32), jnp.asarray(pick), precision=HIGHEST)
    base = jnp.where(jnp.asarray(valid)[None, None], base, -jnp.inf)
    tab = jnp.stack([base[:, NA_KH - 1 - o:2 * NA_KH - 1 - o] for o in range(NA_KH)], axis=1)
    tab = jnp.transpose(tab, (0, 1, 3, 2, 4))
    return tab.reshape(rpb.shape[0], NA_KH, GRID_W, NA_KH * GRID_W)


def _na_kernel(q_ref, k_ref, v_ref, bias_ref, o_ref, *, rows, rblk):
    rb = pl.program_id(1)
    lane = lax.broadcasted_iota(jnp.int32, (GRID_W, LANES), 1)
    low = lane < HEAD

    npair = NA_W // LANES
    every = lambda fn, *cols: [fn(*args) for args in zip(*cols)]

    def row_pair(jj, carry):
        js = [2 * jj, 2 * jj + 1]
        units = [(u, c) for u in range(2) for c in range(npair)]
        loaded = []
        for j in js:
            r = rb * rblk + j
            rs = jnp.clip(r - NA_KH // 2, 0, rows - NA_KH)
            loaded.append((q_ref[0, j],
                           k_ref[0, pl.ds(rs, NA_KH)].reshape(NA_KH * GRID_W, NA_W),
                           v_ref[0, pl.ds(rs, NA_KH)].reshape(NA_KH * GRID_W, NA_W),
                           r - rs))

        def scores(u, c):
            q, kmat, _, off = loaded[u]
            sl = slice(c * LANES, (c + 1) * LANES)
            q2 = q[:, sl].astype(F32)
            lhs = jnp.concatenate([jnp.where(low, q2, 0.0), jnp.where(low, 0.0, q2)], axis=0).astype(BF16)
            s = lax.dot_general(lhs, kmat[:, sl], (((1,), (1,)), ((), ())), preferred_element_type=F32)
            return s + jnp.concatenate([bias_ref[2 * c, off], bias_ref[2 * c + 1, off]], axis=0)

        s = every(scores, *zip(*units))
        m = every(lambda t: jnp.max(t, axis=-1, keepdims=True), s)
        p = every(lambda t, mx: jnp.exp(t - mx), s, m)
        l = every(lambda t: jnp.sum(t, axis=-1, keepdims=True), p)
        o = every(lambda t, den, uc: jnp.dot(t.astype(BF16), loaded[uc[0]][2][:, uc[1] * LANES:(uc[1] + 1) * LANES],
                                             preferred_element_type=F32) / den, p, l, units)
        o = every(lambda t: jnp.where(low, t[:GRID_W], t[GRID_W:]), o)
        for u, j in enumerate(js):
            o_ref[0, j] = jnp.concatenate(o[u * npair:(u + 1) * npair], axis=1).astype(o_ref.dtype)
        return carry

    lax.fori_loop(0, rblk // 2, row_pair, 0)


def _na_mixer(q, k, v, bias):
    bsz, seq, w = q.shape
    rows = seq // GRID_W
    rblk = 8
    g4 = lambda a: a.reshape(bsz, rows, GRID_W, w)
    img = pl.BlockSpec((1, rows, GRID_W, w), lambda b, i: (b, 0, 0, 0))
    blk = pl.BlockSpec((1, rblk, GRID_W, w), lambda b, i: (b, i, 0, 0))
    out = pl.pallas_call(
        functools.partial(_na_kernel, rows=rows, rblk=rblk),
        grid=(bsz, rows // rblk),
        in_specs=[blk, img, img, pl.BlockSpec(bias.shape, lambda b, i: (0, 0, 0, 0))],
        out_specs=blk,
        out_shape=jax.ShapeDtypeStruct((bsz, rows, GRID_W, w), BF16),
        compiler_params=_cparams(("arbitrary", "arbitrary")),
        name="na_mixer",
    )(g4(q), g4(k), g4(v), bias)
    return out.reshape(bsz, seq, w)


def _outproj_kernel(x_ref, s5_ref, yf_ref, yb_ref, gate_ref, bonus_ref, na_ref,
                    lnw_ref, lnb_ref, obm_ref, w_ref, gm_ref, g2_ref, sh_ref, sc_ref,
                    xo_ref, h_ref):
    segm = lambda t: _seg_sum(t, obm_ref[...])
    y = yf_ref[...] + yb_ref[...]
    yc = y - segm(y)
    yn = yc * lax.rsqrt(segm(yc * yc) + RWKV_GN_EPS) * lnw_ref[...] + lnb_ref[...]
    rw = (yn + bonus_ref[...]) * gate_ref[...]
    mixed = jnp.concatenate([s5_ref[...].astype(BF16), rw.astype(BF16), na_ref[...].astype(BF16)], axis=1)
    xo = x_ref[...] + gm_ref[0] * jnp.dot(mixed, w_ref[...], preferred_element_type=F32)
    xo_ref[...] = xo
    h_ref[...] = _rms_mod(xo, g2_ref[...], sh_ref[0], sc_ref[0])


def _out_proj(x2, s5o, yf, yb, gate, bonus, nao, lnw, lnb, obm, w_bf, gate_mix, g2, shift, scale, seq):
    t, d = x2.shape
    tm = TOKEN_TILE
    per_b = seq // tm
    row = lambda w: pl.BlockSpec((tm, w), lambda i: (i, 0))
    full = lambda a: pl.BlockSpec(a.shape, lambda i, _n=a.ndim: (0,) * _n)
    bvec = pl.BlockSpec((1, 1, d), lambda i: (i // per_b, 0, 0))
    o = jax.ShapeDtypeStruct((t, d), F32)
    return pl.pallas_call(
        _outproj_kernel,
        grid=(t // tm,),
        in_specs=[row(d), row(S5_WIDTH)] + [row(RW)] * 5 +
                 [full(lnw), full(lnb), full(obm), full(w_bf), bvec, full(g2), bvec, bvec],
        out_specs=[row(d), row(d)],
        out_shape=[o, o],
        compiler_params=_cparams(("arbitrary",)),
        name="out_proj",
    )(x2, s5o, yf, yb, gate, bonus, nao, lnw, lnb, obm, w_bf, gate_mix, g2, shift, scale)


MOE_TILE = 256
SEG_ALIGN = 8
MOE_SLOTS = -(-(MOE_TILE * TOP_K + N_EXPERTS * (SEG_ALIGN - 1)) // LANES) * LANES


def _router_kernel(h_ref, w_ref, b_ref, tri_ref, upper_ref, slot_ref, gate_ref, cnt_ref, base_ref, loc_ref,
                   carry_ref):
    @pl.when(pl.program_id(0) == 0)
    def _():
        carry_ref[...] = jnp.zeros_like(carry_ref)

    logits = jnp.dot(h_ref[...], w_ref[...], precision=HIGHEST, preferred_element_type=F32) + b_ref[...]
    tm = logits.shape[0]
    lane = lax.broadcasted_iota(jnp.int32, (tm, LANES), 1)
    lane_f = lane.astype(F32)
    vals, hots = [], []
    cur = logits
    for _ in range(TOP_K):
        m = jnp.max(cur, axis=-1, keepdims=True)
        idx = jnp.min(jnp.where(cur == m, lane_f, float(LANES)), axis=-1, keepdims=True)
        hot = lane_f == idx
        vals.append(m)
        hots.append(hot)
        cur = jnp.where(hot, -jnp.inf, cur)
    exps = [jnp.exp(v - vals[0]) for v in vals]
    den = exps[0] + exps[1] + exps[2] + exps[3]
    assign = sum(h.astype(F32) for h in hots)
    before = jnp.dot(tri_ref[...], assign.astype(BF16), preferred_element_type=F32)
    cnt = jnp.sum(assign, axis=0, keepdims=True)
    cnt = jnp.floor((cnt + (SEG_ALIGN - 1)) * (1.0 / SEG_ALIGN)) * SEG_ALIGN
    cnt8 = jnp.broadcast_to(cnt, (8, LANES)).astype(BF16)
    loc = jnp.dot(cnt8, upper_ref[...], preferred_element_type=F32)[0:1, :]
    place = before + loc
    s_out = jnp.zeros((tm, LANES), jnp.int32)
    g_out = jnp.zeros((tm, LANES), F32)
    for kk in range(TOP_K):
        slot = jnp.sum(jnp.where(hots[kk], place, 0.0), axis=-1, keepdims=True)
        sel = lane == kk
        s_out = jnp.where(sel, slot.astype(jnp.int32), s_out)
        g_out = jnp.where(sel, exps[kk] / den, g_out)
    slot_ref[...] = s_out
    gate_ref[...] = g_out
    cnt_ref[0] = cnt.astype(jnp.int32)
    base_ref[0] = carry_ref[...].astype(jnp.int32)
    loc_ref[0] = loc.astype(jnp.int32)
    carry_ref[...] = carry_ref[...] + cnt


def _router(h2, rw_pad, rb_pad):
    t, d = h2.shape
    tm = MOE_TILE
    nt = t // tm
    tri = jnp.asarray(np.tril(np.ones((tm, tm), np.float32), -1), BF16)
    upper = jnp.asarray(np.triu(np.ones((LANES, LANES), np.float32), 1), BF16)
    row = pl.BlockSpec((tm, LANES), lambda i: (i, 0))
    per_tile = pl.BlockSpec((1, 1, LANES), lambda i: (i, 0, 0))
    full = lambda a: pl.BlockSpec(a.shape, lambda i, _n=a.ndim: (0,) * _n)
    tile_i32 = jax.ShapeDtypeStruct((nt, 1, LANES), jnp.int32)
    return pl.pallas_call(
        _router_kernel,
        grid=(nt,),
        in_specs=[pl.BlockSpec((tm, d), lambda i: (i, 0)), full(rw_pad), full(rb_pad), full(tri), full(upper)],
        out_specs=[row, row, per_tile, per_tile, per_tile],
        out_shape=[jax.ShapeDtypeStruct((t, LANES), jnp.int32),
                   jax.ShapeDtypeStruct((t, LANES), F32),
                   tile_i32, tile_i32, tile_i32],
        scratch_shapes=[pltpu.VMEM((1, LANES), F32)],
        compiler_params=_cparams(("arbitrary",)),
        name="moe_router",
    )(h2, rw_pad, rb_pad, tri, upper)


def _segment_dmas(cnt_ref, loc_ref, row_ref, tile, ne, make_copy, wait):
    def per_expert(e, carry):
        i = tile * ne + e
        n = cnt_ref[i]

        @pl.when(n > 0)
        def _():
            cp = make_copy(pl.multiple_of(loc_ref[i], SEG_ALIGN), pl.multiple_of(row_ref[i], SEG_ALIGN),
                           pl.multiple_of(n, SEG_ALIGN))
            if wait:
                cp.wait()
            else:
                cp.start()

        return carry

    lax.fori_loop(0, ne, per_expert, 0)


def _dispatch_kernel(cnt_ref, loc_ref, row_ref, zcnt_ref, zoff_ref, zrow_ref, nu_ref,
                     slot_ref, h_ref, xb_ref, sorted_ref, zero_ref, sem, zsem, *, ne, nblk):
    i = pl.program_id(0)
    last = pl.num_programs(0) - 1
    cur = i % 2
    tm = h_ref.shape[0]
    ns = MOE_SLOTS

    def zero_copy(off, row, p):
        return pltpu.make_async_copy(zero_ref.at[pl.ds(off, p), :], xb_ref.at[pl.ds(row, p), :], zsem)

    def tail_copy(j):
        row = pl.multiple_of((nu_ref[0] + j) * MOE_BLOCK, MOE_BLOCK)
        return pltpu.make_async_copy(zero_ref, xb_ref.at[pl.ds(row, MOE_BLOCK), :], zsem)

    def zero_fill(wait):
        _segment_dmas(zcnt_ref, zoff_ref, zrow_ref, 0, ne, zero_copy, wait)

        def tail(j, c):
            if wait:
                tail_copy(j).wait()
            else:
                tail_copy(j).start()
            return c

        lax.fori_loop(0, nblk - nu_ref[0], tail, 0)

    @pl.when(i == 0)
    def _():
        zero_ref[...] = jnp.zeros_like(zero_ref)
        zero_fill(wait=False)

    slot_t = jnp.transpose(slot_ref[...].astype(F32))
    srow = lax.broadcasted_iota(jnp.int32, (ns, tm), 0).astype(F32)
    pick = jnp.zeros((ns, tm), F32)
    for kk in range(TOP_K):
        pick = jnp.where(srow == slot_t[kk:kk + 1, :], 1.0, pick)
    sorted_ref[cur] = jnp.dot(pick.astype(BF16), h_ref[...].astype(BF16), preferred_element_type=F32)

    def copy_from(buf):
        def copy(off, row, p):
            return pltpu.make_async_copy(sorted_ref.at[buf, pl.ds(off, p), :], xb_ref.at[pl.ds(row, p), :],
                                         sem.at[buf])
        return copy

    _segment_dmas(cnt_ref, loc_ref, row_ref, i, ne, copy_from(cur), wait=False)

    @pl.when(i > 0)
    def _():
        _segment_dmas(cnt_ref, loc_ref, row_ref, i - 1, ne, copy_from(1 - cur), wait=True)

    @pl.when(i == last)
    def _():
        _segment_dmas(cnt_ref, loc_ref, row_ref, i, ne, copy_from(cur), wait=True)
        zero_fill(wait=True)


def _dispatch(cnt, loc, rowstart, zcnt, zrow, n_used, slot, h2, n_rows, ne):
    t, d = h2.shape
    tm = MOE_TILE
    zoff = jnp.zeros_like(zcnt)
    grid_spec = pltpu.PrefetchScalarGridSpec(
        num_scalar_prefetch=7,
        grid=(t // tm,),
        in_specs=[pl.BlockSpec((tm, LANES), lambda i, *_: (i, 0)),
                  pl.BlockSpec((tm, d), lambda i, *_: (i, 0))],
        out_specs=pl.BlockSpec(memory_space=pl.ANY),
        scratch_shapes=[pltpu.VMEM((2, MOE_SLOTS, d), F32), pltpu.VMEM((MOE_BLOCK, d), F32),
                        pltpu.SemaphoreType.DMA((2,)), pltpu.SemaphoreType.DMA(())],
    )
    return pl.pallas_call(
        functools.partial(_dispatch_kernel, ne=ne, nblk=n_rows // MOE_BLOCK),
        grid_spec=grid_spec,
        out_shape=jax.ShapeDtypeStruct((n_rows, d), F32),
        compiler_params=_cparams(("arbitrary",)),
        name="moe_dispatch",
    )(cnt, loc, rowstart, zcnt, zoff, zrow, n_used, slot, h2)


PAIR_GROUP = 2 * LANES


def _pair_perm():
    p = np.zeros((PAIR_GROUP, PAIR_GROUP), np.float32)
    j = np.arange(LANES)
    p[2 * j, j] = 1.0
    p[2 * j + 1, LANES + j] = 1.0
    return jnp.asarray(p, BF16)


def _expert_kernel(be_ref, nu_ref, slot_ref, nxt_ref, x_ref, w1_hbm, b1_ref, w2_hbm, b2_ref, perm_ref, y_ref,
                   w1s_ref, w2s_ref, w1buf, w2buf, sem, *, layer):
    i = pl.program_id(0)
    f2 = w1buf.shape[2]
    ngrp = f2 // PAIR_GROUP

    def fetch(e, s):
        return (pltpu.make_async_copy(w1_hbm.at[layer, e], w1buf.at[s], sem.at[s, 0]),
                pltpu.make_async_copy(w2_hbm.at[layer, e], w2buf.at[s], sem.at[s, 1]))

    @pl.when(i >= nu_ref[0])
    def _():
        y_ref[...] = jnp.zeros_like(y_ref)

    @pl.when(i == 0)
    def _():
        for cp in fetch(be_ref[0], 0):
            cp.start()

    @pl.when(i < nu_ref[0])
    def _():
        @pl.when((i == 0) | (be_ref[i] != be_ref[jnp.maximum(i - 1, 0)]))
        def _():
            s = slot_ref[i]
            for cp in fetch(be_ref[i], s):
                cp.wait()

            @pl.when(nxt_ref[i] >= 0)
            def _():
                for cp in fetch(nxt_ref[i], 1 - s):
                    cp.start()

            for g in range(ngrp):
                sl = slice(g * PAIR_GROUP, (g + 1) * PAIR_GROUP)
                w1s_ref[:, sl] = jnp.dot(w1buf[s, :, sl].astype(BF16), perm_ref[...],
                                         preferred_element_type=F32).astype(BF16)
            w2s_ref[...] = w2buf[s].astype(BF16)

        hdn = jnp.dot(x_ref[...].astype(BF16), w1s_ref[...], preferred_element_type=F32) + b1_ref[0]
        glu = jnp.concatenate([hdn[:, g * PAIR_GROUP:g * PAIR_GROUP + LANES] for g in range(ngrp)], axis=1)
        lin = jnp.concatenate([hdn[:, g * PAIR_GROUP + LANES:(g + 1) * PAIR_GROUP] for g in range(ngrp)], axis=1)
        glu = jnp.minimum(glu, SWIGLU_LIMIT)
        lin = jnp.clip(lin, -SWIGLU_LIMIT, SWIGLU_LIMIT)
        act = glu * jax.nn.sigmoid(SWIGLU_ALPHA * glu) * (lin + 1.0)
        y_ref[...] = jnp.dot(act.astype(BF16), w2s_ref[...], preferred_element_type=F32) + b2_ref[0]


def _experts(block_e, n_used, xb, w1, b1_grp, w2, b2, layer):
    n_rows, d = xb.shape
    _, ne, _, f2 = w1.shape
    dff = w2.shape[2]
    nblk = n_rows // MOE_BLOCK
    perm = _pair_perm()
    blk = lambda i, be, nu, *_: (jnp.minimum(i, nu[0] - 1), 0)
    wsel = lambda i, be, *_: (layer, be[i], 0, 0)
    idx = jnp.arange(nblk, dtype=jnp.int32)
    first = (idx == 0) | (block_e != jnp.roll(block_e, 1))
    slot = ((jnp.cumsum(first.astype(jnp.int32)) - 1) % 2).astype(jnp.int32)
    pos = jnp.where(first, idx, nblk)
    nxt_pos = jnp.concatenate([lax.cummin(pos[::-1])[::-1][1:], jnp.full((1,), nblk, jnp.int32)])
    nxt = jnp.where(nxt_pos < nblk, block_e[jnp.minimum(nxt_pos, nblk - 1)], -1).astype(jnp.int32)
    grid_spec = pltpu.PrefetchScalarGridSpec(
        num_scalar_prefetch=4,
        grid=(nblk,),
        in_specs=[pl.BlockSpec((MOE_BLOCK, d), blk),
                  pl.BlockSpec(memory_space=pl.ANY),
                  pl.BlockSpec((None, 1, 1, f2), wsel),
                  pl.BlockSpec(memory_space=pl.ANY),
                  pl.BlockSpec((None, 1, 1, d), wsel),
                  pl.BlockSpec(perm.shape, lambda i, *_: (0, 0))],
        out_specs=pl.BlockSpec((MOE_BLOCK, d), lambda i, *_: (i, 0)),
        scratch_shapes=[pltpu.VMEM((d, f2), BF16), pltpu.VMEM((dff, d), BF16),
                        pltpu.VMEM((2, d, f2), F32), pltpu.VMEM((2, dff, d), F32),
                        pltpu.SemaphoreType.DMA((2, 2))],
    )
    nl = w1.shape[0]
    return pl.pallas_call(
        functools.partial(_expert_kernel, layer=layer),
        grid_spec=grid_spec,
        out_shape=jax.ShapeDtypeStruct((n_rows, d), F32),
        compiler_params=_cparams(("arbitrary",)),
        name="moe_experts",
    )(block_e, n_used, slot, nxt, xb, w1, b1_grp.reshape(nl, ne, 1, f2), w2, b2.reshape(nl, ne, 1, d), perm)


def _combine_kernel(cnt_ref, loc_ref, row_ref, slot_ref, gates_ref, x_ref, gf_ref, yb_ref, o_ref, sorted_ref, sem,
                    *, ne):
    i = pl.program_id(0)
    last = pl.num_programs(0) - 1
    cur = i % 2
    tm = x_ref.shape[0]
    ns = MOE_SLOTS

    def copy_into(buf):
        def copy(off, row, p):
            return pltpu.make_async_copy(yb_ref.at[pl.ds(row, p), :], sorted_ref.at[buf, pl.ds(off, p), :],
                                         sem.at[buf])
        return copy

    @pl.when(i == 0)
    def _():
        sorted_ref[...] = jnp.zeros_like(sorted_ref)
        _segment_dmas(cnt_ref, loc_ref, row_ref, i, ne, copy_into(cur), wait=False)

    @pl.when(i < last)
    def _():
        _segment_dmas(cnt_ref, loc_ref, row_ref, i + 1, ne, copy_into(1 - cur), wait=False)

    slot = slot_ref[...]
    gates = gates_ref[...]
    scol = lax.broadcasted_iota(jnp.int32, (tm, ns), 1)
    gmat = jnp.zeros((tm, ns), F32)
    for kk in range(TOP_K):
        gmat = jnp.where(scol == slot[:, kk:kk + 1], gates[:, kk:kk + 1], gmat)
    _segment_dmas(cnt_ref, loc_ref, row_ref, i, ne, copy_into(cur), wait=True)
    acc = jnp.dot(gmat.astype(BF16), sorted_ref[cur].astype(BF16), preferred_element_type=F32)
    o_ref[...] = x_ref[...] + gf_ref[0] * acc


def _combine(cnt, loc, rowstart, slot, gates, x2, gate_ffn, yb, seq, ne):
    t, d = x2.shape
    tm = MOE_TILE
    per_b = seq // tm
    grid_spec = pltpu.PrefetchScalarGridSpec(
        num_scalar_prefetch=3,
        grid=(t // tm,),
        in_specs=[pl.BlockSpec((tm, LANES), lambda i, *_: (i, 0)),
                  pl.BlockSpec((tm, LANES), lambda i, *_: (i, 0)),
                  pl.BlockSpec((tm, d), lambda i, *_: (i, 0)),
                  pl.BlockSpec((1, 1, d), lambda i, *_: (i // per_b, 0, 0)),
                  pl.BlockSpec(memory_space=pl.ANY)],
        out_specs=pl.BlockSpec((tm, d), lambda i, *_: (i, 0)),
        scratch_shapes=[pltpu.VMEM((2, MOE_SLOTS, d), F32), pltpu.SemaphoreType.DMA((2,))],
    )
    return pl.pallas_call(
        functools.partial(_combine_kernel, ne=ne),
        grid_spec=grid_spec,
        out_shape=jax.ShapeDtypeStruct((t, d), F32),
        compiler_params=_cparams(("arbitrary",)),
        name="moe_combine",
    )(cnt, loc, rowstart, slot, gates, x2, gate_ffn, yb)


def _group_pairs(b1):
    lead = b1.shape[:-1]
    g = b1.reshape(lead + (-1, LANES, 2))
    return jnp.swapaxes(g, -1, -2).reshape(b1.shape)


def _moe_layer(x2, h2, gate_ffn, router_w, router_b, w1, b1_grp, w2, b2, seq, layer):
    t, d = x2.shape
    ne = router_w.shape[1]
    rw_pad = jnp.zeros((d, LANES), F32).at[:, :ne].set(router_w.astype(F32))
    rb_pad = jnp.full((1, LANES), -jnp.inf, F32).at[0, :ne].set(router_b.astype(F32))
    slot, gates, cnt3, base3, loc3 = _router(h2, rw_pad, rb_pad)
    n_assign = t * TOP_K
    n_tiles = t // MOE_TILE
    n_blocks = -(-(n_assign + n_tiles * ne * (SEG_ALIGN - 1)) // MOE_BLOCK) + ne
    cnt = cnt3[:, 0, :ne]
    base = base3[:, 0, :ne]
    loc = loc3[:, 0, :ne]
    total = base[-1] + cnt[-1]
    padded = ((total + MOE_BLOCK - 1) // MOE_BLOCK) * MOE_BLOCK
    pad_end = jnp.cumsum(padded)
    pad_start = pad_end - padded
    rowstart = (pad_start[None, :] + base).reshape(-1).astype(jnp.int32)
    n_used = (pad_end[-1] // MOE_BLOCK).astype(jnp.int32)
    block_start = jnp.minimum(jnp.arange(n_blocks, dtype=jnp.int32), n_used - 1) * MOE_BLOCK
    block_e = jnp.minimum(jnp.sum(block_start[:, None] >= pad_end[None, :], axis=-1), ne - 1).astype(jnp.int32)
    cnt_f = cnt.reshape(-1)
    loc_f = loc.reshape(-1)
    n_used = n_used.reshape(1)
    xb = _dispatch(cnt_f, loc_f, rowstart, (padded - total).astype(jnp.int32), (pad_start + total).astype(jnp.int32),
                   n_used, slot, h2, n_blocks * MOE_BLOCK, ne)
    yb = _experts(block_e, n_used, xb, w1, b1_grp, w2, b2, layer)
    return _combine(cnt_f, loc_f, rowstart, slot, gates, x2, gate_ffn, yb, seq, ne)


def kernel(x, c, ada_w, ada_b, norm1_g, norm2_g, w_in, w_out, s5_lam_re, s5_lam_im, s5_log_dt, s5_b_re, s5_b_im, s5_c_re, s5_c_im, s5_d, s5_glu_w, rwkv_mu, rwkv_w0, rwkv_w1, rwkv_w2, rwkv_a0, rwkv_a1, rwkv_a2, rwkv_g1, rwkv_g2, rwkv_k_k, rwkv_k_a, rwkv_r_k, rwkv_ln_w, rwkv_ln_b, na_q_g, na_k_g, na_rpb, router_w, router_b, exp_w1, exp_b1, exp_w2, exp_b2):
    bsz, seq, d = x.shape
    depth = ada_w.shape[0]
    t = bsz * seq
    mod = _ada_mod(c, ada_w, ada_b).reshape(depth, bsz, 6, 1, d)
    x2 = x.reshape(t, d)
    seg_ones = _seg_ones(RW, dtype=BF16)
    seg_mean = _seg_ones(RW, dtype=BF16, scale=1.0 / HEAD)
    row = lambda a: a.reshape(1, -1).astype(F32)
    b1_grp = _group_pairs(exp_b1.astype(F32))
    exp_b2f = exp_b2.astype(F32)
    for l in range(depth):
        m = lambda j: mod[l, :, j]
        s5u, xr, qkv = _in_proj(x2, row(norm1_g[l]), m(0), m(1), w_in[l].astype(BF16), seq)
        prep_params = dict(
            mu=row(rwkv_mu[l]), k_k=row(rwkv_k_k[l]), k_a=row(rwkv_k_a[l]), r_k=row(rwkv_r_k[l]),
            w0=rwkv_w0[l].astype(F32), a0=rwkv_a0[l].astype(F32),
            w1=rwkv_w1[l].astype(BF16), w2=rwkv_w2[l].astype(BF16),
            a1=rwkv_a1[l].astype(BF16), a2=rwkv_a2[l].astype(BF16),
            g1=rwkv_g1[l].astype(BF16), g2=rwkv_g2[l].astype(BF16),
            q_g=row(jnp.tile(na_q_g[l], NA_W // HEAD)), k_g=row(jnp.tile(na_k_g[l], NA_W // HEAD)), ob=seg_ones)
        (nkk, r, v, lw0, b0, k0, lw1, b1, k1, gate, bonus, naq, nak, nav) = _prep(
            xr.reshape(bsz, seq, 4 * RW), qkv.reshape(bsz, seq, 3 * NA_W), prep_params)
        yf, yb = _wkv_scan(dict(nkk=nkk, r=r, v=v, lw0=lw0, b0=b0, k0=k0, lw1=lw1, b1=b1, k1=k1))
        bblk, cblk, lam, lamc = _s5_params(s5_lam_re[l], s5_lam_im[l], s5_log_dt[l], s5_b_re[l], s5_b_im[l],
                                           s5_c_re[l], s5_c_im[l])
        s5o = _s5_mixer(s5u.reshape(bsz, seq, S5_WIDTH), bblk, cblk, lam, lamc, row(s5_d[l]),
                        s5_glu_w[l].astype(BF16))
        nao = _na_mixer(naq, nak, nav, _na_bias_table(na_rpb[l]))
        flat = lambda a: a.reshape(t, -1)
        x2, h2 = _out_proj(x2, flat(s5o), flat(yf), flat(yb), flat(gate), flat(bonus), flat(nao),
                           row(rwkv_ln_w[l]), row(rwkv_ln_b[l]), seg_mean, w_out[l].astype(BF16),
                           m(2), row(norm2_g[l]), m(3), m(4), seq)
        x2 = _moe_layer(x2, h2, m(5), router_w[l], router_b[l], exp_w1, b1_grp, exp_w2, exp_b2f, seq, l)
    return x2.reshape(bsz, seq, d)
```
